```python
import numpy as np
import jax
import jax.numpy as jnp
from jax import lax

D_MODEL = 2048
BATCH = 8
SEQ = 2048
DEPTH = 2

HEAD_DIM = 128
MIX_HEADS = D_MODEL // HEAD_DIM
MLA_HEADS = MIX_HEADS // 2
NSA_HEADS = MIX_HEADS // 4
MOBA_HEADS = MIX_HEADS - MLA_HEADS - NSA_HEADS
MIX_WIDTH = (MLA_HEADS + NSA_HEADS + MOBA_HEADS) * HEAD_DIM
ROPE_THETA = 500000.0
PARTIAL_ROT = HEAD_DIM // 4
QBLK = 128
MLA_Q_RANK = 512
MLA_KV_RANK = 512
MLA_NOPE = 128
MLA_ROPE = 64
MLA_V = HEAD_DIM
NSA_CMP_LEN = 32
NSA_CMP_STRIDE = 16
NSA_CMP_HIDDEN = HEAD_DIM
NSA_SEL_LEN = 64
NSA_SEL_TOPK = 16
NSA_WINDOW = 512
NSA_FORCE_SCORE = 1.0e4
MOBA_BLOCK = 256
MOBA_TOPK = 3
MOBA_QCHUNK = 64
MEM_LEN = 256
MEM_HEADS = 4
D_FF = 4 * D_MODEL
DEEPNORM_ALPHA = (2 * DEPTH) ** 0.25
DEEPNORM_BETA = (8 * DEPTH) ** -0.25
IN_SPLITS = (MLA_Q_RANK, MLA_KV_RANK, MLA_ROPE,
             NSA_HEADS * HEAD_DIM, HEAD_DIM, HEAD_DIM, HEAD_DIM, HEAD_DIM, HEAD_DIM, HEAD_DIM, 3 * NSA_HEADS,
             MOBA_HEADS * HEAD_DIM, MOBA_HEADS * HEAD_DIM, MOBA_HEADS * HEAD_DIM)
IN_WIDTH = sum(IN_SPLITS)

kernel_name = 'hybrid_mla_nsa_moba_deepnorm'


def layer_norm(x, g, b, eps=1e-5):
    xf = x.astype(jnp.float32)
    mu = jnp.mean(xf, axis=-1, keepdims=True)
    var = jnp.mean(jnp.square(xf - mu), axis=-1, keepdims=True)
    y = (xf - mu) * lax.rsqrt(var + eps)
    return (y * g.astype(jnp.float32) + b.astype(jnp.float32)).astype(x.dtype)


def rms_norm(x, g, eps=1e-6):
    xf = x.astype(jnp.float32)
    y = xf * lax.rsqrt(jnp.mean(jnp.square(xf), axis=-1, keepdims=True) + eps)
    return (y * g.astype(jnp.float32)).astype(x.dtype)


def masked_softmax(s, mask):
    s = jnp.where(mask, s.astype(jnp.float32), -jnp.inf)
    m = jnp.max(s, axis=-1, keepdims=True)
    m = jnp.where(jnp.isfinite(m), m, 0.0)
    e = jnp.exp(s - m)
    return e / jnp.maximum(jnp.sum(e, axis=-1, keepdims=True), 1e-30)


def rope_tables(n_pos, dim):
    inv = ROPE_THETA ** (-jnp.arange(0, dim, 2, dtype=jnp.float32) / dim)
    ang = jnp.arange(n_pos, dtype=jnp.float32)[:, None] * inv[None, :]
    return jnp.cos(ang), jnp.sin(ang)


def apply_rope(x, cos, sin):
    half = x.shape[-1] // 2
    c = cos[None, :, None, :].astype(x.dtype)
    s = sin[None, :, None, :].astype(x.dtype)
    x1, x2 = x[..., :half], x[..., half:]
    return jnp.concatenate([x1 * c - x2 * s, x2 * c + x1 * s], axis=-1)


def partial_rope(x, cos, sin):
    rot = 2 * cos.shape[-1]
    return jnp.concatenate([apply_rope(x[..., :rot], cos, sin), x[..., rot:]], axis=-1)


def map_batch_chunks(fn, n_batch, n_chunk):
    items = jnp.arange(n_batch * n_chunk)
    return lax.map(lambda i: fn(i // n_chunk, i % n_chunk), items)


def causal_attention_qblocks(q, k, v, scale):
    B, S, H, dk = q.shape
    nq = S // QBLK
    qb = q.reshape(B, nq, QBLK, H, dk).transpose(1, 0, 2, 3, 4)
    kpos = jnp.arange(S)

    def one(args):
        qi, i = args
        s = jnp.einsum('bqhd,bkhd->bhqk', qi, k).astype(jnp.float32) * scale
        qpos = i * QBLK + jnp.arange(QBLK)
        p = masked_softmax(s, kpos[None, :] <= qpos[:, None])
        return jnp.einsum('bhqk,bkhd->bqhd', p.astype(v.dtype), v)

    out = lax.map(one, (qb, jnp.arange(nq)))
    return out.transpose(1, 0, 2, 3, 4).reshape(B, S, H, v.shape[-1])


def mla_mixer(c_q, c_kv, k_rope_in, q_norm, kv_norm, w_uq, w_ukv):
    B, S, _ = c_q.shape
    cos, sin = rope_tables(S, MLA_ROPE)
    q = (rms_norm(c_q, q_norm) @ w_uq).reshape(B, S, MLA_HEADS, MLA_NOPE + MLA_ROPE)
    q = jnp.concatenate([q[..., :MLA_NOPE], apply_rope(q[..., MLA_NOPE:], cos, sin)], axis=-1)
    kv = (rms_norm(c_kv, kv_norm) @ w_ukv).reshape(B, S, MLA_HEADS, MLA_NOPE + MLA_V)
    k_rope = apply_rope(k_rope_in[:, :, None, :], cos, sin)
    k = jnp.concatenate([kv[..., :MLA_NOPE], jnp.broadcast_to(k_rope, (B, S, MLA_HEADS, MLA_ROPE))], axis=-1)
    v = kv[..., MLA_NOPE:]
    o = causal_attention_qblocks(q, k, v, (MLA_NOPE + MLA_ROPE) ** -0.5)
    return o.reshape(B, S, MLA_HEADS * MLA_V)


def nsa_mixer(q_in, k_c, v_c, k_s, v_s, k_w, v_w, gate_in, cmp_w1, cmp_w2, cmp_pos):
    B, S, _ = q_in.shape
    H, Dh = NSA_HEADS, HEAD_DIM
    cos, sin = rope_tables(S, PARTIAL_ROT)
    q = partial_rope(q_in.reshape(B, S, H, Dh), cos, sin)
    rot = lambda t: partial_rope(t[:, :, None, :], cos, sin)[:, :, 0, :]
    k_c, k_s, k_w = rot(k_c), rot(k_s), rot(k_w)
    scale = Dh ** -0.5
    tpos = jnp.arange(S)

    n_cmp = (S - NSA_CMP_LEN) // NSA_CMP_STRIDE + 1
    starts = np.arange(n_cmp) * NSA_CMP_STRIDE
    gidx = starts[:, None] + np.arange(NSA_CMP_LEN)[None, :]

    def compress(t, i):
        blk = t[:, gidx] + cmp_pos[i]
        hid = jax.nn.gelu(blk.reshape(B, n_cmp, NSA_CMP_LEN * Dh) @ cmp_w1[i])
        return hid @ cmp_w2[i]

    kc, vc = compress(k_c, 0), compress(v_c, 1)
    s_cmp = jnp.einsum('bshd,bnd->bhsn', q, kc).astype(jnp.float32) * scale
    cmp_mask = (starts + NSA_CMP_LEN - 1)[None, :] <= tpos[:, None]
    p_cmp = masked_softmax(s_cmp, cmp_mask)
    o_cmp = jnp.einsum('bhsn,bnd->bshd', p_cmp.astype(vc.dtype), vc)

    n_sel = S // NSA_SEL_LEN
    sel_start = np.arange(n_sel) * NSA_SEL_LEN
    overlap = ((starts[:, None] < sel_start[None, :] + NSA_SEL_LEN)
               & (starts[:, None] + NSA_CMP_LEN > sel_start[None, :])).astype(np.float32)
    imp = jnp.einsum('bhsn,nj->bsj', p_cmp, jnp.asarray(overlap))
    cur = tpos // NSA_SEL_LEN
    j = jnp.arange(n_sel)
    eligible = j[None, :] <= cur[:, None]
    forced = (j[None, :] == 0) | (j[None, :] == cur[:, None]) | (j[None, :] == cur[:, None] - 1)
    score = jnp.where(eligible, jnp.where(forced, NSA_FORCE_SCORE, imp), -jnp.inf)
    _, sel_idx = lax.top_k(score, min(NSA_SEL_TOPK, n_sel))
    n_k = sel_idx.shape[-1]
    n_ch = S // QBLK

    def sel_chunk(b, c):
        t0 = c * QBLK
        qc = lax.dynamic_slice_in_dim(q[b], t0, QBLK, 0)
        ic = lax.dynamic_slice_in_dim(sel_idx[b], t0, QBLK, 0)
        kg = k_s[b].reshape(n_sel, NSA_SEL_LEN, Dh)[ic].reshape(QBLK, n_k * NSA_SEL_LEN, Dh)
        vg = v_s[b].reshape(n_sel, NSA_SEL_LEN, Dh)[ic].reshape(QBLK, n_k * NSA_SEL_LEN, Dh)
        kpos = (ic[:, :, None] * NSA_SEL_LEN + jnp.arange(NSA_SEL_LEN)).reshape(QBLK, -1)
        mask = kpos <= (t0 + jnp.arange(QBLK))[:, None]
        s = jnp.einsum('thd,tkd->htk', qc, kg).astype(jnp.float32) * scale
        p = masked_softmax(s, mask[None])
        return jnp.einsum('htk,tkd->thd', p.astype(vg.dtype), vg)

    o_sel = map_batch_chunks(sel_chunk, B, n_ch).reshape(B, S, H, Dh)

    n_band = NSA_WINDOW // QBLK + 1
    bidx = np.arange(n_ch)[:, None] + np.arange(n_band)[None, :]

    def band(t):
        tp = jnp.pad(t, ((0, 0), (NSA_WINDOW, 0), (0, 0))).reshape(B, n_ch + NSA_WINDOW // QBLK, QBLK, Dh)
        return tp[:, bidx].reshape(B, n_ch, n_band * QBLK, Dh)

    kb, vb = band(k_w), band(v_w)
    kpos = (np.arange(n_ch)[:, None] * QBLK - NSA_WINDOW) + np.arange(n_band * QBLK)[None, :]
    qpos = np.arange(n_ch * QBLK).reshape(n_ch, QBLK)
    diff = qpos[:, :, None] - kpos[:, None, :]
    wmask = jnp.asarray((diff >= 0) & (diff < NSA_WINDOW) & (kpos[:, None, :] >= 0))
    qb = q.reshape(B, n_ch, QBLK, H, Dh)
    s_w = jnp.einsum('bnqhd,bnkd->bnhqk', qb, kb).astype(jnp.float32) * scale
    p_w = masked_softmax(s_w, wmask[None, :, None])
    o_win = jnp.einsum('bnhqk,bnkd->bnqhd', p_w.astype(vb.dtype), vb).reshape(B, S, H, Dh)

    g = jax.nn.sigmoid(gate_in.astype(jnp.float32)).reshape(B, S, H, 3).astype(q.dtype)
    o = g[..., 0:1] * o_cmp + g[..., 1:2] * o_sel + g[..., 2:3] * o_win
    return o.reshape(B, S, H * Dh)


def moba_mixer(q_in, k_in, v_in):
    B, S, _ = q_in.shape
    H, Dh, BLK = MOBA_HEADS, HEAD_DIM, MOBA_BLOCK
    cos, sin = rope_tables(S, PARTIAL_ROT)
    q = partial_rope(q_in.reshape(B, S, H, Dh), cos, sin)
    k = partial_rope(k_in.reshape(B, S, H, Dh), cos, sin)
    v = v_in.reshape(B, S, H, Dh)
    Sp = -(-S // BLK) * BLK
    padw = ((0, 0), (0, Sp - S), (0, 0), (0, 0))
    q, k, v = jnp.pad(q, padw), jnp.pad(k, padw), jnp.pad(v, padw)
    nb = Sp // BLK
    kb_all = k.reshape(B, nb, BLK, H, Dh).transpose(0, 3, 1, 2, 4)
    vb_all = v.reshape(B, nb, BLK, H, Dh).transpose(0, 3, 1, 2, 4)
    kmean = jnp.mean(kb_all.astype(jnp.float32), axis=3)
    gate = jnp.einsum('bshd,bhnd->bhsn', q.astype(jnp.float32), kmean)
    tpos = jnp.arange(Sp)
    cur = tpos // BLK
    gate = jnp.where(jnp.arange(nb)[None, :] < cur[:, None], gate, -jnp.inf)
    n_k = min(MOBA_TOPK, nb)
    _, sel = lax.top_k(gate, n_k)
    valid = sel < cur[None, None, :, None]
    scale = Dh ** -0.5
    n_ch = Sp // MOBA_QCHUNK
    gather = jax.vmap(lambda blocks, idx: blocks[idx])

    def chunk(b, c):
        t0 = c * MOBA_QCHUNK
        blk = t0 // BLK
        qc = lax.dynamic_slice_in_dim(q[b], t0, MOBA_QCHUNK, 0)
        ic = lax.dynamic_slice_in_dim(sel[b], t0, MOBA_QCHUNK, 1)
        vl = lax.dynamic_slice_in_dim(valid[b], t0, MOBA_QCHUNK, 1)
        kg = gather(kb_all[b], ic).reshape(H, MOBA_QCHUNK, n_k * BLK, Dh)
        vg = gather(vb_all[b], ic).reshape(H, MOBA_QCHUNK, n_k * BLK, Dh)
        ko = lax.dynamic_slice_in_dim(kb_all[b], blk, 1, 1)[:, 0]
        vo = lax.dynamic_slice_in_dim(vb_all[b], blk, 1, 1)[:, 0]
        tq = t0 + jnp.arange(MOBA_QCHUNK)
        own_mask = (blk * BLK + jnp.arange(BLK))[None, :] <= tq[:, None]
        sel_mask = jnp.repeat(vl, BLK, axis=-1)
        s = jnp.concatenate([jnp.einsum('thd,htkd->htk', qc, kg),
                             jnp.einsum('thd,hkd->htk', qc, ko)], axis=-1).astype(jnp.float32) * scale
        mask = jnp.concatenate([sel_mask, jnp.broadcast_to(own_mask[None], (H, MOBA_QCHUNK, BLK))], axis=-1)
        p = masked_softmax(s, mask).astype(v.dtype)
        n = n_k * BLK
        return jnp.einsum('htk,htkd->thd', p[..., :n], vg) + jnp.einsum('htk,hkd->thd', p[..., n:], vo)

    o = map_batch_chunks(chunk, B, n_ch).reshape(B, Sp, H, Dh)[:, :S]
    return o.reshape(B, S, H * Dh)


def hybrid_mixer(h, w_in, mla_q_norm, mla_kv_norm, mla_w_uq, mla_w_ukv, nsa_cmp_w1, nsa_cmp_w2, nsa_cmp_pos, w_out):
    proj = h @ w_in
    offs = np.cumsum(IN_SPLITS)[:-1].tolist()
    (c_q, c_kv, k_rope, nq, nkc, nvc, nks, nvs, nkw, nvw, ng, mq, mk, mv) = jnp.split(proj, offs, axis=-1)
    o_a = mla_mixer(c_q, c_kv, k_rope, mla_q_norm, mla_kv_norm, mla_w_uq, mla_w_ukv)
    o_b = nsa_mixer(nq, nkc, nvc, nks, nvs, nkw, nvw, ng, nsa_cmp_w1, nsa_cmp_w2, nsa_cmp_pos)
    o_c = moba_mixer(mq, mk, mv)
    return jnp.concatenate([o_a, o_b, o_c], axis=-1) @ w_out


def memory_cross_attention(h, mem, wq, wkv, wo):
    B, S, D = h.shape
    M = mem.shape[1]
    hd = D // MEM_HEADS
    q = (h @ wq).reshape(B, S, MEM_HEADS, hd)
    kv = (mem @ wkv).reshape(B, M, 2, MEM_HEADS, hd)
    s = jnp.einsum('bshd,bmhd->bhsm', q, kv[:, :, 0]).astype(jnp.float32) * hd ** -0.5
    p = jax.nn.softmax(s, axis=-1).astype(h.dtype)
    return jnp.einsum('bhsm,bmhd->bshd', p, kv[:, :, 1]).reshape(B, S, D) @ wo


def squared_relu_mlp(h, w1, w2):
    return jnp.square(jax.nn.relu(h @ w1)) @ w2


def setup_inputs(seed: int = 0) -> dict:
    key = jax.random.key(seed)
    ks = jax.random.split(key, 24)
    L, D = DEPTH, D_MODEL
    f32 = jnp.float32

    def w(k, shape, fan_in, scale=1.0):
        return jax.random.normal(k, shape, f32) * (scale * fan_in ** -0.5)

    def gain(k, shape):
        return 1.0 + 0.02 * jax.random.normal(k, shape, f32)

    def bias(k, shape):
        return 0.02 * jax.random.normal(k, shape, f32)

    return {
        'x': jax.random.normal(ks[0], (BATCH, SEQ, D), f32),
        'mem': jax.random.normal(ks[1], (BATCH, MEM_LEN, D), f32),
        'ln_in_g': gain(ks[2], (D,)),
        'ln_in_b': bias(ks[3], (D,)),
        'w_in': w(ks[4], (L, D, IN_WIDTH), D),
        'mla_q_norm': gain(ks[5], (L, MLA_Q_RANK)),
        'mla_kv_norm': gain(ks[6], (L, MLA_KV_RANK)),
        'mla_w_uq': w(ks[7], (L, MLA_Q_RANK, MLA_HEADS * (MLA_NOPE + MLA_ROPE)), MLA_Q_RANK),
        'mla_w_ukv': w(ks[8], (L, MLA_KV_RANK, MLA_HEADS * (MLA_NOPE + MLA_V)), MLA_KV_RANK),
        'nsa_cmp_w1': w(ks[9], (L, 2, NSA_CMP_LEN * HEAD_DIM, NSA_CMP_HIDDEN), NSA_CMP_LEN * HEAD_DIM),
        'nsa_cmp_w2': w(ks[10], (L, 2, NSA_CMP_HIDDEN, HEAD_DIM), NSA_CMP_HIDDEN),
        'nsa_cmp_pos': 0.1 * jax.random.normal(ks[11], (L, 2, NSA_CMP_LEN, HEAD_DIM), f32),
        'w_out': w(ks[12], (L, MIX_WIDTH, D), MIX_WIDTH, DEEPNORM_BETA),
        'ln1_g': gain(ks[13], (L, D)),
        'ln1_b': bias(ks[14], (L, D)),
        'mem_wq': w(ks[15], (L, D, D), D),
        'mem_wkv': w(ks[16], (L, D, 2 * D), D),
        'mem_wo': w(ks[17], (L, D, D), D, DEEPNORM_BETA),
        'ln2_g': gain(ks[18], (L, D)),
        'ln2_b': bias(ks[19], (L, D)),
        'mlp_w1': w(ks[20], (L, D, D_FF), D),
        'mlp_w2': w(ks[21], (L, D_FF, D), D_FF, DEEPNORM_BETA),
        'ln3_g': gain(ks[22], (L, D)),
        'ln3_b': bias(ks[23], (L, D)),
    }


def reference(x, mem, ln_in_g, ln_in_b, w_in, mla_q_norm, mla_kv_norm, mla_w_uq, mla_w_ukv,
              nsa_cmp_w1, nsa_cmp_w2, nsa_cmp_pos, w_out, ln1_g, ln1_b,
              mem_wq, mem_wkv, mem_wo, ln2_g, ln2_b, mlp_w1, mlp_w2, ln3_g, ln3_b):
    h = layer_norm(x, ln_in_g, ln_in_b)
    for l in range(DEPTH):
        mix = hybrid_mixer(h, w_in[l], mla_q_norm[l], mla_kv_norm[l], mla_w_uq[l], mla_w_ukv[l],
                           nsa_cmp_w1[l], nsa_cmp_w2[l], nsa_cmp_pos[l], w_out[l])
        h = layer_norm(DEEPNORM_ALPHA * h + mix, ln1_g[l], ln1_b[l])
        h = layer_norm(DEEPNORM_ALPHA * h + memory_cross_attention(h, mem, mem_wq[l], mem_wkv[l], mem_wo[l]),
                       ln2_g[l], ln2_b[l])
        h = layer_norm(DEEPNORM_ALPHA * h + squared_relu_mlp(h, mlp_w1[l], mlp_w2[l]), ln3_g[l], ln3_b[l])
    return h
```

```python
import functools

import numpy as np
import jax
import jax.numpy as jnp
from jax import lax
from jax.experimental import pallas as pl
from jax.experimental.pallas import tpu as pltpu

F32 = jnp.float32
BF16 = jnp.bfloat16
NEG_INF = float("-inf")

D_MODEL = 2048
DEPTH = 2
HEAD_DIM = 128
MLA_HEADS = 8
NSA_HEADS = 4
MOBA_HEADS = 4
ROPE_THETA = 500000.0
PARTIAL_ROT = HEAD_DIM // 4
MLA_Q_RANK = 512
MLA_KV_RANK = 512
MLA_NOPE = 128
MLA_ROPE = 64
MLA_SLOT = 256
NSA_CMP_LEN = 32
NSA_CMP_STRIDE = 16
NSA_SEL_LEN = 64
NSA_SEL_TOPK = 16
NSA_WINDOW = 512
NSA_FORCE_SCORE = 1.0e4
MOBA_BLOCK = 256
MOBA_TOPK = 3
MEM_HEADS = 4
D_FF = 4 * D_MODEL
DEEPNORM_ALPHA = (2 * DEPTH) ** 0.25
LANES = 128

PROJ_WIDTH = 4096
COL512 = dict(c_q=0, c_kv=1, nq=2, mq=3, mk=4, mv=5)
COL128 = dict(k_rope=24, nkc=25, nvc=26, nks=27, nvs=28, nkw=29, nvw=30, ng=31)

VMEM_LIMIT = 56 * 1024 * 1024


def _cparams(*sem):
    return pltpu.CompilerParams(dimension_semantics=sem, vmem_limit_bytes=VMEM_LIMIT)


def _dot(a, b):
    return jnp.dot(a, b, preferred_element_type=F32)


def _dot_nt(a, b):
    return lax.dot_general(a, b, (((1,), (1,)), ((), ())), preferred_element_type=F32)


def _split_bf16(x):
    hi = x.astype(BF16)
    return hi, (x - hi.astype(F32)).astype(BF16)


def _ln_rows(x, g, b, eps=1e-5):
    mu = jnp.mean(x, axis=-1, keepdims=True)
    xc = x - mu
    var = jnp.mean(xc * xc, axis=-1, keepdims=True)
    return xc * lax.rsqrt(var + eps) * g + b


def _rms_rows(x, g, eps=1e-6):
    return x * lax.rsqrt(jnp.mean(x * x, axis=-1, keepdims=True) + eps) * g


def _rope128(x, c, s1, s2, half):
    return x * c + pltpu.roll(x, LANES - half, 1) * s1 + pltpu.roll(x, half, 1) * s2


def _rope_tables(n_pos, dim):
    half = dim // 2
    inv = ROPE_THETA ** (-jnp.arange(0, dim, 2, dtype=F32) / dim)
    ang = jnp.arange(n_pos, dtype=F32)[:, None] * inv[None, :]
    cos, sin = jnp.cos(ang), jnp.sin(ang)
    ones = jnp.ones((n_pos, LANES - dim), F32)
    z = lambda w: jnp.zeros((n_pos, w), F32)
    c = jnp.concatenate([cos, cos, ones], axis=1)
    s1 = jnp.concatenate([-sin, z(LANES - half)], axis=1)
    s2 = jnp.concatenate([z(half), sin, z(LANES - dim)], axis=1)
    return c, s1, s2


def _softmax_step(s, mask, m, l, acc, v):
    s = jnp.where(mask, s, NEG_INF)
    m_new = jnp.maximum(m, jnp.max(s, axis=-1, keepdims=True))
    m_safe = jnp.where(m_new == NEG_INF, 0.0, m_new)
    alpha = jnp.exp(m - m_safe)
    p = jnp.exp(s - m_safe)
    l_new = alpha * l + jnp.sum(p, axis=-1, keepdims=True)
    acc_new = alpha * acc + _dot(p.astype(BF16), v)
    return m_new, l_new, acc_new


def _rank_before(score, cols, lane):
    rank = jnp.zeros(score.shape, F32)
    for jp in cols:
        col = score[:, jp:jp + 1]
        ahead = (col > score) | ((col == score) & (lane > jp))
        rank = rank + jnp.where(ahead, 1.0, 0.0)
    return rank


def _ln_kernel(x_ref, g_ref, b_ref, o_ref):
    o_ref[...] = _ln_rows(x_ref[...], g_ref[...], b_ref[...])


def _layer_norm(x, g, b, tm=512):
    n, d = x.shape
    return pl.pallas_call(
        _ln_kernel,
        grid=(n // tm,),
        in_specs=[pl.BlockSpec((tm, d), lambda i: (i, 0)),
                  pl.BlockSpec((1, d), lambda i: (0, 0)),
                  pl.BlockSpec((1, d), lambda i: (0, 0))],
        out_specs=pl.BlockSpec((tm, d), lambda i: (i, 0)),
        out_shape=jax.ShapeDtypeStruct((n, d), F32),
        compiler_params=_cparams("parallel"),
        name="ln_in",
    )(x, g.reshape(1, d), b.reshape(1, d))


def _mm_kernel(a_ref, w_ref, o_ref, abf_ref):
    @pl.when(pl.program_id(1) == 0)
    def _():
        abf_ref[...] = a_ref[...].astype(BF16)

    o_ref[...] = _dot(abf_ref[...], w_ref[...]).astype(o_ref.dtype)


def _matmul(a, w, out_dtype, tm=1024, tn=512, name="mm"):
    m, k = a.shape
    n = w.shape[1]
    tm = min(tm, m)
    return pl.pallas_call(
        _mm_kernel,
        grid=(m // tm, n // tn),
        in_specs=[pl.BlockSpec((tm, k), lambda i, j: (i, 0)),
                  pl.BlockSpec((k, tn), lambda i, j: (0, j))],
        out_specs=pl.BlockSpec((tm, tn), lambda i, j: (i, j)),
        out_shape=jax.ShapeDtypeStruct((m, n), out_dtype),
        scratch_shapes=[pltpu.VMEM((tm, k), BF16)],
        compiler_params=_cparams("parallel", "arbitrary"),
        name=name,
    )(a, w)


def _out_ln_kernel(*refs, widths):
    n_a = len(widths)
    a_refs = refs[:n_a]
    w_ref, h_ref, g_ref, b_ref, o_ref = refs[n_a:]
    acc = DEEPNORM_ALPHA * h_ref[...]
    off = 0
    for a_ref, wd in zip(a_refs, widths):
        acc = acc + _dot(a_ref[...], w_ref[off:off + wd, :])
        off += wd
    o_ref[...] = _ln_rows(acc, g_ref[...], b_ref[...])


def _out_ln(a_list, w, h, g, b, tm=512, name="out_ln"):
    n, d = h.shape
    widths = tuple(a.shape[1] for a in a_list)
    k = sum(widths)
    in_specs = [pl.BlockSpec((tm, wd), lambda i: (i, 0)) for wd in widths]
    in_specs += [pl.BlockSpec((k, d), lambda i: (0, 0)),
                 pl.BlockSpec((tm, d), lambda i: (i, 0)),
                 pl.BlockSpec((1, d), lambda i: (0, 0)),
                 pl.BlockSpec((1, d), lambda i: (0, 0))]
    return pl.pallas_call(
        functools.partial(_out_ln_kernel, widths=widths),
        grid=(n // tm,),
        in_specs=in_specs,
        out_specs=pl.BlockSpec((tm, d), lambda i: (i, 0)),
        out_shape=jax.ShapeDtypeStruct((n, d), F32),
        compiler_params=_cparams("parallel"),
        name=name,
    )(*a_list, w, h, g.reshape(1, d), b.reshape(1, d))


def _mlp_kernel(h_ref, w1_ref, w2_ref, g_ref, b_ref, o_ref, hbf_ref, acc_ref):
    f = pl.program_id(1)

    @pl.when(f == 0)
    def _():
        hbf_ref[...] = h_ref[...].astype(BF16)
        acc_ref[...] = jnp.zeros_like(acc_ref)

    u = jnp.maximum(_dot(hbf_ref[...], w1_ref[...]), 0.0)
    acc_ref[...] += _dot((u * u).astype(BF16), w2_ref[...])

    @pl.when(f == pl.num_programs(1) - 1)
    def _():
        y = DEEPNORM_ALPHA * h_ref[...] + acc_ref[...]
        o_ref[...] = _ln_rows(y, g_ref[...], b_ref[...])


def _mlp_ln(h, w1, w2, g, b, tm=512, tf=512):
    n, d = h.shape
    dff = w1.shape[1]
    return pl.pallas_call(
        _mlp_kernel,
        grid=(n // tm, dff // tf),
        in_specs=[pl.BlockSpec((tm, d), lambda i, f: (i, 0)),
                  pl.BlockSpec((d, tf), lambda i, f: (0, f)),
                  pl.BlockSpec((tf, d), lambda i, f: (f, 0)),
                  pl.BlockSpec((1, d), lambda i, f: (0, 0)),
                  pl.BlockSpec((1, d), lambda i, f: (0, 0))],
        out_specs=pl.BlockSpec((tm, d), lambda i, f: (i, 0)),
        out_shape=jax.ShapeDtypeStruct((n, d), F32),
        scratch_shapes=[pltpu.VMEM((tm, d), BF16), pltpu.VMEM((tm, d), F32)],
        compiler_params=_cparams("parallel", "arbitrary"),
        name="mlp_ln",
    )(h, w1, w2, g.reshape(1, d), b.reshape(1, d))


def _mla_up_kernel(cq_ref, ckv_ref, kr_ref, gq_ref, gkv_ref, wq_ref, wkv_ref, tc_ref, ts1_ref, ts2_ref,
                   q_ref, k_ref, v_ref, *, scale):
    half = MLA_ROPE // 2
    c, s1, s2 = tc_ref[...], ts1_ref[...], ts2_ref[...]
    nq = _rms_rows(cq_ref[...], gq_ref[...]).astype(BF16)
    nkv = _rms_rows(ckv_ref[...], gkv_ref[...]).astype(BF16)
    qf = _dot(nq, wq_ref[...])
    kvf = _dot(nkv, wkv_ref[...])
    kr = _rope128(kr_ref[...], c, s1, s2, half).astype(BF16)
    for h in range(MLA_HEADS):
        o = h * MLA_SLOT
        q_ref[0, h, :, 0:LANES] = (qf[:, o:o + LANES] * scale).astype(BF16)
        q_ref[0, h, :, LANES:] = (_rope128(qf[:, o + LANES:o + MLA_SLOT], c, s1, s2, half) * scale).astype(BF16)
        k_ref[0, h, :, 0:LANES] = kvf[:, h * LANES:(h + 1) * LANES].astype(BF16)
        k_ref[0, h, :, LANES:] = kr
        v_ref[0, h] = kvf[:, (MLA_HEADS + h) * LANES:(MLA_HEADS + h + 1) * LANES].astype(BF16)


def _mla_up(proj, gq, gkv, wq, wkv, tabs, batch, seq, tm=512):
    nt = seq // tm
    row = lambda b, i: b * nt + i
    tab_spec = pl.BlockSpec((tm, LANES), lambda b, i: (i, 0))
    hm = lambda w: pl.BlockSpec((1, MLA_HEADS, tm, w), lambda b, i: (b, 0, i, 0))
    return pl.pallas_call(
        functools.partial(_mla_up_kernel, scale=(MLA_NOPE + MLA_ROPE) ** -0.5),
        grid=(batch, nt),
        in_specs=[pl.BlockSpec((tm, 512), lambda b, i: (row(b, i), COL512["c_q"])),
                  pl.BlockSpec((tm, 512), lambda b, i: (row(b, i), COL512["c_kv"])),
                  pl.BlockSpec((tm, LANES), lambda b, i: (row(b, i), COL128["k_rope"])),
                  pl.BlockSpec((1, MLA_Q_RANK), lambda b, i: (0, 0)),
                  pl.BlockSpec((1, MLA_KV_RANK), lambda b, i: (0, 0)),
                  pl.BlockSpec(wq.shape, lambda b, i: (0, 0)),
                  pl.BlockSpec(wkv.shape, lambda b, i: (0, 0)),
                  tab_spec, tab_spec, tab_spec],
        out_specs=[hm(MLA_SLOT), hm(MLA_SLOT), hm(HEAD_DIM)],
        out_shape=[jax.ShapeDtypeStruct((batch, MLA_HEADS, seq, MLA_SLOT), BF16),
                   jax.ShapeDtypeStruct((batch, MLA_HEADS, seq, MLA_SLOT), BF16),
                   jax.ShapeDtypeStruct((batch, MLA_HEADS, seq, HEAD_DIM), BF16)],
        compiler_params=_cparams("parallel", "parallel"),
        name="mla_up",
    )(proj, proj, proj, gq.reshape(1, -1), gkv.reshape(1, -1), wq, wkv, *tabs)


def _mla_attn_kernel(q_ref, k_ref, v_ref, o_ref, *, tq):
    qi = pl.program_id(2)
    q = q_ref[0, 0]
    rows = qi * tq + lax.broadcasted_iota(jnp.int32, (tq, tq), 0)
    cols0 = lax.broadcasted_iota(jnp.int32, (tq, tq), 1)

    def body(kb, carry):
        r = pl.ds(pl.multiple_of(kb * tq, tq), tq)
        s = _dot_nt(q, k_ref[0, 0, r, :])
        return _softmax_step(s, kb * tq + cols0 <= rows, *carry, v_ref[0, 0, r, :])

    init = (jnp.full((tq, 1), NEG_INF, F32), jnp.zeros((tq, 1), F32), jnp.zeros((tq, HEAD_DIM), F32))
    _, l, acc = lax.fori_loop(0, qi + 1, body, init)
    o_ref[0] = (acc / jnp.maximum(l, 1e-30)).astype(o_ref.dtype)


def _mla_attn(q, k, v, tq=256):
    batch, heads, seq, _ = q.shape
    return pl.pallas_call(
        functools.partial(_mla_attn_kernel, tq=tq),
        grid=(batch, heads, seq // tq),
        in_specs=[pl.BlockSpec((1, 1, tq, MLA_SLOT), lambda b, h, i: (b, h, i, 0)),
                  pl.BlockSpec((1, 1, seq, MLA_SLOT), lambda b, h, i: (b, h, 0, 0)),
                  pl.BlockSpec((1, 1, seq, HEAD_DIM), lambda b, h, i: (b, h, 0, 0))],
        out_specs=pl.BlockSpec((1, tq, HEAD_DIM), lambda b, h, i: (b, i, h)),
        out_shape=jax.ShapeDtypeStruct((batch, seq, heads * HEAD_DIM), BF16),
        compiler_params=_cparams("parallel", "parallel", "arbitrary"),
        name="mla_attn",
    )(q, k, v)


def _nsa_kernel(nq_ref, kc_ref, vc_ref, ks_ref, vs_ref, kw_ref, vw_ref, ng_ref,
                w1_ref, w2_ref, pos_ref, ov_ref, tc_ref, ts1_ref, ts2_ref,
                o_ref,
                tmp_sc, kc_sc, vc_sc, ks_sc, vs_sc, kw_sc, vw_sc, *, tq, seq, scale):
    qi = pl.program_id(1)
    half = PARTIAL_ROT // 2
    heads = NSA_HEADS
    n_chunk = seq // NSA_CMP_STRIDE

    @pl.when(qi == 0)
    def _prep():
        c, s1, s2 = tc_ref[...], ts1_ref[...], ts2_ref[...]
        ks_sc[...] = _rope128(ks_ref[...], c, s1, s2, half).astype(BF16)
        kw_sc[...] = _rope128(kw_ref[...], c, s1, s2, half).astype(BF16)
        vs_sc[...] = vs_ref[...].astype(BF16)
        vw_sc[...] = vw_ref[...].astype(BF16)
        for i, (src, dst) in enumerate(((kc_ref, kc_sc), (vc_ref, vc_sc))):
            tmp_sc[...] = _rope128(src[...], c, s1, s2, half) if i == 0 else src[...]
            lo, hi = [], []
            for t in range(NSA_CMP_STRIDE):
                x = tmp_sc[pl.ds(t, n_chunk, stride=NSA_CMP_STRIDE), :]
                lo.append((x + pos_ref[i, t:t + 1, :]).astype(BF16))
                hi.append((x + pos_ref[i, NSA_CMP_STRIDE + t:NSA_CMP_STRIDE + t + 1, :]).astype(BF16))
            kw1 = NSA_CMP_STRIDE * HEAD_DIM
            a = _dot(jnp.concatenate(lo, axis=1), w1_ref[i, 0:kw1, :])
            bm = _dot(jnp.concatenate(hi, axis=1), w1_ref[i, kw1:2 * kw1, :])
            hid = jax.nn.gelu(a + pltpu.roll(bm, n_chunk - 1, 0))
            dst[...] = _dot(hid.astype(BF16), w2_ref[i]).astype(BF16)

    t0 = pl.multiple_of(qi * tq, tq)
    rq = pl.ds(t0, tq)
    c, s1, s2 = tc_ref[rq, :], ts1_ref[rq, :], ts2_ref[rq, :]
    qf = nq_ref[...]
    qs = jnp.concatenate(
        [_rope128(qf[:, h * LANES:(h + 1) * LANES], c, s1, s2, half) * scale for h in range(heads)],
        axis=0).astype(BF16)

    lane = lax.broadcasted_iota(jnp.int32, (tq, LANES), 1)
    tpos = t0 + lax.broadcasted_iota(jnp.int32, (tq, LANES), 0)

    s = _dot_nt(qs, kc_sc[...]).reshape(heads, tq, LANES)
    cmask = (lane * NSA_CMP_STRIDE + (NSA_CMP_LEN - 1) <= tpos)[None]
    s = jnp.where(cmask, s, NEG_INF)
    m = jnp.max(s, axis=-1, keepdims=True)
    m = jnp.where(m == NEG_INF, 0.0, m)
    e = jnp.exp(s - m)
    p_cmp = e / jnp.maximum(jnp.sum(e, axis=-1, keepdims=True), 1e-30)
    o_cmp = _dot(p_cmp.reshape(heads * tq, LANES).astype(BF16), vc_sc[...])

    psum = p_cmp[0] + p_cmp[1] + p_cmp[2] + p_cmp[3]
    p_hi, p_lo = _split_bf16(psum)
    imp = _dot(p_hi, ov_ref[...]) + _dot(p_lo, ov_ref[...])
    cur = tpos >> 6
    n_sel = seq // NSA_SEL_LEN
    eligible = lane <= cur
    forced = (lane == 0) | (lane == cur) | (lane == cur - 1)
    score = jnp.where(eligible, jnp.where(forced, NSA_FORCE_SCORE, imp), NEG_INF)
    rank = _rank_before(score, range(n_sel), lane)
    selm = jnp.where(eligible & (rank < NSA_SEL_TOPK), 1.0, 0.0).astype(BF16)

    rows = t0 + lax.broadcasted_iota(jnp.int32, (tq, tq), 0)
    cols0 = lax.broadcasted_iota(jnp.int32, (tq, tq), 1)
    blk_row = lax.broadcasted_iota(jnp.int32, (LANES, tq), 0)
    blk_col = lax.broadcasted_iota(jnp.int32, (LANES, tq), 1)

    def branch(k_sc, v_sc, lo, mask_fn):
        def body(kb, carry):
            r = pl.ds(pl.multiple_of(kb * tq, tq), tq)
            s = _dot_nt(qs, k_sc[r, :]).reshape(heads, tq, tq)
            mk = mask_fn(kb)[None]
            m, l, acc = carry
            s = jnp.where(mk, s, NEG_INF)
            m_new = jnp.maximum(m, jnp.max(s, axis=-1, keepdims=True))
            m_safe = jnp.where(m_new == NEG_INF, 0.0, m_new)
            alpha = jnp.exp(m - m_safe)
            p = jnp.exp(s - m_safe)
            l_new = alpha * l + jnp.sum(p, axis=-1, keepdims=True)
            pv = _dot(p.reshape(heads * tq, tq).astype(BF16), v_sc[r, :]).reshape(heads, tq, HEAD_DIM)
            return m_new, l_new, alpha * acc + pv

        init = (jnp.full((heads, tq, 1), NEG_INF, F32), jnp.zeros((heads, tq, 1), F32),
                jnp.zeros((heads, tq, HEAD_DIM), F32))
        _, l, acc = lax.fori_loop(lo, qi + 1, body, init)
        return acc / jnp.maximum(l, 1e-30)

    def sel_mask(kb):
        expand = jnp.where(((kb * tq + blk_col) >> 6) == blk_row, 1.0, 0.0).astype(BF16)
        return (_dot(selm, expand) > 0.5) & (kb * tq + cols0 <= rows)

    def win_mask(kb):
        diff = rows - (kb * tq + cols0)
        return (diff >= 0) & (diff < NSA_WINDOW)

    o_sel = branch(ks_sc, vs_sc, 0, sel_mask)
    o_win = branch(kw_sc, vw_sc, jnp.maximum(qi - NSA_WINDOW // tq, 0), win_mask)
    o_cmp = o_cmp.reshape(heads, tq, HEAD_DIM)

    g = jax.nn.sigmoid(ng_ref[...])
    for h in range(heads):
        o = (g[:, 3 * h:3 * h + 1] * o_cmp[h] + g[:, 3 * h + 1:3 * h + 2] * o_sel[h]
             + g[:, 3 * h + 2:3 * h + 3] * o_win[h])
        o_ref[:, h * LANES:(h + 1) * LANES] = o.astype(o_ref.dtype)


def _nsa(proj, w1, w2, pos, tabs, batch, seq, tq=256):
    nt = seq // tq
    n_cmp = (seq - NSA_CMP_LEN) // NSA_CMP_STRIDE + 1
    n_sel = seq // NSA_SEL_LEN
    starts = np.arange(LANES) * NSA_CMP_STRIDE
    sel_start = np.arange(LANES) * NSA_SEL_LEN
    overlap = ((starts[:, None] < sel_start[None, :] + NSA_SEL_LEN)
               & (starts[:, None] + NSA_CMP_LEN > sel_start[None, :])
               & (np.arange(LANES)[:, None] < n_cmp) & (np.arange(LANES)[None, :] < n_sel))
    ov = jnp.asarray(overlap.astype(np.float32), BF16)
    seq_col = lambda name: pl.BlockSpec((seq, LANES), lambda b, i: (b, COL128[name]))
    full = lambda a: pl.BlockSpec(a.shape, lambda b, i: (0,) * a.ndim)
    return pl.pallas_call(
        functools.partial(_nsa_kernel, tq=tq, seq=seq, scale=HEAD_DIM ** -0.5),
        grid=(batch, nt),
        in_specs=[pl.BlockSpec((tq, 512), lambda b, i: (b * nt + i, COL512["nq"])),
                  seq_col("nkc"), seq_col("nvc"), seq_col("nks"), seq_col("nvs"), seq_col("nkw"), seq_col("nvw"),
                  pl.BlockSpec((tq, LANES), lambda b, i: (b * nt + i, COL128["ng"])),
                  full(w1), full(w2), full(pos), full(ov), full(tabs[0]), full(tabs[1]), full(tabs[2])],
        out_specs=pl.BlockSpec((tq, NSA_HEADS * HEAD_DIM), lambda b, i: (b * nt + i, 0)),
        out_shape=jax.ShapeDtypeStruct((batch * seq, NSA_HEADS * HEAD_DIM), BF16),
        scratch_shapes=[pltpu.VMEM((seq, LANES), F32),
                        pltpu.VMEM((LANES, LANES), BF16), pltpu.VMEM((LANES, LANES), BF16),
                        pltpu.VMEM((seq, LANES), BF16), pltpu.VMEM((seq, LANES), BF16),
                        pltpu.VMEM((seq, LANES), BF16), pltpu.VMEM((seq, LANES), BF16)],
        compiler_params=_cparams("parallel", "arbitrary"),
        name="nsa",
    )(proj, proj, proj, proj, proj, proj, proj, proj, w1, w2, pos, ov, *tabs)


def _moba_kernel(mq_ref, mk_ref, mv_ref, tc_ref, ts1_ref, ts2_ref, o_ref, k_sc, v_sc, km_sc, *, tq, seq, scale):
    qi = pl.program_id(1)
    half = PARTIAL_ROT // 2
    n_blk = seq // MOBA_BLOCK

    @pl.when(qi == 0)
    def _prep():
        c, s1, s2 = tc_ref[...], ts1_ref[...], ts2_ref[...]
        km_sc[...] = jnp.zeros_like(km_sc)
        for h in range(MOBA_HEADS):
            hs = slice(h * LANES, (h + 1) * LANES)
            kh = _rope128(mk_ref[:, hs], c, s1, s2, half)
            k_sc[:, hs] = kh.astype(BF16)
            for j in range(n_blk):
                km_sc[h * n_blk + j:h * n_blk + j + 1, :] = jnp.mean(
                    kh[j * MOBA_BLOCK:(j + 1) * MOBA_BLOCK], axis=0, keepdims=True)
        v_sc[...] = mv_ref[...].astype(BF16)

    t0 = pl.multiple_of(qi * tq, tq)
    rq = pl.ds(t0, tq)
    c, s1, s2 = tc_ref[rq, :], ts1_ref[rq, :], ts2_ref[rq, :]
    lane = lax.broadcasted_iota(jnp.int32, (tq, LANES), 1)
    causal = lax.broadcasted_iota(jnp.int32, (tq, tq), 1) <= lax.broadcasted_iota(jnp.int32, (tq, tq), 0)
    km_hi, km_lo = _split_bf16(km_sc[...])

    for h in range(MOBA_HEADS):
        hs = slice(h * LANES, (h + 1) * LANES)
        qf = _rope128(mq_ref[:, hs], c, s1, s2, half) * scale
        q, q_lo = _split_bf16(qf)
        gate = _dot_nt(q, km_hi) + (_dot_nt(q, km_lo) + _dot_nt(q_lo, km_hi))
        jl = lane - h * n_blk
        eligible = (jl >= 0) & (jl < qi)
        score = jnp.where(eligible, gate, NEG_INF)
        rank = _rank_before(score, range(h * n_blk, (h + 1) * n_blk), lane)
        selm = jnp.where(eligible & (rank < MOBA_TOPK), 1.0, 0.0)

        def body(kb, carry, hs=hs, selm=selm, q=q, h=h):
            r = pl.ds(pl.multiple_of(kb * tq, tq), tq)
            s = _dot_nt(q, k_sc[r, hs])
            picked = jnp.sum(jnp.where(lane == h * n_blk + kb, selm, 0.0), axis=-1, keepdims=True) > 0.5
            return _softmax_step(s, picked, *carry, v_sc[r, hs])

        init = (jnp.full((tq, 1), NEG_INF, F32), jnp.zeros((tq, 1), F32), jnp.zeros((tq, HEAD_DIM), F32))
        carry = lax.fori_loop(0, qi, body, init)
        s_own = _dot_nt(q, k_sc[rq, hs])
        _, l, acc = _softmax_step(s_own, causal, *carry, v_sc[rq, hs])
        o_ref[:, hs] = (acc / jnp.maximum(l, 1e-30)).astype(o_ref.dtype)


def _moba(proj, tabs, batch, seq):
    tq = MOBA_BLOCK
    nt = seq // tq
    width = MOBA_HEADS * HEAD_DIM
    full = lambda a: pl.BlockSpec(a.shape, lambda b, i: (0,) * a.ndim)
    return pl.pallas_call(
        functools.partial(_moba_kernel, tq=tq, seq=seq, scale=HEAD_DIM ** -0.5),
        grid=(batch, nt),
        in_specs=[pl.BlockSpec((tq, width), lambda b, i: (b * nt + i, COL512["mq"])),
                  pl.BlockSpec((seq, width), lambda b, i: (b, COL512["mk"])),
                  pl.BlockSpec((seq, width), lambda b, i: (b, COL512["mv"])),
                  full(tabs[0]), full(tabs[1]), full(tabs[2])],
        out_specs=pl.BlockSpec((tq, width), lambda b, i: (b * nt + i, 0)),
        out_shape=jax.ShapeDtypeStruct((batch * seq, width), BF16),
        scratch_shapes=[pltpu.VMEM((seq, width), BF16), pltpu.VMEM((seq, width), BF16),
                        pltpu.VMEM((LANES, LANES), F32)],
        compiler_params=_cparams("parallel", "arbitrary"),
        name="moba",
    )(proj, proj, proj, *tabs)


def _xattn_kernel(q_ref, kv_ref, o_ref):
    d = q_ref.shape[1]
    hd = d // MEM_HEADS
    for h in range(MEM_HEADS):
        hs = slice(h * hd, (h + 1) * hd)
        s = _dot_nt(q_ref[:, hs], kv_ref[0, :, hs])
        e = jnp.exp(s - jnp.max(s, axis=-1, keepdims=True))
        p = e / jnp.sum(e, axis=-1, keepdims=True)
        o_ref[:, hs] = _dot(p.astype(BF16), kv_ref[0, :, d + h * hd:d + (h + 1) * hd]).astype(o_ref.dtype)


def _xattn(q, kv, batch, seq, tq=512):
    d = q.shape[1]
    nt = seq // tq
    m_len = kv.shape[0] // batch
    return pl.pallas_call(
        _xattn_kernel,
        grid=(batch, nt),
        in_specs=[pl.BlockSpec((tq, d), lambda b, i: (b * nt + i, 0)),
                  pl.BlockSpec((1, m_len, 2 * d), lambda b, i: (b, 0, 0))],
        out_specs=pl.BlockSpec((tq, d), lambda b, i: (b * nt + i, 0)),
        out_shape=jax.ShapeDtypeStruct((batch * seq, d), BF16),
        compiler_params=_cparams("parallel", "parallel"),
        name="xattn",
    )(q, kv.reshape(batch, m_len, 2 * d))


def _pack_w_in(w):
    d = w.shape[0]
    sizes = (512, 512, 64, 512, 128, 128, 128, 128, 128, 128, 12, 512, 512, 512)
    names = ("c_q", "c_kv", "k_rope", "nq", "nkc", "nvc", "nks", "nvs", "nkw", "nvw", "ng", "mq", "mk", "mv")
    offs = np.cumsum((0,) + sizes)
    piece = {n: w[:, offs[i]:offs[i + 1]] for i, n in enumerate(names)}
    pad = lambda a, wd: jnp.pad(a, ((0, 0), (0, wd - a.shape[1])))
    order512 = sorted(COL512, key=COL512.get)
    order128 = sorted(COL128, key=COL128.get)
    cols = [piece[n] for n in order512] + [pad(piece[n], LANES) for n in order128]
    out = jnp.concatenate(cols, axis=1).astype(BF16)
    assert out.shape == (d, PROJ_WIDTH)
    return out


def _pack_w_uq(w):
    r = w.shape[0]
    w = w.reshape(r, MLA_HEADS, MLA_NOPE + MLA_ROPE)
    w = jnp.pad(w, ((0, 0), (0, 0), (0, MLA_SLOT - MLA_NOPE - MLA_ROPE)))
    return w.reshape(r, MLA_HEADS * MLA_SLOT).astype(BF16)


def _pack_w_ukv(w):
    r = w.shape[0]
    w = w.reshape(r, MLA_HEADS, 2, HEAD_DIM).transpose(0, 2, 1, 3)
    return w.reshape(r, 2 * MLA_HEADS * HEAD_DIM).astype(BF16)


def kernel(x, mem, ln_in_g, ln_in_b, w_in, mla_q_norm, mla_kv_norm, mla_w_uq, mla_w_ukv, nsa_cmp_w1, nsa_cmp_w2, nsa_cmp_pos, w_out, ln1_g, ln1_b, mem_wq, mem_wkv, mem_wo, ln2_g, ln2_b, mlp_w1, mlp_w2, ln3_g, ln3_b):
    batch, seq, d = x.shape
    n = batch * seq
    mla_tabs = _rope_tables(seq, MLA_ROPE)
    rot_tabs = _rope_tables(seq, PARTIAL_ROT)
    mem2 = mem.reshape(batch * mem.shape[1], d)

    h = _layer_norm(x.reshape(n, d), ln_in_g, ln_in_b)
    for l in range(DEPTH):
        proj = _matmul(h, _pack_w_in(w_in[l]), F32, name="in_proj")
        q, k, v = _mla_up(proj, mla_q_norm[l], mla_kv_norm[l], _pack_w_uq(mla_w_uq[l]), _pack_w_ukv(mla_w_ukv[l]),
                          mla_tabs, batch, seq)
        o_a = _mla_attn(q, k, v).reshape(n, MLA_HEADS * HEAD_DIM)
        o_b = _nsa(proj, nsa_cmp_w1[l].astype(BF16), nsa_cmp_w2[l].astype(BF16), nsa_cmp_pos[l], rot_tabs, batch, seq)
        o_c = _moba(proj, rot_tabs, batch, seq)
        h = _out_ln([o_a, o_b, o_c], w_out[l].astype(BF16), h, ln1_g[l], ln1_b[l], name="mix_out_ln")

        wq_scaled = (mem_wq[l] * (d // MEM_HEADS) ** -0.5).astype(BF16)
        xq = _matmul(h, wq_scaled, BF16, name="mem_q")
        xkv = _matmul(mem2, mem_wkv[l].astype(BF16), BF16, name="mem_kv")
        ctx = _xattn(xq, xkv, batch, seq)
        h = _out_ln([ctx], mem_wo[l].astype(BF16), h, ln2_g[l], ln2_b[l], name="mem_out_ln")

        h = _mlp_ln(h, mlp_w1[l].astype(BF16), mlp_w2[l].astype(BF16), ln3_g[l], ln3_b[l])
    return h.reshape(batch, seq, d)
```

```python
import functools

import numpy as np
import jax
import jax.numpy as jnp
from jax import lax
from jax.experimental import pallas as pl
from jax.experimental.pallas import tpu as pltpu

F32 = jnp.float32
BF16 = jnp.bfloat16
NEG_INF = float("-inf")

D_MODEL = 2048
DEPTH = 2
HEAD_DIM = 128
MLA_HEADS = 8
NSA_HEADS = 4
MOBA_HEADS = 4
ROPE_THETA = 500000.0
PARTIAL_ROT = HEAD_DIM // 4
MLA_Q_RANK = 512
MLA_KV_RANK = 512
MLA_NOPE = 128
MLA_ROPE = 64
MLA_SLOT = 256
NSA_CMP_LEN = 32
NSA_CMP_STRIDE = 16
NSA_SEL_LEN = 64
NSA_SEL_TOPK = 16
NSA_WINDOW = 512
NSA_FORCE_SCORE = 1.0e4
MOBA_BLOCK = 256
MOBA_TOPK = 3
MEM_HEADS = 4
D_FF = 4 * D_MODEL
DEEPNORM_ALPHA = (2 * DEPTH) ** 0.25
LANES = 128

PROJ_WIDTH = 4096
COL512 = dict(c_q=0, c_kv=1, nq=2, mq=3, mk=4, mv=5)
COL128 = dict(k_rope=24, nkc=25, nvc=26, nks=27, nvs=28, nkw=29, nvw=30, ng=31)

VMEM_LIMIT = 56 * 1024 * 1024


def _cparams(*sem):
    return pltpu.CompilerParams(dimension_semantics=sem, vmem_limit_bytes=VMEM_LIMIT)


def _dot(a, b):
    return jnp.dot(a, b, preferred_element_type=F32)


def _dot_nt(a, b):
    return lax.dot_general(a, b, (((1,), (1,)), ((), ())), preferred_element_type=F32)


def _split_bf16(x):
    hi = x.astype(BF16)
    return hi, (x - hi.astype(F32)).astype(BF16)


def _ln_rows(x, g, b, eps=1e-5):
    mu = jnp.mean(x, axis=-1, keepdims=True)
    xc = x - mu
    var = jnp.mean(xc * xc, axis=-1, keepdims=True)
    return xc * lax.rsqrt(var + eps) * g + b


def _rms_rows(x, g, eps=1e-6):
    return x * lax.rsqrt(jnp.mean(x * x, axis=-1, keepdims=True) + eps) * g


def _rope128(x, c, s1, s2, half):
    return x * c + pltpu.roll(x, LANES - half, 1) * s1 + pltpu.roll(x, half, 1) * s2


def _rope_tables(n_pos, dim):
    half = dim // 2
    inv = ROPE_THETA ** (-jnp.arange(0, dim, 2, dtype=F32) / dim)
    ang = jnp.arange(n_pos, dtype=F32)[:, None] * inv[None, :]
    cos, sin = jnp.cos(ang), jnp.sin(ang)
    ones = jnp.ones((n_pos, LANES - dim), F32)
    z = lambda w: jnp.zeros((n_pos, w), F32)
    c = jnp.concatenate([cos, cos, ones], axis=1)
    s1 = jnp.concatenate([-sin, z(LANES - half)], axis=1)
    s2 = jnp.concatenate([z(half), sin, z(LANES - dim)], axis=1)
    return c, s1, s2


def _softmax_step(s, mask, m, l, acc, v):
    s = jnp.where(mask, s, NEG_INF)
    m_new = jnp.maximum(m, jnp.max(s, axis=-1, keepdims=True))
    m_safe = jnp.where(m_new == NEG_INF, 0.0, m_new)
    alpha = jnp.exp(m - m_safe)
    p = jnp.exp(s - m_safe)
    l_new = alpha * l + jnp.sum(p, axis=-1, keepdims=True)
    acc_new = alpha * acc + _dot(p.astype(BF16), v)
    return m_new, l_new, acc_new


def _rank_before(score, cols, lane):
    rank = jnp.zeros(score.shape, F32)
    for jp in cols:
        col = score[:, jp:jp + 1]
        ahead = (col > score) | ((col == score) & (lane > jp))
        rank = rank + jnp.where(ahead, 1.0, 0.0)
    return rank


def _ln_kernel(x_ref, g_ref, b_ref, o_ref):
    o_ref[...] = _ln_rows(x_ref[...], g_ref[...], b_ref[...])


def _layer_norm(x, g, b, tm=512):
    n, d = x.shape
    return pl.pallas_call(
        _ln_kernel,
        grid=(n // tm,),
        in_specs=[pl.BlockSpec((tm, d), lambda i: (i, 0)),
                  pl.BlockSpec((1, d), lambda i: (0, 0)),
                  pl.BlockSpec((1, d), lambda i: (0, 0))],
        out_specs=pl.BlockSpec((tm, d), lambda i: (i, 0)),
        out_shape=jax.ShapeDtypeStruct((n, d), F32),
        compiler_params=_cparams("parallel"),
        name="ln_in",
    )(x, g.reshape(1, d), b.reshape(1, d))


def _mm_kernel(a_ref, w_ref, o_ref, abf_ref):
    @pl.when(pl.program_id(1) == 0)
    def _():
        abf_ref[...] = a_ref[...].astype(BF16)

    o_ref[...] = _dot(abf_ref[...], w_ref[...]).astype(o_ref.dtype)


def _matmul(a, w, out_dtype, tm=1024, tn=512, name="mm"):
    m, k = a.shape
    n = w.shape[1]
    tm = min(tm, m)
    return pl.pallas_call(
        _mm_kernel,
        grid=(m // tm, n // tn),
        in_specs=[pl.BlockSpec((tm, k), lambda i, j: (i, 0)),
                  pl.BlockSpec((k, tn), lambda i, j: (0, j))],
        out_specs=pl.BlockSpec((tm, tn), lambda i, j: (i, j)),
        out_shape=jax.ShapeDtypeStruct((m, n), out_dtype),
        scratch_shapes=[pltpu.VMEM((tm, k), BF16)],
        compiler_params=_cparams("parallel", "arbitrary"),
        name=name,
    )(a, w)


def _out_ln_kernel(*refs, widths):
    n_a = len(widths)
    a_refs = refs[:n_a]
    w_ref, h_ref, g_ref, b_ref, o_ref = refs[n_a:]
    acc = DEEPNORM_ALPHA * h_ref[...]
    off = 0
    for a_ref, wd in zip(a_refs, widths):
        acc = acc + _dot(a_ref[...], w_ref[off:off + wd, :])
        off += wd
    o_ref[...] = _ln_rows(acc, g_ref[...], b_ref[...])


def _out_ln(a_list, w, h, g, b, tm=512, name="out_ln"):
    n, d = h.shape
    widths = tuple(a.shape[1] for a in a_list)
    k = sum(widths)
    in_specs = [pl.BlockSpec((tm, wd), lambda i: (i, 0)) for wd in widths]
    in_specs += [pl.BlockSpec((k, d), lambda i: (0, 0)),
                 pl.BlockSpec((tm, d), lambda i: (i, 0)),
                 pl.BlockSpec((1, d), lambda i: (0, 0)),
                 pl.BlockSpec((1, d), lambda i: (0, 0))]
    return pl.pallas_call(
        functools.partial(_out_ln_kernel, widths=widths),
        grid=(n // tm,),
        in_specs=in_specs,
        out_specs=pl.BlockSpec((tm, d), lambda i: (i, 0)),
        out_shape=jax.ShapeDtypeStruct((n, d), F32),
        compiler_params=_cparams("parallel"),
        name=name,
    )(*a_list, w, h, g.reshape(1, d), b.reshape(1, d))


def _mlp_kernel(h_ref, w1_ref, w2_ref, g_ref, b_ref, o_ref, hbf_ref, acc_ref):
    f = pl.program_id(1)

    @pl.when(f == 0)
    def _():
        hbf_ref[...] = h_ref[...].astype(BF16)
        acc_ref[...] = jnp.zeros_like(acc_ref)

    u = jnp.maximum(_dot(hbf_ref[...], w1_ref[...]), 0.0)
    acc_ref[...] += _dot((u * u).astype(BF16), w2_ref[...])

    @pl.when(f == pl.num_programs(1) - 1)
    def _():
        y = DEEPNORM_ALPHA * h_ref[...] + acc_ref[...]
        o_ref[...] = _ln_rows(y, g_ref[...], b_ref[...])


def _mlp_ln(h, w1, w2, g, b, tm=512, tf=512):
    n, d = h.shape
    dff = w1.shape[1]
    return pl.pallas_call(
        _mlp_kernel,
        grid=(n // tm, dff // tf),
        in_specs=[pl.BlockSpec((tm, d), lambda i, f: (i, 0)),
                  pl.BlockSpec((d, tf), lambda i, f: (0, f)),
                  pl.BlockSpec((tf, d), lambda i, f: (f, 0)),
                  pl.BlockSpec((1, d), lambda i, f: (0, 0)),
                  pl.BlockSpec((1, d), lambda i, f: (0, 0))],
        out_specs=pl.BlockSpec((tm, d), lambda i, f: (i, 0)),
        out_shape=jax.ShapeDtypeStruct((n, d), F32),
        scratch_shapes=[pltpu.VMEM((tm, d), BF16), pltpu.VMEM((tm, d), F32)],
        compiler_params=_cparams("parallel", "arbitrary"),
        name="mlp_ln",
    )(h, w1, w2, g.reshape(1, d), b.reshape(1, d))


def _mla_up_kernel(cq_ref, ckv_ref, kr_ref, gq_ref, gkv_ref, wq_ref, wkv_ref, tc_ref, ts1_ref, ts2_ref,
                   q_ref, k_ref, v_ref, *, scale):
    half = MLA_ROPE // 2
    c, s1, s2 = tc_ref[...], ts1_ref[...], ts2_ref[...]
    nq = _rms_rows(cq_ref[...], gq_ref[...]).astype(BF16)
    nkv = _rms_rows(ckv_ref[...], gkv_ref[...]).astype(BF16)
    qf = _dot(nq, wq_ref[...])
    kvf = _dot(nkv, wkv_ref[...])
    kr = _rope128(kr_ref[...], c, s1, s2, half).astype(BF16)
    for h in range(MLA_HEADS):
        o = h * MLA_SLOT
        q_ref[0, h, :, 0:LANES] = (qf[:, o:o + LANES] * scale).astype(BF16)
        q_ref[0, h, :, LANES:] = (_rope128(qf[:, o + LANES:o + MLA_SLOT], c, s1, s2, half) * scale).astype(BF16)
        k_ref[0, h, :, 0:LANES] = kvf[:, h * LANES:(h + 1) * LANES].astype(BF16)
        k_ref[0, h, :, LANES:] = kr
        v_ref[0, h, :, 0:LANES] = kvf[:, (MLA_HEADS + h) * LANES:(MLA_HEADS + h + 1) * LANES].astype(BF16)
        v_ref[0, h, :, LANES:] = jnp.ones((kr.shape[0], LANES), BF16)


def _mla_up(proj, gq, gkv, wq, wkv, tabs, batch, seq, tm=512):
    nt = seq // tm
    row = lambda b, i: b * nt + i
    tab_spec = pl.BlockSpec((tm, LANES), lambda b, i: (i, 0))
    hm = lambda w: pl.BlockSpec((1, MLA_HEADS, tm, w), lambda b, i: (b, 0, i, 0))
    return pl.pallas_call(
        functools.partial(_mla_up_kernel, scale=(MLA_NOPE + MLA_ROPE) ** -0.5),
        grid=(batch, nt),
        in_specs=[pl.BlockSpec((tm, 512), lambda b, i: (row(b, i), COL512["c_q"])),
                  pl.BlockSpec((tm, 512), lambda b, i: (row(b, i), COL512["c_kv"])),
                  pl.BlockSpec((tm, LANES), lambda b, i: (row(b, i), COL128["k_rope"])),
                  pl.BlockSpec((1, MLA_Q_RANK), lambda b, i: (0, 0)),
                  pl.BlockSpec((1, MLA_KV_RANK), lambda b, i: (0, 0)),
                  pl.BlockSpec(wq.shape, lambda b, i: (0, 0)),
                  pl.BlockSpec(wkv.shape, lambda b, i: (0, 0)),
                  tab_spec, tab_spec, tab_spec],
        out_specs=[hm(MLA_SLOT), hm(MLA_SLOT), hm(2 * HEAD_DIM)],
        out_shape=[jax.ShapeDtypeStruct((batch, MLA_HEADS, seq, MLA_SLOT), BF16),
                   jax.ShapeDtypeStruct((batch, MLA_HEADS, seq, MLA_SLOT), BF16),
                   jax.ShapeDtypeStruct((batch, MLA_HEADS, seq, 2 * HEAD_DIM), BF16)],
        compiler_params=_cparams("parallel", "parallel"),
        name="mla_up",
    )(proj, proj, proj, gq.reshape(1, -1), gkv.reshape(1, -1), wq, wkv, *tabs)


def _mla_attn_kernel(q_ref, k_ref, v_ref, o_ref, *, tq):
    qi = pl.program_id(1)
    heads = q_ref.shape[1]
    causal = lax.broadcasted_iota(jnp.int32, (tq, tq), 1) <= lax.broadcasted_iota(jnp.int32, (tq, tq), 0)

    def step(r, carry, mask):
        out = []
        for h in range(heads):
            m, acc = carry[h]
            s = _dot_nt(q_ref[0, h], k_ref[0, h, r, :])
            if mask is not None:
                s = jnp.where(mask, s, NEG_INF)
            m_new = jnp.maximum(m, jnp.max(s, axis=-1, keepdims=True))
            p = jnp.exp(s - m_new).astype(BF16)
            out.append((m_new, jnp.exp(m - m_new) * acc + _dot(p, v_ref[0, h, r, :])))
        return tuple(out)

    def body(kb, carry):
        return step(pl.ds(pl.multiple_of(kb * tq, tq), tq), carry, None)

    init = tuple((jnp.full((tq, 1), NEG_INF, F32), jnp.zeros((tq, 2 * HEAD_DIM), F32)) for _ in range(heads))
    carry = lax.fori_loop(0, qi, body, init)
    carry = step(pl.ds(pl.multiple_of(qi * tq, tq), tq), carry, causal)
    for h in range(heads):
        acc = carry[h][1]
        o_ref[0, :, h * HEAD_DIM:(h + 1) * HEAD_DIM] = (acc[:, :HEAD_DIM] / acc[:, HEAD_DIM:]).astype(o_ref.dtype)


def _mla_attn(q, k, v, tq=256):
    batch, heads, seq, _ = q.shape
    return pl.pallas_call(
        functools.partial(_mla_attn_kernel, tq=tq),
        grid=(batch, seq // tq),
        in_specs=[pl.BlockSpec((1, heads, tq, MLA_SLOT), lambda b, i: (b, 0, i, 0)),
                  pl.BlockSpec((1, heads, seq, MLA_SLOT), lambda b, i: (b, 0, 0, 0)),
                  pl.BlockSpec((1, heads, seq, 2 * HEAD_DIM), lambda b, i: (b, 0, 0, 0))],
        out_specs=pl.BlockSpec((1, tq, heads * HEAD_DIM), lambda b, i: (b, i, 0)),
        out_shape=jax.ShapeDtypeStruct((batch, seq, heads * HEAD_DIM), BF16),
        compiler_params=_cparams("parallel", "arbitrary"),
        name="mla_attn",
    )(q, k, v)


def _nsa_kernel(nq_ref, kc_ref, vc_ref, ks_ref, vs_ref, kw_ref, vw_ref, ng_ref,
                w1_ref, w2_ref, pos_ref, ov_ref, tc_ref, ts1_ref, ts2_ref,
                o_ref,
                tmp_sc, kc_sc, vc_sc, ks_sc, vs_sc, kw_sc, vw_sc, *, tq, seq, scale):
    qi = pl.program_id(1)
    half = PARTIAL_ROT // 2
    heads = NSA_HEADS
    n_chunk = seq // NSA_CMP_STRIDE

    @pl.when(qi == 0)
    def _prep():
        c, s1, s2 = tc_ref[...], ts1_ref[...], ts2_ref[...]
        ks_sc[...] = _rope128(ks_ref[...], c, s1, s2, half).astype(BF16)
        kw_sc[...] = _rope128(kw_ref[...], c, s1, s2, half).astype(BF16)
        vs_sc[...] = vs_ref[...].astype(BF16)
        vw_sc[...] = vw_ref[...].astype(BF16)
        for i, (src, dst) in enumerate(((kc_ref, kc_sc), (vc_ref, vc_sc))):
            tmp_sc[...] = _rope128(src[...], c, s1, s2, half) if i == 0 else src[...]
            lo, hi = [], []
            for t in range(NSA_CMP_STRIDE):
                x = tmp_sc[pl.ds(t, n_chunk, stride=NSA_CMP_STRIDE), :]
                lo.append((x + pos_ref[i, t:t + 1, :]).astype(BF16))
                hi.append((x + pos_ref[i, NSA_CMP_STRIDE + t:NSA_CMP_STRIDE + t + 1, :]).astype(BF16))
            kw1 = NSA_CMP_STRIDE * HEAD_DIM
            a = _dot(jnp.concatenate(lo, axis=1), w1_ref[i, 0:kw1, :])
            bm = _dot(jnp.concatenate(hi, axis=1), w1_ref[i, kw1:2 * kw1, :])
            hid = jax.nn.gelu(a + pltpu.roll(bm, n_chunk - 1, 0))
            dst[...] = _dot(hid.astype(BF16), w2_ref[i]).astype(BF16)

    t0 = pl.multiple_of(qi * tq, tq)
    rq = pl.ds(t0, tq)
    c, s1, s2 = tc_ref[rq, :], ts1_ref[rq, :], ts2_ref[rq, :]
    qf = nq_ref[...]
    qs = jnp.concatenate(
        [_rope128(qf[:, h * LANES:(h + 1) * LANES], c, s1, s2, half) * scale for h in range(heads)],
        axis=0).astype(BF16)

    lane = lax.broadcasted_iota(jnp.int32, (tq, LANES), 1)
    tpos = t0 + lax.broadcasted_iota(jnp.int32, (tq, LANES), 0)

    s = _dot_nt(qs, kc_sc[...]).reshape(heads, tq, LANES)
    cmask = (lane * NSA_CMP_STRIDE + (NSA_CMP_LEN - 1) <= tpos)[None]
    s = jnp.where(cmask, s, NEG_INF)
    m = jnp.max(s, axis=-1, keepdims=True)
    m = jnp.where(m == NEG_INF, 0.0, m)
    e = jnp.exp(s - m)
    p_cmp = e / jnp.maximum(jnp.sum(e, axis=-1, keepdims=True), 1e-30)
    o_cmp = _dot(p_cmp.reshape(heads * tq, LANES).astype(BF16), vc_sc[...])

    psum = p_cmp[0] + p_cmp[1] + p_cmp[2] + p_cmp[3]
    p_hi, p_lo = _split_bf16(psum)
    imp = _dot(p_hi, ov_ref[...]) + _dot(p_lo, ov_ref[...])
    cur = tpos >> 6
    n_sel = seq // NSA_SEL_LEN
    eligible = lane <= cur
    forced = (lane == 0) | (lane == cur) | (lane == cur - 1)
    score = jnp.where(eligible, jnp.where(forced, NSA_FORCE_SCORE, imp), NEG_INF)
    rank = _rank_before(score, range(n_sel), lane)
    selm = jnp.where(eligible & (rank < NSA_SEL_TOPK), 1.0, 0.0).astype(BF16)

    rows = t0 + lax.broadcasted_iota(jnp.int32, (tq, tq), 0)
    cols0 = lax.broadcasted_iota(jnp.int32, (tq, tq), 1)
    blk_row = lax.broadcasted_iota(jnp.int32, (LANES, tq), 0)
    blk_col = lax.broadcasted_iota(jnp.int32, (LANES, tq), 1)

    def branch(k_sc, v_sc, lo, mask_fn):
        def body(kb, carry):
            r = pl.ds(pl.multiple_of(kb * tq, tq), tq)
            s = _dot_nt(qs, k_sc[r, :]).reshape(heads, tq, tq)
            mk = mask_fn(kb)[None]
            m, l, acc = carry
            s = jnp.where(mk, s, NEG_INF)
            m_new = jnp.maximum(m, jnp.max(s, axis=-1, keepdims=True))
            m_safe = jnp.where(m_new == NEG_INF, 0.0, m_new)
            alpha = jnp.exp(m - m_safe)
            p = jnp.exp(s - m_safe)
            l_new = alpha * l + jnp.sum(p, axis=-1, keepdims=True)
            pv = _dot(p.reshape(heads * tq, tq).astype(BF16), v_sc[r, :]).reshape(heads, tq, HEAD_DIM)
            return m_new, l_new, alpha * acc + pv

        init = (jnp.full((heads, tq, 1), NEG_INF, F32), jnp.zeros((heads, tq, 1), F32),
                jnp.zeros((heads, tq, HEAD_DIM), F32))
        _, l, acc = lax.fori_loop(lo, qi + 1, body, init)
        return acc / jnp.maximum(l, 1e-30)

    def sel_mask(kb):
        expand = jnp.where(((kb * tq + blk_col) >> 6) == blk_row, 1.0, 0.0).astype(BF16)
        return (_dot(selm, expand) > 0.5) & (kb * tq + cols0 <= rows)

    def win_mask(kb):
        diff = rows - (kb * tq + cols0)
        return (diff >= 0) & (diff < NSA_WINDOW)

    o_sel = branch(ks_sc, vs_sc, 0, sel_mask)
    o_win = branch(kw_sc, vw_sc, jnp.maximum(qi - NSA_WINDOW // tq, 0), win_mask)
    o_cmp = o_cmp.reshape(heads, tq, HEAD_DIM)

    g = jax.nn.sigmoid(ng_ref[...])
    for h in range(heads):
        o = (g[:, 3 * h:3 * h + 1] * o_cmp[h] + g[:, 3 * h + 1:3 * h + 2] * o_sel[h]
             + g[:, 3 * h + 2:3 * h + 3] * o_win[h])
        o_ref[:, h * LANES:(h + 1) * LANES] = o.astype(o_ref.dtype)


def _nsa(proj, w1, w2, pos, tabs, batch, seq, tq=256):
    nt = seq // tq
    n_cmp = (seq - NSA_CMP_LEN) // NSA_CMP_STRIDE + 1
    n_sel = seq // NSA_SEL_LEN
    starts = np.arange(LANES) * NSA_CMP_STRIDE
    sel_start = np.arange(LANES) * NSA_SEL_LEN
    overlap = ((starts[:, None] < sel_start[None, :] + NSA_SEL_LEN)
               & (starts[:, None] + NSA_CMP_LEN > sel_start[None, :])
               & (np.arange(LANES)[:, None] < n_cmp) & (np.arange(LANES)[None, :] < n_sel))
    ov = jnp.asarray(overlap.astype(np.float32), BF16)
    seq_col = lambda name: pl.BlockSpec((seq, LANES), lambda b, i: (b, COL128[name]))
    full = lambda a: pl.BlockSpec(a.shape, lambda b, i: (0,) * a.ndim)
    return pl.pallas_call(
        functools.partial(_nsa_kernel, tq=tq, seq=seq, scale=HEAD_DIM ** -0.5),
        grid=(batch, nt),
        in_specs=[pl.BlockSpec((tq, 512), lambda b, i: (b * nt + i, COL512["nq"])),
                  seq_col("nkc"), seq_col("nvc"), seq_col("nks"), seq_col("nvs"), seq_col("nkw"), seq_col("nvw"),
                  pl.BlockSpec((tq, LANES), lambda b, i: (b * nt + i, COL128["ng"])),
                  full(w1), full(w2), full(pos), full(ov), full(tabs[0]), full(tabs[1]), full(tabs[2])],
        out_specs=pl.BlockSpec((tq, NSA_HEADS * HEAD_DIM), lambda b, i: (b * nt + i, 0)),
        out_shape=jax.ShapeDtypeStruct((batch * seq, NSA_HEADS * HEAD_DIM), BF16),
        scratch_shapes=[pltpu.VMEM((seq, LANES), F32),
                        pltpu.VMEM((LANES, LANES), BF16), pltpu.VMEM((LANES, LANES), BF16),
                        pltpu.VMEM((seq, LANES), BF16), pltpu.VMEM((seq, LANES), BF16),
                        pltpu.VMEM((seq, LANES), BF16), pltpu.VMEM((seq, LANES), BF16)],
        compiler_params=_cparams("parallel", "arbitrary"),
        name="nsa",
    )(proj, proj, proj, proj, proj, proj, proj, proj, w1, w2, pos, ov, *tabs)


def _moba_kernel(mq_ref, mk_ref, mv_ref, tc_ref, ts1_ref, ts2_ref, o_ref, k_sc, v_sc, km_sc, *, tq, seq, scale):
    qi = pl.program_id(1)
    half = PARTIAL_ROT // 2
    n_blk = seq // MOBA_BLOCK

    @pl.when(qi == 0)
    def _prep():
        c, s1, s2 = tc_ref[...], ts1_ref[...], ts2_ref[...]
        km_sc[...] = jnp.zeros_like(km_sc)
        for h in range(MOBA_HEADS):
            hs = slice(h * LANES, (h + 1) * LANES)
            kh = _rope128(mk_ref[:, hs], c, s1, s2, half)
            k_sc[:, hs] = kh.astype(BF16)
            for j in range(n_blk):
                km_sc[h * n_blk + j:h * n_blk + j + 1, :] = jnp.mean(
                    kh[j * MOBA_BLOCK:(j + 1) * MOBA_BLOCK], axis=0, keepdims=True)
        v_sc[...] = mv_ref[...].astype(BF16)

    t0 = pl.multiple_of(qi * tq, tq)
    rq = pl.ds(t0, tq)
    c, s1, s2 = tc_ref[rq, :], ts1_ref[rq, :], ts2_ref[rq, :]
    lane = lax.broadcasted_iota(jnp.int32, (tq, LANES), 1)
    causal = lax.broadcasted_iota(jnp.int32, (tq, tq), 1) <= lax.broadcasted_iota(jnp.int32, (tq, tq), 0)
    km_hi, km_lo = _split_bf16(km_sc[...])

    for h in range(MOBA_HEADS):
        hs = slice(h * LANES, (h + 1) * LANES)
        qf = _rope128(mq_ref[:, hs], c, s1, s2, half) * scale
        q, q_lo = _split_bf16(qf)
        gate = _dot_nt(q, km_hi) + (_dot_nt(q, km_lo) + _dot_nt(q_lo, km_hi))
        jl = lane - h * n_blk
        eligible = (jl >= 0) & (jl < qi)
        score = jnp.where(eligible, gate, NEG_INF)
        rank = _rank_before(score, range(h * n_blk, (h + 1) * n_blk), lane)
        selm = jnp.where(eligible & (rank < MOBA_TOPK), 1.0, 0.0)

        def body(kb, carry, hs=hs, selm=selm, q=q, h=h):
            r = pl.ds(pl.multiple_of(kb * tq, tq), tq)
            s = _dot_nt(q, k_sc[r, hs])
            picked = jnp.sum(jnp.where(lane == h * n_blk + kb, selm, 0.0), axis=-1, keepdims=True) > 0.5
            return _softmax_step(s, picked, *carry, v_sc[r, hs])

        init = (jnp.full((tq, 1), NEG_INF, F32), jnp.zeros((tq, 1), F32), jnp.zeros((tq, HEAD_DIM), F32))
        carry = lax.fori_loop(0, qi, body, init)
        s_own = _dot_nt(q, k_sc[rq, hs])
        _, l, acc = _softmax_step(s_own, causal, *carry, v_sc[rq, hs])
        o_ref[:, hs] = (acc / jnp.maximum(l, 1e-30)).astype(o_ref.dtype)


def _moba(proj, tabs, batch, seq):
    tq = MOBA_BLOCK
    nt = seq // tq
    width = MOBA_HEADS * HEAD_DIM
    full = lambda a: pl.BlockSpec(a.shape, lambda b, i: (0,) * a.ndim)
    return pl.pallas_call(
        functools.partial(_moba_kernel, tq=tq, seq=seq, scale=HEAD_DIM ** -0.5),
        grid=(batch, nt),
        in_specs=[pl.BlockSpec((tq, width), lambda b, i: (b * nt + i, COL512["mq"])),
                  pl.BlockSpec((seq, width), lambda b, i: (b, COL512["mk"])),
                  pl.BlockSpec((seq, width), lambda b, i: (b, COL512["mv"])),
                  full(tabs[0]), full(tabs[1]), full(tabs[2])],
        out_specs=pl.BlockSpec((tq, width), lambda b, i: (b * nt + i, 0)),
        out_shape=jax.ShapeDtypeStruct((batch * seq, width), BF16),
        scratch_shapes=[pltpu.VMEM((seq, width), BF16), pltpu.VMEM((seq, width), BF16),
                        pltpu.VMEM((LANES, LANES), F32)],
        compiler_params=_cparams("parallel", "arbitrary"),
        name="moba",
    )(proj, proj, proj, *tabs)


def _xattn_kernel(q_ref, kv_ref, o_ref):
    d = q_ref.shape[1]
    hd = d // MEM_HEADS
    for h in range(MEM_HEADS):
        hs = slice(h * hd, (h + 1) * hd)
        s = _dot_nt(q_ref[:, hs], kv_ref[0, :, hs])
        e = jnp.exp(s - jnp.max(s, axis=-1, keepdims=True))
        p = e / jnp.sum(e, axis=-1, keepdims=True)
        o_ref[:, hs] = _dot(p.astype(BF16), kv_ref[0, :, d + h * hd:d + (h + 1) * hd]).astype(o_ref.dtype)


def _xattn(q, kv, batch, seq, tq=512):
    d = q.shape[1]
    nt = seq // tq
    m_len = kv.shape[0] // batch
    return pl.pallas_call(
        _xattn_kernel,
        grid=(batch, nt),
        in_specs=[pl.BlockSpec((tq, d), lambda b, i: (b * nt + i, 0)),
                  pl.BlockSpec((1, m_len, 2 * d), lambda b, i: (b, 0, 0))],
        out_specs=pl.BlockSpec((tq, d), lambda b, i: (b * nt + i, 0)),
        out_shape=jax.ShapeDtypeStruct((batch * seq, d), BF16),
        compiler_params=_cparams("parallel", "parallel"),
        name="xattn",
    )(q, kv.reshape(batch, m_len, 2 * d))


IN_SPLIT_NAMES = ("c_q", "c_kv", "k_rope", "nq", "nkc", "nvc", "nks", "nvs", "nkw", "nvw", "ng", "mq", "mk", "mv")
IN_SPLIT_SIZES = (512, 512, 64, 512, 128, 128, 128, 128, 128, 128, 12, 512, 512, 512)


def _pack_w_in_kernel(w_ref, o_ref):
    rows = w_ref.shape[0]
    off = 0
    for name, sz in zip(IN_SPLIT_NAMES, IN_SPLIT_SIZES):
        width = 512 if name in COL512 else LANES
        dst = COL512[name] * 512 if name in COL512 else COL128[name] * LANES
        x = w_ref[:, off:off + sz].astype(BF16)
        if sz < width:
            x = jnp.concatenate([x, jnp.zeros((rows, width - sz), BF16)], axis=1)
        o_ref[0, :, dst:dst + width] = x
        off += sz


def _pack_w_in(w, tr=256):
    layers, d, width = w.shape
    return pl.pallas_call(
        _pack_w_in_kernel,
        grid=(layers, d // tr),
        in_specs=[pl.BlockSpec((None, tr, width), lambda l, i: (l, i, 0))],
        out_specs=pl.BlockSpec((1, tr, PROJ_WIDTH), lambda l, i: (l, i, 0)),
        out_shape=jax.ShapeDtypeStruct((layers, d, PROJ_WIDTH), BF16),
        compiler_params=_cparams("parallel", "parallel"),
        name="pack_w_in",
    )(w)


def _pack_w_uq(w):
    r = w.shape[0]
    w = w.reshape(r, MLA_HEADS, MLA_NOPE + MLA_ROPE)
    w = jnp.pad(w, ((0, 0), (0, 0), (0, MLA_SLOT - MLA_NOPE - MLA_ROPE)))
    return w.reshape(r, MLA_HEADS * MLA_SLOT).astype(BF16)


def _pack_w_ukv(w):
    r = w.shape[0]
    w = w.reshape(r, MLA_HEADS, 2, HEAD_DIM).transpose(0, 2, 1, 3)
    return w.reshape(r, 2 * MLA_HEADS * HEAD_DIM).astype(BF16)


def kernel(x, mem, ln_in_g, ln_in_b, w_in, mla_q_norm, mla_kv_norm, mla_w_uq, mla_w_ukv, nsa_cmp_w1, nsa_cmp_w2, nsa_cmp_pos, w_out, ln1_g, ln1_b, mem_wq, mem_wkv, mem_wo, ln2_g, ln2_b, mlp_w1, mlp_w2, ln3_g, ln3_b):
    batch, seq, d = x.shape
    n = batch * seq
    mla_tabs = _rope_tables(seq, MLA_ROPE)
    rot_tabs = _rope_tables(seq, PARTIAL_ROT)
    mem2 = mem.reshape(batch * mem.shape[1], d)
    w_in_packed = _pack_w_in(w_in)

    h = _layer_norm(x.reshape(n, d), ln_in_g, ln_in_b)
    for l in range(DEPTH):
        proj = _matmul(h, w_in_packed[l], F32, name="in_proj")
        q, k, v = _mla_up(proj, mla_q_norm[l], mla_kv_norm[l], _pack_w_uq(mla_w_uq[l]), _pack_w_ukv(mla_w_ukv[l]),
                          mla_tabs, batch, seq)
        o_a = _mla_attn(q, k, v).reshape(n, MLA_HEADS * HEAD_DIM)
        o_b = _nsa(proj, nsa_cmp_w1[l].astype(BF16), nsa_cmp_w2[l].astype(BF16), nsa_cmp_pos[l], rot_tabs, batch, seq)
        o_c = _moba(proj, rot_tabs, batch, seq)
        h = _out_ln([o_a, o_b, o_c], w_out[l].astype(BF16), h, ln1_g[l], ln1_b[l], name="mix_out_ln")

        wq_scaled = (mem_wq[l] * (d // MEM_HEADS) ** -0.5).astype(BF16)
        xq = _matmul(h, wq_scaled, BF16, name="mem_q")
        xkv = _matmul(mem2, mem_wkv[l].astype(BF16), BF16, name="mem_kv")
        ctx = _xattn(xq, xkv, batch, seq)
        h = _out_ln([ctx], mem_wo[l].astype(BF16), h, ln2_g[l], ln2_b[l], name="mem_out_ln")

        h = _mlp_ln(h, mlp_w1[l].astype(BF16), mlp_w2[l].astype(BF16), ln3_g[l], ln3_b[l])
    return h.reshape(batch, seq, d)
```

```python
import functools

import numpy as np
import jax
import jax.numpy as jnp
from jax import lax
from jax.experimental import pallas as pl
from jax.experimental.pallas import tpu as pltpu

F32 = jnp.float32
BF16 = jnp.bfloat16
NEG_INF = float("-inf")

D_MODEL = 2048
DEPTH = 2
HEAD_DIM = 128
MLA_HEADS = 8
NSA_HEADS = 4
MOBA_HEADS = 4
ROPE_THETA = 500000.0
PARTIAL_ROT = HEAD_DIM // 4
MLA_Q_RANK = 512
MLA_KV_RANK = 512
MLA_NOPE = 128
MLA_ROPE = 64
MLA_SLOT = 256
NSA_CMP_LEN = 32
NSA_CMP_STRIDE = 16
NSA_SEL_LEN = 64
NSA_SEL_TOPK = 16
NSA_WINDOW = 512
NSA_FORCE_SCORE = 1.0e4
MOBA_BLOCK = 256
MOBA_TOPK = 3
MEM_HEADS = 4
D_FF = 4 * D_MODEL
DEEPNORM_ALPHA = (2 * DEPTH) ** 0.25
LANES = 128

PROJ_WIDTH = 4096
COL512 = dict(c_q=0, c_kv=1, nq=2, mq=3, mk=4, mv=5)
COL128 = dict(k_rope=24, nkc=25, nvc=26, nks=27, nvs=28, nkw=29, nvw=30, ng=31)

VMEM_LIMIT = 56 * 1024 * 1024


def _cparams(*sem):
    return pltpu.CompilerParams(dimension_semantics=sem, vmem_limit_bytes=VMEM_LIMIT)


def _dot(a, b):
    return jnp.dot(a, b, preferred_element_type=F32)


def _dot_nt(a, b):
    return lax.dot_general(a, b, (((1,), (1,)), ((), ())), preferred_element_type=F32)


def _split_bf16(x):
    hi = x.astype(BF16)
    return hi, (x - hi.astype(F32)).astype(BF16)


def _ln_rows(x, g, b, eps=1e-5):
    mu = jnp.mean(x, axis=-1, keepdims=True)
    xc = x - mu
    var = jnp.mean(xc * xc, axis=-1, keepdims=True)
    return xc * lax.rsqrt(var + eps) * g + b


def _rms_rows(x, g, eps=1e-6):
    return x * lax.rsqrt(jnp.mean(x * x, axis=-1, keepdims=True) + eps) * g


def _rope128(x, c, s1, s2, half):
    return x * c + pltpu.roll(x, LANES - half, 1) * s1 + pltpu.roll(x, half, 1) * s2


def _rope_tables(n_pos, dim):
    half = dim // 2
    inv = ROPE_THETA ** (-jnp.arange(0, dim, 2, dtype=F32) / dim)
    ang = jnp.arange(n_pos, dtype=F32)[:, None] * inv[None, :]
    cos, sin = jnp.cos(ang), jnp.sin(ang)
    ones = jnp.ones((n_pos, LANES - dim), F32)
    z = lambda w: jnp.zeros((n_pos, w), F32)
    c = jnp.concatenate([cos, cos, ones], axis=1)
    s1 = jnp.concatenate([-sin, z(LANES - half)], axis=1)
    s2 = jnp.concatenate([z(half), sin, z(LANES - dim)], axis=1)
    return c, s1, s2


MASKED = -1.0e30


def _rank_rows(score, n_cand):
    row = lax.broadcasted_iota(jnp.int32, score.shape, 0)
    rank = jnp.zeros(score.shape, F32)
    for jp in range(n_cand):
        cand = score[jp:jp + 1, :]
        ahead = (cand > score) | ((cand == score) & (row > jp))
        rank = rank + jnp.where(ahead, 1.0, 0.0)
    return rank


def _rows_to_lanes(x_t, tq):
    eye = jnp.where(lax.broadcasted_iota(jnp.int32, (tq, tq), 0) == lax.broadcasted_iota(jnp.int32, (tq, tq), 1),
                    1.0, 0.0).astype(BF16)
    return _dot_nt(eye, x_t.astype(BF16))


def _flash_update(s, m, acc, v_aug):
    m_new = jnp.maximum(m, jnp.max(s, axis=-1, keepdims=True))
    p = jnp.exp(s - m_new).astype(BF16)
    return m_new, jnp.exp(m - m_new) * acc + _dot(p, v_aug)


def _ln_kernel(x_ref, g_ref, b_ref, o_ref):
    o_ref[...] = _ln_rows(x_ref[...], g_ref[...], b_ref[...])


def _layer_norm(x, g, b, tm=512):
    n, d = x.shape
    return pl.pallas_call(
        _ln_kernel,
        grid=(n // tm,),
        in_specs=[pl.BlockSpec((tm, d), lambda i: (i, 0)),
                  pl.BlockSpec((1, d), lambda i: (0, 0)),
                  pl.BlockSpec((1, d), lambda i: (0, 0))],
        out_specs=pl.BlockSpec((tm, d), lambda i: (i, 0)),
        out_shape=jax.ShapeDtypeStruct((n, d), F32),
        compiler_params=_cparams("parallel"),
        name="ln_in",
    )(x, g.reshape(1, d), b.reshape(1, d))


def _mm_kernel(a_ref, w_ref, o_ref, abf_ref):
    @pl.when(pl.program_id(1) == 0)
    def _():
        abf_ref[...] = a_ref[...].astype(BF16)

    o_ref[...] = _dot(abf_ref[...], w_ref[...]).astype(o_ref.dtype)


def _matmul(a, w, out_dtype, tm=1024, tn=512, name="mm"):
    m, k = a.shape
    n = w.shape[1]
    tm = min(tm, m)
    return pl.pallas_call(
        _mm_kernel,
        grid=(m // tm, n // tn),
        in_specs=[pl.BlockSpec((tm, k), lambda i, j: (i, 0)),
                  pl.BlockSpec((k, tn), lambda i, j: (0, j))],
        out_specs=pl.BlockSpec((tm, tn), lambda i, j: (i, j)),
        out_shape=jax.ShapeDtypeStruct((m, n), out_dtype),
        scratch_shapes=[pltpu.VMEM((tm, k), BF16)],
        compiler_params=_cparams("parallel", "arbitrary"),
        name=name,
    )(a, w)


def _out_ln_kernel(*refs, widths):
    n_a = len(widths)
    a_refs = refs[:n_a]
    w_ref, h_ref, g_ref, b_ref, o_ref = refs[n_a:]
    acc = DEEPNORM_ALPHA * h_ref[...]
    off = 0
    for a_ref, wd in zip(a_refs, widths):
        acc = acc + _dot(a_ref[...], w_ref[off:off + wd, :])
        off += wd
    o_ref[...] = _ln_rows(acc, g_ref[...], b_ref[...])


def _out_ln(a_list, w, h, g, b, tm=512, name="out_ln"):
    n, d = h.shape
    widths = tuple(a.shape[1] for a in a_list)
    k = sum(widths)
    in_specs = [pl.BlockSpec((tm, wd), lambda i: (i, 0)) for wd in widths]
    in_specs += [pl.BlockSpec((k, d), lambda i: (0, 0)),
                 pl.BlockSpec((tm, d), lambda i: (i, 0)),
                 pl.BlockSpec((1, d), lambda i: (0, 0)),
                 pl.BlockSpec((1, d), lambda i: (0, 0))]
    return pl.pallas_call(
        functools.partial(_out_ln_kernel, widths=widths),
        grid=(n // tm,),
        in_specs=in_specs,
        out_specs=pl.BlockSpec((tm, d), lambda i: (i, 0)),
        out_shape=jax.ShapeDtypeStruct((n, d), F32),
        compiler_params=_cparams("parallel"),
        name=name,
    )(*a_list, w, h, g.reshape(1, d), b.reshape(1, d))


def _mlp_kernel(h_ref, w1_ref, w2_ref, g_ref, b_ref, o_ref, hbf_ref, acc_ref):
    f = pl.program_id(1)

    @pl.when(f == 0)
    def _():
        hbf_ref[...] = h_ref[...].astype(BF16)
        acc_ref[...] = jnp.zeros_like(acc_ref)

    u = jnp.maximum(_dot(hbf_ref[...], w1_ref[...]), 0.0)
    acc_ref[...] += _dot((u * u).astype(BF16), w2_ref[...])

    @pl.when(f == pl.num_programs(1) - 1)
    def _():
        y = DEEPNORM_ALPHA * h_ref[...] + acc_ref[...]
        o_ref[...] = _ln_rows(y, g_ref[...], b_ref[...])


def _mlp_ln(h, w1, w2, g, b, tm=512, tf=512):
    n, d = h.shape
    dff = w1.shape[1]
    return pl.pallas_call(
        _mlp_kernel,
        grid=(n // tm, dff // tf),
        in_specs=[pl.BlockSpec((tm, d), lambda i, f: (i, 0)),
                  pl.BlockSpec((d, tf), lambda i, f: (0, f)),
                  pl.BlockSpec((tf, d), lambda i, f: (f, 0)),
                  pl.BlockSpec((1, d), lambda i, f: (0, 0)),
                  pl.BlockSpec((1, d), lambda i, f: (0, 0))],
        out_specs=pl.BlockSpec((tm, d), lambda i, f: (i, 0)),
        out_shape=jax.ShapeDtypeStruct((n, d), F32),
        scratch_shapes=[pltpu.VMEM((tm, d), BF16), pltpu.VMEM((tm, d), F32)],
        compiler_params=_cparams("parallel", "arbitrary"),
        name="mlp_ln",
    )(h, w1, w2, g.reshape(1, d), b.reshape(1, d))


def _mla_up_kernel(cq_ref, ckv_ref, kr_ref, gq_ref, gkv_ref, wq_ref, wkv_ref, tc_ref, ts1_ref, ts2_ref,
                   q_ref, k_ref, v_ref, *, scale):
    half = MLA_ROPE // 2
    c, s1, s2 = tc_ref[...], ts1_ref[...], ts2_ref[...]
    nq = _rms_rows(cq_ref[...], gq_ref[...]).astype(BF16)
    nkv = _rms_rows(ckv_ref[...], gkv_ref[...]).astype(BF16)
    qf = _dot(nq, wq_ref[...])
    kvf = _dot(nkv, wkv_ref[...])
    kr = _rope128(kr_ref[...], c, s1, s2, half).astype(BF16)
    for h in range(MLA_HEADS):
        o = h * MLA_SLOT
        q_ref[0, h, :, 0:LANES] = (qf[:, o:o + LANES] * scale).astype(BF16)
        q_ref[0, h, :, LANES:] = (_rope128(qf[:, o + LANES:o + MLA_SLOT], c, s1, s2, half) * scale).astype(BF16)
        k_ref[0, h, :, 0:LANES] = kvf[:, h * LANES:(h + 1) * LANES].astype(BF16)
        k_ref[0, h, :, LANES:] = kr
        v_ref[0, h, :, 0:LANES] = kvf[:, (MLA_HEADS + h) * LANES:(MLA_HEADS + h + 1) * LANES].astype(BF16)
        v_ref[0, h, :, LANES:] = jnp.ones((kr.shape[0], LANES), BF16)


def _mla_up(proj, gq, gkv, wq, wkv, tabs, batch, seq, tm=512):
    nt = seq // tm
    row = lambda b, i: b * nt + i
    tab_spec = pl.BlockSpec((tm, LANES), lambda b, i: (i, 0))
    hm = lambda w: pl.BlockSpec((1, MLA_HEADS, tm, w), lambda b, i: (b, 0, i, 0))
    return pl.pallas_call(
        functools.partial(_mla_up_kernel, scale=(MLA_NOPE + MLA_ROPE) ** -0.5),
        grid=(batch, nt),
        in_specs=[pl.BlockSpec((tm, 512), lambda b, i: (row(b, i), COL512["c_q"])),
                  pl.BlockSpec((tm, 512), lambda b, i: (row(b, i), COL512["c_kv"])),
                  pl.BlockSpec((tm, LANES), lambda b, i: (row(b, i), COL128["k_rope"])),
                  pl.BlockSpec((1, MLA_Q_RANK), lambda b, i: (0, 0)),
                  pl.BlockSpec((1, MLA_KV_RANK), lambda b, i: (0, 0)),
                  pl.BlockSpec(wq.shape, lambda b, i: (0, 0)),
                  pl.BlockSpec(wkv.shape, lambda b, i: (0, 0)),
                  tab_spec, tab_spec, tab_spec],
        out_specs=[hm(MLA_SLOT), hm(MLA_SLOT), hm(2 * HEAD_DIM)],
        out_shape=[jax.ShapeDtypeStruct((batch, MLA_HEADS, seq, MLA_SLOT), BF16),
                   jax.ShapeDtypeStruct((batch, MLA_HEADS, seq, MLA_SLOT), BF16),
                   jax.ShapeDtypeStruct((batch, MLA_HEADS, seq, 2 * HEAD_DIM), BF16)],
        compiler_params=_cparams("parallel", "parallel"),
        name="mla_up",
    )(proj, proj, proj, gq.reshape(1, -1), gkv.reshape(1, -1), wq, wkv, *tabs)


def _mla_attn_kernel(q_ref, k_ref, v_ref, o_ref, *, tq):
    qi = pl.program_id(1)
    heads = q_ref.shape[1]
    causal = lax.broadcasted_iota(jnp.int32, (tq, tq), 1) <= lax.broadcasted_iota(jnp.int32, (tq, tq), 0)

    def step(r, carry, mask):
        out = []
        for h in range(heads):
            s = _dot_nt(q_ref[0, h], k_ref[0, h, r, :])
            if mask is not None:
                s = jnp.where(mask, s, MASKED)
            out.append(_flash_update(s, *carry[h], v_ref[0, h, r, :]))
        return tuple(out)

    def body(kb, carry):
        return step(pl.ds(pl.multiple_of(kb * tq, tq), tq), carry, None)

    init = tuple((jnp.full((tq, 1), NEG_INF, F32), jnp.zeros((tq, 2 * HEAD_DIM), F32)) for _ in range(heads))
    carry = lax.fori_loop(0, qi, body, init)
    carry = step(pl.ds(pl.multiple_of(qi * tq, tq), tq), carry, causal)
    for h in range(heads):
        acc = carry[h][1]
        o_ref[0, :, h * HEAD_DIM:(h + 1) * HEAD_DIM] = (acc[:, :HEAD_DIM] / acc[:, HEAD_DIM:]).astype(o_ref.dtype)


def _mla_attn(q, k, v, tq=256):
    batch, heads, seq, _ = q.shape
    return pl.pallas_call(
        functools.partial(_mla_attn_kernel, tq=tq),
        grid=(batch, seq // tq),
        in_specs=[pl.BlockSpec((1, heads, tq, MLA_SLOT), lambda b, i: (b, 0, i, 0)),
                  pl.BlockSpec((1, heads, seq, MLA_SLOT), lambda b, i: (b, 0, 0, 0)),
                  pl.BlockSpec((1, heads, seq, 2 * HEAD_DIM), lambda b, i: (b, 0, 0, 0))],
        out_specs=pl.BlockSpec((1, tq, heads * HEAD_DIM), lambda b, i: (b, i, 0)),
        out_shape=jax.ShapeDtypeStruct((batch, seq, heads * HEAD_DIM), BF16),
        compiler_params=_cparams("parallel", "arbitrary"),
        name="mla_attn",
    )(q, k, v)


def _nsa_kernel(nq_ref, kc_ref, vc_ref, ks_ref, vs_ref, kw_ref, vw_ref, ng_ref,
                w1_ref, w2_ref, pos_ref, ovt_ref, tc_ref, ts1_ref, ts2_ref,
                o_ref,
                tmp_sc, kc_sc, vc_sc, ks_sc, vs_sc, kw_sc, vw_sc, *, tq, seq, scale):
    qi = pl.program_id(1)
    half = PARTIAL_ROT // 2
    heads = NSA_HEADS
    n_chunk = seq // NSA_CMP_STRIDE
    n_sel = seq // NSA_SEL_LEN

    @pl.when(qi == 0)
    def _prep():
        c, s1, s2 = tc_ref[...], ts1_ref[...], ts2_ref[...]
        ones = jnp.ones((seq, LANES), BF16)
        blk = lax.broadcasted_iota(jnp.int32, (seq, LANES), 0) >> 6
        ks_sc[:, 0:LANES] = _rope128(ks_ref[...], c, s1, s2, half).astype(BF16)
        ks_sc[:, LANES:] = jnp.where(blk == lax.broadcasted_iota(jnp.int32, (seq, LANES), 1), 1.0, 0.0).astype(BF16)
        kw_sc[...] = _rope128(kw_ref[...], c, s1, s2, half).astype(BF16)
        vs_sc[:, 0:LANES] = vs_ref[...].astype(BF16)
        vs_sc[:, LANES:] = ones
        vw_sc[:, 0:LANES] = vw_ref[...].astype(BF16)
        vw_sc[:, LANES:] = ones
        for i, (src, dst) in enumerate(((kc_ref, kc_sc), (vc_ref, vc_sc))):
            tmp_sc[...] = _rope128(src[...], c, s1, s2, half) if i == 0 else src[...]
            lo, hi = [], []
            for t in range(NSA_CMP_STRIDE):
                x = tmp_sc[pl.ds(t, n_chunk, stride=NSA_CMP_STRIDE), :]
                lo.append((x + pos_ref[i, t:t + 1, :]).astype(BF16))
                hi.append((x + pos_ref[i, NSA_CMP_STRIDE + t:NSA_CMP_STRIDE + t + 1, :]).astype(BF16))
            kw1 = NSA_CMP_STRIDE * HEAD_DIM
            a = _dot(jnp.concatenate(lo, axis=1), w1_ref[i, 0:kw1, :])
            bm = _dot(jnp.concatenate(hi, axis=1), w1_ref[i, kw1:2 * kw1, :])
            hid = jax.nn.gelu(a + pltpu.roll(bm, n_chunk - 1, 0))
            dst[...] = _dot(hid.astype(BF16), w2_ref[i]).astype(BF16)

    t0 = pl.multiple_of(qi * tq, tq)
    rq = pl.ds(t0, tq)
    c, s1, s2 = tc_ref[rq, :], ts1_ref[rq, :], ts2_ref[rq, :]
    qf = nq_ref[...]
    qs = jnp.concatenate(
        [_rope128(qf[:, h * LANES:(h + 1) * LANES], c, s1, s2, half) * scale for h in range(heads)],
        axis=0)
    qs_bf = qs.astype(BF16)

    lane = lax.broadcasted_iota(jnp.int32, (tq, LANES), 1)
    tpos = t0 + lax.broadcasted_iota(jnp.int32, (tq, LANES), 0)

    s = _dot_nt(qs_bf, kc_sc[...]).reshape(heads, tq, LANES)
    cmask = (lane * NSA_CMP_STRIDE + (NSA_CMP_LEN - 1) <= tpos)[None]
    s = jnp.where(cmask, s, NEG_INF)
    m = jnp.max(s, axis=-1, keepdims=True)
    m = jnp.where(m == NEG_INF, 0.0, m)
    e = jnp.exp(s - m)
    p_cmp = e / jnp.maximum(jnp.sum(e, axis=-1, keepdims=True), 1e-30)
    o_cmp = _dot(p_cmp.reshape(heads * tq, LANES).astype(BF16), vc_sc[...]).reshape(heads, tq, HEAD_DIM)

    p_hi, p_lo = _split_bf16(p_cmp[0] + p_cmp[1] + p_cmp[2] + p_cmp[3])
    imp_t = (_dot_nt(ovt_ref[...], p_hi) + _dot_nt(ovt_ref[...], p_lo))[0:n_sel]
    blk_t = lax.broadcasted_iota(jnp.int32, (n_sel, tq), 0)
    cur_t = (t0 + lax.broadcasted_iota(jnp.int32, (n_sel, tq), 1)) >> 6
    eligible = blk_t <= cur_t
    forced = (blk_t == 0) | (blk_t == cur_t) | (blk_t == cur_t - 1)
    score = jnp.where(eligible, jnp.where(forced, NSA_FORCE_SCORE, imp_t), NEG_INF)
    rank = _rank_rows(score, n_sel)
    bias_t = jnp.where(eligible & (rank < NSA_SEL_TOPK), 0.0, MASKED)
    bias_t = jnp.concatenate([bias_t, jnp.zeros((LANES - n_sel, tq), F32)], axis=0)
    bias = _rows_to_lanes(bias_t, tq)
    q_aug = jnp.concatenate([qs, jnp.concatenate([bias] * heads, axis=0)], axis=1).astype(BF16)

    local_r = lax.broadcasted_iota(jnp.int32, (tq, tq), 0)
    local_c = lax.broadcasted_iota(jnp.int32, (tq, tq), 1)
    causal = (local_c <= local_r)[None]

    def masked3(s, mask):
        return jnp.where(mask, s.reshape(heads, tq, tq), MASKED).reshape(heads * tq, tq)

    init = (jnp.full((heads * tq, 1), NEG_INF, F32), jnp.zeros((heads * tq, 2 * HEAD_DIM), F32))

    def sel_body(kb, carry):
        r = pl.ds(pl.multiple_of(kb * tq, tq), tq)
        return _flash_update(_dot_nt(q_aug, ks_sc[r, :]), *carry, vs_sc[r, :])

    carry = lax.fori_loop(0, qi, sel_body, init)
    _, acc = _flash_update(masked3(_dot_nt(q_aug, ks_sc[rq, :]), causal), *carry, vs_sc[rq, :])
    o_sel = (acc[:, :HEAD_DIM] / acc[:, HEAD_DIM:]).reshape(heads, tq, HEAD_DIM)

    carry = init
    for back, mask in ((2, (local_c > local_r)[None]), (1, None), (0, causal)):
        kb = qi - back
        r = pl.ds(pl.multiple_of(jnp.maximum(kb, 0) * tq, tq), tq)
        s = _dot_nt(qs_bf, kw_sc[r, :])
        if back:
            valid = kb >= 0
            mask = valid if mask is None else mask & valid
        carry = _flash_update(masked3(s, mask), *carry, vw_sc[r, :])
    acc = carry[1]
    o_win = (acc[:, :HEAD_DIM] / acc[:, HEAD_DIM:]).reshape(heads, tq, HEAD_DIM)

    g = jax.nn.sigmoid(ng_ref[...])
    for h in range(heads):
        o = (g[:, 3 * h:3 * h + 1] * o_cmp[h] + g[:, 3 * h + 1:3 * h + 2] * o_sel[h]
             + g[:, 3 * h + 2:3 * h + 3] * o_win[h])
        o_ref[:, h * LANES:(h + 1) * LANES] = o.astype(o_ref.dtype)


def _nsa(proj, w1, w2, pos, tabs, batch, seq, tq=256):
    nt = seq // tq
    n_cmp = (seq - NSA_CMP_LEN) // NSA_CMP_STRIDE + 1
    n_sel = seq // NSA_SEL_LEN
    starts = np.arange(LANES) * NSA_CMP_STRIDE
    sel_start = np.arange(LANES) * NSA_SEL_LEN
    overlap = ((starts[:, None] < sel_start[None, :] + NSA_SEL_LEN)
               & (starts[:, None] + NSA_CMP_LEN > sel_start[None, :])
               & (np.arange(LANES)[:, None] < n_cmp) & (np.arange(LANES)[None, :] < n_sel))
    ovt = jnp.asarray(overlap.T.astype(np.float32), BF16)
    seq_col = lambda name: pl.BlockSpec((seq, LANES), lambda b, i: (b, COL128[name]))
    full = lambda a: pl.BlockSpec(a.shape, lambda b, i: (0,) * a.ndim)
    return pl.pallas_call(
        functools.partial(_nsa_kernel, tq=tq, seq=seq, scale=HEAD_DIM ** -0.5),
        grid=(batch, nt),
        in_specs=[pl.BlockSpec((tq, 512), lambda b, i: (b * nt + i, COL512["nq"])),
                  seq_col("nkc"), seq_col("nvc"), seq_col("nks"), seq_col("nvs"), seq_col("nkw"), seq_col("nvw"),
                  pl.BlockSpec((tq, LANES), lambda b, i: (b * nt + i, COL128["ng"])),
                  full(w1), full(w2), full(pos), full(ovt), full(tabs[0]), full(tabs[1]), full(tabs[2])],
        out_specs=pl.BlockSpec((tq, NSA_HEADS * HEAD_DIM), lambda b, i: (b * nt + i, 0)),
        out_shape=jax.ShapeDtypeStruct((batch * seq, NSA_HEADS * HEAD_DIM), BF16),
        scratch_shapes=[pltpu.VMEM((seq, LANES), F32),
                        pltpu.VMEM((LANES, LANES), BF16), pltpu.VMEM((LANES, LANES), BF16),
                        pltpu.VMEM((seq, 2 * LANES), BF16), pltpu.VMEM((seq, 2 * LANES), BF16),
                        pltpu.VMEM((seq, LANES), BF16), pltpu.VMEM((seq, 2 * LANES), BF16)],
        compiler_params=_cparams("parallel", "arbitrary"),
        name="nsa",
    )(proj, proj, proj, proj, proj, proj, proj, proj, w1, w2, pos, ovt, *tabs)


def _moba_kernel(mq_ref, mk_ref, mv_ref, tc_ref, ts1_ref, ts2_ref, o_ref, k_sc, v_sc, km_sc, *, tq, seq, scale):
    qi = pl.program_id(1)
    half = PARTIAL_ROT // 2
    heads = MOBA_HEADS
    n_blk = seq // MOBA_BLOCK
    slot = 2 * LANES

    @pl.when(qi == 0)
    def _prep():
        c, s1, s2 = tc_ref[...], ts1_ref[...], ts2_ref[...]
        km_sc[...] = jnp.zeros_like(km_sc)
        ones = jnp.ones((seq, LANES), BF16)
        blk = lax.broadcasted_iota(jnp.int32, (seq, LANES), 0) >> 8
        onehot = jnp.where(blk == lax.broadcasted_iota(jnp.int32, (seq, LANES), 1), 1.0, 0.0).astype(BF16)
        for h in range(heads):
            hs = slice(h * LANES, (h + 1) * LANES)
            kh = _rope128(mk_ref[:, hs], c, s1, s2, half)
            k_sc[:, h * slot:h * slot + LANES] = kh.astype(BF16)
            k_sc[:, h * slot + LANES:(h + 1) * slot] = onehot
            v_sc[:, h * slot:h * slot + LANES] = mv_ref[:, hs].astype(BF16)
            v_sc[:, h * slot + LANES:(h + 1) * slot] = ones
            for j in range(n_blk):
                km_sc[h * n_blk + j:h * n_blk + j + 1, :] = jnp.mean(
                    kh[j * MOBA_BLOCK:(j + 1) * MOBA_BLOCK], axis=0, keepdims=True)

    t0 = pl.multiple_of(qi * tq, tq)
    rq = pl.ds(t0, tq)
    c, s1, s2 = tc_ref[rq, :], ts1_ref[rq, :], ts2_ref[rq, :]
    km_hi, km_lo = _split_bf16(km_sc[...])
    blk_t = lax.broadcasted_iota(jnp.int32, (n_blk, tq), 0)
    causal = lax.broadcasted_iota(jnp.int32, (tq, tq), 1) <= lax.broadcasted_iota(jnp.int32, (tq, tq), 0)

    q_aug = []
    for h in range(heads):
        qf = _rope128(mq_ref[:, h * LANES:(h + 1) * LANES], c, s1, s2, half) * scale
        q_hi, q_lo = _split_bf16(qf)
        gate_t = (_dot_nt(km_hi, q_hi) + (_dot_nt(km_lo, q_hi) + _dot_nt(km_hi, q_lo)))[h * n_blk:(h + 1) * n_blk]
        eligible = blk_t < qi
        score = jnp.where(eligible, gate_t, NEG_INF)
        picked = eligible & (_rank_rows(score, n_blk) < MOBA_TOPK)
        bias_t = jnp.where(picked | (blk_t == qi), 0.0, MASKED)
        bias_t = jnp.concatenate([bias_t, jnp.zeros((LANES - n_blk, tq), F32)], axis=0)
        q_aug.append(jnp.concatenate([qf, _rows_to_lanes(bias_t, tq)], axis=1).astype(BF16))

    def step(r, carry, mask):
        out = []
        for h in range(heads):
            s = _dot_nt(q_aug[h], k_sc[r, h * slot:(h + 1) * slot])
            if mask is not None:
                s = jnp.where(mask, s, MASKED)
            out.append(_flash_update(s, *carry[h], v_sc[r, h * slot:(h + 1) * slot]))
        return tuple(out)

    def body(kb, carry):
        return step(pl.ds(pl.multiple_of(kb * tq, tq), tq), carry, None)

    init = tuple((jnp.full((tq, 1), NEG_INF, F32), jnp.zeros((tq, 2 * HEAD_DIM), F32)) for _ in range(heads))
    carry = lax.fori_loop(0, qi, body, init)
    carry = step(rq, carry, causal)
    for h in range(heads):
        acc = carry[h][1]
        o_ref[:, h * LANES:(h + 1) * LANES] = (acc[:, :HEAD_DIM] / acc[:, HEAD_DIM:]).astype(o_ref.dtype)


def _moba(proj, tabs, batch, seq):
    tq = MOBA_BLOCK
    nt = seq // tq
    width = MOBA_HEADS * HEAD_DIM
    full = lambda a: pl.BlockSpec(a.shape, lambda b, i: (0,) * a.ndim)
    return pl.pallas_call(
        functools.partial(_moba_kernel, tq=tq, seq=seq, scale=HEAD_DIM ** -0.5),
        grid=(batch, nt),
        in_specs=[pl.BlockSpec((tq, width), lambda b, i: (b * nt + i, COL512["mq"])),
                  pl.BlockSpec((seq, width), lambda b, i: (b, COL512["mk"])),
                  pl.BlockSpec((seq, width), lambda b, i: (b, COL512["mv"])),
                  full(tabs[0]), full(tabs[1]), full(tabs[2])],
        out_specs=pl.BlockSpec((tq, width), lambda b, i: (b * nt + i, 0)),
        out_shape=jax.ShapeDtypeStruct((batch * seq, width), BF16),
        scratch_shapes=[pltpu.VMEM((seq, 2 * width), BF16), pltpu.VMEM((seq, 2 * width), BF16),
                        pltpu.VMEM((LANES, LANES), F32)],
        compiler_params=_cparams("parallel", "arbitrary"),
        name="moba",
    )(proj, proj, proj, *tabs)


def _xattn_kernel(q_ref, kv_ref, o_ref):
    d = q_ref.shape[1]
    hd = d // MEM_HEADS
    for h in range(MEM_HEADS):
        hs = slice(h * hd, (h + 1) * hd)
        s = _dot_nt(q_ref[:, hs], kv_ref[0, :, hs])
        e = jnp.exp(s - jnp.max(s, axis=-1, keepdims=True))
        p = e / jnp.sum(e, axis=-1, keepdims=True)
        o_ref[:, hs] = _dot(p.astype(BF16), kv_ref[0, :, d + h * hd:d + (h + 1) * hd]).astype(o_ref.dtype)


def _xattn(q, kv, batch, seq, tq=512):
    d = q.shape[1]
    nt = seq // tq
    m_len = kv.shape[0] // batch
    return pl.pallas_call(
        _xattn_kernel,
        grid=(batch, nt),
        in_specs=[pl.BlockSpec((tq, d), lambda b, i: (b * nt + i, 0)),
                  pl.BlockSpec((1, m_len, 2 * d), lambda b, i: (b, 0, 0))],
        out_specs=pl.BlockSpec((tq, d), lambda b, i: (b * nt + i, 0)),
        out_shape=jax.ShapeDtypeStruct((batch * seq, d), BF16),
        compiler_params=_cparams("parallel", "parallel"),
        name="xattn",
    )(q, kv.reshape(batch, m_len, 2 * d))


IN_SPLIT_NAMES = ("c_q", "c_kv", "k_rope", "nq", "nkc", "nvc", "nks", "nvs", "nkw", "nvw", "ng", "mq", "mk", "mv")
IN_SPLIT_SIZES = (512, 512, 64, 512, 128, 128, 128, 128, 128, 128, 12, 512, 512, 512)


def _pack_w_in_kernel(w_ref, o_ref):
    rows = w_ref.shape[0]
    off = 0
    for name, sz in zip(IN_SPLIT_NAMES, IN_SPLIT_SIZES):
        width = 512 if name in COL512 else LANES
        dst = COL512[name] * 512 if name in COL512 else COL128[name] * LANES
        x = w_ref[:, off:off + sz].astype(BF16)
        if sz < width:
            x = jnp.concatenate([x, jnp.zeros((rows, width - sz), BF16)], axis=1)
        o_ref[0, :, dst:dst + width] = x
        off += sz


def _pack_w_in(w, tr=256):
    layers, d, width = w.shape
    return pl.pallas_call(
        _pack_w_in_kernel,
        grid=(layers, d // tr),
        in_specs=[pl.BlockSpec((None, tr, width), lambda l, i: (l, i, 0))],
        out_specs=pl.BlockSpec((1, tr, PROJ_WIDTH), lambda l, i: (l, i, 0)),
        out_shape=jax.ShapeDtypeStruct((layers, d, PROJ_WIDTH), BF16),
        compiler_params=_cparams("parallel", "parallel"),
        name="pack_w_in",
    )(w)


def _pack_w_uq(w):
    r = w.shape[0]
    w = w.reshape(r, MLA_HEADS, MLA_NOPE + MLA_ROPE)
    w = jnp.pad(w, ((0, 0), (0, 0), (0, MLA_SLOT - MLA_NOPE - MLA_ROPE)))
    return w.reshape(r, MLA_HEADS * MLA_SLOT).astype(BF16)


def _pack_w_ukv(w):
    r = w.shape[0]
    w = w.reshape(r, MLA_HEADS, 2, HEAD_DIM).transpose(0, 2, 1, 3)
    return w.reshape(r, 2 * MLA_HEADS * HEAD_DIM).astype(BF16)


def kernel(x, mem, ln_in_g, ln_in_b, w_in, mla_q_norm, mla_kv_norm, mla_w_uq, mla_w_ukv, nsa_cmp_w1, nsa_cmp_w2, nsa_cmp_pos, w_out, ln1_g, ln1_b, mem_wq, mem_wkv, mem_wo, ln2_g, ln2_b, mlp_w1, mlp_w2, ln3_g, ln3_b):
    batch, seq, d = x.shape
    n = batch * seq
    mla_tabs = _rope_tables(seq, MLA_ROPE)
    rot_tabs = _rope_tables(seq, PARTIAL_ROT)
    mem2 = mem.reshape(batch * mem.shape[1], d)
    w_in_packed = _pack_w_in(w_in)

    h = _layer_norm(x.reshape(n, d), ln_in_g, ln_in_b)
    for l in range(DEPTH):
        proj = _matmul(h, w_in_packed[l], F32, name="in_proj")
        q, k, v = _mla_up(proj, mla_q_norm[l], mla_kv_norm[l], _pack_w_uq(mla_w_uq[l]), _pack_w_ukv(mla_w_ukv[l]),
                          mla_tabs, batch, seq)
        o_a = _mla_attn(q, k, v).reshape(n, MLA_HEADS * HEAD_DIM)
        o_b = _nsa(proj, nsa_cmp_w1[l].astype(BF16), nsa_cmp_w2[l].astype(BF16), nsa_cmp_pos[l], rot_tabs, batch, seq)
        o_c = _moba(proj, rot_tabs, batch, seq)
        h = _out_ln([o_a, o_b, o_c], w_out[l].astype(BF16), h, ln1_g[l], ln1_b[l], name="mix_out_ln")

        wq_scaled = (mem_wq[l] * (d // MEM_HEADS) ** -0.5).astype(BF16)
        xq = _matmul(h, wq_scaled, BF16, name="mem_q")
        xkv = _matmul(mem2, mem_wkv[l].astype(BF16), BF16, name="mem_kv")
        ctx = _xattn(xq, xkv, batch, seq)
        h = _out_ln([ctx], mem_wo[l].astype(BF16), h, ln2_g[l], ln2_b[l], name="mem_out_ln")

        h = _mlp_ln(h, mlp_w1[l].astype(BF16), mlp_w2[l].astype(BF16), ln3_g[l], ln3_b[l])
    return h.reshape(batch, seq, d)
```

```python
import functools

import numpy as np
import jax
import jax.numpy as jnp
from jax import lax
from jax.experimental import pallas as pl
from jax.experimental.pallas import tpu as pltpu

F32 = jnp.float32
BF16 = jnp.bfloat16
NEG_INF = float("-inf")

D_MODEL = 2048
DEPTH = 2
HEAD_DIM = 128
MLA_HEADS = 8
NSA_HEADS = 4
MOBA_HEADS = 4
ROPE_THETA = 500000.0
PARTIAL_ROT = HEAD_DIM // 4
MLA_Q_RANK = 512
MLA_KV_RANK = 512
MLA_NOPE = 128
MLA_ROPE = 64
MLA_SLOT = 256
NSA_CMP_LEN = 32
NSA_CMP_STRIDE = 16
NSA_SEL_LEN = 64
NSA_SEL_TOPK = 16
NSA_WINDOW = 512
NSA_FORCE_SCORE = 1.0e4
MOBA_BLOCK = 256
MOBA_TOPK = 3
MEM_HEADS = 4
D_FF = 4 * D_MODEL
DEEPNORM_ALPHA = (2 * DEPTH) ** 0.25
LANES = 128

PROJ_WIDTH = 4096
COL512 = dict(c_q=0, c_kv=1, nq=2, mq=3, mk=4, mv=5)
COL128 = dict(k_rope=24, nkc=25, nvc=26, nks=27, nvs=28, nkw=29, nvw=30, ng=31)

VMEM_LIMIT = 56 * 1024 * 1024


def _cparams(*sem):
    return pltpu.CompilerParams(dimension_semantics=sem, vmem_limit_bytes=VMEM_LIMIT)


def _dot(a, b):
    return jnp.dot(a, b, preferred_element_type=F32)


def _dot_nt(a, b):
    return lax.dot_general(a, b, (((1,), (1,)), ((), ())), preferred_element_type=F32)


def _split_bf16(x):
    hi = x.astype(BF16)
    return hi, (x - hi.astype(F32)).astype(BF16)


def _ln_rows(x, g, b, eps=1e-5):
    mu = jnp.mean(x, axis=-1, keepdims=True)
    xc = x - mu
    var = jnp.mean(xc * xc, axis=-1, keepdims=True)
    return xc * lax.rsqrt(var + eps) * g + b


def _rms_rows(x, g, eps=1e-6):
    return x * lax.rsqrt(jnp.mean(x * x, axis=-1, keepdims=True) + eps) * g


def _rope128(x, c, s1, s2, half):
    return x * c + pltpu.roll(x, LANES - half, 1) * s1 + pltpu.roll(x, half, 1) * s2


def _rope_tables(n_pos, dim):
    half = dim // 2
    inv = ROPE_THETA ** (-jnp.arange(0, dim, 2, dtype=F32) / dim)
    ang = jnp.arange(n_pos, dtype=F32)[:, None] * inv[None, :]
    cos, sin = jnp.cos(ang), jnp.sin(ang)
    ones = jnp.ones((n_pos, LANES - dim), F32)
    z = lambda w: jnp.zeros((n_pos, w), F32)
    c = jnp.concatenate([cos, cos, ones], axis=1)
    s1 = jnp.concatenate([-sin, z(LANES - half)], axis=1)
    s2 = jnp.concatenate([z(half), sin, z(LANES - dim)], axis=1)
    return c, s1, s2


MASKED = -1.0e30


def _rank_rows(score, n_cand):
    row = lax.broadcasted_iota(jnp.int32, score.shape, 0)
    rank = jnp.zeros(score.shape, F32)
    for jp in range(n_cand):
        cand = score[jp:jp + 1, :]
        ahead = (cand > score) | ((cand == score) & (row > jp))
        rank = rank + jnp.where(ahead, 1.0, 0.0)
    return rank


def _rows_to_lanes(x_t, tq):
    eye = jnp.where(lax.broadcasted_iota(jnp.int32, (tq, tq), 0) == lax.broadcasted_iota(jnp.int32, (tq, tq), 1),
                    1.0, 0.0).astype(BF16)
    return _dot_nt(eye, x_t.astype(BF16))


def _flash_update(s, m, acc, v_aug):
    m_new = jnp.maximum(m, jnp.max(s, axis=-1, keepdims=True))
    p = jnp.exp(s - m_new).astype(BF16)
    return m_new, jnp.exp(m - m_new) * acc + _dot(p, v_aug)


def _ln_kernel(x_ref, g_ref, b_ref, o_ref):
    o_ref[...] = _ln_rows(x_ref[...], g_ref[...], b_ref[...])


def _layer_norm(x, g, b, tm=512):
    n, d = x.shape
    return pl.pallas_call(
        _ln_kernel,
        grid=(n // tm,),
        in_specs=[pl.BlockSpec((tm, d), lambda i: (i, 0)),
                  pl.BlockSpec((1, d), lambda i: (0, 0)),
                  pl.BlockSpec((1, d), lambda i: (0, 0))],
        out_specs=pl.BlockSpec((tm, d), lambda i: (i, 0)),
        out_shape=jax.ShapeDtypeStruct((n, d), F32),
        compiler_params=_cparams("parallel"),
        name="ln_in",
    )(x, g.reshape(1, d), b.reshape(1, d))


def _mm_kernel(a_ref, w_ref, o_ref, abf_ref):
    @pl.when(pl.program_id(1) == 0)
    def _():
        abf_ref[...] = a_ref[...].astype(BF16)

    o_ref[...] = _dot(abf_ref[...], w_ref[...]).astype(o_ref.dtype)


def _matmul(a, w, out_dtype, tm=1024, tn=512, name="mm"):
    m, k = a.shape
    n = w.shape[1]
    tm = min(tm, m)
    return pl.pallas_call(
        _mm_kernel,
        grid=(m // tm, n // tn),
        in_specs=[pl.BlockSpec((tm, k), lambda i, j: (i, 0)),
                  pl.BlockSpec((k, tn), lambda i, j: (0, j))],
        out_specs=pl.BlockSpec((tm, tn), lambda i, j: (i, j)),
        out_shape=jax.ShapeDtypeStruct((m, n), out_dtype),
        scratch_shapes=[pltpu.VMEM((tm, k), BF16)],
        compiler_params=_cparams("parallel", "arbitrary"),
        name=name,
    )(a, w)


def _out_ln_kernel(*refs, widths):
    n_a = len(widths)
    a_refs = refs[:n_a]
    w_ref, h_ref, g_ref, b_ref, o_ref = refs[n_a:]
    acc = DEEPNORM_ALPHA * h_ref[...]
    off = 0
    for a_ref, wd in zip(a_refs, widths):
        acc = acc + _dot(a_ref[...], w_ref[off:off + wd, :])
        off += wd
    o_ref[...] = _ln_rows(acc, g_ref[...], b_ref[...])


def _out_ln(a_list, w, h, g, b, tm=512, name="out_ln"):
    n, d = h.shape
    widths = tuple(a.shape[1] for a in a_list)
    k = sum(widths)
    in_specs = [pl.BlockSpec((tm, wd), lambda i: (i, 0)) for wd in widths]
    in_specs += [pl.BlockSpec((k, d), lambda i: (0, 0)),
                 pl.BlockSpec((tm, d), lambda i: (i, 0)),
                 pl.BlockSpec((1, d), lambda i: (0, 0)),
                 pl.BlockSpec((1, d), lambda i: (0, 0))]
    return pl.pallas_call(
        functools.partial(_out_ln_kernel, widths=widths),
        grid=(n // tm,),
        in_specs=in_specs,
        out_specs=pl.BlockSpec((tm, d), lambda i: (i, 0)),
        out_shape=jax.ShapeDtypeStruct((n, d), F32),
        compiler_params=_cparams("parallel"),
        name=name,
    )(*a_list, w, h, g.reshape(1, d), b.reshape(1, d))


def _mlp_kernel(h_ref, w1_ref, w2_ref, g_ref, b_ref, o_ref, hbf_ref, acc_ref):
    f = pl.program_id(1)

    @pl.when(f == 0)
    def _():
        hbf_ref[...] = h_ref[...].astype(BF16)
        acc_ref[...] = jnp.zeros_like(acc_ref)

    u = jnp.maximum(_dot(hbf_ref[...], w1_ref[...]), 0.0)
    acc_ref[...] += _dot((u * u).astype(BF16), w2_ref[...])

    @pl.when(f == pl.num_programs(1) - 1)
    def _():
        y = DEEPNORM_ALPHA * h_ref[...] + acc_ref[...]
        o_ref[...] = _ln_rows(y, g_ref[...], b_ref[...])


def _mlp_ln(h, w1, w2, g, b, tm=512, tf=512):
    n, d = h.shape
    dff = w1.shape[1]
    return pl.pallas_call(
        _mlp_kernel,
        grid=(n // tm, dff // tf),
        in_specs=[pl.BlockSpec((tm, d), lambda i, f: (i, 0)),
                  pl.BlockSpec((d, tf), lambda i, f: (0, f)),
                  pl.BlockSpec((tf, d), lambda i, f: (f, 0)),
                  pl.BlockSpec((1, d), lambda i, f: (0, 0)),
                  pl.BlockSpec((1, d), lambda i, f: (0, 0))],
        out_specs=pl.BlockSpec((tm, d), lambda i, f: (i, 0)),
        out_shape=jax.ShapeDtypeStruct((n, d), F32),
        scratch_shapes=[pltpu.VMEM((tm, d), BF16), pltpu.VMEM((tm, d), F32)],
        compiler_params=_cparams("parallel", "arbitrary"),
        name="mlp_ln",
    )(h, w1, w2, g.reshape(1, d), b.reshape(1, d))


def _mla_up_kernel(cq_ref, ckv_ref, kr_ref, gq_ref, gkv_ref, wq_ref, wkv_ref, tc_ref, ts1_ref, ts2_ref,
                   q_ref, k_ref, v_ref, *, scale):
    half = MLA_ROPE // 2
    c, s1, s2 = tc_ref[...], ts1_ref[...], ts2_ref[...]
    nq = _rms_rows(cq_ref[...], gq_ref[...]).astype(BF16)
    nkv = _rms_rows(ckv_ref[...], gkv_ref[...]).astype(BF16)
    qf = _dot(nq, wq_ref[...])
    kvf = _dot(nkv, wkv_ref[...])
    kr = _rope128(kr_ref[...], c, s1, s2, half).astype(BF16)
    for h in range(MLA_HEADS):
        o = h * MLA_SLOT
        q_ref[0, h, :, 0:LANES] = (qf[:, o:o + LANES] * scale).astype(BF16)
        q_ref[0, h, :, LANES:] = (_rope128(qf[:, o + LANES:o + MLA_SLOT], c, s1, s2, half) * scale).astype(BF16)
        k_ref[0, h, :, 0:LANES] = kvf[:, h * LANES:(h + 1) * LANES].astype(BF16)
        k_ref[0, h, :, LANES:] = kr
        v_ref[0, h, :, 0:LANES] = kvf[:, (MLA_HEADS + h) * LANES:(MLA_HEADS + h + 1) * LANES].astype(BF16)
        v_ref[0, h, :, LANES:] = jnp.ones((kr.shape[0], LANES), BF16)


def _mla_up(proj, gq, gkv, wq, wkv, tabs, batch, seq, tm=512):
    nt = seq // tm
    row = lambda b, i: b * nt + i
    tab_spec = pl.BlockSpec((tm, LANES), lambda b, i: (i, 0))
    hm = lambda w: pl.BlockSpec((1, MLA_HEADS, tm, w), lambda b, i: (b, 0, i, 0))
    return pl.pallas_call(
        functools.partial(_mla_up_kernel, scale=(MLA_NOPE + MLA_ROPE) ** -0.5),
        grid=(batch, nt),
        in_specs=[pl.BlockSpec((tm, 512), lambda b, i: (row(b, i), COL512["c_q"])),
                  pl.BlockSpec((tm, 512), lambda b, i: (row(b, i), COL512["c_kv"])),
                  pl.BlockSpec((tm, LANES), lambda b, i: (row(b, i), COL128["k_rope"])),
                  pl.BlockSpec((1, MLA_Q_RANK), lambda b, i: (0, 0)),
                  pl.BlockSpec((1, MLA_KV_RANK), lambda b, i: (0, 0)),
                  pl.BlockSpec(wq.shape, lambda b, i: (0, 0)),
                  pl.BlockSpec(wkv.shape, lambda b, i: (0, 0)),
                  tab_spec, tab_spec, tab_spec],
        out_specs=[hm(MLA_SLOT), hm(MLA_SLOT), hm(2 * HEAD_DIM)],
        out_shape=[jax.ShapeDtypeStruct((batch, MLA_HEADS, seq, MLA_SLOT), BF16),
                   jax.ShapeDtypeStruct((batch, MLA_HEADS, seq, MLA_SLOT), BF16),
                   jax.ShapeDtypeStruct((batch, MLA_HEADS, seq, 2 * HEAD_DIM), BF16)],
        compiler_params=_cparams("parallel", "parallel"),
        name="mla_up",
    )(proj, proj, proj, gq.reshape(1, -1), gkv.reshape(1, -1), wq, wkv, *tabs)


def _mla_attn_kernel(q_ref, k_ref, v_ref, o_ref, *, tq, tk):
    qi = pl.program_id(1)
    heads = q_ref.shape[1]
    sub = tq // tk
    row = lax.broadcasted_iota(jnp.int32, (tq, tk), 0)
    col = lax.broadcasted_iota(jnp.int32, (tq, tk), 1)

    def step(kb, carry, mask):
        r = pl.ds(pl.multiple_of(kb * tk, tk), tk)
        out = []
        for h in range(heads):
            s = _dot_nt(q_ref[0, h], k_ref[0, h, r, :])
            if mask is not None:
                s = jnp.where(mask, s, MASKED)
            out.append(_flash_update(s, *carry[h], v_ref[0, h, r, :]))
        return tuple(out)

    init = tuple((jnp.full((tq, 1), NEG_INF, F32), jnp.zeros((tq, 2 * HEAD_DIM), F32)) for _ in range(heads))
    carry = lax.fori_loop(0, qi * sub, lambda kb, c: step(kb, c, None), init)
    for j in range(sub):
        carry = step(qi * sub + j, carry, col + j * tk <= row)
    for h in range(heads):
        acc = carry[h][1]
        o_ref[0, :, h * HEAD_DIM:(h + 1) * HEAD_DIM] = (acc[:, :HEAD_DIM] / acc[:, HEAD_DIM:]).astype(o_ref.dtype)


def _mla_attn(q, k, v, tq=512, tk=256):
    batch, heads, seq, _ = q.shape
    return pl.pallas_call(
        functools.partial(_mla_attn_kernel, tq=tq, tk=tk),
        grid=(batch, seq // tq),
        in_specs=[pl.BlockSpec((1, heads, tq, MLA_SLOT), lambda b, i: (b, 0, i, 0)),
                  pl.BlockSpec((1, heads, seq, MLA_SLOT), lambda b, i: (b, 0, 0, 0)),
                  pl.BlockSpec((1, heads, seq, 2 * HEAD_DIM), lambda b, i: (b, 0, 0, 0))],
        out_specs=pl.BlockSpec((1, tq, heads * HEAD_DIM), lambda b, i: (b, i, 0)),
        out_shape=jax.ShapeDtypeStruct((batch, seq, heads * HEAD_DIM), BF16),
        compiler_params=_cparams("parallel", "arbitrary"),
        name="mla_attn",
    )(q, k, v)


def _nsa_kernel(nq_ref, kc_ref, vc_ref, ks_ref, vs_ref, kw_ref, vw_ref, ng_ref,
                w1_ref, w2_ref, pos_ref, ovt_ref, tc_ref, ts1_ref, ts2_ref,
                o_ref,
                tmp_sc, kc_sc, vc_sc, ks_sc, vs_sc, kw_sc, vw_sc, *, tq, seq, scale):
    qi = pl.program_id(1)
    half = PARTIAL_ROT // 2
    heads = NSA_HEADS
    n_chunk = seq // NSA_CMP_STRIDE
    n_sel = seq // NSA_SEL_LEN

    @pl.when(qi == 0)
    def _prep():
        c, s1, s2 = tc_ref[...], ts1_ref[...], ts2_ref[...]
        ones = jnp.ones((seq, LANES), BF16)
        blk = lax.broadcasted_iota(jnp.int32, (seq, LANES), 0) >> 6
        ks_sc[:, 0:LANES] = _rope128(ks_ref[...], c, s1, s2, half).astype(BF16)
        ks_sc[:, LANES:] = jnp.where(blk == lax.broadcasted_iota(jnp.int32, (seq, LANES), 1), 1.0, 0.0).astype(BF16)
        kw_sc[...] = _rope128(kw_ref[...], c, s1, s2, half).astype(BF16)
        vs_sc[:, 0:LANES] = vs_ref[...].astype(BF16)
        vs_sc[:, LANES:] = ones
        vw_sc[:, 0:LANES] = vw_ref[...].astype(BF16)
        vw_sc[:, LANES:] = ones
        for i, (src, dst) in enumerate(((kc_ref, kc_sc), (vc_ref, vc_sc))):
            tmp_sc[...] = _rope128(src[...], c, s1, s2, half) if i == 0 else src[...]
            lo, hi = [], []
            for t in range(NSA_CMP_STRIDE):
                x = tmp_sc[pl.ds(t, n_chunk, stride=NSA_CMP_STRIDE), :]
                lo.append((x + pos_ref[i, t:t + 1, :]).astype(BF16))
                hi.append((x + pos_ref[i, NSA_CMP_STRIDE + t:NSA_CMP_STRIDE + t + 1, :]).astype(BF16))
            kw1 = NSA_CMP_STRIDE * HEAD_DIM
            a = _dot(jnp.concatenate(lo, axis=1), w1_ref[i, 0:kw1, :])
            bm = _dot(jnp.concatenate(hi, axis=1), w1_ref[i, kw1:2 * kw1, :])
            hid = jax.nn.gelu(a + pltpu.roll(bm, n_chunk - 1, 0))
            dst[...] = _dot(hid.astype(BF16), w2_ref[i]).astype(BF16)

    t0 = pl.multiple_of(qi * tq, tq)
    rq = pl.ds(t0, tq)
    c, s1, s2 = tc_ref[rq, :], ts1_ref[rq, :], ts2_ref[rq, :]
    qf = nq_ref[...]
    qs = jnp.concatenate(
        [_rope128(qf[:, h * LANES:(h + 1) * LANES], c, s1, s2, half) * scale for h in range(heads)],
        axis=0)
    qs_bf = qs.astype(BF16)

    lane = lax.broadcasted_iota(jnp.int32, (tq, LANES), 1)
    tpos = t0 + lax.broadcasted_iota(jnp.int32, (tq, LANES), 0)

    s = _dot_nt(qs_bf, kc_sc[...]).reshape(heads, tq, LANES)
    cmask = (lane * NSA_CMP_STRIDE + (NSA_CMP_LEN - 1) <= tpos)[None]
    s = jnp.where(cmask, s, NEG_INF)
    m = jnp.max(s, axis=-1, keepdims=True)
    m = jnp.where(m == NEG_INF, 0.0, m)
    e = jnp.exp(s - m)
    p_cmp = e / jnp.maximum(jnp.sum(e, axis=-1, keepdims=True), 1e-30)
    o_cmp = _dot(p_cmp.reshape(heads * tq, LANES).astype(BF16), vc_sc[...]).reshape(heads, tq, HEAD_DIM)

    p_hi, p_lo = _split_bf16(p_cmp[0] + p_cmp[1] + p_cmp[2] + p_cmp[3])
    imp_t = (_dot_nt(ovt_ref[...], p_hi) + _dot_nt(ovt_ref[...], p_lo))[0:n_sel]
    blk_t = lax.broadcasted_iota(jnp.int32, (n_sel, tq), 0)
    cur_t = (t0 + lax.broadcasted_iota(jnp.int32, (n_sel, tq), 1)) >> 6
    eligible = blk_t <= cur_t
    forced = (blk_t == 0) | (blk_t == cur_t) | (blk_t == cur_t - 1)
    score = jnp.where(eligible, jnp.where(forced, NSA_FORCE_SCORE, imp_t), NEG_INF)
    rank = _rank_rows(score, n_sel)
    bias_t = jnp.where(eligible & (rank < NSA_SEL_TOPK), 0.0, MASKED)
    bias_t = jnp.concatenate([bias_t, jnp.zeros((LANES - n_sel, tq), F32)], axis=0)
    bias = _rows_to_lanes(bias_t, tq)
    q_aug = jnp.concatenate([qs, jnp.concatenate([bias] * heads, axis=0)], axis=1).astype(BF16)

    local_r = lax.broadcasted_iota(jnp.int32, (tq, tq), 0)
    local_c = lax.broadcasted_iota(jnp.int32, (tq, tq), 1)
    causal = (local_c <= local_r)[None]

    def masked3(s, mask):
        return jnp.where(mask, s.reshape(heads, tq, tq), MASKED).reshape(heads * tq, tq)

    init = (jnp.full((heads * tq, 1), NEG_INF, F32), jnp.zeros((heads * tq, 2 * HEAD_DIM), F32))

    def sel_body(kb, carry):
        r = pl.ds(pl.multiple_of(kb * tq, tq), tq)
        return _flash_update(_dot_nt(q_aug, ks_sc[r, :]), *carry, vs_sc[r, :])

    carry = lax.fori_loop(0, qi, sel_body, init)
    _, acc = _flash_update(masked3(_dot_nt(q_aug, ks_sc[rq, :]), causal), *carry, vs_sc[rq, :])
    o_sel = (acc[:, :HEAD_DIM] / acc[:, HEAD_DIM:]).reshape(heads, tq, HEAD_DIM)

    carry = init
    for back, mask in ((2, (local_c > local_r)[None]), (1, None), (0, causal)):
        kb = qi - back
        r = pl.ds(pl.multiple_of(jnp.maximum(kb, 0) * tq, tq), tq)
        s = _dot_nt(qs_bf, kw_sc[r, :])
        if back:
            valid = kb >= 0
            mask = valid if mask is None else mask & valid
        carry = _flash_update(masked3(s, mask), *carry, vw_sc[r, :])
    acc = carry[1]
    o_win = (acc[:, :HEAD_DIM] / acc[:, HEAD_DIM:]).reshape(heads, tq, HEAD_DIM)

    g = jax.nn.sigmoid(ng_ref[...])
    for h in range(heads):
        o = (g[:, 3 * h:3 * h + 1] * o_cmp[h] + g[:, 3 * h + 1:3 * h + 2] * o_sel[h]
             + g[:, 3 * h + 2:3 * h + 3] * o_win[h])
        o_ref[:, h * LANES:(h + 1) * LANES] = o.astype(o_ref.dtype)


def _nsa(proj, w1, w2, pos, tabs, batch, seq, tq=256):
    nt = seq // tq
    n_cmp = (seq - NSA_CMP_LEN) // NSA_CMP_STRIDE + 1
    n_sel = seq // NSA_SEL_LEN
    starts = np.arange(LANES) * NSA_CMP_STRIDE
    sel_start = np.arange(LANES) * NSA_SEL_LEN
    overlap = ((starts[:, None] < sel_start[None, :] + NSA_SEL_LEN)
               & (starts[:, None] + NSA_CMP_LEN > sel_start[None, :])
               & (np.arange(LANES)[:, None] < n_cmp) & (np.arange(LANES)[None, :] < n_sel))
    ovt = jnp.asarray(overlap.T.astype(np.float32), BF16)
    seq_col = lambda name: pl.BlockSpec((seq, LANES), lambda b, i: (b, COL128[name]))
    full = lambda a: pl.BlockSpec(a.shape, lambda b, i: (0,) * a.ndim)
    return pl.pallas_call(
        functools.partial(_nsa_kernel, tq=tq, seq=seq, scale=HEAD_DIM ** -0.5),
        grid=(batch, nt),
        in_specs=[pl.BlockSpec((tq, 512), lambda b, i: (b * nt + i, COL512["nq"])),
                  seq_col("nkc"), seq_col("nvc"), seq_col("nks"), seq_col("nvs"), seq_col("nkw"), seq_col("nvw"),
                  pl.BlockSpec((tq, LANES), lambda b, i: (b * nt + i, COL128["ng"])),
                  full(w1), full(w2), full(pos), full(ovt), full(tabs[0]), full(tabs[1]), full(tabs[2])],
        out_specs=pl.BlockSpec((tq, NSA_HEADS * HEAD_DIM), lambda b, i: (b * nt + i, 0)),
        out_shape=jax.ShapeDtypeStruct((batch * seq, NSA_HEADS * HEAD_DIM), BF16),
        scratch_shapes=[pltpu.VMEM((seq, LANES), F32),
                        pltpu.VMEM((LANES, LANES), BF16), pltpu.VMEM((LANES, LANES), BF16),
                        pltpu.VMEM((seq, 2 * LANES), BF16), pltpu.VMEM((seq, 2 * LANES), BF16),
                        pltpu.VMEM((seq, LANES), BF16), pltpu.VMEM((seq, 2 * LANES), BF16)],
        compiler_params=_cparams("parallel", "arbitrary"),
        name="nsa",
    )(proj, proj, proj, proj, proj, proj, proj, proj, w1, w2, pos, ovt, *tabs)


def _moba_kernel(mq_ref, mk_ref, mv_ref, tc_ref, ts1_ref, ts2_ref, o_ref, k_sc, v_sc, km_sc, *, tq, seq, scale):
    qi = pl.program_id(1)
    half = PARTIAL_ROT // 2
    heads = MOBA_HEADS
    n_blk = seq // MOBA_BLOCK
    slot = 2 * LANES

    @pl.when(qi == 0)
    def _prep():
        c, s1, s2 = tc_ref[...], ts1_ref[...], ts2_ref[...]
        km_sc[...] = jnp.zeros_like(km_sc)
        ones = jnp.ones((seq, LANES), BF16)
        blk = lax.broadcasted_iota(jnp.int32, (seq, LANES), 0) >> 8
        onehot = jnp.where(blk == lax.broadcasted_iota(jnp.int32, (seq, LANES), 1), 1.0, 0.0).astype(BF16)
        for h in range(heads):
            hs = slice(h * LANES, (h + 1) * LANES)
            kh = _rope128(mk_ref[:, hs], c, s1, s2, half)
            k_sc[:, h * slot:h * slot + LANES] = kh.astype(BF16)
            k_sc[:, h * slot + LANES:(h + 1) * slot] = onehot
            v_sc[:, h * slot:h * slot + LANES] = mv_ref[:, hs].astype(BF16)
            v_sc[:, h * slot + LANES:(h + 1) * slot] = ones
            for j in range(n_blk):
                km_sc[h * n_blk + j:h * n_blk + j + 1, :] = jnp.mean(
                    kh[j * MOBA_BLOCK:(j + 1) * MOBA_BLOCK], axis=0, keepdims=True)

    t0 = pl.multiple_of(qi * tq, tq)
    rq = pl.ds(t0, tq)
    c, s1, s2 = tc_ref[rq, :], ts1_ref[rq, :], ts2_ref[rq, :]
    km_hi, km_lo = _split_bf16(km_sc[...])
    tk = MOBA_BLOCK
    sub = tq // tk
    blk_t = lax.broadcasted_iota(jnp.int32, (n_blk, tq), 0)
    cur_t = (t0 + lax.broadcasted_iota(jnp.int32, (n_blk, tq), 1)) >> 8
    row = lax.broadcasted_iota(jnp.int32, (tq, tk), 0)
    col = lax.broadcasted_iota(jnp.int32, (tq, tk), 1)

    q_aug = []
    for h in range(heads):
        qf = _rope128(mq_ref[:, h * LANES:(h + 1) * LANES], c, s1, s2, half) * scale
        q_hi, q_lo = _split_bf16(qf)
        gate_t = (_dot_nt(km_hi, q_hi) + (_dot_nt(km_lo, q_hi) + _dot_nt(km_hi, q_lo)))[h * n_blk:(h + 1) * n_blk]
        eligible = blk_t < cur_t
        score = jnp.where(eligible, gate_t, NEG_INF)
        picked = eligible & (_rank_rows(score, n_blk) < MOBA_TOPK)
        bias_t = jnp.where(picked | (blk_t == cur_t), 0.0, MASKED)
        bias_t = jnp.concatenate([bias_t, jnp.zeros((LANES - n_blk, tq), F32)], axis=0)
        q_aug.append(jnp.concatenate([qf, _rows_to_lanes(bias_t, tq)], axis=1).astype(BF16))

    def step(kb, carry, mask):
        r = pl.ds(pl.multiple_of(kb * tk, tk), tk)
        out = []
        for h in range(heads):
            s = _dot_nt(q_aug[h], k_sc[r, h * slot:(h + 1) * slot])
            if mask is not None:
                s = jnp.where(mask, s, MASKED)
            out.append(_flash_update(s, *carry[h], v_sc[r, h * slot:(h + 1) * slot]))
        return tuple(out)

    init = tuple((jnp.full((tq, 1), NEG_INF, F32), jnp.zeros((tq, 2 * HEAD_DIM), F32)) for _ in range(heads))
    carry = lax.fori_loop(0, qi * sub, lambda kb, cr: step(kb, cr, None), init)
    for j in range(sub):
        carry = step(qi * sub + j, carry, col + j * tk <= row)
    for h in range(heads):
        acc = carry[h][1]
        o_ref[:, h * LANES:(h + 1) * LANES] = (acc[:, :HEAD_DIM] / acc[:, HEAD_DIM:]).astype(o_ref.dtype)


def _moba(proj, tabs, batch, seq):
    tq = 2 * MOBA_BLOCK
    nt = seq // tq
    width = MOBA_HEADS * HEAD_DIM
    full = lambda a: pl.BlockSpec(a.shape, lambda b, i: (0,) * a.ndim)
    return pl.pallas_call(
        functools.partial(_moba_kernel, tq=tq, seq=seq, scale=HEAD_DIM ** -0.5),
        grid=(batch, nt),
        in_specs=[pl.BlockSpec((tq, width), lambda b, i: (b * nt + i, COL512["mq"])),
                  pl.BlockSpec((seq, width), lambda b, i: (b, COL512["mk"])),
                  pl.BlockSpec((seq, width), lambda b, i: (b, COL512["mv"])),
                  full(tabs[0]), full(tabs[1]), full(tabs[2])],
        out_specs=pl.BlockSpec((tq, width), lambda b, i: (b * nt + i, 0)),
        out_shape=jax.ShapeDtypeStruct((batch * seq, width), BF16),
        scratch_shapes=[pltpu.VMEM((seq, 2 * width), BF16), pltpu.VMEM((seq, 2 * width), BF16),
                        pltpu.VMEM((LANES, LANES), F32)],
        compiler_params=_cparams("parallel", "arbitrary"),
        name="moba",
    )(proj, proj, proj, *tabs)


def _xattn_kernel(q_ref, kv_ref, o_ref):
    d = q_ref.shape[1]
    hd = d // MEM_HEADS
    for h in range(MEM_HEADS):
        hs = slice(h * hd, (h + 1) * hd)
        s = _dot_nt(q_ref[:, hs], kv_ref[0, :, hs])
        e = jnp.exp(s - jnp.max(s, axis=-1, keepdims=True))
        p = e / jnp.sum(e, axis=-1, keepdims=True)
        o_ref[:, hs] = _dot(p.astype(BF16), kv_ref[0, :, d + h * hd:d + (h + 1) * hd]).astype(o_ref.dtype)


def _xattn(q, kv, batch, seq, tq=512):
    d = q.shape[1]
    nt = seq // tq
    m_len = kv.shape[0] // batch
    return pl.pallas_call(
        _xattn_kernel,
        grid=(batch, nt),
        in_specs=[pl.BlockSpec((tq, d), lambda b, i: (b * nt + i, 0)),
                  pl.BlockSpec((1, m_len, 2 * d), lambda b, i: (b, 0, 0))],
        out_specs=pl.BlockSpec((tq, d), lambda b, i: (b * nt + i, 0)),
        out_shape=jax.ShapeDtypeStruct((batch * seq, d), BF16),
        compiler_params=_cparams("parallel", "parallel"),
        name="xattn",
    )(q, kv.reshape(batch, m_len, 2 * d))


IN_SPLIT_NAMES = ("c_q", "c_kv", "k_rope", "nq", "nkc", "nvc", "nks", "nvs", "nkw", "nvw", "ng", "mq", "mk", "mv")
IN_SPLIT_SIZES = (512, 512, 64, 512, 128, 128, 128, 128, 128, 128, 12, 512, 512, 512)


def _pack_w_in_kernel(w_ref, o_ref):
    rows = w_ref.shape[0]
    off = 0
    for name, sz in zip(IN_SPLIT_NAMES, IN_SPLIT_SIZES):
        width = 512 if name in COL512 else LANES
        dst = COL512[name] * 512 if name in COL512 else COL128[name] * LANES
        x = w_ref[:, off:off + sz].astype(BF16)
        if sz < width:
            x = jnp.concatenate([x, jnp.zeros((rows, width - sz), BF16)], axis=1)
        o_ref[0, :, dst:dst + width] = x
        off += sz


def _pack_w_in(w, tr=256):
    layers, d, width = w.shape
    return pl.pallas_call(
        _pack_w_in_kernel,
        grid=(layers, d // tr),
        in_specs=[pl.BlockSpec((None, tr, width), lambda l, i: (l, i, 0))],
        out_specs=pl.BlockSpec((1, tr, PROJ_WIDTH), lambda l, i: (l, i, 0)),
        out_shape=jax.ShapeDtypeStruct((layers, d, PROJ_WIDTH), BF16),
        compiler_params=_cparams("parallel", "parallel"),
        name="pack_w_in",
    )(w)


def _pack_w_uq(w):
    r = w.shape[0]
    w = w.reshape(r, MLA_HEADS, MLA_NOPE + MLA_ROPE)
    w = jnp.pad(w, ((0, 0), (0, 0), (0, MLA_SLOT - MLA_NOPE - MLA_ROPE)))
    return w.reshape(r, MLA_HEADS * MLA_SLOT).astype(BF16)


def _pack_w_ukv(w):
    r = w.shape[0]
    w = w.reshape(r, MLA_HEADS, 2, HEAD_DIM).transpose(0, 2, 1, 3)
    return w.reshape(r, 2 * MLA_HEADS * HEAD_DIM).astype(BF16)


def kernel(x, mem, ln_in_g, ln_in_b, w_in, mla_q_norm, mla_kv_norm, mla_w_uq, mla_w_ukv, nsa_cmp_w1, nsa_cmp_w2, nsa_cmp_pos, w_out, ln1_g, ln1_b, mem_wq, mem_wkv, mem_wo, ln2_g, ln2_b, mlp_w1, mlp_w2, ln3_g, ln3_b):
    batch, seq, d = x.shape
    n = batch * seq
    mla_tabs = _rope_tables(seq, MLA_ROPE)
    rot_tabs = _rope_tables(seq, PARTIAL_ROT)
    mem2 = mem.reshape(batch * mem.shape[1], d)
    w_in_packed = _pack_w_in(w_in)

    h = _layer_norm(x.reshape(n, d), ln_in_g, ln_in_b)
    for l in range(DEPTH):
        proj = _matmul(h, w_in_packed[l], F32, name="in_proj")
        q, k, v = _mla_up(proj, mla_q_norm[l], mla_kv_norm[l], _pack_w_uq(mla_w_uq[l]), _pack_w_ukv(mla_w_ukv[l]),
                          mla_tabs, batch, seq)
        o_a = _mla_attn(q, k, v).reshape(n, MLA_HEADS * HEAD_DIM)
        o_b = _nsa(proj, nsa_cmp_w1[l].astype(BF16), nsa_cmp_w2[l].astype(BF16), nsa_cmp_pos[l], rot_tabs, batch, seq)
        o_c = _moba(proj, rot_tabs, batch, seq)
        h = _out_ln([o_a, o_b, o_c], w_out[l].astype(BF16), h, ln1_g[l], ln1_b[l], name="mix_out_ln")

        wq_scaled = (mem_wq[l] * (d // MEM_HEADS) ** -0.5).astype(BF16)
        xq = _matmul(h, wq_scaled, BF16, name="mem_q")
        xkv = _matmul(mem2, mem_wkv[l].astype(BF16), BF16, name="mem_kv")
        ctx = _xattn(xq, xkv, batch, seq)
        h = _out_ln([ctx], mem_wo[l].astype(BF16), h, ln2_g[l], ln2_b[l], name="mem_out_ln")

        h = _mlp_ln(h, mlp_w1[l].astype(BF16), mlp_w2[l].astype(BF16), ln3_g[l], ln3_b[l])
    return h.reshape(batch, seq, d)
```

```python
import functools

import numpy as np
import jax
import jax.numpy as jnp
from jax import lax
from jax.experimental import pallas as pl
from jax.experimental.pallas import tpu as pltpu

F32 = jnp.float32
BF16 = jnp.bfloat16
NEG_INF = float("-inf")

D_MODEL = 2048
DEPTH = 2
HEAD_DIM = 128
MLA_HEADS = 8
NSA_HEADS = 4
MOBA_HEADS = 4
ROPE_THETA = 500000.0
PARTIAL_ROT = HEAD_DIM // 4
MLA_Q_RANK = 512
MLA_KV_RANK = 512
MLA_NOPE = 128
MLA_ROPE = 64
MLA_SLOT = 256
NSA_CMP_LEN = 32
NSA_CMP_STRIDE = 16
NSA_SEL_LEN = 64
NSA_SEL_TOPK = 16
NSA_WINDOW = 512
NSA_FORCE_SCORE = 1.0e4
MOBA_BLOCK = 256
MOBA_TOPK = 3
MEM_HEADS = 4
D_FF = 4 * D_MODEL
DEEPNORM_ALPHA = (2 * DEPTH) ** 0.25
LANES = 128

PROJ_WIDTH = 4096
COL512 = dict(c_q=0, c_kv=1, nq=2, mq=3, mk=4, mv=5)
COL128 = dict(k_rope=24, nkc=25, nvc=26, nks=27, nvs=28, nkw=29, nvw=30, ng=31)

VMEM_LIMIT = 56 * 1024 * 1024


def _cparams(*sem):
    return pltpu.CompilerParams(dimension_semantics=sem, vmem_limit_bytes=VMEM_LIMIT)


def _dot(a, b):
    return jnp.dot(a, b, preferred_element_type=F32)


def _dot_nt(a, b):
    return lax.dot_general(a, b, (((1,), (1,)), ((), ())), preferred_element_type=F32)


def _split_bf16(x):
    hi = x.astype(BF16)
    return hi, (x - hi.astype(F32)).astype(BF16)


def _ln_rows(x, g, b, eps=1e-5):
    mu = jnp.mean(x, axis=-1, keepdims=True)
    xc = x - mu
    var = jnp.mean(xc * xc, axis=-1, keepdims=True)
    return xc * lax.rsqrt(var + eps) * g + b


def _rms_rows(x, g, eps=1e-6):
    return x * lax.rsqrt(jnp.mean(x * x, axis=-1, keepdims=True) + eps) * g


def _rope128(x, c, s1, s2, half):
    return x * c + pltpu.roll(x, LANES - half, 1) * s1 + pltpu.roll(x, half, 1) * s2


def _rope_tables(n_pos, dim):
    half = dim // 2
    inv = ROPE_THETA ** (-jnp.arange(0, dim, 2, dtype=F32) / dim)
    ang = jnp.arange(n_pos, dtype=F32)[:, None] * inv[None, :]
    cos, sin = jnp.cos(ang), jnp.sin(ang)
    ones = jnp.ones((n_pos, LANES - dim), F32)
    z = lambda w: jnp.zeros((n_pos, w), F32)
    c = jnp.concatenate([cos, cos, ones], axis=1)
    s1 = jnp.concatenate([-sin, z(LANES - half)], axis=1)
    s2 = jnp.concatenate([z(half), sin, z(LANES - dim)], axis=1)
    return c, s1, s2


MASKED = -1.0e30


def _rank_rows(score, n_cand):
    row = lax.broadcasted_iota(jnp.int32, score.shape, 0)
    rank = jnp.zeros(score.shape, F32)
    for jp in range(n_cand):
        cand = score[jp:jp + 1, :]
        ahead = (cand > score) | ((cand == score) & (row > jp))
        rank = rank + jnp.where(ahead, 1.0, 0.0)
    return rank


def _rows_to_lanes(x_t, tq):
    eye = jnp.where(lax.broadcasted_iota(jnp.int32, (tq, tq), 0) == lax.broadcasted_iota(jnp.int32, (tq, tq), 1),
                    1.0, 0.0).astype(BF16)
    return _dot_nt(eye, x_t.astype(BF16))


VT_ROWS = 144
KEY_TILE = 256


def _eye(n):
    return jnp.where(lax.broadcasted_iota(jnp.int32, (n, n), 0) == lax.broadcasted_iota(jnp.int32, (n, n), 1),
                     1.0, 0.0).astype(BF16)


def _transpose_bf16(x):
    return _dot_nt(_eye(x.shape[1]), x).astype(BF16)


def _ones_rows(width):
    return jnp.where(lax.broadcasted_iota(jnp.int32, (VT_ROWS - HEAD_DIM, width), 0) == 0, 1.0, 0.0).astype(BF16)


def _flash_update_t(s_t, m, acc_t, v_t):
    m_new = jnp.maximum(m, jnp.max(s_t, axis=0, keepdims=True))
    p_t = jnp.exp(s_t - m_new).astype(BF16)
    return m_new, jnp.exp(m - m_new) * acc_t + _dot(v_t, p_t)


def _flash_update(s, m, acc, v_aug):
    m_new = jnp.maximum(m, jnp.max(s, axis=-1, keepdims=True))
    p = jnp.exp(s - m_new).astype(BF16)
    return m_new, jnp.exp(m - m_new) * acc + _dot(p, v_aug)


def _ln_kernel(x_ref, g_ref, b_ref, o_ref):
    o_ref[...] = _ln_rows(x_ref[...], g_ref[...], b_ref[...])


def _layer_norm(x, g, b, tm=512):
    n, d = x.shape
    return pl.pallas_call(
        _ln_kernel,
        grid=(n // tm,),
        in_specs=[pl.BlockSpec((tm, d), lambda i: (i, 0)),
                  pl.BlockSpec((1, d), lambda i: (0, 0)),
                  pl.BlockSpec((1, d), lambda i: (0, 0))],
        out_specs=pl.BlockSpec((tm, d), lambda i: (i, 0)),
        out_shape=jax.ShapeDtypeStruct((n, d), F32),
        compiler_params=_cparams("parallel"),
        name="ln_in",
    )(x, g.reshape(1, d), b.reshape(1, d))


def _mm_kernel(a_ref, w_ref, o_ref, abf_ref):
    @pl.when(pl.program_id(1) == 0)
    def _():
        abf_ref[...] = a_ref[...].astype(BF16)

    o_ref[...] = _dot(abf_ref[...], w_ref[...]).astype(o_ref.dtype)


def _matmul(a, w, out_dtype, tm=1024, tn=512, name="mm"):
    m, k = a.shape
    n = w.shape[1]
    tm = min(tm, m)
    return pl.pallas_call(
        _mm_kernel,
        grid=(m // tm, n // tn),
        in_specs=[pl.BlockSpec((tm, k), lambda i, j: (i, 0)),
                  pl.BlockSpec((k, tn), lambda i, j: (0, j))],
        out_specs=pl.BlockSpec((tm, tn), lambda i, j: (i, j)),
        out_shape=jax.ShapeDtypeStruct((m, n), out_dtype),
        scratch_shapes=[pltpu.VMEM((tm, k), BF16)],
        compiler_params=_cparams("parallel", "arbitrary"),
        name=name,
    )(a, w)


def _out_ln_kernel(*refs, widths):
    n_a = len(widths)
    a_refs = refs[:n_a]
    w_ref, h_ref, g_ref, b_ref, o_ref = refs[n_a:]
    acc = DEEPNORM_ALPHA * h_ref[...]
    off = 0
    for a_ref, wd in zip(a_refs, widths):
        acc = acc + _dot(a_ref[...], w_ref[off:off + wd, :])
        off += wd
    o_ref[...] = _ln_rows(acc, g_ref[...], b_ref[...])


def _out_ln(a_list, w, h, g, b, tm=512, name="out_ln"):
    n, d = h.shape
    widths = tuple(a.shape[1] for a in a_list)
    k = sum(widths)
    in_specs = [pl.BlockSpec((tm, wd), lambda i: (i, 0)) for wd in widths]
    in_specs += [pl.BlockSpec((k, d), lambda i: (0, 0)),
                 pl.BlockSpec((tm, d), lambda i: (i, 0)),
                 pl.BlockSpec((1, d), lambda i: (0, 0)),
                 pl.BlockSpec((1, d), lambda i: (0, 0))]
    return pl.pallas_call(
        functools.partial(_out_ln_kernel, widths=widths),
        grid=(n // tm,),
        in_specs=in_specs,
        out_specs=pl.BlockSpec((tm, d), lambda i: (i, 0)),
        out_shape=jax.ShapeDtypeStruct((n, d), F32),
        compiler_params=_cparams("parallel"),
        name=name,
    )(*a_list, w, h, g.reshape(1, d), b.reshape(1, d))


def _mlp_kernel(h_ref, w1_ref, w2_ref, g_ref, b_ref, o_ref, hbf_ref, acc_ref):
    f = pl.program_id(1)

    @pl.when(f == 0)
    def _():
        hbf_ref[...] = h_ref[...].astype(BF16)
        acc_ref[...] = jnp.zeros_like(acc_ref)

    u = jnp.maximum(_dot(hbf_ref[...], w1_ref[...]), 0.0)
    acc_ref[...] += _dot((u * u).astype(BF16), w2_ref[...])

    @pl.when(f == pl.num_programs(1) - 1)
    def _():
        y = DEEPNORM_ALPHA * h_ref[...] + acc_ref[...]
        o_ref[...] = _ln_rows(y, g_ref[...], b_ref[...])


def _mlp_ln(h, w1, w2, g, b, tm=512, tf=512):
    n, d = h.shape
    dff = w1.shape[1]
    return pl.pallas_call(
        _mlp_kernel,
        grid=(n // tm, dff // tf),
        in_specs=[pl.BlockSpec((tm, d), lambda i, f: (i, 0)),
                  pl.BlockSpec((d, tf), lambda i, f: (0, f)),
                  pl.BlockSpec((tf, d), lambda i, f: (f, 0)),
                  pl.BlockSpec((1, d), lambda i, f: (0, 0)),
                  pl.BlockSpec((1, d), lambda i, f: (0, 0))],
        out_specs=pl.BlockSpec((tm, d), lambda i, f: (i, 0)),
        out_shape=jax.ShapeDtypeStruct((n, d), F32),
        scratch_shapes=[pltpu.VMEM((tm, d), BF16), pltpu.VMEM((tm, d), F32)],
        compiler_params=_cparams("parallel", "arbitrary"),
        name="mlp_ln",
    )(h, w1, w2, g.reshape(1, d), b.reshape(1, d))


def _mla_up_kernel(cq_ref, ckv_ref, kr_ref, gq_ref, gkv_ref, wq_ref, wkv_ref, tc_ref, ts1_ref, ts2_ref,
                   q_ref, k_ref, v_ref, *, scale):
    half = MLA_ROPE // 2
    c, s1, s2 = tc_ref[...], ts1_ref[...], ts2_ref[...]
    nq = _rms_rows(cq_ref[...], gq_ref[...]).astype(BF16)
    nkv = _rms_rows(ckv_ref[...], gkv_ref[...]).astype(BF16)
    qf = _dot(nq, wq_ref[...])
    kvf = _dot(nkv, wkv_ref[...])
    kr = _rope128(kr_ref[...], c, s1, s2, half).astype(BF16)
    ones_rows = _ones_rows(KEY_TILE)
    for h in range(MLA_HEADS):
        o = h * MLA_SLOT
        q = jnp.concatenate([qf[:, o:o + LANES], _rope128(qf[:, o + LANES:o + MLA_SLOT], c, s1, s2, half)], axis=1)
        q_ref[0, h] = _transpose_bf16((q * scale).astype(BF16))
        k_ref[0, h, :, 0:LANES] = kvf[:, h * LANES:(h + 1) * LANES].astype(BF16)
        k_ref[0, h, :, LANES:] = kr
        v = kvf[:, (MLA_HEADS + h) * LANES:(MLA_HEADS + h + 1) * LANES].astype(BF16)
        for j in range(kr.shape[0] // KEY_TILE):
            v_ref[0, h, j, 0:HEAD_DIM, :] = _transpose_bf16(v[j * KEY_TILE:(j + 1) * KEY_TILE])
            v_ref[0, h, j, HEAD_DIM:, :] = ones_rows


def _mla_up(proj, gq, gkv, wq, wkv, tabs, batch, seq, tm=512):
    nt = seq // tm
    row = lambda b, i: b * nt + i
    tab_spec = pl.BlockSpec((tm, LANES), lambda b, i: (i, 0))
    return pl.pallas_call(
        functools.partial(_mla_up_kernel, scale=(MLA_NOPE + MLA_ROPE) ** -0.5),
        grid=(batch, nt),
        in_specs=[pl.BlockSpec((tm, 512), lambda b, i: (row(b, i), COL512["c_q"])),
                  pl.BlockSpec((tm, 512), lambda b, i: (row(b, i), COL512["c_kv"])),
                  pl.BlockSpec((tm, LANES), lambda b, i: (row(b, i), COL128["k_rope"])),
                  pl.BlockSpec((1, MLA_Q_RANK), lambda b, i: (0, 0)),
                  pl.BlockSpec((1, MLA_KV_RANK), lambda b, i: (0, 0)),
                  pl.BlockSpec(wq.shape, lambda b, i: (0, 0)),
                  pl.BlockSpec(wkv.shape, lambda b, i: (0, 0)),
                  tab_spec, tab_spec, tab_spec],
        out_specs=[pl.BlockSpec((1, MLA_HEADS, MLA_SLOT, tm), lambda b, i: (b, 0, 0, i)),
                   pl.BlockSpec((1, MLA_HEADS, tm, MLA_SLOT), lambda b, i: (b, 0, i, 0)),
                   pl.BlockSpec((1, MLA_HEADS, tm // KEY_TILE, VT_ROWS, KEY_TILE), lambda b, i: (b, 0, i, 0, 0))],
        out_shape=[jax.ShapeDtypeStruct((batch, MLA_HEADS, MLA_SLOT, seq), BF16),
                   jax.ShapeDtypeStruct((batch, MLA_HEADS, seq, MLA_SLOT), BF16),
                   jax.ShapeDtypeStruct((batch, MLA_HEADS, seq // KEY_TILE, VT_ROWS, KEY_TILE), BF16)],
        compiler_params=_cparams("parallel", "parallel"),
        name="mla_up",
    )(proj, proj, proj, gq.reshape(1, -1), gkv.reshape(1, -1), wq, wkv, *tabs)


def _mla_attn_kernel(q_ref, k_ref, v_ref, o_ref, *, tq, tk):
    qi = pl.program_id(1)
    heads = q_ref.shape[1]
    sub = tq // tk
    key = lax.broadcasted_iota(jnp.int32, (tk, tq), 0)
    qry = lax.broadcasted_iota(jnp.int32, (tk, tq), 1)

    def step(kb, carry, mask):
        r = pl.ds(pl.multiple_of(kb * tk, tk), tk)
        out = []
        for h in range(heads):
            s_t = _dot(k_ref[0, h, r, :], q_ref[0, h])
            if mask is not None:
                s_t = jnp.where(mask, s_t, MASKED)
            out.append(_flash_update_t(s_t, *carry[h], v_ref[0, h, kb]))
        return tuple(out)

    init = tuple((jnp.full((1, tq), NEG_INF, F32), jnp.zeros((VT_ROWS, tq), F32)) for _ in range(heads))
    carry = lax.fori_loop(0, qi * sub, lambda kb, c: step(kb, c, None), init)
    for j in range(sub):
        carry = step(qi * sub + j, carry, key + j * tk <= qry)
    eye = _eye(tq)
    for h in range(heads):
        acc_t = carry[h][1]
        o_t = (acc_t[0:HEAD_DIM] / acc_t[HEAD_DIM:HEAD_DIM + 1]).astype(BF16)
        o_ref[0, :, h * HEAD_DIM:(h + 1) * HEAD_DIM] = _dot_nt(eye, o_t).astype(o_ref.dtype)


def _mla_attn(q_t, k, v_t, tq=512, tk=KEY_TILE):
    batch, heads, seq, _ = k.shape
    return pl.pallas_call(
        functools.partial(_mla_attn_kernel, tq=tq, tk=tk),
        grid=(batch, seq // tq),
        in_specs=[pl.BlockSpec((1, heads, MLA_SLOT, tq), lambda b, i: (b, 0, 0, i)),
                  pl.BlockSpec((1, heads, seq, MLA_SLOT), lambda b, i: (b, 0, 0, 0)),
                  pl.BlockSpec((1, heads, seq // tk, VT_ROWS, tk), lambda b, i: (b, 0, 0, 0, 0))],
        out_specs=pl.BlockSpec((1, tq, heads * HEAD_DIM), lambda b, i: (b, i, 0)),
        out_shape=jax.ShapeDtypeStruct((batch, seq, heads * HEAD_DIM), BF16),
        compiler_params=_cparams("parallel", "arbitrary"),
        name="mla_attn",
    )(q_t, k, v_t)


def _nsa_kernel(nq_ref, kc_ref, vc_ref, ks_ref, vs_ref, kw_ref, vw_ref, ng_ref,
                w1_ref, w2_ref, pos_ref, ovt_ref, tc_ref, ts1_ref, ts2_ref,
                o_ref,
                tmp_sc, kc_sc, vc_sc, ks_sc, vs_sc, kw_sc, vw_sc, *, tq, seq, scale):
    qi = pl.program_id(1)
    half = PARTIAL_ROT // 2
    heads = NSA_HEADS
    n_chunk = seq // NSA_CMP_STRIDE
    n_sel = seq // NSA_SEL_LEN

    @pl.when(qi == 0)
    def _prep():
        c, s1, s2 = tc_ref[...], ts1_ref[...], ts2_ref[...]
        ones = jnp.ones((seq, LANES), BF16)
        blk = lax.broadcasted_iota(jnp.int32, (seq, LANES), 0) >> 6
        ks_sc[:, 0:LANES] = _rope128(ks_ref[...], c, s1, s2, half).astype(BF16)
        ks_sc[:, LANES:] = jnp.where(blk == lax.broadcasted_iota(jnp.int32, (seq, LANES), 1), 1.0, 0.0).astype(BF16)
        kw_sc[...] = _rope128(kw_ref[...], c, s1, s2, half).astype(BF16)
        vs_sc[:, 0:LANES] = vs_ref[...].astype(BF16)
        vs_sc[:, LANES:] = ones
        vw_sc[:, 0:LANES] = vw_ref[...].astype(BF16)
        vw_sc[:, LANES:] = ones
        for i, (src, dst) in enumerate(((kc_ref, kc_sc), (vc_ref, vc_sc))):
            tmp_sc[...] = _rope128(src[...], c, s1, s2, half) if i == 0 else src[...]
            lo, hi = [], []
            for t in range(NSA_CMP_STRIDE):
                x = tmp_sc[pl.ds(t, n_chunk, stride=NSA_CMP_STRIDE), :]
                lo.append((x + pos_ref[i, t:t + 1, :]).astype(BF16))
                hi.append((x + pos_ref[i, NSA_CMP_STRIDE + t:NSA_CMP_STRIDE + t + 1, :]).astype(BF16))
            kw1 = NSA_CMP_STRIDE * HEAD_DIM
            a = _dot(jnp.concatenate(lo, axis=1), w1_ref[i, 0:kw1, :])
            bm = _dot(jnp.concatenate(hi, axis=1), w1_ref[i, kw1:2 * kw1, :])
            hid = jax.nn.gelu(a + pltpu.roll(bm, n_chunk - 1, 0))
            dst[...] = _dot(hid.astype(BF16), w2_ref[i]).astype(BF16)

    t0 = pl.multiple_of(qi * tq, tq)
    rq = pl.ds(t0, tq)
    c, s1, s2 = tc_ref[rq, :], ts1_ref[rq, :], ts2_ref[rq, :]
    qf = nq_ref[...]
    qs = jnp.concatenate(
        [_rope128(qf[:, h * LANES:(h + 1) * LANES], c, s1, s2, half) * scale for h in range(heads)],
        axis=0)
    qs_bf = qs.astype(BF16)

    lane = lax.broadcasted_iota(jnp.int32, (tq, LANES), 1)
    tpos = t0 + lax.broadcasted_iota(jnp.int32, (tq, LANES), 0)

    s = _dot_nt(qs_bf, kc_sc[...]).reshape(heads, tq, LANES)
    cmask = (lane * NSA_CMP_STRIDE + (NSA_CMP_LEN - 1) <= tpos)[None]
    s = jnp.where(cmask, s, NEG_INF)
    m = jnp.max(s, axis=-1, keepdims=True)
    m = jnp.where(m == NEG_INF, 0.0, m)
    e = jnp.exp(s - m)
    p_cmp = e / jnp.maximum(jnp.sum(e, axis=-1, keepdims=True), 1e-30)
    o_cmp = _dot(p_cmp.reshape(heads * tq, LANES).astype(BF16), vc_sc[...]).reshape(heads, tq, HEAD_DIM)

    p_hi, p_lo = _split_bf16(p_cmp[0] + p_cmp[1] + p_cmp[2] + p_cmp[3])
    imp_t = (_dot_nt(ovt_ref[...], p_hi) + _dot_nt(ovt_ref[...], p_lo))[0:n_sel]
    blk_t = lax.broadcasted_iota(jnp.int32, (n_sel, tq), 0)
    cur_t = (t0 + lax.broadcasted_iota(jnp.int32, (n_sel, tq), 1)) >> 6
    eligible = blk_t <= cur_t
    forced = (blk_t == 0) | (blk_t == cur_t) | (blk_t == cur_t - 1)
    score = jnp.where(eligible, jnp.where(forced, NSA_FORCE_SCORE, imp_t), NEG_INF)
    rank = _rank_rows(score, n_sel)
    bias_t = jnp.where(eligible & (rank < NSA_SEL_TOPK), 0.0, MASKED)
    bias_t = jnp.concatenate([bias_t, jnp.zeros((LANES - n_sel, tq), F32)], axis=0)
    bias = _rows_to_lanes(bias_t, tq)
    q_aug = jnp.concatenate([qs, jnp.concatenate([bias] * heads, axis=0)], axis=1).astype(BF16)

    local_r = lax.broadcasted_iota(jnp.int32, (tq, tq), 0)
    local_c = lax.broadcasted_iota(jnp.int32, (tq, tq), 1)
    causal = (local_c <= local_r)[None]

    def masked3(s, mask):
        return jnp.where(mask, s.reshape(heads, tq, tq), MASKED).reshape(heads * tq, tq)

    init = (jnp.full((heads * tq, 1), NEG_INF, F32), jnp.zeros((heads * tq, 2 * HEAD_DIM), F32))

    def sel_body(kb, carry):
        r = pl.ds(pl.multiple_of(kb * tq, tq), tq)
        return _flash_update(_dot_nt(q_aug, ks_sc[r, :]), *carry, vs_sc[r, :])

    carry = lax.fori_loop(0, qi, sel_body, init)
    _, acc = _flash_update(masked3(_dot_nt(q_aug, ks_sc[rq, :]), causal), *carry, vs_sc[rq, :])
    o_sel = (acc[:, :HEAD_DIM] / acc[:, HEAD_DIM:]).reshape(heads, tq, HEAD_DIM)

    carry = init
    for back, mask in ((2, (local_c > local_r)[None]), (1, None), (0, causal)):
        kb = qi - back
        r = pl.ds(pl.multiple_of(jnp.maximum(kb, 0) * tq, tq), tq)
        s = _dot_nt(qs_bf, kw_sc[r, :])
        if back:
            valid = kb >= 0
            mask = valid if mask is None else mask & valid
        carry = _flash_update(masked3(s, mask), *carry, vw_sc[r, :])
    acc = carry[1]
    o_win = (acc[:, :HEAD_DIM] / acc[:, HEAD_DIM:]).reshape(heads, tq, HEAD_DIM)

    g = jax.nn.sigmoid(ng_ref[...])
    for h in range(heads):
        o = (g[:, 3 * h:3 * h + 1] * o_cmp[h] + g[:, 3 * h + 1:3 * h + 2] * o_sel[h]
             + g[:, 3 * h + 2:3 * h + 3] * o_win[h])
        o_ref[:, h * LANES:(h + 1) * LANES] = o.astype(o_ref.dtype)


def _nsa(proj, w1, w2, pos, tabs, batch, seq, tq=256):
    nt = seq // tq
    n_cmp = (seq - NSA_CMP_LEN) // NSA_CMP_STRIDE + 1
    n_sel = seq // NSA_SEL_LEN
    starts = np.arange(LANES) * NSA_CMP_STRIDE
    sel_start = np.arange(LANES) * NSA_SEL_LEN
    overlap = ((starts[:, None] < sel_start[None, :] + NSA_SEL_LEN)
               & (starts[:, None] + NSA_CMP_LEN > sel_start[None, :])
               & (np.arange(LANES)[:, None] < n_cmp) & (np.arange(LANES)[None, :] < n_sel))
    ovt = jnp.asarray(overlap.T.astype(np.float32), BF16)
    seq_col = lambda name: pl.BlockSpec((seq, LANES), lambda b, i: (b, COL128[name]))
    full = lambda a: pl.BlockSpec(a.shape, lambda b, i: (0,) * a.ndim)
    return pl.pallas_call(
        functools.partial(_nsa_kernel, tq=tq, seq=seq, scale=HEAD_DIM ** -0.5),
        grid=(batch, nt),
        in_specs=[pl.BlockSpec((tq, 512), lambda b, i: (b * nt + i, COL512["nq"])),
                  seq_col("nkc"), seq_col("nvc"), seq_col("nks"), seq_col("nvs"), seq_col("nkw"), seq_col("nvw"),
                  pl.BlockSpec((tq, LANES), lambda b, i: (b * nt + i, COL128["ng"])),
                  full(w1), full(w2), full(pos), full(ovt), full(tabs[0]), full(tabs[1]), full(tabs[2])],
        out_specs=pl.BlockSpec((tq, NSA_HEADS * HEAD_DIM), lambda b, i: (b * nt + i, 0)),
        out_shape=jax.ShapeDtypeStruct((batch * seq, NSA_HEADS * HEAD_DIM), BF16),
        scratch_shapes=[pltpu.VMEM((seq, LANES), F32),
                        pltpu.VMEM((LANES, LANES), BF16), pltpu.VMEM((LANES, LANES), BF16),
                        pltpu.VMEM((seq, 2 * LANES), BF16), pltpu.VMEM((seq, 2 * LANES), BF16),
                        pltpu.VMEM((seq, LANES), BF16), pltpu.VMEM((seq, 2 * LANES), BF16)],
        compiler_params=_cparams("parallel", "arbitrary"),
        name="nsa",
    )(proj, proj, proj, proj, proj, proj, proj, proj, w1, w2, pos, ovt, *tabs)


def _moba_kernel(mq_ref, mk_ref, mv_ref, tc_ref, ts1_ref, ts2_ref, o_ref, k_sc, v_sc, km_sc, *, tq, seq, scale):
    qi = pl.program_id(1)
    half = PARTIAL_ROT // 2
    heads = MOBA_HEADS
    n_blk = seq // MOBA_BLOCK
    slot = 2 * LANES

    @pl.when(qi == 0)
    def _prep():
        c, s1, s2 = tc_ref[...], ts1_ref[...], ts2_ref[...]
        km_sc[...] = jnp.zeros_like(km_sc)
        ones = jnp.ones((seq, LANES), BF16)
        blk = lax.broadcasted_iota(jnp.int32, (seq, LANES), 0) >> 8
        onehot = jnp.where(blk == lax.broadcasted_iota(jnp.int32, (seq, LANES), 1), 1.0, 0.0).astype(BF16)
        for h in range(heads):
            hs = slice(h * LANES, (h + 1) * LANES)
            kh = _rope128(mk_ref[:, hs], c, s1, s2, half)
            k_sc[:, h * slot:h * slot + LANES] = kh.astype(BF16)
            k_sc[:, h * slot + LANES:(h + 1) * slot] = onehot
            v_sc[:, h * slot:h * slot + LANES] = mv_ref[:, hs].astype(BF16)
            v_sc[:, h * slot + LANES:(h + 1) * slot] = ones
            for j in range(n_blk):
                km_sc[h * n_blk + j:h * n_blk + j + 1, :] = jnp.mean(
                    kh[j * MOBA_BLOCK:(j + 1) * MOBA_BLOCK], axis=0, keepdims=True)

    t0 = pl.multiple_of(qi * tq, tq)
    rq = pl.ds(t0, tq)
    c, s1, s2 = tc_ref[rq, :], ts1_ref[rq, :], ts2_ref[rq, :]
    km_hi, km_lo = _split_bf16(km_sc[...])
    tk = MOBA_BLOCK
    sub = tq // tk
    blk_t = lax.broadcasted_iota(jnp.int32, (n_blk, tq), 0)
    cur_t = (t0 + lax.broadcasted_iota(jnp.int32, (n_blk, tq), 1)) >> 8
    row = lax.broadcasted_iota(jnp.int32, (tq, tk), 0)
    col = lax.broadcasted_iota(jnp.int32, (tq, tk), 1)

    q_aug = []
    for h in range(heads):
        qf = _rope128(mq_ref[:, h * LANES:(h + 1) * LANES], c, s1, s2, half) * scale
        q_hi, q_lo = _split_bf16(qf)
        gate_t = (_dot_nt(km_hi, q_hi) + (_dot_nt(km_lo, q_hi) + _dot_nt(km_hi, q_lo)))[h * n_blk:(h + 1) * n_blk]
        eligible = blk_t < cur_t
        score = jnp.where(eligible, gate_t, NEG_INF)
        picked = eligible & (_rank_rows(score, n_blk) < MOBA_TOPK)
        bias_t = jnp.where(picked | (blk_t == cur_t), 0.0, MASKED)
        bias_t = jnp.concatenate([bias_t, jnp.zeros((LANES - n_blk, tq), F32)], axis=0)
        q_aug.append(jnp.concatenate([qf, _rows_to_lanes(bias_t, tq)], axis=1).astype(BF16))

    def step(kb, carry, mask):
        r = pl.ds(pl.multiple_of(kb * tk, tk), tk)
        out = []
        for h in range(heads):
            s = _dot_nt(q_aug[h], k_sc[r, h * slot:(h + 1) * slot])
            if mask is not None:
                s = jnp.where(mask, s, MASKED)
            out.append(_flash_update(s, *carry[h], v_sc[r, h * slot:(h + 1) * slot]))
        return tuple(out)

    init = tuple((jnp.full((tq, 1), NEG_INF, F32), jnp.zeros((tq, 2 * HEAD_DIM), F32)) for _ in range(heads))
    carry = lax.fori_loop(0, qi * sub, lambda kb, cr: step(kb, cr, None), init)
    for j in range(sub):
        carry = step(qi * sub + j, carry, col + j * tk <= row)
    for h in range(heads):
        acc = carry[h][1]
        o_ref[:, h * LANES:(h + 1) * LANES] = (acc[:, :HEAD_DIM] / acc[:, HEAD_DIM:]).astype(o_ref.dtype)


def _moba(proj, tabs, batch, seq):
    tq = 2 * MOBA_BLOCK
    nt = seq // tq
    width = MOBA_HEADS * HEAD_DIM
    full = lambda a: pl.BlockSpec(a.shape, lambda b, i: (0,) * a.ndim)
    return pl.pallas_call(
        functools.partial(_moba_kernel, tq=tq, seq=seq, scale=HEAD_DIM ** -0.5),
        grid=(batch, nt),
        in_specs=[pl.BlockSpec((tq, width), lambda b, i: (b * nt + i, COL512["mq"])),
                  pl.BlockSpec((seq, width), lambda b, i: (b, COL512["mk"])),
                  pl.BlockSpec((seq, width), lambda b, i: (b, COL512["mv"])),
                  full(tabs[0]), full(tabs[1]), full(tabs[2])],
        out_specs=pl.BlockSpec((tq, width), lambda b, i: (b * nt + i, 0)),
        out_shape=jax.ShapeDtypeStruct((batch * seq, width), BF16),
        scratch_shapes=[pltpu.VMEM((seq, 2 * width), BF16), pltpu.VMEM((seq, 2 * width), BF16),
                        pltpu.VMEM((LANES, LANES), F32)],
        compiler_params=_cparams("parallel", "arbitrary"),
        name="moba",
    )(proj, proj, proj, *tabs)


def _xattn_kernel(q_ref, kv_ref, o_ref):
    d = q_ref.shape[1]
    hd = d // MEM_HEADS
    for h in range(MEM_HEADS):
        hs = slice(h * hd, (h + 1) * hd)
        s = _dot_nt(q_ref[:, hs], kv_ref[0, :, hs])
        e = jnp.exp(s - jnp.max(s, axis=-1, keepdims=True))
        p = e / jnp.sum(e, axis=-1, keepdims=True)
        o_ref[:, hs] = _dot(p.astype(BF16), kv_ref[0, :, d + h * hd:d + (h + 1) * hd]).astype(o_ref.dtype)


def _xattn(q, kv, batch, seq, tq=512):
    d = q.shape[1]
    nt = seq // tq
    m_len = kv.shape[0] // batch
    return pl.pallas_call(
        _xattn_kernel,
        grid=(batch, nt),
        in_specs=[pl.BlockSpec((tq, d), lambda b, i: (b * nt + i, 0)),
                  pl.BlockSpec((1, m_len, 2 * d), lambda b, i: (b, 0, 0))],
        out_specs=pl.BlockSpec((tq, d), lambda b, i: (b * nt + i, 0)),
        out_shape=jax.ShapeDtypeStruct((batch * seq, d), BF16),
        compiler_params=_cparams("parallel", "parallel"),
        name="xattn",
    )(q, kv.reshape(batch, m_len, 2 * d))


IN_SPLIT_NAMES = ("c_q", "c_kv", "k_rope", "nq", "nkc", "nvc", "nks", "nvs", "nkw", "nvw", "ng", "mq", "mk", "mv")
IN_SPLIT_SIZES = (512, 512, 64, 512, 128, 128, 128, 128, 128, 128, 12, 512, 512, 512)


def _pack_w_in_kernel(w_ref, o_ref):
    rows = w_ref.shape[0]
    off = 0
    for name, sz in zip(IN_SPLIT_NAMES, IN_SPLIT_SIZES):
        width = 512 if name in COL512 else LANES
        dst = COL512[name] * 512 if name in COL512 else COL128[name] * LANES
        x = w_ref[:, off:off + sz].astype(BF16)
        if sz < width:
            x = jnp.concatenate([x, jnp.zeros((rows, width - sz), BF16)], axis=1)
        o_ref[0, :, dst:dst + width] = x
        off += sz


def _pack_w_in(w, tr=256):
    layers, d, width = w.shape
    return pl.pallas_call(
        _pack_w_in_kernel,
        grid=(layers, d // tr),
        in_specs=[pl.BlockSpec((None, tr, width), lambda l, i: (l, i, 0))],
        out_specs=pl.BlockSpec((1, tr, PROJ_WIDTH), lambda l, i: (l, i, 0)),
        out_shape=jax.ShapeDtypeStruct((layers, d, PROJ_WIDTH), BF16),
        compiler_params=_cparams("parallel", "parallel"),
        name="pack_w_in",
    )(w)


def _pack_w_uq(w):
    r = w.shape[0]
    w = w.reshape(r, MLA_HEADS, MLA_NOPE + MLA_ROPE)
    w = jnp.pad(w, ((0, 0), (0, 0), (0, MLA_SLOT - MLA_NOPE - MLA_ROPE)))
    return w.reshape(r, MLA_HEADS * MLA_SLOT).astype(BF16)


def _pack_w_ukv(w):
    r = w.shape[0]
    w = w.reshape(r, MLA_HEADS, 2, HEAD_DIM).transpose(0, 2, 1, 3)
    return w.reshape(r, 2 * MLA_HEADS * HEAD_DIM).astype(BF16)


def kernel(x, mem, ln_in_g, ln_in_b, w_in, mla_q_norm, mla_kv_norm, mla_w_uq, mla_w_ukv, nsa_cmp_w1, nsa_cmp_w2, nsa_cmp_pos, w_out, ln1_g, ln1_b, mem_wq, mem_wkv, mem_wo, ln2_g, ln2_b, mlp_w1, mlp_w2, ln3_g, ln3_b):
    batch, seq, d = x.shape
    n = batch * seq
    mla_tabs = _rope_tables(seq, MLA_ROPE)
    rot_tabs = _rope_tables(seq, PARTIAL_ROT)
    mem2 = mem.reshape(batch * mem.shape[1], d)
    w_in_packed = _pack_w_in(w_in)

    h = _layer_norm(x.reshape(n, d), ln_in_g, ln_in_b)
    for l in range(DEPTH):
        proj = _matmul(h, w_in_packed[l], F32, name="in_proj")
        q, k, v = _mla_up(proj, mla_q_norm[l], mla_kv_norm[l], _pack_w_uq(mla_w_uq[l]), _pack_w_ukv(mla_w_ukv[l]),
                          mla_tabs, batch, seq)
        o_a = _mla_attn(q, k, v).reshape(n, MLA_HEADS * HEAD_DIM)
        o_b = _nsa(proj, nsa_cmp_w1[l].astype(BF16), nsa_cmp_w2[l].astype(BF16), nsa_cmp_pos[l], rot_tabs, batch, seq)
        o_c = _moba(proj, rot_tabs, batch, seq)
        h = _out_ln([o_a, o_b, o_c], w_out[l].astype(BF16), h, ln1_g[l], ln1_b[l], name="mix_out_ln")

        wq_scaled = (mem_wq[l] * (d // MEM_HEADS) ** -0.5).astype(BF16)
        xq = _matmul(h, wq_scaled, BF16, name="mem_q")
        xkv = _matmul(mem2, mem_wkv[l].astype(BF16), BF16, name="mem_kv")
        ctx = _xattn(xq, xkv, batch, seq)
        h = _out_ln([ctx], mem_wo[l].astype(BF16), h, ln2_g[l], ln2_b[l], name="mem_out_ln")

        h = _mlp_ln(h, mlp_w1[l].astype(BF16), mlp_w2[l].astype(BF16), ln3_g[l], ln3_b[l])
    return h.reshape(batch, seq, d)
```

```python
import functools

import numpy as np
import jax
import jax.numpy as jnp
from jax import lax
from jax.experimental import pallas as pl
from jax.experimental.pallas import tpu as pltpu

F32 = jnp.float32
BF16 = jnp.bfloat16
NEG_INF = float("-inf")

D_MODEL = 2048
DEPTH = 2
HEAD_DIM = 128
MLA_HEADS = 8
NSA_HEADS = 4
MOBA_HEADS = 4
ROPE_THETA = 500000.0
PARTIAL_ROT = HEAD_DIM // 4
MLA_Q_RANK = 512
MLA_KV_RANK = 512
MLA_NOPE = 128
MLA_ROPE = 64
MLA_SLOT = 256
NSA_CMP_LEN = 32
NSA_CMP_STRIDE = 16
NSA_SEL_LEN = 64
NSA_SEL_TOPK = 16
NSA_WINDOW = 512
NSA_FORCE_SCORE = 1.0e4
MOBA_BLOCK = 256
MOBA_TOPK = 3
MEM_HEADS = 4
D_FF = 4 * D_MODEL
DEEPNORM_ALPHA = (2 * DEPTH) ** 0.25
LANES = 128

PROJ_WIDTH = 4096
COL512 = dict(c_q=0, c_kv=1, nq=2, mq=3, mk=4, mv=5)
COL128 = dict(k_rope=24, nkc=25, nvc=26, nks=27, nvs=28, nkw=29, nvw=30, ng=31)

VMEM_LIMIT = 56 * 1024 * 1024


def _cparams(*sem):
    return pltpu.CompilerParams(dimension_semantics=sem, vmem_limit_bytes=VMEM_LIMIT)


def _dot(a, b):
    return jnp.dot(a, b, preferred_element_type=F32)


def _dot_nt(a, b):
    return lax.dot_general(a, b, (((1,), (1,)), ((), ())), preferred_element_type=F32)


def _split_bf16(x):
    hi = x.astype(BF16)
    return hi, (x - hi.astype(F32)).astype(BF16)


def _ln_rows(x, g, b, eps=1e-5):
    mu = jnp.mean(x, axis=-1, keepdims=True)
    xc = x - mu
    var = jnp.mean(xc * xc, axis=-1, keepdims=True)
    return xc * lax.rsqrt(var + eps) * g + b


def _rms_rows(x, g, eps=1e-6):
    return x * lax.rsqrt(jnp.mean(x * x, axis=-1, keepdims=True) + eps) * g


def _rope128(x, c, s1, s2, half):
    return x * c + pltpu.roll(x, LANES - half, 1) * s1 + pltpu.roll(x, half, 1) * s2


def _rope_tables(n_pos, dim):
    half = dim // 2
    inv = ROPE_THETA ** (-jnp.arange(0, dim, 2, dtype=F32) / dim)
    ang = jnp.arange(n_pos, dtype=F32)[:, None] * inv[None, :]
    cos, sin = jnp.cos(ang), jnp.sin(ang)
    ones = jnp.ones((n_pos, LANES - dim), F32)
    z = lambda w: jnp.zeros((n_pos, w), F32)
    c = jnp.concatenate([cos, cos, ones], axis=1)
    s1 = jnp.concatenate([-sin, z(LANES - half)], axis=1)
    s2 = jnp.concatenate([z(half), sin, z(LANES - dim)], axis=1)
    return c, s1, s2


MASKED = -1.0e30


def _rank_rows(score, n_cand):
    row = lax.broadcasted_iota(jnp.int32, score.shape, 0)
    rank = jnp.zeros(score.shape, F32)
    for jp in range(n_cand):
        cand = score[jp:jp + 1, :]
        ahead = (cand > score) | ((cand == score) & (row > jp))
        rank = rank + jnp.where(ahead, 1.0, 0.0)
    return rank


def _rows_to_lanes(x_t, tq):
    eye = jnp.where(lax.broadcasted_iota(jnp.int32, (tq, tq), 0) == lax.broadcasted_iota(jnp.int32, (tq, tq), 1),
                    1.0, 0.0).astype(BF16)
    return _dot_nt(eye, x_t.astype(BF16))


def _flash_update(s, m, acc, v_aug):
    m_new = jnp.maximum(m, jnp.max(s, axis=-1, keepdims=True))
    p = jnp.exp(s - m_new).astype(BF16)
    return m_new, jnp.exp(m - m_new) * acc + _dot(p, v_aug)


def _ln_kernel(x_ref, g_ref, b_ref, o_ref):
    o_ref[...] = _ln_rows(x_ref[...], g_ref[...], b_ref[...])


def _layer_norm(x, g, b, tm=512):
    n, d = x.shape
    return pl.pallas_call(
        _ln_kernel,
        grid=(n // tm,),
        in_specs=[pl.BlockSpec((tm, d), lambda i: (i, 0)),
                  pl.BlockSpec((1, d), lambda i: (0, 0)),
                  pl.BlockSpec((1, d), lambda i: (0, 0))],
        out_specs=pl.BlockSpec((tm, d), lambda i: (i, 0)),
        out_shape=jax.ShapeDtypeStruct((n, d), F32),
        compiler_params=_cparams("parallel"),
        name="ln_in",
    )(x, g.reshape(1, d), b.reshape(1, d))


def _mm_kernel(a_ref, w_ref, o_ref, abf_ref):
    @pl.when(pl.program_id(1) == 0)
    def _():
        abf_ref[...] = a_ref[...].astype(BF16)

    o_ref[...] = _dot(abf_ref[...], w_ref[...]).astype(o_ref.dtype)


def _matmul(a, w, out_dtype, tm=1024, tn=512, name="mm"):
    m, k = a.shape
    n = w.shape[1]
    tm = min(tm, m)
    return pl.pallas_call(
        _mm_kernel,
        grid=(m // tm, n // tn),
        in_specs=[pl.BlockSpec((tm, k), lambda i, j: (i, 0)),
                  pl.BlockSpec((k, tn), lambda i, j: (0, j))],
        out_specs=pl.BlockSpec((tm, tn), lambda i, j: (i, j)),
        out_shape=jax.ShapeDtypeStruct((m, n), out_dtype),
        scratch_shapes=[pltpu.VMEM((tm, k), BF16)],
        compiler_params=_cparams("parallel", "arbitrary"),
        name=name,
    )(a, w)


def _out_ln_kernel(*refs, widths):
    n_a = len(widths)
    a_refs = refs[:n_a]
    w_ref, h_ref, g_ref, b_ref, o_ref = refs[n_a:]
    acc = DEEPNORM_ALPHA * h_ref[...]
    off = 0
    for a_ref, wd in zip(a_refs, widths):
        acc = acc + _dot(a_ref[...], w_ref[off:off + wd, :])
        off += wd
    o_ref[...] = _ln_rows(acc, g_ref[...], b_ref[...])


def _out_ln(a_list, w, h, g, b, tm=512, name="out_ln"):
    n, d = h.shape
    widths = tuple(a.shape[1] for a in a_list)
    k = sum(widths)
    in_specs = [pl.BlockSpec((tm, wd), lambda i: (i, 0)) for wd in widths]
    in_specs += [pl.BlockSpec((k, d), lambda i: (0, 0)),
                 pl.BlockSpec((tm, d), lambda i: (i, 0)),
                 pl.BlockSpec((1, d), lambda i: (0, 0)),
                 pl.BlockSpec((1, d), lambda i: (0, 0))]
    return pl.pallas_call(
        functools.partial(_out_ln_kernel, widths=widths),
        grid=(n // tm,),
        in_specs=in_specs,
        out_specs=pl.BlockSpec((tm, d), lambda i: (i, 0)),
        out_shape=jax.ShapeDtypeStruct((n, d), F32),
        compiler_params=_cparams("parallel"),
        name=name,
    )(*a_list, w, h, g.reshape(1, d), b.reshape(1, d))


def _mlp_kernel(h_ref, w1_ref, w2_ref, g_ref, b_ref, o_ref, hbf_ref, acc_ref):
    f = pl.program_id(1)

    @pl.when(f == 0)
    def _():
        hbf_ref[...] = h_ref[...].astype(BF16)
        acc_ref[...] = jnp.zeros_like(acc_ref)

    u = jnp.maximum(_dot(hbf_ref[...], w1_ref[...]), 0.0)
    acc_ref[...] += _dot((u * u).astype(BF16), w2_ref[...])

    @pl.when(f == pl.num_programs(1) - 1)
    def _():
        y = DEEPNORM_ALPHA * h_ref[...] + acc_ref[...]
        o_ref[...] = _ln_rows(y, g_ref[...], b_ref[...])


def _mlp_ln(h, w1, w2, g, b, tm=512, tf=512):
    n, d = h.shape
    dff = w1.shape[1]
    return pl.pallas_call(
        _mlp_kernel,
        grid=(n // tm, dff // tf),
        in_specs=[pl.BlockSpec((tm, d), lambda i, f: (i, 0)),
                  pl.BlockSpec((d, tf), lambda i, f: (0, f)),
                  pl.BlockSpec((tf, d), lambda i, f: (f, 0)),
                  pl.BlockSpec((1, d), lambda i, f: (0, 0)),
                  pl.BlockSpec((1, d), lambda i, f: (0, 0))],
        out_specs=pl.BlockSpec((tm, d), lambda i, f: (i, 0)),
        out_shape=jax.ShapeDtypeStruct((n, d), F32),
        scratch_shapes=[pltpu.VMEM((tm, d), BF16), pltpu.VMEM((tm, d), F32)],
        compiler_params=_cparams("parallel", "arbitrary"),
        name="mlp_ln",
    )(h, w1, w2, g.reshape(1, d), b.reshape(1, d))


def _mla_up_kernel(cq_ref, ckv_ref, kr_ref, gq_ref, gkv_ref, wq_ref, wkv_ref, tc_ref, ts1_ref, ts2_ref,
                   q_ref, k_ref, v_ref, *, scale):
    half = MLA_ROPE // 2
    c, s1, s2 = tc_ref[...], ts1_ref[...], ts2_ref[...]
    nq = _rms_rows(cq_ref[...], gq_ref[...]).astype(BF16)
    nkv = _rms_rows(ckv_ref[...], gkv_ref[...]).astype(BF16)
    qf = _dot(nq, wq_ref[...])
    kvf = _dot(nkv, wkv_ref[...])
    kr = _rope128(kr_ref[...], c, s1, s2, half).astype(BF16)
    for h in range(MLA_HEADS):
        o = h * MLA_SLOT
        q_ref[0, h, :, 0:LANES] = (qf[:, o:o + LANES] * scale).astype(BF16)
        q_ref[0, h, :, LANES:] = (_rope128(qf[:, o + LANES:o + MLA_SLOT], c, s1, s2, half) * scale).astype(BF16)
        k_ref[0, h, :, 0:LANES] = kvf[:, h * LANES:(h + 1) * LANES].astype(BF16)
        k_ref[0, h, :, LANES:] = kr
        v_ref[0, h, :, 0:LANES] = kvf[:, (MLA_HEADS + h) * LANES:(MLA_HEADS + h + 1) * LANES].astype(BF16)
        v_ref[0, h, :, LANES:] = jnp.ones((kr.shape[0], LANES), BF16)


def _mla_up(proj, gq, gkv, wq, wkv, tabs, batch, seq, tm=512):
    nt = seq // tm
    row = lambda b, i: b * nt + i
    tab_spec = pl.BlockSpec((tm, LANES), lambda b, i: (i, 0))
    hm = lambda w: pl.BlockSpec((1, MLA_HEADS, tm, w), lambda b, i: (b, 0, i, 0))
    return pl.pallas_call(
        functools.partial(_mla_up_kernel, scale=(MLA_NOPE + MLA_ROPE) ** -0.5),
        grid=(batch, nt),
        in_specs=[pl.BlockSpec((tm, 512), lambda b, i: (row(b, i), COL512["c_q"])),
                  pl.BlockSpec((tm, 512), lambda b, i: (row(b, i), COL512["c_kv"])),
                  pl.BlockSpec((tm, LANES), lambda b, i: (row(b, i), COL128["k_rope"])),
                  pl.BlockSpec((1, MLA_Q_RANK), lambda b, i: (0, 0)),
                  pl.BlockSpec((1, MLA_KV_RANK), lambda b, i: (0, 0)),
                  pl.BlockSpec(wq.shape, lambda b, i: (0, 0)),
                  pl.BlockSpec(wkv.shape, lambda b, i: (0, 0)),
                  tab_spec, tab_spec, tab_spec],
        out_specs=[hm(MLA_SLOT), hm(MLA_SLOT), hm(2 * HEAD_DIM)],
        out_shape=[jax.ShapeDtypeStruct((batch, MLA_HEADS, seq, MLA_SLOT), BF16),
                   jax.ShapeDtypeStruct((batch, MLA_HEADS, seq, MLA_SLOT), BF16),
                   jax.ShapeDtypeStruct((batch, MLA_HEADS, seq, 2 * HEAD_DIM), BF16)],
        compiler_params=_cparams("parallel", "parallel"),
        name="mla_up",
    )(proj, proj, proj, gq.reshape(1, -1), gkv.reshape(1, -1), wq, wkv, *tabs)


def _mla_attn_kernel(q_ref, k_ref, v_ref, o_ref, acc_sc, *, tq, tk, rs, ahead):
    qi = pl.program_id(1)
    heads = q_ref.shape[1]
    sub = tq // tk
    row = lax.broadcasted_iota(jnp.int32, (tq, tk), 0)
    col = lax.broadcasted_iota(jnp.int32, (tq, tk), 1)

    nrs = tq // rs

    def step(kb, carry, mask):
        r = pl.ds(pl.multiple_of(kb * tk, tk), tk)
        def scores(c):
            h, rows = c // nrs, slice((c % nrs) * rs, (c % nrs + 1) * rs)
            s = _dot_nt(q_ref[0, h, rows, :], k_ref[0, h, r, :])
            return s if mask is None else jnp.where(mask[rows], s, MASKED)

        n_chain = heads * nrs
        out, queue = [], [scores(c) for c in range(ahead)]
        for c in range(n_chain):
            if c + ahead < n_chain:
                queue.append(scores(c + ahead))
            m_new, acc_sc[c] = _flash_update(queue.pop(0), carry[c], acc_sc[c], v_ref[0, c // nrs, r, :])
            out.append(m_new)
        return tuple(out)

    acc_sc[...] = jnp.zeros(acc_sc.shape, F32)
    carry = tuple(jnp.full((rs, 1), NEG_INF, F32) for _ in range(heads * nrs))
    carry = lax.fori_loop(0, qi * sub, lambda kb, c: step(kb, c, None), carry)
    for j in range(sub):
        carry = step(qi * sub + j, carry, col + j * tk <= row)
    for h in range(heads):
        for i in range(nrs):
            acc = acc_sc[h * nrs + i]
            o_ref[0, i * rs:(i + 1) * rs, h * HEAD_DIM:(h + 1) * HEAD_DIM] = (
                acc[:, :HEAD_DIM] / acc[:, HEAD_DIM:]).astype(o_ref.dtype)


def _mla_attn(q, k, v, tq=512, tk=256, rs=256, ahead=1):
    batch, heads, seq, _ = q.shape
    return pl.pallas_call(
        functools.partial(_mla_attn_kernel, tq=tq, tk=tk, rs=rs, ahead=ahead),
        grid=(batch, seq // tq),
        in_specs=[pl.BlockSpec((1, heads, tq, MLA_SLOT), lambda b, i: (b, 0, i, 0)),
                  pl.BlockSpec((1, heads, seq, MLA_SLOT), lambda b, i: (b, 0, 0, 0), pipeline_mode=pl.Buffered(1)),
                  pl.BlockSpec((1, heads, seq, 2 * HEAD_DIM), lambda b, i: (b, 0, 0, 0),
                               pipeline_mode=pl.Buffered(1))],
        out_specs=pl.BlockSpec((1, tq, heads * HEAD_DIM), lambda b, i: (b, i, 0)),
        out_shape=jax.ShapeDtypeStruct((batch, seq, heads * HEAD_DIM), BF16),
        scratch_shapes=[pltpu.VMEM((heads * tq // rs, rs, 2 * HEAD_DIM), F32)],
        compiler_params=_cparams("parallel", "arbitrary"),
        name="mla_attn",
    )(q, k, v)


def _nsa_kernel(nq_ref, kc_ref, vc_ref, ks_ref, vs_ref, kw_ref, vw_ref, ng_ref,
                w1_ref, w2_ref, pos_ref, ovt_ref, tc_ref, ts1_ref, ts2_ref,
                o_ref,
                tmp_sc, kc_sc, vc_sc, ks_sc, vs_sc, kw_sc, vw_sc, acc_sc, *, tq, seq, scale):
    qi = pl.program_id(1)
    half = PARTIAL_ROT // 2
    heads = NSA_HEADS
    n_chunk = seq // NSA_CMP_STRIDE
    n_sel = seq // NSA_SEL_LEN

    @pl.when(qi == 0)
    def _prep():
        c, s1, s2 = tc_ref[...], ts1_ref[...], ts2_ref[...]
        ones = jnp.ones((seq, LANES), BF16)
        blk = lax.broadcasted_iota(jnp.int32, (seq, LANES), 0) >> 6
        ks_sc[:, 0:LANES] = _rope128(ks_ref[...], c, s1, s2, half).astype(BF16)
        ks_sc[:, LANES:] = jnp.where(blk == lax.broadcasted_iota(jnp.int32, (seq, LANES), 1), 1.0, 0.0).astype(BF16)
        kw_sc[...] = _rope128(kw_ref[...], c, s1, s2, half).astype(BF16)
        vs_sc[:, 0:LANES] = vs_ref[...].astype(BF16)
        vs_sc[:, LANES:] = ones
        vw_sc[:, 0:LANES] = vw_ref[...].astype(BF16)
        vw_sc[:, LANES:] = ones
        for i, (src, dst) in enumerate(((kc_ref, kc_sc), (vc_ref, vc_sc))):
            tmp_sc[...] = _rope128(src[...], c, s1, s2, half) if i == 0 else src[...]
            lo, hi = [], []
            for t in range(NSA_CMP_STRIDE):
                x = tmp_sc[pl.ds(t, n_chunk, stride=NSA_CMP_STRIDE), :]
                lo.append((x + pos_ref[i, t:t + 1, :]).astype(BF16))
                hi.append((x + pos_ref[i, NSA_CMP_STRIDE + t:NSA_CMP_STRIDE + t + 1, :]).astype(BF16))
            kw1 = NSA_CMP_STRIDE * HEAD_DIM
            a = _dot(jnp.concatenate(lo, axis=1), w1_ref[i, 0:kw1, :])
            bm = _dot(jnp.concatenate(hi, axis=1), w1_ref[i, kw1:2 * kw1, :])
            hid = jax.nn.gelu(a + pltpu.roll(bm, n_chunk - 1, 0))
            dst[...] = _dot(hid.astype(BF16), w2_ref[i]).astype(BF16)

    t0 = pl.multiple_of(qi * tq, tq)
    rq = pl.ds(t0, tq)
    c, s1, s2 = tc_ref[rq, :], ts1_ref[rq, :], ts2_ref[rq, :]
    qf = nq_ref[...]
    qs = jnp.concatenate(
        [_rope128(qf[:, h * LANES:(h + 1) * LANES], c, s1, s2, half) * scale for h in range(heads)],
        axis=0)
    qs_bf = qs.astype(BF16)

    lane = lax.broadcasted_iota(jnp.int32, (tq, LANES), 1)
    tpos = t0 + lax.broadcasted_iota(jnp.int32, (tq, LANES), 0)

    s = _dot_nt(qs_bf, kc_sc[...]).reshape(heads, tq, LANES)
    cmask = (lane * NSA_CMP_STRIDE + (NSA_CMP_LEN - 1) <= tpos)[None]
    s = jnp.where(cmask, s, NEG_INF)
    m = jnp.max(s, axis=-1, keepdims=True)
    m = jnp.where(m == NEG_INF, 0.0, m)
    e = jnp.exp(s - m)
    p_cmp = e / jnp.maximum(jnp.sum(e, axis=-1, keepdims=True), 1e-30)
    o_cmp = _dot(p_cmp.reshape(heads * tq, LANES).astype(BF16), vc_sc[...]).reshape(heads, tq, HEAD_DIM)

    p_hi, p_lo = _split_bf16(p_cmp[0] + p_cmp[1] + p_cmp[2] + p_cmp[3])
    imp_t = (_dot_nt(ovt_ref[...], p_hi) + _dot_nt(ovt_ref[...], p_lo))[0:n_sel]
    blk_t = lax.broadcasted_iota(jnp.int32, (n_sel, tq), 0)
    cur_t = (t0 + lax.broadcasted_iota(jnp.int32, (n_sel, tq), 1)) >> 6
    eligible = blk_t <= cur_t
    forced = (blk_t == 0) | (blk_t == cur_t) | (blk_t == cur_t - 1)
    score = jnp.where(eligible, jnp.where(forced, NSA_FORCE_SCORE, imp_t), NEG_INF)
    rank = _rank_rows(score, n_sel)
    bias_t = jnp.where(eligible & (rank < NSA_SEL_TOPK), 0.0, MASKED)
    bias_t = jnp.concatenate([bias_t, jnp.zeros((LANES - n_sel, tq), F32)], axis=0)
    bias = _rows_to_lanes(bias_t, tq)
    q_aug = jnp.concatenate([qs, jnp.concatenate([bias] * heads, axis=0)], axis=1).astype(BF16)

    local_r = lax.broadcasted_iota(jnp.int32, (tq, tq), 0)
    local_c = lax.broadcasted_iota(jnp.int32, (tq, tq), 1)
    causal = (local_c <= local_r)[None]

    n_chain = 2
    hpc = heads // n_chain
    crow = hpc * tq

    def run(q_rows, k_sc, v_sc, kb, carry, mask):
        r = pl.ds(pl.multiple_of(kb * tq, tq), tq)

        def scores(c):
            s = _dot_nt(q_rows[c * crow:(c + 1) * crow], k_sc[r, :])
            if mask is None:
                return s
            return jnp.where(mask, s.reshape(hpc, tq, tq), MASKED).reshape(crow, tq)

        s0, s1 = scores(0), scores(1)
        m0, acc_sc[0] = _flash_update(s0, carry[0], acc_sc[0], v_sc[r, :])
        m1, acc_sc[1] = _flash_update(s1, carry[1], acc_sc[1], v_sc[r, :])
        return m0, m1

    def finish():
        acc = jnp.concatenate([acc_sc[0], acc_sc[1]], axis=0)
        return (acc[:, :HEAD_DIM] / acc[:, HEAD_DIM:]).reshape(heads, tq, HEAD_DIM)

    init = (jnp.full((crow, 1), NEG_INF, F32),) * n_chain

    acc_sc[...] = jnp.zeros(acc_sc.shape, F32)
    carry = lax.fori_loop(0, qi, lambda kb, cr: run(q_aug, ks_sc, vs_sc, kb, cr, None), init)
    run(q_aug, ks_sc, vs_sc, qi, carry, causal)
    o_sel = finish()

    acc_sc[...] = jnp.zeros(acc_sc.shape, F32)
    carry = init
    for back, mask in ((2, (local_c > local_r)[None]), (1, None), (0, causal)):
        kb = qi - back
        if back:
            valid = kb >= 0
            mask = valid if mask is None else mask & valid
        carry = run(qs_bf, kw_sc, vw_sc, jnp.maximum(kb, 0), carry, mask)
    o_win = finish()

    g = jax.nn.sigmoid(ng_ref[...])
    for h in range(heads):
        o = (g[:, 3 * h:3 * h + 1] * o_cmp[h] + g[:, 3 * h + 1:3 * h + 2] * o_sel[h]
             + g[:, 3 * h + 2:3 * h + 3] * o_win[h])
        o_ref[:, h * LANES:(h + 1) * LANES] = o.astype(o_ref.dtype)


def _nsa(proj, w1, w2, pos, tabs, batch, seq, tq=256):
    nt = seq // tq
    n_cmp = (seq - NSA_CMP_LEN) // NSA_CMP_STRIDE + 1
    n_sel = seq // NSA_SEL_LEN
    starts = np.arange(LANES) * NSA_CMP_STRIDE
    sel_start = np.arange(LANES) * NSA_SEL_LEN
    overlap = ((starts[:, None] < sel_start[None, :] + NSA_SEL_LEN)
               & (starts[:, None] + NSA_CMP_LEN > sel_start[None, :])
               & (np.arange(LANES)[:, None] < n_cmp) & (np.arange(LANES)[None, :] < n_sel))
    ovt = jnp.asarray(overlap.T.astype(np.float32), BF16)
    seq_col = lambda name: pl.BlockSpec((seq, LANES), lambda b, i: (b, COL128[name]))
    full = lambda a: pl.BlockSpec(a.shape, lambda b, i: (0,) * a.ndim)
    return pl.pallas_call(
        functools.partial(_nsa_kernel, tq=tq, seq=seq, scale=HEAD_DIM ** -0.5),
        grid=(batch, nt),
        in_specs=[pl.BlockSpec((tq, 512), lambda b, i: (b * nt + i, COL512["nq"])),
                  seq_col("nkc"), seq_col("nvc"), seq_col("nks"), seq_col("nvs"), seq_col("nkw"), seq_col("nvw"),
                  pl.BlockSpec((tq, LANES), lambda b, i: (b * nt + i, COL128["ng"])),
                  full(w1), full(w2), full(pos), full(ovt), full(tabs[0]), full(tabs[1]), full(tabs[2])],
        out_specs=pl.BlockSpec((tq, NSA_HEADS * HEAD_DIM), lambda b, i: (b * nt + i, 0)),
        out_shape=jax.ShapeDtypeStruct((batch * seq, NSA_HEADS * HEAD_DIM), BF16),
        scratch_shapes=[pltpu.VMEM((seq, LANES), F32),
                        pltpu.VMEM((LANES, LANES), BF16), pltpu.VMEM((LANES, LANES), BF16),
                        pltpu.VMEM((seq, 2 * LANES), BF16), pltpu.VMEM((seq, 2 * LANES), BF16),
                        pltpu.VMEM((seq, LANES), BF16), pltpu.VMEM((seq, 2 * LANES), BF16),
                        pltpu.VMEM((2, NSA_HEADS // 2 * tq, 2 * HEAD_DIM), F32)],
        compiler_params=_cparams("parallel", "arbitrary"),
        name="nsa",
    )(proj, proj, proj, proj, proj, proj, proj, proj, w1, w2, pos, ovt, *tabs)


def _moba_kernel(mq_ref, mk_ref, mv_ref, tc_ref, ts1_ref, ts2_ref, o_ref, k_sc, v_sc, km_sc, acc_sc,
                 *, tq, seq, scale):
    qi = pl.program_id(1)
    half = PARTIAL_ROT // 2
    heads = MOBA_HEADS
    n_blk = seq // MOBA_BLOCK
    slot = 2 * LANES

    @pl.when(qi == 0)
    def _prep():
        c, s1, s2 = tc_ref[...], ts1_ref[...], ts2_ref[...]
        km_sc[...] = jnp.zeros_like(km_sc)
        ones = jnp.ones((seq, LANES), BF16)
        blk = lax.broadcasted_iota(jnp.int32, (seq, LANES), 0) >> 8
        onehot = jnp.where(blk == lax.broadcasted_iota(jnp.int32, (seq, LANES), 1), 1.0, 0.0).astype(BF16)
        for h in range(heads):
            hs = slice(h * LANES, (h + 1) * LANES)
            kh = _rope128(mk_ref[:, hs], c, s1, s2, half)
            k_sc[:, h * slot:h * slot + LANES] = kh.astype(BF16)
            k_sc[:, h * slot + LANES:(h + 1) * slot] = onehot
            v_sc[:, h * slot:h * slot + LANES] = mv_ref[:, hs].astype(BF16)
            v_sc[:, h * slot + LANES:(h + 1) * slot] = ones
            for j in range(n_blk):
                km_sc[h * n_blk + j:h * n_blk + j + 1, :] = jnp.mean(
                    kh[j * MOBA_BLOCK:(j + 1) * MOBA_BLOCK], axis=0, keepdims=True)

    t0 = pl.multiple_of(qi * tq, tq)
    rq = pl.ds(t0, tq)
    c, s1, s2 = tc_ref[rq, :], ts1_ref[rq, :], ts2_ref[rq, :]
    km_hi, km_lo = _split_bf16(km_sc[...])
    tk = MOBA_BLOCK
    sub = tq // tk
    blk_t = lax.broadcasted_iota(jnp.int32, (n_blk, tq), 0)
    cur_t = (t0 + lax.broadcasted_iota(jnp.int32, (n_blk, tq), 1)) >> 8
    row = lax.broadcasted_iota(jnp.int32, (tq, tk), 0)
    col = lax.broadcasted_iota(jnp.int32, (tq, tk), 1)

    q_aug = []
    for h in range(heads):
        qf = _rope128(mq_ref[:, h * LANES:(h + 1) * LANES], c, s1, s2, half) * scale
        q_hi, q_lo = _split_bf16(qf)
        gate_t = (_dot_nt(km_hi, q_hi) + (_dot_nt(km_lo, q_hi) + _dot_nt(km_hi, q_lo)))[h * n_blk:(h + 1) * n_blk]
        eligible = blk_t < cur_t
        score = jnp.where(eligible, gate_t, NEG_INF)
        picked = eligible & (_rank_rows(score, n_blk) < MOBA_TOPK)
        bias_t = jnp.where(picked | (blk_t == cur_t), 0.0, MASKED)
        bias_t = jnp.concatenate([bias_t, jnp.zeros((LANES - n_blk, tq), F32)], axis=0)
        q_aug.append(jnp.concatenate([qf, _rows_to_lanes(bias_t, tq)], axis=1).astype(BF16))

    n_chain = heads * sub

    def step(kb, carry, mask):
        r = pl.ds(pl.multiple_of(kb * tk, tk), tk)

        def scores(c):
            h, rows = c // sub, slice((c % sub) * tk, (c % sub + 1) * tk)
            s = _dot_nt(q_aug[h][rows], k_sc[r, h * slot:(h + 1) * slot])
            return s if mask is None else jnp.where(mask[rows], s, MASKED)

        out, s_next = [], scores(0)
        for c in range(n_chain):
            s, s_next = s_next, (scores(c + 1) if c + 1 < n_chain else None)
            h = c // sub
            m_new, acc_sc[c] = _flash_update(s, carry[c], acc_sc[c], v_sc[r, h * slot:(h + 1) * slot])
            out.append(m_new)
        return tuple(out)

    acc_sc[...] = jnp.zeros(acc_sc.shape, F32)
    carry = tuple(jnp.full((tk, 1), NEG_INF, F32) for _ in range(n_chain))
    carry = lax.fori_loop(0, qi * sub, lambda kb, cr: step(kb, cr, None), carry)
    for j in range(sub):
        carry = step(qi * sub + j, carry, col + j * tk <= row)
    for c in range(n_chain):
        h, r0 = c // sub, (c % sub) * tk
        acc = acc_sc[c]
        o_ref[r0:r0 + tk, h * LANES:(h + 1) * LANES] = (acc[:, :HEAD_DIM] / acc[:, HEAD_DIM:]).astype(o_ref.dtype)


def _moba(proj, tabs, batch, seq):
    tq = 2 * MOBA_BLOCK
    nt = seq // tq
    width = MOBA_HEADS * HEAD_DIM
    full = lambda a: pl.BlockSpec(a.shape, lambda b, i: (0,) * a.ndim)
    return pl.pallas_call(
        functools.partial(_moba_kernel, tq=tq, seq=seq, scale=HEAD_DIM ** -0.5),
        grid=(batch, nt),
        in_specs=[pl.BlockSpec((tq, width), lambda b, i: (b * nt + i, COL512["mq"])),
                  pl.BlockSpec((seq, width), lambda b, i: (b, COL512["mk"])),
                  pl.BlockSpec((seq, width), lambda b, i: (b, COL512["mv"])),
                  full(tabs[0]), full(tabs[1]), full(tabs[2])],
        out_specs=pl.BlockSpec((tq, width), lambda b, i: (b * nt + i, 0)),
        out_shape=jax.ShapeDtypeStruct((batch * seq, width), BF16),
        scratch_shapes=[pltpu.VMEM((seq, 2 * width), BF16), pltpu.VMEM((seq, 2 * width), BF16),
                        pltpu.VMEM((LANES, LANES), F32),
                        pltpu.VMEM((MOBA_HEADS * tq // MOBA_BLOCK, MOBA_BLOCK, 2 * HEAD_DIM), F32)],
        compiler_params=_cparams("parallel", "arbitrary"),
        name="moba",
    )(proj, proj, proj, *tabs)


def _xattn_kernel(q_ref, kv_ref, o_ref):
    d = q_ref.shape[1]
    hd = d // MEM_HEADS
    for h in range(MEM_HEADS):
        hs = slice(h * hd, (h + 1) * hd)
        s = _dot_nt(q_ref[:, hs], kv_ref[0, :, hs])
        e = jnp.exp(s - jnp.max(s, axis=-1, keepdims=True))
        p = e / jnp.sum(e, axis=-1, keepdims=True)
        o_ref[:, hs] = _dot(p.astype(BF16), kv_ref[0, :, d + h * hd:d + (h + 1) * hd]).astype(o_ref.dtype)


def _xattn(q, kv, batch, seq, tq=512):
    d = q.shape[1]
    nt = seq // tq
    m_len = kv.shape[0] // batch
    return pl.pallas_call(
        _xattn_kernel,
        grid=(batch, nt),
        in_specs=[pl.BlockSpec((tq, d), lambda b, i: (b * nt + i, 0)),
                  pl.BlockSpec((1, m_len, 2 * d), lambda b, i: (b, 0, 0))],
        out_specs=pl.BlockSpec((tq, d), lambda b, i: (b * nt + i, 0)),
        out_shape=jax.ShapeDtypeStruct((batch * seq, d), BF16),
        compiler_params=_cparams("parallel", "parallel"),
        name="xattn",
    )(q, kv.reshape(batch, m_len, 2 * d))


IN_SPLIT_NAMES = ("c_q", "c_kv", "k_rope", "nq", "nkc", "nvc", "nks", "nvs", "nkw", "nvw", "ng", "mq", "mk", "mv")
IN_SPLIT_SIZES = (512, 512, 64, 512, 128, 128, 128, 128, 128, 128, 12, 512, 512, 512)


def _pack_w_in_kernel(w_ref, o_ref):
    rows = w_ref.shape[0]
    off = 0
    for name, sz in zip(IN_SPLIT_NAMES, IN_SPLIT_SIZES):
        width = 512 if name in COL512 else LANES
        dst = COL512[name] * 512 if name in COL512 else COL128[name] * LANES
        x = w_ref[:, off:off + sz].astype(BF16)
        if sz < width:
            x = jnp.concatenate([x, jnp.zeros((rows, width - sz), BF16)], axis=1)
        o_ref[0, :, dst:dst + width] = x
        off += sz


def _pack_w_in(w, tr=256):
    layers, d, width = w.shape
    return pl.pallas_call(
        _pack_w_in_kernel,
        grid=(layers, d // tr),
        in_specs=[pl.BlockSpec((None, tr, width), lambda l, i: (l, i, 0))],
        out_specs=pl.BlockSpec((1, tr, PROJ_WIDTH), lambda l, i: (l, i, 0)),
        out_shape=jax.ShapeDtypeStruct((layers, d, PROJ_WIDTH), BF16),
        compiler_params=_cparams("parallel", "parallel"),
        name="pack_w_in",
    )(w)


def _pack_w_uq(w):
    r = w.shape[0]
    w = w.reshape(r, MLA_HEADS, MLA_NOPE + MLA_ROPE)
    w = jnp.pad(w, ((0, 0), (0, 0), (0, MLA_SLOT - MLA_NOPE - MLA_ROPE)))
    return w.reshape(r, MLA_HEADS * MLA_SLOT).astype(BF16)


def _pack_w_ukv(w):
    r = w.shape[0]
    w = w.reshape(r, MLA_HEADS, 2, HEAD_DIM).transpose(0, 2, 1, 3)
    return w.reshape(r, 2 * MLA_HEADS * HEAD_DIM).astype(BF16)


def kernel(x, mem, ln_in_g, ln_in_b, w_in, mla_q_norm, mla_kv_norm, mla_w_uq, mla_w_ukv, nsa_cmp_w1, nsa_cmp_w2, nsa_cmp_pos, w_out, ln1_g, ln1_b, mem_wq, mem_wkv, mem_wo, ln2_g, ln2_b, mlp_w1, mlp_w2, ln3_g, ln3_b):
    batch, seq, d = x.shape
    n = batch * seq
    mla_tabs = _rope_tables(seq, MLA_ROPE)
    rot_tabs = _rope_tables(seq, PARTIAL_ROT)
    mem2 = mem.reshape(batch * mem.shape[1], d)
    w_in_packed = _pack_w_in(w_in)

    h = _layer_norm(x.reshape(n, d), ln_in_g, ln_in_b)
    for l in range(DEPTH):
        proj = _matmul(h, w_in_packed[l], F32, name="in_proj")
        q, k, v = _mla_up(proj, mla_q_norm[l], mla_kv_norm[l], _pack_w_uq(mla_w_uq[l]), _pack_w_ukv(mla_w_ukv[l]),
                          mla_tabs, batch, seq)
        o_a = _mla_attn(q, k, v).reshape(n, MLA_HEADS * HEAD_DIM)
        o_b = _nsa(proj, nsa_cmp_w1[l].astype(BF16), nsa_cmp_w2[l].astype(BF16), nsa_cmp_pos[l], rot_tabs, batch, seq)
        o_c = _moba(proj, rot_tabs, batch, seq)
        h = _out_ln([o_a, o_b, o_c], w_out[l].astype(BF16), h, ln1_g[l], ln1_b[l], name="mix_out_ln")

        wq_scaled = (mem_wq[l] * (d // MEM_HEADS) ** -0.5).astype(BF16)
        xq = _matmul(h, wq_scaled, BF16, name="mem_q")
        xkv = _matmul(mem2, mem_wkv[l].astype(BF16), BF16, name="mem_kv")
        ctx = _xattn(xq, xkv, batch, seq)
        h = _out_ln([ctx], mem_wo[l].astype(BF16), h, ln2_g[l], ln2_b[l], name="mem_out_ln")

        h = _mlp_ln(h, mlp_w1[l].astype(BF16), mlp_w2[l].astype(BF16), ln3_g[l], ln3_b[l])
    return h.reshape(batch, seq, d)
```

```python
import functools

import numpy as np
import jax
import jax.numpy as jnp
from jax import lax
from jax.experimental import pallas as pl
from jax.experimental.pallas import tpu as pltpu

F32 = jnp.float32
BF16 = jnp.bfloat16
NEG_INF = float("-inf")

D_MODEL = 2048
DEPTH = 2
HEAD_DIM = 128
MLA_HEADS = 8
NSA_HEADS = 4
MOBA_HEADS = 4
ROPE_THETA = 500000.0
PARTIAL_ROT = HEAD_DIM // 4
MLA_Q_RANK = 512
MLA_KV_RANK = 512
MLA_NOPE = 128
MLA_ROPE = 64
MLA_SLOT = 256
NSA_CMP_LEN = 32
NSA_CMP_STRIDE = 16
NSA_SEL_LEN = 64
NSA_SEL_TOPK = 16
NSA_WINDOW = 512
NSA_FORCE_SCORE = 1.0e4
MOBA_BLOCK = 256
MOBA_TOPK = 3
MEM_HEADS = 4
D_FF = 4 * D_MODEL
DEEPNORM_ALPHA = (2 * DEPTH) ** 0.25
LANES = 128

PROJ_WIDTH = 4096
COL512 = dict(c_q=0, c_kv=1, nq=2, mq=3, mk=4, mv=5)
COL128 = dict(k_rope=24, nkc=25, nvc=26, nks=27, nvs=28, nkw=29, nvw=30, ng=31)

VMEM_LIMIT = 56 * 1024 * 1024


def _cparams(*sem):
    return pltpu.CompilerParams(dimension_semantics=sem, vmem_limit_bytes=VMEM_LIMIT)


def _dot(a, b):
    return jnp.dot(a, b, preferred_element_type=F32)


def _dot_nt(a, b):
    return lax.dot_general(a, b, (((1,), (1,)), ((), ())), preferred_element_type=F32)


def _split_bf16(x):
    hi = x.astype(BF16)
    return hi, (x - hi.astype(F32)).astype(BF16)


def _ln_rows(x, g, b, eps=1e-5):
    mu = jnp.mean(x, axis=-1, keepdims=True)
    xc = x - mu
    var = jnp.mean(xc * xc, axis=-1, keepdims=True)
    return xc * lax.rsqrt(var + eps) * g + b


def _rms_rows(x, g, eps=1e-6):
    return x * lax.rsqrt(jnp.mean(x * x, axis=-1, keepdims=True) + eps) * g


def _rope128(x, c, s1, s2, half):
    return x * c + pltpu.roll(x, LANES - half, 1) * s1 + pltpu.roll(x, half, 1) * s2


def _rope_tables(n_pos, dim):
    half = dim // 2
    inv = ROPE_THETA ** (-jnp.arange(0, dim, 2, dtype=F32) / dim)
    ang = jnp.arange(n_pos, dtype=F32)[:, None] * inv[None, :]
    cos, sin = jnp.cos(ang), jnp.sin(ang)
    ones = jnp.ones((n_pos, LANES - dim), F32)
    z = lambda w: jnp.zeros((n_pos, w), F32)
    c = jnp.concatenate([cos, cos, ones], axis=1)
    s1 = jnp.concatenate([-sin, z(LANES - half)], axis=1)
    s2 = jnp.concatenate([z(half), sin, z(LANES - dim)], axis=1)
    return c, s1, s2


MASKED = -1.0e30


def _rank_rows(score, n_cand):
    row = lax.broadcasted_iota(jnp.int32, score.shape, 0)
    rank = jnp.zeros(score.shape, F32)
    for jp in range(n_cand):
        cand = score[jp:jp + 1, :]
        ahead = (cand > score) | ((cand == score) & (row > jp))
        rank = rank + jnp.where(ahead, 1.0, 0.0)
    return rank


def _rows_to_lanes(x_t, tq):
    eye = jnp.where(lax.broadcasted_iota(jnp.int32, (tq, tq), 0) == lax.broadcasted_iota(jnp.int32, (tq, tq), 1),
                    1.0, 0.0).astype(BF16)
    return _dot_nt(eye, x_t.astype(BF16))


def _flash_update(s, m, acc, v_aug):
    m_new = jnp.maximum(m, jnp.max(s, axis=-1, keepdims=True))
    p = jnp.exp(s - m_new).astype(BF16)
    return m_new, jnp.exp(m - m_new) * acc + _dot(p, v_aug)


def _ln_kernel(x_ref, g_ref, b_ref, o_ref):
    o_ref[...] = _ln_rows(x_ref[...], g_ref[...], b_ref[...])


def _layer_norm(x, g, b, tm=512):
    n, d = x.shape
    return pl.pallas_call(
        _ln_kernel,
        grid=(n // tm,),
        in_specs=[pl.BlockSpec((tm, d), lambda i: (i, 0)),
                  pl.BlockSpec((1, d), lambda i: (0, 0)),
                  pl.BlockSpec((1, d), lambda i: (0, 0))],
        out_specs=pl.BlockSpec((tm, d), lambda i: (i, 0)),
        out_shape=jax.ShapeDtypeStruct((n, d), F32),
        compiler_params=_cparams("parallel"),
        name="ln_in",
    )(x, g.reshape(1, d), b.reshape(1, d))


def _mm_kernel(a_ref, w_ref, o_ref, abf_ref):
    @pl.when(pl.program_id(1) == 0)
    def _():
        abf_ref[...] = a_ref[...].astype(BF16)

    o_ref[...] = _dot(abf_ref[...], w_ref[...]).astype(o_ref.dtype)


def _matmul(a, w, layer, out_dtype, tm=1024, tn=512, name="mm"):
    m, k = a.shape
    n = w.shape[2]
    tm = min(tm, m)
    return pl.pallas_call(
        _mm_kernel,
        grid=(m // tm, n // tn),
        in_specs=[pl.BlockSpec((tm, k), lambda i, j: (i, 0)),
                  pl.BlockSpec((None, k, tn), lambda i, j: (layer, 0, j))],
        out_specs=pl.BlockSpec((tm, tn), lambda i, j: (i, j)),
        out_shape=jax.ShapeDtypeStruct((m, n), out_dtype),
        scratch_shapes=[pltpu.VMEM((tm, k), BF16)],
        compiler_params=_cparams("parallel", "arbitrary"),
        name=name,
    )(a, w)


def _out_ln_kernel(*refs, widths):
    n_a = len(widths)
    a_refs = refs[:n_a]
    w_ref, h_ref, g_ref, b_ref, o_ref = refs[n_a:]
    tm = h_ref.shape[0]
    for rows in (slice(0, tm // 2), slice(tm // 2, tm)):
        acc = DEEPNORM_ALPHA * h_ref[rows, :]
        off = 0
        for a_ref, wd in zip(a_refs, widths):
            acc = acc + _dot(a_ref[rows, :], w_ref[off:off + wd, :])
            off += wd
        o_ref[rows, :] = _ln_rows(acc, g_ref[...], b_ref[...])


def _out_ln(a_list, w, layer, h, g, b, tm=512, name="out_ln"):
    n, d = h.shape
    widths = tuple(a.shape[1] for a in a_list)
    k = sum(widths)
    in_specs = [pl.BlockSpec((tm, wd), lambda i: (i, 0)) for wd in widths]
    in_specs += [pl.BlockSpec((None, k, d), lambda i: (layer, 0, 0)),
                 pl.BlockSpec((tm, d), lambda i: (i, 0)),
                 pl.BlockSpec((1, d), lambda i: (0, 0)),
                 pl.BlockSpec((1, d), lambda i: (0, 0))]
    return pl.pallas_call(
        functools.partial(_out_ln_kernel, widths=widths),
        grid=(n // tm,),
        in_specs=in_specs,
        out_specs=pl.BlockSpec((tm, d), lambda i: (i, 0)),
        out_shape=jax.ShapeDtypeStruct((n, d), F32),
        compiler_params=_cparams("parallel"),
        name=name,
    )(*a_list, w, h, g.reshape(1, d), b.reshape(1, d))


def _mlp_kernel(h_ref, w1_ref, w2_ref, g_ref, b_ref, o_ref, hbf_ref, acc_ref):
    f = pl.program_id(1)

    @pl.when(f == 0)
    def _():
        hbf_ref[...] = h_ref[...].astype(BF16)
        acc_ref[...] = jnp.zeros_like(acc_ref)

    u = jnp.maximum(_dot(hbf_ref[...], w1_ref[...]), 0.0)
    acc_ref[...] += _dot((u * u).astype(BF16), w2_ref[...])

    @pl.when(f == pl.num_programs(1) - 1)
    def _():
        y = DEEPNORM_ALPHA * h_ref[...] + acc_ref[...]
        o_ref[...] = _ln_rows(y, g_ref[...], b_ref[...])


def _mlp_ln(h, w1, w2, layer, g, b, tm=512, tf=1024):
    n, d = h.shape
    dff = w1.shape[2]
    return pl.pallas_call(
        _mlp_kernel,
        grid=(n // tm, dff // tf),
        in_specs=[pl.BlockSpec((tm, d), lambda i, f: (i, 0)),
                  pl.BlockSpec((None, d, tf), lambda i, f: (layer, 0, f)),
                  pl.BlockSpec((None, tf, d), lambda i, f: (layer, f, 0)),
                  pl.BlockSpec((1, d), lambda i, f: (0, 0)),
                  pl.BlockSpec((1, d), lambda i, f: (0, 0))],
        out_specs=pl.BlockSpec((tm, d), lambda i, f: (i, 0)),
        out_shape=jax.ShapeDtypeStruct((n, d), F32),
        scratch_shapes=[pltpu.VMEM((tm, d), BF16), pltpu.VMEM((tm, d), F32)],
        compiler_params=_cparams("parallel", "arbitrary"),
        name="mlp_ln",
    )(h, w1, w2, g.reshape(1, d), b.reshape(1, d))


def _mla_up_kernel(cq_ref, ckv_ref, kr_ref, gq_ref, gkv_ref, wq_ref, wkv_ref, tc_ref, ts1_ref, ts2_ref,
                   q_ref, k_ref, v_ref, *, scale):
    half = MLA_ROPE // 2
    c, s1, s2 = tc_ref[...], ts1_ref[...], ts2_ref[...]
    nq = _rms_rows(cq_ref[...], gq_ref[...]).astype(BF16)
    nkv = _rms_rows(ckv_ref[...], gkv_ref[...]).astype(BF16)
    qf = _dot(nq, wq_ref[...])
    kvf = _dot(nkv, wkv_ref[...])
    kr = _rope128(kr_ref[...], c, s1, s2, half).astype(BF16)
    for h in range(MLA_HEADS):
        o = h * MLA_SLOT
        q_ref[0, h, :, 0:LANES] = (qf[:, o:o + LANES] * scale).astype(BF16)
        q_ref[0, h, :, LANES:] = (_rope128(qf[:, o + LANES:o + MLA_SLOT], c, s1, s2, half) * scale).astype(BF16)
        k_ref[0, h, :, 0:LANES] = kvf[:, h * LANES:(h + 1) * LANES].astype(BF16)
        k_ref[0, h, :, LANES:] = kr
        v_ref[0, h, :, 0:LANES] = kvf[:, (MLA_HEADS + h) * LANES:(MLA_HEADS + h + 1) * LANES].astype(BF16)
        v_ref[0, h, :, LANES:] = jnp.ones((kr.shape[0], LANES), BF16)


def _mla_up(proj, gq, gkv, wq, wkv, tabs, batch, seq, tm=512):
    nt = seq // tm
    row = lambda b, i: b * nt + i
    tab_spec = pl.BlockSpec((tm, LANES), lambda b, i: (i, 0))
    hm = lambda w: pl.BlockSpec((1, MLA_HEADS, tm, w), lambda b, i: (b, 0, i, 0))
    return pl.pallas_call(
        functools.partial(_mla_up_kernel, scale=(MLA_NOPE + MLA_ROPE) ** -0.5),
        grid=(batch, nt),
        in_specs=[pl.BlockSpec((tm, 512), lambda b, i: (row(b, i), COL512["c_q"])),
                  pl.BlockSpec((tm, 512), lambda b, i: (row(b, i), COL512["c_kv"])),
                  pl.BlockSpec((tm, LANES), lambda b, i: (row(b, i), COL128["k_rope"])),
                  pl.BlockSpec((1, MLA_Q_RANK), lambda b, i: (0, 0)),
                  pl.BlockSpec((1, MLA_KV_RANK), lambda b, i: (0, 0)),
                  pl.BlockSpec(wq.shape, lambda b, i: (0, 0)),
                  pl.BlockSpec(wkv.shape, lambda b, i: (0, 0)),
                  tab_spec, tab_spec, tab_spec],
        out_specs=[hm(MLA_SLOT), hm(MLA_SLOT), hm(2 * HEAD_DIM)],
        out_shape=[jax.ShapeDtypeStruct((batch, MLA_HEADS, seq, MLA_SLOT), BF16),
                   jax.ShapeDtypeStruct((batch, MLA_HEADS, seq, MLA_SLOT), BF16),
                   jax.ShapeDtypeStruct((batch, MLA_HEADS, seq, 2 * HEAD_DIM), BF16)],
        compiler_params=_cparams("parallel", "parallel"),
        name="mla_up",
    )(proj, proj, proj, gq.reshape(1, -1), gkv.reshape(1, -1), wq, wkv, *tabs)


def _mla_attn_kernel(q_ref, k_ref, v_ref, o_ref, acc_sc, *, tq, tk, rs, ahead):
    qi = pl.program_id(1)
    heads = q_ref.shape[1]
    sub = tq // tk
    row = lax.broadcasted_iota(jnp.int32, (tq, tk), 0)
    col = lax.broadcasted_iota(jnp.int32, (tq, tk), 1)

    nrs = tq // rs

    def step(kb, carry, mask):
        r = pl.ds(pl.multiple_of(kb * tk, tk), tk)
        def scores(c):
            h, rows = c // nrs, slice((c % nrs) * rs, (c % nrs + 1) * rs)
            s = _dot_nt(q_ref[0, h, rows, :], k_ref[0, h, r, :])
            return s if mask is None else jnp.where(mask[rows], s, MASKED)

        n_chain = heads * nrs
        out, queue = [], [scores(c) for c in range(ahead)]
        for c in range(n_chain):
            if c + ahead < n_chain:
                queue.append(scores(c + ahead))
            m_new, acc_sc[c] = _flash_update(queue.pop(0), carry[c], acc_sc[c], v_ref[0, c // nrs, r, :])
            out.append(m_new)
        return tuple(out)

    acc_sc[...] = jnp.zeros(acc_sc.shape, F32)
    carry = tuple(jnp.full((rs, 1), NEG_INF, F32) for _ in range(heads * nrs))
    carry = lax.fori_loop(0, qi * sub, lambda kb, c: step(kb, c, None), carry)
    for j in range(sub):
        carry = step(qi * sub + j, carry, col + j * tk <= row)
    for h in range(heads):
        for i in range(nrs):
            acc = acc_sc[h * nrs + i]
            o_ref[0, i * rs:(i + 1) * rs, h * HEAD_DIM:(h + 1) * HEAD_DIM] = (
                acc[:, :HEAD_DIM] / acc[:, HEAD_DIM:]).astype(o_ref.dtype)


def _mla_attn(q, k, v, tq=512, tk=256, rs=256, ahead=1):
    batch, heads, seq, _ = q.shape
    return pl.pallas_call(
        functools.partial(_mla_attn_kernel, tq=tq, tk=tk, rs=rs, ahead=ahead),
        grid=(batch, seq // tq),
        in_specs=[pl.BlockSpec((1, heads, tq, MLA_SLOT), lambda b, i: (b, 0, i, 0)),
                  pl.BlockSpec((1, heads, seq, MLA_SLOT), lambda b, i: (b, 0, 0, 0), pipeline_mode=pl.Buffered(1)),
                  pl.BlockSpec((1, heads, seq, 2 * HEAD_DIM), lambda b, i: (b, 0, 0, 0),
                               pipeline_mode=pl.Buffered(1))],
        out_specs=pl.BlockSpec((1, tq, heads * HEAD_DIM), lambda b, i: (b, i, 0)),
        out_shape=jax.ShapeDtypeStruct((batch, seq, heads * HEAD_DIM), BF16),
        scratch_shapes=[pltpu.VMEM((heads * tq // rs, rs, 2 * HEAD_DIM), F32)],
        compiler_params=_cparams("parallel", "arbitrary"),
        name="mla_attn",
    )(q, k, v)


def _nsa_kernel(nq_ref, kc_ref, vc_ref, ks_ref, vs_ref, kw_ref, vw_ref, ng_ref,
                w1_ref, w2_ref, pos_ref, ovt_ref, tc_ref, ts1_ref, ts2_ref,
                o_ref,
                tmp_sc, kc_sc, vc_sc, ks_sc, vs_sc, kw_sc, vw_sc, acc_sc, *, tq, seq, scale):
    qi = pl.program_id(1)
    half = PARTIAL_ROT // 2
    heads = NSA_HEADS
    n_chunk = seq // NSA_CMP_STRIDE
    n_sel = seq // NSA_SEL_LEN

    @pl.when(qi == 0)
    def _prep():
        c, s1, s2 = tc_ref[...], ts1_ref[...], ts2_ref[...]
        ones = jnp.ones((seq, LANES), BF16)
        blk = lax.broadcasted_iota(jnp.int32, (seq, LANES), 0) >> 6
        ks_sc[:, 0:LANES] = _rope128(ks_ref[...], c, s1, s2, half).astype(BF16)
        ks_sc[:, LANES:] = jnp.where(blk == lax.broadcasted_iota(jnp.int32, (seq, LANES), 1), 1.0, 0.0).astype(BF16)
        kw_sc[...] = _rope128(kw_ref[...], c, s1, s2, half).astype(BF16)
        vs_sc[:, 0:LANES] = vs_ref[...].astype(BF16)
        vs_sc[:, LANES:] = ones
        vw_sc[:, 0:LANES] = vw_ref[...].astype(BF16)
        vw_sc[:, LANES:] = ones
        for i, (src, dst) in enumerate(((kc_ref, kc_sc), (vc_ref, vc_sc))):
            tmp_sc[...] = _rope128(src[...], c, s1, s2, half) if i == 0 else src[...]
            lo, hi = [], []
            for t in range(NSA_CMP_STRIDE):
                x = tmp_sc[pl.ds(t, n_chunk, stride=NSA_CMP_STRIDE), :]
                lo.append((x + pos_ref[i, t:t + 1, :]).astype(BF16))
                hi.append((x + pos_ref[i, NSA_CMP_STRIDE + t:NSA_CMP_STRIDE + t + 1, :]).astype(BF16))
            kw1 = NSA_CMP_STRIDE * HEAD_DIM
            a = _dot(jnp.concatenate(lo, axis=1), w1_ref[i, 0:kw1, :])
            bm = _dot(jnp.concatenate(hi, axis=1), w1_ref[i, kw1:2 * kw1, :])
            hid = jax.nn.gelu(a + pltpu.roll(bm, n_chunk - 1, 0))
            dst[...] = _dot(hid.astype(BF16), w2_ref[i]).astype(BF16)

    t0 = pl.multiple_of(qi * tq, tq)
    rq = pl.ds(t0, tq)
    c, s1, s2 = tc_ref[rq, :], ts1_ref[rq, :], ts2_ref[rq, :]
    qf = nq_ref[...]
    qs = jnp.concatenate(
        [_rope128(qf[:, h * LANES:(h + 1) * LANES], c, s1, s2, half) * scale for h in range(heads)],
        axis=0)
    qs_bf = qs.astype(BF16)

    lane = lax.broadcasted_iota(jnp.int32, (tq, LANES), 1)
    tpos = t0 + lax.broadcasted_iota(jnp.int32, (tq, LANES), 0)

    s = _dot_nt(qs_bf, kc_sc[...]).reshape(heads, tq, LANES)
    cmask = (lane * NSA_CMP_STRIDE + (NSA_CMP_LEN - 1) <= tpos)[None]
    s = jnp.where(cmask, s, NEG_INF)
    m = jnp.max(s, axis=-1, keepdims=True)
    m = jnp.where(m == NEG_INF, 0.0, m)
    e = jnp.exp(s - m)
    p_cmp = e / jnp.maximum(jnp.sum(e, axis=-1, keepdims=True), 1e-30)
    o_cmp = _dot(p_cmp.reshape(heads * tq, LANES).astype(BF16), vc_sc[...]).reshape(heads, tq, HEAD_DIM)

    p_hi, p_lo = _split_bf16(p_cmp[0] + p_cmp[1] + p_cmp[2] + p_cmp[3])
    imp_t = (_dot_nt(ovt_ref[...], p_hi) + _dot_nt(ovt_ref[...], p_lo))[0:n_sel]
    blk_t = lax.broadcasted_iota(jnp.int32, (n_sel, tq), 0)
    cur_t = (t0 + lax.broadcasted_iota(jnp.int32, (n_sel, tq), 1)) >> 6
    eligible = blk_t <= cur_t
    forced = (blk_t == 0) | (blk_t == cur_t) | (blk_t == cur_t - 1)
    score = jnp.where(eligible, jnp.where(forced, NSA_FORCE_SCORE, imp_t), NEG_INF)
    rank = _rank_rows(score, n_sel)
    bias_t = jnp.where(eligible & (rank < NSA_SEL_TOPK), 0.0, MASKED)
    bias_t = jnp.concatenate([bias_t, jnp.zeros((LANES - n_sel, tq), F32)], axis=0)
    bias = _rows_to_lanes(bias_t, tq)
    q_aug = jnp.concatenate([qs, jnp.concatenate([bias] * heads, axis=0)], axis=1).astype(BF16)

    local_r = lax.broadcasted_iota(jnp.int32, (tq, tq), 0)
    local_c = lax.broadcasted_iota(jnp.int32, (tq, tq), 1)
    causal = (local_c <= local_r)[None]

    n_chain = 2
    hpc = heads // n_chain
    crow = hpc * tq

    def run(q_rows, k_sc, v_sc, kb, carry, mask):
        r = pl.ds(pl.multiple_of(kb * tq, tq), tq)

        def scores(c):
            s = _dot_nt(q_rows[c * crow:(c + 1) * crow], k_sc[r, :])
            if mask is None:
                return s
            return jnp.where(mask, s.reshape(hpc, tq, tq), MASKED).reshape(crow, tq)

        s0, s1 = scores(0), scores(1)
        m0, acc_sc[0] = _flash_update(s0, carry[0], acc_sc[0], v_sc[r, :])
        m1, acc_sc[1] = _flash_update(s1, carry[1], acc_sc[1], v_sc[r, :])
        return m0, m1

    def finish():
        acc = jnp.concatenate([acc_sc[0], acc_sc[1]], axis=0)
        return (acc[:, :HEAD_DIM] / acc[:, HEAD_DIM:]).reshape(heads, tq, HEAD_DIM)

    init = (jnp.full((crow, 1), NEG_INF, F32),) * n_chain

    acc_sc[...] = jnp.zeros(acc_sc.shape, F32)
    carry = lax.fori_loop(0, qi, lambda kb, cr: run(q_aug, ks_sc, vs_sc, kb, cr, None), init)
    run(q_aug, ks_sc, vs_sc, qi, carry, causal)
    o_sel = finish()

    acc_sc[...] = jnp.zeros(acc_sc.shape, F32)
    carry = init
    for back, mask in ((2, (local_c > local_r)[None]), (1, None), (0, causal)):
        kb = qi - back
        if back:
            valid = kb >= 0
            mask = valid if mask is None else mask & valid
        carry = run(qs_bf, kw_sc, vw_sc, jnp.maximum(kb, 0), carry, mask)
    o_win = finish()

    g = jax.nn.sigmoid(ng_ref[...])
    for h in range(heads):
        o = (g[:, 3 * h:3 * h + 1] * o_cmp[h] + g[:, 3 * h + 1:3 * h + 2] * o_sel[h]
             + g[:, 3 * h + 2:3 * h + 3] * o_win[h])
        o_ref[:, h * LANES:(h + 1) * LANES] = o.astype(o_ref.dtype)


def _nsa(proj, w1, w2, pos, tabs, batch, seq, tq=256):
    nt = seq // tq
    n_cmp = (seq - NSA_CMP_LEN) // NSA_CMP_STRIDE + 1
    n_sel = seq // NSA_SEL_LEN
    starts = np.arange(LANES) * NSA_CMP_STRIDE
    sel_start = np.arange(LANES) * NSA_SEL_LEN
    overlap = ((starts[:, None] < sel_start[None, :] + NSA_SEL_LEN)
               & (starts[:, None] + NSA_CMP_LEN > sel_start[None, :])
               & (np.arange(LANES)[:, None] < n_cmp) & (np.arange(LANES)[None, :] < n_sel))
    ovt = jnp.asarray(overlap.T.astype(np.float32), BF16)
    seq_col = lambda name: pl.BlockSpec((seq, LANES), lambda b, i: (b, COL128[name]))
    full = lambda a: pl.BlockSpec(a.shape, lambda b, i: (0,) * a.ndim)
    return pl.pallas_call(
        functools.partial(_nsa_kernel, tq=tq, seq=seq, scale=HEAD_DIM ** -0.5),
        grid=(batch, nt),
        in_specs=[pl.BlockSpec((tq, 512), lambda b, i: (b * nt + i, COL512["nq"])),
                  seq_col("nkc"), seq_col("nvc"), seq_col("nks"), seq_col("nvs"), seq_col("nkw"), seq_col("nvw"),
                  pl.BlockSpec((tq, LANES), lambda b, i: (b * nt + i, COL128["ng"])),
                  full(w1), full(w2), full(pos), full(ovt), full(tabs[0]), full(tabs[1]), full(tabs[2])],
        out_specs=pl.BlockSpec((tq, NSA_HEADS * HEAD_DIM), lambda b, i: (b * nt + i, 0)),
        out_shape=jax.ShapeDtypeStruct((batch * seq, NSA_HEADS * HEAD_DIM), BF16),
        scratch_shapes=[pltpu.VMEM((seq, LANES), F32),
                        pltpu.VMEM((LANES, LANES), BF16), pltpu.VMEM((LANES, LANES), BF16),
                        pltpu.VMEM((seq, 2 * LANES), BF16), pltpu.VMEM((seq, 2 * LANES), BF16),
                        pltpu.VMEM((seq, LANES), BF16), pltpu.VMEM((seq, 2 * LANES), BF16),
                        pltpu.VMEM((2, NSA_HEADS // 2 * tq, 2 * HEAD_DIM), F32)],
        compiler_params=_cparams("parallel", "arbitrary"),
        name="nsa",
    )(proj, proj, proj, proj, proj, proj, proj, proj, w1, w2, pos, ovt, *tabs)


def _moba_kernel(mq_ref, mk_ref, mv_ref, tc_ref, ts1_ref, ts2_ref, o_ref, k_sc, v_sc, km_sc, acc_sc,
                 *, tq, seq, scale):
    qi = pl.program_id(1)
    half = PARTIAL_ROT // 2
    heads = MOBA_HEADS
    n_blk = seq // MOBA_BLOCK
    slot = 2 * LANES

    @pl.when(qi == 0)
    def _prep():
        c, s1, s2 = tc_ref[...], ts1_ref[...], ts2_ref[...]
        km_sc[...] = jnp.zeros_like(km_sc)
        ones = jnp.ones((seq, LANES), BF16)
        blk = lax.broadcasted_iota(jnp.int32, (seq, LANES), 0) >> 8
        onehot = jnp.where(blk == lax.broadcasted_iota(jnp.int32, (seq, LANES), 1), 1.0, 0.0).astype(BF16)
        for h in range(heads):
            hs = slice(h * LANES, (h + 1) * LANES)
            kh = _rope128(mk_ref[:, hs], c, s1, s2, half)
            k_sc[:, h * slot:h * slot + LANES] = kh.astype(BF16)
            k_sc[:, h * slot + LANES:(h + 1) * slot] = onehot
            v_sc[:, h * slot:h * slot + LANES] = mv_ref[:, hs].astype(BF16)
            v_sc[:, h * slot + LANES:(h + 1) * slot] = ones
            for j in range(n_blk):
                km_sc[h * n_blk + j:h * n_blk + j + 1, :] = jnp.mean(
                    kh[j * MOBA_BLOCK:(j + 1) * MOBA_BLOCK], axis=0, keepdims=True)

    t0 = pl.multiple_of(qi * tq, tq)
    rq = pl.ds(t0, tq)
    c, s1, s2 = tc_ref[rq, :], ts1_ref[rq, :], ts2_ref[rq, :]
    km_hi, km_lo = _split_bf16(km_sc[...])
    tk = MOBA_BLOCK
    sub = tq // tk
    blk_t = lax.broadcasted_iota(jnp.int32, (n_blk, tq), 0)
    cur_t = (t0 + lax.broadcasted_iota(jnp.int32, (n_blk, tq), 1)) >> 8
    row = lax.broadcasted_iota(jnp.int32, (tq, tk), 0)
    col = lax.broadcasted_iota(jnp.int32, (tq, tk), 1)

    q_aug = []
    for h in range(heads):
        qf = _rope128(mq_ref[:, h * LANES:(h + 1) * LANES], c, s1, s2, half) * scale
        q_hi, q_lo = _split_bf16(qf)
        gate_t = (_dot_nt(km_hi, q_hi) + (_dot_nt(km_lo, q_hi) + _dot_nt(km_hi, q_lo)))[h * n_blk:(h + 1) * n_blk]
        eligible = blk_t < cur_t
        score = jnp.where(eligible, gate_t, NEG_INF)
        picked = eligible & (_rank_rows(score, n_blk) < MOBA_TOPK)
        bias_t = jnp.where(picked | (blk_t == cur_t), 0.0, MASKED)
        bias_t = jnp.concatenate([bias_t, jnp.zeros((LANES - n_blk, tq), F32)], axis=0)
        q_aug.append(jnp.concatenate([qf, _rows_to_lanes(bias_t, tq)], axis=1).astype(BF16))

    n_chain = heads * sub

    def step(kb, carry, mask):
        r = pl.ds(pl.multiple_of(kb * tk, tk), tk)

        def scores(c):
            h, rows = c // sub, slice((c % sub) * tk, (c % sub + 1) * tk)
            s = _dot_nt(q_aug[h][rows], k_sc[r, h * slot:(h + 1) * slot])
            return s if mask is None else jnp.where(mask[rows], s, MASKED)

        out, s_next = [], scores(0)
        for c in range(n_chain):
            s, s_next = s_next, (scores(c + 1) if c + 1 < n_chain else None)
            h = c // sub
            m_new, acc_sc[c] = _flash_update(s, carry[c], acc_sc[c], v_sc[r, h * slot:(h + 1) * slot])
            out.append(m_new)
        return tuple(out)

    acc_sc[...] = jnp.zeros(acc_sc.shape, F32)
    carry = tuple(jnp.full((tk, 1), NEG_INF, F32) for _ in range(n_chain))
    carry = lax.fori_loop(0, qi * sub, lambda kb, cr: step(kb, cr, None), carry)
    for j in range(sub):
        carry = step(qi * sub + j, carry, col + j * tk <= row)
    for c in range(n_chain):
        h, r0 = c // sub, (c % sub) * tk
        acc = acc_sc[c]
        o_ref[r0:r0 + tk, h * LANES:(h + 1) * LANES] = (acc[:, :HEAD_DIM] / acc[:, HEAD_DIM:]).astype(o_ref.dtype)


def _moba(proj, tabs, batch, seq):
    tq = 2 * MOBA_BLOCK
    nt = seq // tq
    width = MOBA_HEADS * HEAD_DIM
    full = lambda a: pl.BlockSpec(a.shape, lambda b, i: (0,) * a.ndim)
    return pl.pallas_call(
        functools.partial(_moba_kernel, tq=tq, seq=seq, scale=HEAD_DIM ** -0.5),
        grid=(batch, nt),
        in_specs=[pl.BlockSpec((tq, width), lambda b, i: (b * nt + i, COL512["mq"])),
                  pl.BlockSpec((seq, width), lambda b, i: (b, COL512["mk"])),
                  pl.BlockSpec((seq, width), lambda b, i: (b, COL512["mv"])),
                  full(tabs[0]), full(tabs[1]), full(tabs[2])],
        out_specs=pl.BlockSpec((tq, width), lambda b, i: (b * nt + i, 0)),
        out_shape=jax.ShapeDtypeStruct((batch * seq, width), BF16),
        scratch_shapes=[pltpu.VMEM((seq, 2 * width), BF16), pltpu.VMEM((seq, 2 * width), BF16),
                        pltpu.VMEM((LANES, LANES), F32),
                        pltpu.VMEM((MOBA_HEADS * tq // MOBA_BLOCK, MOBA_BLOCK, 2 * HEAD_DIM), F32)],
        compiler_params=_cparams("parallel", "arbitrary"),
        name="moba",
    )(proj, proj, proj, *tabs)


def _xattn_kernel(q_ref, kv_ref, o_ref):
    d = q_ref.shape[1]
    hd = d // MEM_HEADS
    for h in range(MEM_HEADS):
        hs = slice(h * hd, (h + 1) * hd)
        s = _dot_nt(q_ref[:, hs], kv_ref[0, :, hs])
        e = jnp.exp(s - jnp.max(s, axis=-1, keepdims=True))
        p = e / jnp.sum(e, axis=-1, keepdims=True)
        o_ref[:, hs] = _dot(p.astype(BF16), kv_ref[0, :, d + h * hd:d + (h + 1) * hd]).astype(o_ref.dtype)


def _xattn(q, kv, batch, seq, tq=512):
    d = q.shape[1]
    nt = seq // tq
    m_len = kv.shape[0] // batch
    return pl.pallas_call(
        _xattn_kernel,
        grid=(batch, nt),
        in_specs=[pl.BlockSpec((tq, d), lambda b, i: (b * nt + i, 0)),
                  pl.BlockSpec((1, m_len, 2 * d), lambda b, i: (b, 0, 0))],
        out_specs=pl.BlockSpec((tq, d), lambda b, i: (b * nt + i, 0)),
        out_shape=jax.ShapeDtypeStruct((batch * seq, d), BF16),
        compiler_params=_cparams("parallel", "parallel"),
        name="xattn",
    )(q, kv.reshape(batch, m_len, 2 * d))


IN_SPLIT_NAMES = ("c_q", "c_kv", "k_rope", "nq", "nkc", "nvc", "nks", "nvs", "nkw", "nvw", "ng", "mq", "mk", "mv")
IN_SPLIT_SIZES = (512, 512, 64, 512, 128, 128, 128, 128, 128, 128, 12, 512, 512, 512)


def _pack_w_in_kernel(w_ref, o_ref):
    rows = w_ref.shape[0]
    off = 0
    for name, sz in zip(IN_SPLIT_NAMES, IN_SPLIT_SIZES):
        width = 512 if name in COL512 else LANES
        dst = COL512[name] * 512 if name in COL512 else COL128[name] * LANES
        x = w_ref[:, off:off + sz].astype(BF16)
        if sz < width:
            x = jnp.concatenate([x, jnp.zeros((rows, width - sz), BF16)], axis=1)
        o_ref[0, :, dst:dst + width] = x
        off += sz


def _pack_w_in(w, tr=256):
    layers, d, width = w.shape
    return pl.pallas_call(
        _pack_w_in_kernel,
        grid=(layers, d // tr),
        in_specs=[pl.BlockSpec((None, tr, width), lambda l, i: (l, i, 0))],
        out_specs=pl.BlockSpec((1, tr, PROJ_WIDTH), lambda l, i: (l, i, 0)),
        out_shape=jax.ShapeDtypeStruct((layers, d, PROJ_WIDTH), BF16),
        compiler_params=_cparams("parallel", "parallel"),
        name="pack_w_in",
    )(w)


def _pack_w_uq(w):
    r = w.shape[0]
    w = w.reshape(r, MLA_HEADS, MLA_NOPE + MLA_ROPE)
    w = jnp.pad(w, ((0, 0), (0, 0), (0, MLA_SLOT - MLA_NOPE - MLA_ROPE)))
    return w.reshape(r, MLA_HEADS * MLA_SLOT).astype(BF16)


def _pack_w_ukv(w):
    r = w.shape[0]
    w = w.reshape(r, MLA_HEADS, 2, HEAD_DIM).transpose(0, 2, 1, 3)
    return w.reshape(r, 2 * MLA_HEADS * HEAD_DIM).astype(BF16)


def kernel(x, mem, ln_in_g, ln_in_b, w_in, mla_q_norm, mla_kv_norm, mla_w_uq, mla_w_ukv, nsa_cmp_w1, nsa_cmp_w2, nsa_cmp_pos, w_out, ln1_g, ln1_b, mem_wq, mem_wkv, mem_wo, ln2_g, ln2_b, mlp_w1, mlp_w2, ln3_g, ln3_b):
    batch, seq, d = x.shape
    n = batch * seq
    mla_tabs = _rope_tables(seq, MLA_ROPE)
    rot_tabs = _rope_tables(seq, PARTIAL_ROT)
    mem2 = mem.reshape(batch * mem.shape[1], d)
    w_in_packed = _pack_w_in(w_in)
    w_out_bf, mem_wkv_bf, mem_wo_bf = w_out.astype(BF16), mem_wkv.astype(BF16), mem_wo.astype(BF16)
    mem_wq_bf = (mem_wq * (d // MEM_HEADS) ** -0.5).astype(BF16)
    mlp_w1_bf, mlp_w2_bf = mlp_w1.astype(BF16), mlp_w2.astype(BF16)

    h = _layer_norm(x.reshape(n, d), ln_in_g, ln_in_b)
    for l in range(DEPTH):
        proj = _matmul(h, w_in_packed, l, F32, name="in_proj")
        q, k, v = _mla_up(proj, mla_q_norm[l], mla_kv_norm[l], _pack_w_uq(mla_w_uq[l]), _pack_w_ukv(mla_w_ukv[l]),
                          mla_tabs, batch, seq)
        o_a = _mla_attn(q, k, v).reshape(n, MLA_HEADS * HEAD_DIM)
        o_b = _nsa(proj, nsa_cmp_w1[l].astype(BF16), nsa_cmp_w2[l].astype(BF16), nsa_cmp_pos[l], rot_tabs, batch, seq)
        o_c = _moba(proj, rot_tabs, batch, seq)
        h = _out_ln([o_a, o_b, o_c], w_out_bf, l, h, ln1_g[l], ln1_b[l], name="mix_out_ln")

        xq = _matmul(h, mem_wq_bf, l, BF16, name="mem_q")
        xkv = _matmul(mem2, mem_wkv_bf, l, BF16, name="mem_kv")
        ctx = _xattn(xq, xkv, batch, seq)
        h = _out_ln([ctx], mem_wo_bf, l, h, ln2_g[l], ln2_b[l], name="mem_out_ln")

        h = _mlp_ln(h, mlp_w1_bf, mlp_w2_bf, l, ln3_g[l], ln3_b[l])
    return h.reshape(batch, seq, d)
```

```python
import functools

import numpy as np
import jax
import jax.numpy as jnp
from jax import lax
from jax.experimental import pallas as pl
from jax.experimental.pallas import tpu as pltpu

F32 = jnp.float32
BF16 = jnp.bfloat16
NEG_INF = float("-inf")

D_MODEL = 2048
DEPTH = 2
HEAD_DIM = 128
MLA_HEADS = 8
NSA_HEADS = 4
MOBA_HEADS = 4
ROPE_THETA = 500000.0
PARTIAL_ROT = HEAD_DIM // 4
MLA_Q_RANK = 512
MLA_KV_RANK = 512
MLA_NOPE = 128
MLA_ROPE = 64
MLA_SLOT = 256
NSA_CMP_LEN = 32
NSA_CMP_STRIDE = 16
NSA_SEL_LEN = 64
NSA_SEL_TOPK = 16
NSA_WINDOW = 512
NSA_FORCE_SCORE = 1.0e4
MOBA_BLOCK = 256
MOBA_TOPK = 3
MEM_HEADS = 4
D_FF = 4 * D_MODEL
DEEPNORM_ALPHA = (2 * DEPTH) ** 0.25
LANES = 128

PROJ_WIDTH = 4096
COL512 = dict(c_q=0, c_kv=1, nq=2, mq=3, mk=4, mv=5)
COL128 = dict(k_rope=24, nkc=25, nvc=26, nks=27, nvs=28, nkw=29, nvw=30, ng=31)

VMEM_LIMIT = 56 * 1024 * 1024


def _cparams(*sem):
    return pltpu.CompilerParams(dimension_semantics=sem, vmem_limit_bytes=VMEM_LIMIT)


def _dot(a, b):
    return jnp.dot(a, b, preferred_element_type=F32)


def _dot_nt(a, b):
    return lax.dot_general(a, b, (((1,), (1,)), ((), ())), preferred_element_type=F32)


def _split_bf16(x):
    hi = x.astype(BF16)
    return hi, (x - hi.astype(F32)).astype(BF16)


def _ln_rows(x, g, b, eps=1e-5):
    mu = jnp.mean(x, axis=-1, keepdims=True)
    xc = x - mu
    var = jnp.mean(xc * xc, axis=-1, keepdims=True)
    return xc * lax.rsqrt(var + eps) * g + b


def _rms_rows(x, g, eps=1e-6):
    return x * lax.rsqrt(jnp.mean(x * x, axis=-1, keepdims=True) + eps) * g


def _rope128(x, c, s1, s2, half):
    return x * c + pltpu.roll(x, LANES - half, 1) * s1 + pltpu.roll(x, half, 1) * s2


def _rope_tables(n_pos, dim):
    half = dim // 2
    inv = ROPE_THETA ** (-jnp.arange(0, dim, 2, dtype=F32) / dim)
    ang = jnp.arange(n_pos, dtype=F32)[:, None] * inv[None, :]
    cos, sin = jnp.cos(ang), jnp.sin(ang)
    ones = jnp.ones((n_pos, LANES - dim), F32)
    z = lambda w: jnp.zeros((n_pos, w), F32)
    c = jnp.concatenate([cos, cos, ones], axis=1)
    s1 = jnp.concatenate([-sin, z(LANES - half)], axis=1)
    s2 = jnp.concatenate([z(half), sin, z(LANES - dim)], axis=1)
    return c, s1, s2


MASKED = -1.0e30


def _rank_rows(score, n_cand):
    row = lax.broadcasted_iota(jnp.int32, score.shape, 0)
    rank = jnp.zeros(score.shape, F32)
    for jp in range(n_cand):
        cand = score[jp:jp + 1, :]
        ahead = (cand > score) | ((cand == score) & (row > jp))
        rank = rank + jnp.where(ahead, 1.0, 0.0)
    return rank


def _rows_to_lanes(x_t, tq):
    eye = jnp.where(lax.broadcasted_iota(jnp.int32, (tq, tq), 0) == lax.broadcasted_iota(jnp.int32, (tq, tq), 1),
                    1.0, 0.0).astype(BF16)
    return _dot_nt(eye, x_t.astype(BF16))


def _flash_update(s, m, acc, v_aug):
    m_new = jnp.maximum(m, jnp.max(s, axis=-1, keepdims=True))
    p = jnp.exp(s - m_new).astype(BF16)
    return m_new, jnp.exp(m - m_new) * acc + _dot(p, v_aug)


def _ln_kernel(x_ref, g_ref, b_ref, o_ref):
    o_ref[...] = _ln_rows(x_ref[...], g_ref[...], b_ref[...])


def _layer_norm(x, g, b, tm=512):
    n, d = x.shape
    return pl.pallas_call(
        _ln_kernel,
        grid=(n // tm,),
        in_specs=[pl.BlockSpec((tm, d), lambda i: (i, 0)),
                  pl.BlockSpec((1, d), lambda i: (0, 0)),
                  pl.BlockSpec((1, d), lambda i: (0, 0))],
        out_specs=pl.BlockSpec((tm, d), lambda i: (i, 0)),
        out_shape=jax.ShapeDtypeStruct((n, d), F32),
        compiler_params=_cparams("parallel"),
        name="ln_in",
    )(x, g.reshape(1, d), b.reshape(1, d))


def _mm_kernel(a_ref, w_ref, o_ref, abf_ref):
    @pl.when(pl.program_id(1) == 0)
    def _():
        abf_ref[...] = a_ref[...].astype(BF16)

    o_ref[...] = _dot(abf_ref[...], w_ref[...]).astype(o_ref.dtype)


def _matmul(a, w, layer, out_dtype, tm=1024, tn=512, name="mm"):
    m, k = a.shape
    n = w.shape[2]
    tm = min(tm, m)
    return pl.pallas_call(
        _mm_kernel,
        grid=(m // tm, n // tn),
        in_specs=[pl.BlockSpec((tm, k), lambda i, j: (i, 0)),
                  pl.BlockSpec((None, k, tn), lambda i, j: (layer, 0, j))],
        out_specs=pl.BlockSpec((tm, tn), lambda i, j: (i, j)),
        out_shape=jax.ShapeDtypeStruct((m, n), out_dtype),
        scratch_shapes=[pltpu.VMEM((tm, k), BF16)],
        compiler_params=_cparams("parallel", "arbitrary"),
        name=name,
    )(a, w)


def _out_ln_kernel(*refs, widths):
    n_a = len(widths)
    a_refs = refs[:n_a]
    w_ref, h_ref, g_ref, b_ref, o_ref = refs[n_a:]
    tm = h_ref.shape[0]
    for rows in (slice(0, tm // 2), slice(tm // 2, tm)):
        acc = DEEPNORM_ALPHA * h_ref[rows, :]
        off = 0
        for a_ref, wd in zip(a_refs, widths):
            acc = acc + _dot(a_ref[rows, :], w_ref[off:off + wd, :])
            off += wd
        o_ref[rows, :] = _ln_rows(acc, g_ref[...], b_ref[...])


def _out_ln(a_list, w, layer, h, g, b, tm=512, name="out_ln"):
    n, d = h.shape
    widths = tuple(a.shape[1] for a in a_list)
    k = sum(widths)
    in_specs = [pl.BlockSpec((tm, wd), lambda i: (i, 0)) for wd in widths]
    in_specs += [pl.BlockSpec((None, k, d), lambda i: (layer, 0, 0)),
                 pl.BlockSpec((tm, d), lambda i: (i, 0)),
                 pl.BlockSpec((1, d), lambda i: (0, 0)),
                 pl.BlockSpec((1, d), lambda i: (0, 0))]
    return pl.pallas_call(
        functools.partial(_out_ln_kernel, widths=widths),
        grid=(n // tm,),
        in_specs=in_specs,
        out_specs=pl.BlockSpec((tm, d), lambda i: (i, 0)),
        out_shape=jax.ShapeDtypeStruct((n, d), F32),
        compiler_params=_cparams("parallel"),
        name=name,
    )(*a_list, w, h, g.reshape(1, d), b.reshape(1, d))


def _mlp_kernel(h_ref, w1_ref, w2_ref, g_ref, b_ref, o_ref, hbf_ref, acc_ref):
    f = pl.program_id(1)

    @pl.when(f == 0)
    def _():
        hbf_ref[...] = h_ref[...].astype(BF16)
        acc_ref[...] = jnp.zeros_like(acc_ref)

    u = jnp.maximum(_dot(hbf_ref[...], w1_ref[...]), 0.0)
    acc_ref[...] += _dot((u * u).astype(BF16), w2_ref[...])

    @pl.when(f == pl.num_programs(1) - 1)
    def _():
        y = DEEPNORM_ALPHA * h_ref[...] + acc_ref[...]
        o_ref[...] = _ln_rows(y, g_ref[...], b_ref[...])


def _mlp_ln(h, w1, w2, layer, g, b, tm=512, tf=1024):
    n, d = h.shape
    dff = w1.shape[2]
    return pl.pallas_call(
        _mlp_kernel,
        grid=(n // tm, dff // tf),
        in_specs=[pl.BlockSpec((tm, d), lambda i, f: (i, 0)),
                  pl.BlockSpec((None, d, tf), lambda i, f: (layer, 0, f)),
                  pl.BlockSpec((None, tf, d), lambda i, f: (layer, f, 0)),
                  pl.BlockSpec((1, d), lambda i, f: (0, 0)),
                  pl.BlockSpec((1, d), lambda i, f: (0, 0))],
        out_specs=pl.BlockSpec((tm, d), lambda i, f: (i, 0)),
        out_shape=jax.ShapeDtypeStruct((n, d), F32),
        scratch_shapes=[pltpu.VMEM((tm, d), BF16), pltpu.VMEM((tm, d), F32)],
        compiler_params=_cparams("parallel", "arbitrary"),
        name="mlp_ln",
    )(h, w1, w2, g.reshape(1, d), b.reshape(1, d))


def _mla_up_kernel(cq_ref, ckv_ref, kr_ref, gq_ref, gkv_ref, wq_ref, wkv_ref, tc_ref, ts1_ref, ts2_ref,
                   q_ref, k_ref, v_ref, *, scale):
    half = MLA_ROPE // 2
    c, s1, s2 = tc_ref[...], ts1_ref[...], ts2_ref[...]
    nq = _rms_rows(cq_ref[...], gq_ref[...]).astype(BF16)
    nkv = _rms_rows(ckv_ref[...], gkv_ref[...]).astype(BF16)
    qf = _dot(nq, wq_ref[...])
    kvf = _dot(nkv, wkv_ref[...])
    kr = _rope128(kr_ref[...], c, s1, s2, half).astype(BF16)
    for h in range(MLA_HEADS):
        o = h * MLA_SLOT
        q_ref[0, h, :, 0:LANES] = (qf[:, o:o + LANES] * scale).astype(BF16)
        q_ref[0, h, :, LANES:] = (_rope128(qf[:, o + LANES:o + MLA_SLOT], c, s1, s2, half) * scale).astype(BF16)
        k_ref[0, h, :, 0:LANES] = kvf[:, h * LANES:(h + 1) * LANES].astype(BF16)
        k_ref[0, h, :, LANES:] = kr
        v_ref[0, h, :, 0:LANES] = kvf[:, (MLA_HEADS + h) * LANES:(MLA_HEADS + h + 1) * LANES].astype(BF16)
        v_ref[0, h, :, LANES:] = jnp.ones((kr.shape[0], LANES), BF16)


def _mla_up(proj, gq, gkv, wq, wkv, tabs, batch, seq, tm=512):
    nt = seq // tm
    row = lambda b, i: b * nt + i
    tab_spec = pl.BlockSpec((tm, LANES), lambda b, i: (i, 0))
    hm = lambda w: pl.BlockSpec((1, MLA_HEADS, tm, w), lambda b, i: (b, 0, i, 0))
    return pl.pallas_call(
        functools.partial(_mla_up_kernel, scale=(MLA_NOPE + MLA_ROPE) ** -0.5),
        grid=(batch, nt),
        in_specs=[pl.BlockSpec((tm, 512), lambda b, i: (row(b, i), COL512["c_q"])),
                  pl.BlockSpec((tm, 512), lambda b, i: (row(b, i), COL512["c_kv"])),
                  pl.BlockSpec((tm, LANES), lambda b, i: (row(b, i), COL128["k_rope"])),
                  pl.BlockSpec((1, MLA_Q_RANK), lambda b, i: (0, 0)),
                  pl.BlockSpec((1, MLA_KV_RANK), lambda b, i: (0, 0)),
                  pl.BlockSpec(wq.shape, lambda b, i: (0, 0)),
                  pl.BlockSpec(wkv.shape, lambda b, i: (0, 0)),
                  tab_spec, tab_spec, tab_spec],
        out_specs=[hm(MLA_SLOT), hm(MLA_SLOT), hm(2 * HEAD_DIM)],
        out_shape=[jax.ShapeDtypeStruct((batch, MLA_HEADS, seq, MLA_SLOT), BF16),
                   jax.ShapeDtypeStruct((batch, MLA_HEADS, seq, MLA_SLOT), BF16),
                   jax.ShapeDtypeStruct((batch, MLA_HEADS, seq, 2 * HEAD_DIM), BF16)],
        compiler_params=_cparams("parallel", "parallel"),
        name="mla_up",
    )(proj, proj, proj, gq.reshape(1, -1), gkv.reshape(1, -1), wq, wkv, *tabs)


def _mla_attn_kernel(q_ref, k_ref, v_ref, o_ref, acc_sc, *, tq, tk, rs, ahead):
    qi = pl.program_id(1)
    heads = q_ref.shape[1]
    sub = tq // tk
    row = lax.broadcasted_iota(jnp.int32, (tq, tk), 0)
    col = lax.broadcasted_iota(jnp.int32, (tq, tk), 1)

    nrs = tq // rs

    def step(kb, carry, mask):
        r = pl.ds(pl.multiple_of(kb * tk, tk), tk)
        def scores(c):
            h, rows = c // nrs, slice((c % nrs) * rs, (c % nrs + 1) * rs)
            s = _dot_nt(q_ref[0, h, rows, :], k_ref[0, h, r, :])
            return s if mask is None else jnp.where(mask[rows], s, MASKED)

        n_chain = heads * nrs
        out, queue = [], [scores(c) for c in range(ahead)]
        for c in range(n_chain):
            if c + ahead < n_chain:
                queue.append(scores(c + ahead))
            m_new, acc_sc[c] = _flash_update(queue.pop(0), carry[c], acc_sc[c], v_ref[0, c // nrs, r, :])
            out.append(m_new)
        return tuple(out)

    acc_sc[...] = jnp.zeros(acc_sc.shape, F32)
    carry = tuple(jnp.full((rs, 1), NEG_INF, F32) for _ in range(heads * nrs))
    carry = lax.fori_loop(0, qi * sub, lambda kb, c: step(kb, c, None), carry)
    for j in range(sub):
        carry = step(qi * sub + j, carry, col + j * tk <= row)
    for h in range(heads):
        for i in range(nrs):
            acc = acc_sc[h * nrs + i]
            o_ref[0, i * rs:(i + 1) * rs, h * HEAD_DIM:(h + 1) * HEAD_DIM] = (
                acc[:, :HEAD_DIM] / acc[:, HEAD_DIM:]).astype(o_ref.dtype)


def _mla_attn(q, k, v, tq=512, tk=512, rs=256, ahead=1):
    batch, heads, seq, _ = q.shape
    return pl.pallas_call(
        functools.partial(_mla_attn_kernel, tq=tq, tk=tk, rs=rs, ahead=ahead),
        grid=(batch, seq // tq),
        in_specs=[pl.BlockSpec((1, heads, tq, MLA_SLOT), lambda b, i: (b, 0, i, 0)),
                  pl.BlockSpec((1, heads, seq, MLA_SLOT), lambda b, i: (b, 0, 0, 0), pipeline_mode=pl.Buffered(1)),
                  pl.BlockSpec((1, heads, seq, 2 * HEAD_DIM), lambda b, i: (b, 0, 0, 0),
                               pipeline_mode=pl.Buffered(1))],
        out_specs=pl.BlockSpec((1, tq, heads * HEAD_DIM), lambda b, i: (b, i, 0)),
        out_shape=jax.ShapeDtypeStruct((batch, seq, heads * HEAD_DIM), BF16),
        scratch_shapes=[pltpu.VMEM((heads * tq // rs, rs, 2 * HEAD_DIM), F32)],
        compiler_params=_cparams("parallel", "arbitrary"),
        name="mla_attn",
    )(q, k, v)


def _nsa_kernel(nq_ref, kc_ref, vc_ref, ks_ref, vs_ref, kw_ref, vw_ref, ng_ref,
                w1_ref, w2_ref, pos_ref, ovt_ref, tc_ref, ts1_ref, ts2_ref,
                o_ref,
                tmp_sc, kc_sc, vc_sc, ks_sc, vs_sc, kw_sc, vw_sc, acc_sc, *, tq, seq, scale):
    qi = pl.program_id(1)
    half = PARTIAL_ROT // 2
    heads = NSA_HEADS
    n_chunk = seq // NSA_CMP_STRIDE
    n_sel = seq // NSA_SEL_LEN

    @pl.when(qi == 0)
    def _prep():
        c, s1, s2 = tc_ref[...], ts1_ref[...], ts2_ref[...]
        ones = jnp.ones((seq, LANES), BF16)
        blk = lax.broadcasted_iota(jnp.int32, (seq, LANES), 0) >> 6
        ks_sc[:, 0:LANES] = _rope128(ks_ref[...], c, s1, s2, half).astype(BF16)
        ks_sc[:, LANES:] = jnp.where(blk == lax.broadcasted_iota(jnp.int32, (seq, LANES), 1), 1.0, 0.0).astype(BF16)
        kw_sc[...] = _rope128(kw_ref[...], c, s1, s2, half).astype(BF16)
        vs_sc[:, 0:LANES] = vs_ref[...].astype(BF16)
        vs_sc[:, LANES:] = ones
        vw_sc[:, 0:LANES] = vw_ref[...].astype(BF16)
        vw_sc[:, LANES:] = ones
        for i, (src, dst) in enumerate(((kc_ref, kc_sc), (vc_ref, vc_sc))):
            tmp_sc[...] = _rope128(src[...], c, s1, s2, half) if i == 0 else src[...]
            lo, hi = [], []
            for t in range(NSA_CMP_STRIDE):
                x = tmp_sc[pl.ds(t, n_chunk, stride=NSA_CMP_STRIDE), :]
                lo.append((x + pos_ref[i, t:t + 1, :]).astype(BF16))
                hi.append((x + pos_ref[i, NSA_CMP_STRIDE + t:NSA_CMP_STRIDE + t + 1, :]).astype(BF16))
            kw1 = NSA_CMP_STRIDE * HEAD_DIM
            a = _dot(jnp.concatenate(lo, axis=1), w1_ref[i, 0:kw1, :])
            bm = _dot(jnp.concatenate(hi, axis=1), w1_ref[i, kw1:2 * kw1, :])
            hid = jax.nn.gelu(a + pltpu.roll(bm, n_chunk - 1, 0))
            dst[...] = _dot(hid.astype(BF16), w2_ref[i]).astype(BF16)

    t0 = pl.multiple_of(qi * tq, tq)
    rq = pl.ds(t0, tq)
    c, s1, s2 = tc_ref[rq, :], ts1_ref[rq, :], ts2_ref[rq, :]
    qf = nq_ref[...]
    qs = jnp.concatenate(
        [_rope128(qf[:, h * LANES:(h + 1) * LANES], c, s1, s2, half) * scale for h in range(heads)],
        axis=0)
    qs_bf = qs.astype(BF16)

    lane = lax.broadcasted_iota(jnp.int32, (tq, LANES), 1)
    tpos = t0 + lax.broadcasted_iota(jnp.int32, (tq, LANES), 0)

    s = _dot_nt(qs_bf, kc_sc[...]).reshape(heads, tq, LANES)
    cmask = (lane * NSA_CMP_STRIDE + (NSA_CMP_LEN - 1) <= tpos)[None]
    s = jnp.where(cmask, s, NEG_INF)
    m = jnp.max(s, axis=-1, keepdims=True)
    m = jnp.where(m == NEG_INF, 0.0, m)
    e = jnp.exp(s - m)
    p_cmp = e / jnp.maximum(jnp.sum(e, axis=-1, keepdims=True), 1e-30)
    o_cmp = _dot(p_cmp.reshape(heads * tq, LANES).astype(BF16), vc_sc[...]).reshape(heads, tq, HEAD_DIM)

    p_hi, p_lo = _split_bf16(p_cmp[0] + p_cmp[1] + p_cmp[2] + p_cmp[3])
    imp_t = (_dot_nt(ovt_ref[...], p_hi) + _dot_nt(ovt_ref[...], p_lo))[0:n_sel]
    blk_t = lax.broadcasted_iota(jnp.int32, (n_sel, tq), 0)
    cur_t = (t0 + lax.broadcasted_iota(jnp.int32, (n_sel, tq), 1)) >> 6
    eligible = blk_t <= cur_t
    forced = (blk_t == 0) | (blk_t == cur_t) | (blk_t == cur_t - 1)
    score = jnp.where(eligible, jnp.where(forced, NSA_FORCE_SCORE, imp_t), NEG_INF)
    rank = _rank_rows(score, n_sel)
    bias_t = jnp.where(eligible & (rank < NSA_SEL_TOPK), 0.0, MASKED)
    bias_t = jnp.concatenate([bias_t, jnp.zeros((LANES - n_sel, tq), F32)], axis=0)
    bias = _rows_to_lanes(bias_t, tq)
    q_aug = jnp.concatenate([qs, jnp.concatenate([bias] * heads, axis=0)], axis=1).astype(BF16)

    local_r = lax.broadcasted_iota(jnp.int32, (tq, tq), 0)
    local_c = lax.broadcasted_iota(jnp.int32, (tq, tq), 1)
    causal = (local_c <= local_r)[None]

    n_chain = 2
    hpc = heads // n_chain
    crow = hpc * tq

    def run(q_rows, k_sc, v_sc, kb, carry, mask):
        r = pl.ds(pl.multiple_of(kb * tq, tq), tq)

        def scores(c):
            s = _dot_nt(q_rows[c * crow:(c + 1) * crow], k_sc[r, :])
            if mask is None:
                return s
            return jnp.where(mask, s.reshape(hpc, tq, tq), MASKED).reshape(crow, tq)

        s0, s1 = scores(0), scores(1)
        m0, acc_sc[0] = _flash_update(s0, carry[0], acc_sc[0], v_sc[r, :])
        m1, acc_sc[1] = _flash_update(s1, carry[1], acc_sc[1], v_sc[r, :])
        return m0, m1

    def finish():
        acc = jnp.concatenate([acc_sc[0], acc_sc[1]], axis=0)
        return (acc[:, :HEAD_DIM] / acc[:, HEAD_DIM:]).reshape(heads, tq, HEAD_DIM)

    init = (jnp.full((crow, 1), NEG_INF, F32),) * n_chain

    acc_sc[...] = jnp.zeros(acc_sc.shape, F32)
    carry = lax.fori_loop(0, qi, lambda kb, cr: run(q_aug, ks_sc, vs_sc, kb, cr, None), init)
    run(q_aug, ks_sc, vs_sc, qi, carry, causal)
    o_sel = finish()

    acc_sc[...] = jnp.zeros(acc_sc.shape, F32)
    carry = init
    for back, mask in ((2, (local_c > local_r)[None]), (1, None), (0, causal)):
        kb = qi - back
        if back:
            valid = kb >= 0
            mask = valid if mask is None else mask & valid
        carry = run(qs_bf, kw_sc, vw_sc, jnp.maximum(kb, 0), carry, mask)
    o_win = finish()

    g = jax.nn.sigmoid(ng_ref[...])
    for h in range(heads):
        o = (g[:, 3 * h:3 * h + 1] * o_cmp[h] + g[:, 3 * h + 1:3 * h + 2] * o_sel[h]
             + g[:, 3 * h + 2:3 * h + 3] * o_win[h])
        o_ref[:, h * LANES:(h + 1) * LANES] = o.astype(o_ref.dtype)


def _nsa(proj, w1, w2, pos, tabs, batch, seq, tq=256):
    nt = seq // tq
    n_cmp = (seq - NSA_CMP_LEN) // NSA_CMP_STRIDE + 1
    n_sel = seq // NSA_SEL_LEN
    starts = np.arange(LANES) * NSA_CMP_STRIDE
    sel_start = np.arange(LANES) * NSA_SEL_LEN
    overlap = ((starts[:, None] < sel_start[None, :] + NSA_SEL_LEN)
               & (starts[:, None] + NSA_CMP_LEN > sel_start[None, :])
               & (np.arange(LANES)[:, None] < n_cmp) & (np.arange(LANES)[None, :] < n_sel))
    ovt = jnp.asarray(overlap.T.astype(np.float32), BF16)
    seq_col = lambda name: pl.BlockSpec((seq, LANES), lambda b, i: (b, COL128[name]))
    full = lambda a: pl.BlockSpec(a.shape, lambda b, i: (0,) * a.ndim)
    return pl.pallas_call(
        functools.partial(_nsa_kernel, tq=tq, seq=seq, scale=HEAD_DIM ** -0.5),
        grid=(batch, nt),
        in_specs=[pl.BlockSpec((tq, 512), lambda b, i: (b * nt + i, COL512["nq"])),
                  seq_col("nkc"), seq_col("nvc"), seq_col("nks"), seq_col("nvs"), seq_col("nkw"), seq_col("nvw"),
                  pl.BlockSpec((tq, LANES), lambda b, i: (b * nt + i, COL128["ng"])),
                  full(w1), full(w2), full(pos), full(ovt), full(tabs[0]), full(tabs[1]), full(tabs[2])],
        out_specs=pl.BlockSpec((tq, NSA_HEADS * HEAD_DIM), lambda b, i: (b * nt + i, 0)),
        out_shape=jax.ShapeDtypeStruct((batch * seq, NSA_HEADS * HEAD_DIM), BF16),
        scratch_shapes=[pltpu.VMEM((seq, LANES), F32),
                        pltpu.VMEM((LANES, LANES), BF16), pltpu.VMEM((LANES, LANES), BF16),
                        pltpu.VMEM((seq, 2 * LANES), BF16), pltpu.VMEM((seq, 2 * LANES), BF16),
                        pltpu.VMEM((seq, LANES), BF16), pltpu.VMEM((seq, 2 * LANES), BF16),
                        pltpu.VMEM((2, NSA_HEADS // 2 * tq, 2 * HEAD_DIM), F32)],
        compiler_params=_cparams("parallel", "arbitrary"),
        name="nsa",
    )(proj, proj, proj, proj, proj, proj, proj, proj, w1, w2, pos, ovt, *tabs)


def _moba_kernel(mq_ref, mk_ref, mv_ref, tc_ref, ts1_ref, ts2_ref, o_ref, k_sc, v_sc, km_sc, acc_sc,
                 *, tq, seq, scale):
    qi = pl.program_id(1)
    half = PARTIAL_ROT // 2
    heads = MOBA_HEADS
    n_blk = seq // MOBA_BLOCK
    slot = 2 * LANES

    @pl.when(qi == 0)
    def _prep():
        c, s1, s2 = tc_ref[...], ts1_ref[...], ts2_ref[...]
        km_sc[...] = jnp.zeros_like(km_sc)
        ones = jnp.ones((seq, LANES), BF16)
        blk = lax.broadcasted_iota(jnp.int32, (seq, LANES), 0) >> 8
        onehot = jnp.where(blk == lax.broadcasted_iota(jnp.int32, (seq, LANES), 1), 1.0, 0.0).astype(BF16)
        for h in range(heads):
            hs = slice(h * LANES, (h + 1) * LANES)
            kh = _rope128(mk_ref[:, hs], c, s1, s2, half)
            k_sc[:, h * slot:h * slot + LANES] = kh.astype(BF16)
            k_sc[:, h * slot + LANES:(h + 1) * slot] = onehot
            v_sc[:, h * slot:h * slot + LANES] = mv_ref[:, hs].astype(BF16)
            v_sc[:, h * slot + LANES:(h + 1) * slot] = ones
            for j in range(n_blk):
                km_sc[h * n_blk + j:h * n_blk + j + 1, :] = jnp.mean(
                    kh[j * MOBA_BLOCK:(j + 1) * MOBA_BLOCK], axis=0, keepdims=True)

    t0 = pl.multiple_of(qi * tq, tq)
    rq = pl.ds(t0, tq)
    c, s1, s2 = tc_ref[rq, :], ts1_ref[rq, :], ts2_ref[rq, :]
    km_hi, km_lo = _split_bf16(km_sc[...])
    tk = tq
    rc = MOBA_BLOCK
    sub = tq // rc
    blk_t = lax.broadcasted_iota(jnp.int32, (n_blk, tq), 0)
    cur_t = (t0 + lax.broadcasted_iota(jnp.int32, (n_blk, tq), 1)) >> 8
    row = lax.broadcasted_iota(jnp.int32, (tq, tk), 0)
    col = lax.broadcasted_iota(jnp.int32, (tq, tk), 1)

    q_aug = []
    for h in range(heads):
        qf = _rope128(mq_ref[:, h * LANES:(h + 1) * LANES], c, s1, s2, half) * scale
        q_hi, q_lo = _split_bf16(qf)
        gate_t = (_dot_nt(km_hi, q_hi) + (_dot_nt(km_lo, q_hi) + _dot_nt(km_hi, q_lo)))[h * n_blk:(h + 1) * n_blk]
        eligible = blk_t < cur_t
        score = jnp.where(eligible, gate_t, NEG_INF)
        picked = eligible & (_rank_rows(score, n_blk) < MOBA_TOPK)
        bias_t = jnp.where(picked | (blk_t == cur_t), 0.0, MASKED)
        bias_t = jnp.concatenate([bias_t, jnp.zeros((LANES - n_blk, tq), F32)], axis=0)
        q_aug.append(jnp.concatenate([qf, _rows_to_lanes(bias_t, tq)], axis=1).astype(BF16))

    n_chain = heads * sub

    def step(kb, carry, mask):
        r = pl.ds(pl.multiple_of(kb * tk, tk), tk)

        def scores(c):
            h, rows = c // sub, slice((c % sub) * rc, (c % sub + 1) * rc)
            s = _dot_nt(q_aug[h][rows], k_sc[r, h * slot:(h + 1) * slot])
            return s if mask is None else jnp.where(mask[rows], s, MASKED)

        out, s_next = [], scores(0)
        for c in range(n_chain):
            s, s_next = s_next, (scores(c + 1) if c + 1 < n_chain else None)
            h = c // sub
            m_new, acc_sc[c] = _flash_update(s, carry[c], acc_sc[c], v_sc[r, h * slot:(h + 1) * slot])
            out.append(m_new)
        return tuple(out)

    acc_sc[...] = jnp.zeros(acc_sc.shape, F32)
    carry = tuple(jnp.full((rc, 1), NEG_INF, F32) for _ in range(n_chain))
    carry = lax.fori_loop(0, qi, lambda kb, cr: step(kb, cr, None), carry)
    carry = step(qi, carry, col <= row)
    for c in range(n_chain):
        h, r0 = c // sub, (c % sub) * rc
        acc = acc_sc[c]
        o_ref[r0:r0 + rc, h * LANES:(h + 1) * LANES] = (acc[:, :HEAD_DIM] / acc[:, HEAD_DIM:]).astype(o_ref.dtype)


def _moba(proj, tabs, batch, seq):
    tq = 2 * MOBA_BLOCK
    nt = seq // tq
    width = MOBA_HEADS * HEAD_DIM
    full = lambda a: pl.BlockSpec(a.shape, lambda b, i: (0,) * a.ndim)
    return pl.pallas_call(
        functools.partial(_moba_kernel, tq=tq, seq=seq, scale=HEAD_DIM ** -0.5),
        grid=(batch, nt),
        in_specs=[pl.BlockSpec((tq, width), lambda b, i: (b * nt + i, COL512["mq"])),
                  pl.BlockSpec((seq, width), lambda b, i: (b, COL512["mk"])),
                  pl.BlockSpec((seq, width), lambda b, i: (b, COL512["mv"])),
                  full(tabs[0]), full(tabs[1]), full(tabs[2])],
        out_specs=pl.BlockSpec((tq, width), lambda b, i: (b * nt + i, 0)),
        out_shape=jax.ShapeDtypeStruct((batch * seq, width), BF16),
        scratch_shapes=[pltpu.VMEM((seq, 2 * width), BF16), pltpu.VMEM((seq, 2 * width), BF16),
                        pltpu.VMEM((LANES, LANES), F32),
                        pltpu.VMEM((MOBA_HEADS * tq // MOBA_BLOCK, MOBA_BLOCK, 2 * HEAD_DIM), F32)],
        compiler_params=_cparams("parallel", "arbitrary"),
        name="moba",
    )(proj, proj, proj, *tabs)


def _xattn_kernel(q_ref, kv_ref, o_ref):
    d = q_ref.shape[1]
    hd = d // MEM_HEADS
    for h in range(MEM_HEADS):
        hs = slice(h * hd, (h + 1) * hd)
        s = _dot_nt(q_ref[:, hs], kv_ref[0, :, hs])
        e = jnp.exp(s - jnp.max(s, axis=-1, keepdims=True))
        p = e / jnp.sum(e, axis=-1, keepdims=True)
        o_ref[:, hs] = _dot(p.astype(BF16), kv_ref[0, :, d + h * hd:d + (h + 1) * hd]).astype(o_ref.dtype)


def _xattn(q, kv, batch, seq, tq=512):
    d = q.shape[1]
    nt = seq // tq
    m_len = kv.shape[0] // batch
    return pl.pallas_call(
        _xattn_kernel,
        grid=(batch, nt),
        in_specs=[pl.BlockSpec((tq, d), lambda b, i: (b * nt + i, 0)),
                  pl.BlockSpec((1, m_len, 2 * d), lambda b, i: (b, 0, 0))],
        out_specs=pl.BlockSpec((tq, d), lambda b, i: (b * nt + i, 0)),
        out_shape=jax.ShapeDtypeStruct((batch * seq, d), BF16),
        compiler_params=_cparams("parallel", "parallel"),
        name="xattn",
    )(q, kv.reshape(batch, m_len, 2 * d))


IN_SPLIT_NAMES = ("c_q", "c_kv", "k_rope", "nq", "nkc", "nvc", "nks", "nvs", "nkw", "nvw", "ng", "mq", "mk", "mv")
IN_SPLIT_SIZES = (512, 512, 64, 512, 128, 128, 128, 128, 128, 128, 12, 512, 512, 512)


def _pack_w_in_kernel(w_ref, o_ref):
    rows = w_ref.shape[0]
    off = 0
    for name, sz in zip(IN_SPLIT_NAMES, IN_SPLIT_SIZES):
        width = 512 if name in COL512 else LANES
        dst = COL512[name] * 512 if name in COL512 else COL128[name] * LANES
        x = w_ref[:, off:off + sz].astype(BF16)
        if sz < width:
            x = jnp.concatenate([x, jnp.zeros((rows, width - sz), BF16)], axis=1)
        o_ref[0, :, dst:dst + width] = x
        off += sz


def _pack_w_in(w, tr=256):
    layers, d, width = w.shape
    return pl.pallas_call(
        _pack_w_in_kernel,
        grid=(layers, d // tr),
        in_specs=[pl.BlockSpec((None, tr, width), lambda l, i: (l, i, 0))],
        out_specs=pl.BlockSpec((1, tr, PROJ_WIDTH), lambda l, i: (l, i, 0)),
        out_shape=jax.ShapeDtypeStruct((layers, d, PROJ_WIDTH), BF16),
        compiler_params=_cparams("parallel", "parallel"),
        name="pack_w_in",
    )(w)


def _pack_w_uq(w):
    r = w.shape[0]
    w = w.reshape(r, MLA_HEADS, MLA_NOPE + MLA_ROPE)
    w = jnp.pad(w, ((0, 0), (0, 0), (0, MLA_SLOT - MLA_NOPE - MLA_ROPE)))
    return w.reshape(r, MLA_HEADS * MLA_SLOT).astype(BF16)


def _pack_w_ukv(w):
    r = w.shape[0]
    w = w.reshape(r, MLA_HEADS, 2, HEAD_DIM).transpose(0, 2, 1, 3)
    return w.reshape(r, 2 * MLA_HEADS * HEAD_DIM).astype(BF16)


def kernel(x, mem, ln_in_g, ln_in_b, w_in, mla_q_norm, mla_kv_norm, mla_w_uq, mla_w_ukv, nsa_cmp_w1, nsa_cmp_w2, nsa_cmp_pos, w_out, ln1_g, ln1_b, mem_wq, mem_wkv, mem_wo, ln2_g, ln2_b, mlp_w1, mlp_w2, ln3_g, ln3_b):
    batch, seq, d = x.shape
    n = batch * seq
    mla_tabs = _rope_tables(seq, MLA_ROPE)
    rot_tabs = _rope_tables(seq, PARTIAL_ROT)
    mem2 = mem.reshape(batch * mem.shape[1], d)
    w_in_packed = _pack_w_in(w_in)
    w_out_bf, mem_wkv_bf, mem_wo_bf = w_out.astype(BF16), mem_wkv.astype(BF16), mem_wo.astype(BF16)
    mem_wq_bf = (mem_wq * (d // MEM_HEADS) ** -0.5).astype(BF16)
    mlp_w1_bf, mlp_w2_bf = mlp_w1.astype(BF16), mlp_w2.astype(BF16)

    h = _layer_norm(x.reshape(n, d), ln_in_g, ln_in_b)
    for l in range(DEPTH):
        proj = _matmul(h, w_in_packed, l, F32, name="in_proj")
        q, k, v = _mla_up(proj, mla_q_norm[l], mla_kv_norm[l], _pack_w_uq(mla_w_uq[l]), _pack_w_ukv(mla_w_ukv[l]),
                          mla_tabs, batch, seq)
        o_a = _mla_attn(q, k, v).reshape(n, MLA_HEADS * HEAD_DIM)
        o_b = _nsa(proj, nsa_cmp_w1[l].astype(BF16), nsa_cmp_w2[l].astype(BF16), nsa_cmp_pos[l], rot_tabs, batch, seq)
        o_c = _moba(proj, rot_tabs, batch, seq)
        h = _out_ln([o_a, o_b, o_c], w_out_bf, l, h, ln1_g[l], ln1_b[l], name="mix_out_ln")

        xq = _matmul(h, mem_wq_bf, l, BF16, name="mem_q")
        xkv = _matmul(mem2, mem_wkv_bf, l, BF16, name="mem_kv")
        ctx = _xattn(xq, xkv, batch, seq)
        h = _out_ln([ctx], mem_wo_bf, l, h, ln2_g[l], ln2_b[l], name="mem_out_ln")

        h = _mlp_ln(h, mlp_w1_bf, mlp_w2_bf, l, ln3_g[l], ln3_b[l])
    return h.reshape(batch, seq, d)
```

```python
import functools

import numpy as np
import jax
import jax.numpy as jnp
from jax import lax
from jax.experimental import pallas as pl
from jax.experimental.pallas import tpu as pltpu

F32 = jnp.float32
BF16 = jnp.bfloat16
NEG_INF = float("-inf")

D_MODEL = 2048
DEPTH = 2
HEAD_DIM = 128
MLA_HEADS = 8
NSA_HEADS = 4
MOBA_HEADS = 4
ROPE_THETA = 500000.0
PARTIAL_ROT = HEAD_DIM // 4
MLA_Q_RANK = 512
MLA_KV_RANK = 512
MLA_NOPE = 128
MLA_ROPE = 64
MLA_SLOT = 256
NSA_CMP_LEN = 32
NSA_CMP_STRIDE = 16
NSA_SEL_LEN = 64
NSA_SEL_TOPK = 16
NSA_WINDOW = 512
NSA_FORCE_SCORE = 1.0e4
MOBA_BLOCK = 256
MOBA_TOPK = 3
MEM_HEADS = 4
D_FF = 4 * D_MODEL
DEEPNORM_ALPHA = (2 * DEPTH) ** 0.25
LANES = 128

PROJ_WIDTH = 4096
COL512 = dict(c_q=0, c_kv=1, nq=2, mq=3, mk=4, mv=5)
COL128 = dict(k_rope=24, nkc=25, nvc=26, nks=27, nvs=28, nkw=29, nvw=30, ng=31)

VMEM_LIMIT = 56 * 1024 * 1024


def _cparams(*sem):
    return pltpu.CompilerParams(dimension_semantics=sem, vmem_limit_bytes=VMEM_LIMIT)


def _dot(a, b):
    return jnp.dot(a, b, preferred_element_type=F32)


def _dot_nt(a, b):
    return lax.dot_general(a, b, (((1,), (1,)), ((), ())), preferred_element_type=F32)


def _split_bf16(x):
    hi = x.astype(BF16)
    return hi, (x - hi.astype(F32)).astype(BF16)


def _ln_rows(x, g, b, eps=1e-5):
    mu = jnp.mean(x, axis=-1, keepdims=True)
    xc = x - mu
    var = jnp.mean(xc * xc, axis=-1, keepdims=True)
    return xc * lax.rsqrt(var + eps) * g + b


def _rms_rows(x, g, eps=1e-6):
    return x * lax.rsqrt(jnp.mean(x * x, axis=-1, keepdims=True) + eps) * g


def _rope128(x, c, s1, s2, half):
    return x * c + pltpu.roll(x, LANES - half, 1) * s1 + pltpu.roll(x, half, 1) * s2


def _rope_tables(n_pos, dim):
    half = dim // 2
    inv = ROPE_THETA ** (-jnp.arange(0, dim, 2, dtype=F32) / dim)
    ang = jnp.arange(n_pos, dtype=F32)[:, None] * inv[None, :]
    cos, sin = jnp.cos(ang), jnp.sin(ang)
    ones = jnp.ones((n_pos, LANES - dim), F32)
    z = lambda w: jnp.zeros((n_pos, w), F32)
    c = jnp.concatenate([cos, cos, ones], axis=1)
    s1 = jnp.concatenate([-sin, z(LANES - half)], axis=1)
    s2 = jnp.concatenate([z(half), sin, z(LANES - dim)], axis=1)
    return c, s1, s2


MASKED = -1.0e30


def _rank_rows(score, n_cand):
    row = lax.broadcasted_iota(jnp.int32, score.shape, 0)
    rank = jnp.zeros(score.shape, F32)
    for jp in range(n_cand):
        cand = score[jp:jp + 1, :]
        ahead = (cand > score) | ((cand == score) & (row > jp))
        rank = rank + jnp.where(ahead, 1.0, 0.0)
    return rank


def _rows_to_lanes(x_t, tq):
    eye = jnp.where(lax.broadcasted_iota(jnp.int32, (tq, tq), 0) == lax.broadcasted_iota(jnp.int32, (tq, tq), 1),
                    1.0, 0.0).astype(BF16)
    return _dot_nt(eye, x_t.astype(BF16))


def _flash_update(s, m, acc, v_aug):
    m_new = jnp.maximum(m, jnp.max(s, axis=-1, keepdims=True))
    p = jnp.exp(s - m_new).astype(BF16)
    return m_new, jnp.exp(m - m_new) * acc + _dot(p, v_aug)


def _ln_kernel(x_ref, g_ref, b_ref, o_ref):
    o_ref[...] = _ln_rows(x_ref[...], g_ref[...], b_ref[...])


def _layer_norm(x, g, b, tm=512):
    n, d = x.shape
    return pl.pallas_call(
        _ln_kernel,
        grid=(n // tm,),
        in_specs=[pl.BlockSpec((tm, d), lambda i: (i, 0)),
                  pl.BlockSpec((1, d), lambda i: (0, 0)),
                  pl.BlockSpec((1, d), lambda i: (0, 0))],
        out_specs=pl.BlockSpec((tm, d), lambda i: (i, 0)),
        out_shape=jax.ShapeDtypeStruct((n, d), F32),
        compiler_params=_cparams("parallel"),
        name="ln_in",
    )(x, g.reshape(1, d), b.reshape(1, d))


def _mm_kernel(a_ref, w_ref, o_ref, abf_ref):
    @pl.when(pl.program_id(1) == 0)
    def _():
        abf_ref[...] = a_ref[...].astype(BF16)

    o_ref[...] = _dot(abf_ref[...], w_ref[...]).astype(o_ref.dtype)


def _matmul(a, w, layer, out_dtype, tm=1024, tn=512, name="mm"):
    m, k = a.shape
    n = w.shape[2]
    tm = min(tm, m)
    return pl.pallas_call(
        _mm_kernel,
        grid=(m // tm, n // tn),
        in_specs=[pl.BlockSpec((tm, k), lambda i, j: (i, 0)),
                  pl.BlockSpec((None, k, tn), lambda i, j: (layer, 0, j))],
        out_specs=pl.BlockSpec((tm, tn), lambda i, j: (i, j)),
        out_shape=jax.ShapeDtypeStruct((m, n), out_dtype),
        scratch_shapes=[pltpu.VMEM((tm, k), BF16)],
        compiler_params=_cparams("parallel", "arbitrary"),
        name=name,
    )(a, w)


def _out_ln_kernel(*refs, widths):
    n_a = len(widths)
    a_refs = refs[:n_a]
    w_ref, h_ref, g_ref, b_ref, o_ref = refs[n_a:]
    tm = h_ref.shape[0]
    for rows in (slice(0, tm // 2), slice(tm // 2, tm)):
        acc = DEEPNORM_ALPHA * h_ref[rows, :]
        off = 0
        for a_ref, wd in zip(a_refs, widths):
            acc = acc + _dot(a_ref[rows, :], w_ref[off:off + wd, :])
            off += wd
        o_ref[rows, :] = _ln_rows(acc, g_ref[...], b_ref[...])


def _out_ln(a_list, w, layer, h, g, b, tm=512, name="out_ln"):
    n, d = h.shape
    widths = tuple(a.shape[1] for a in a_list)
    k = sum(widths)
    in_specs = [pl.BlockSpec((tm, wd), lambda i: (i, 0)) for wd in widths]
    in_specs += [pl.BlockSpec((None, k, d), lambda i: (layer, 0, 0)),
                 pl.BlockSpec((tm, d), lambda i: (i, 0)),
                 pl.BlockSpec((1, d), lambda i: (0, 0)),
                 pl.BlockSpec((1, d), lambda i: (0, 0))]
    return pl.pallas_call(
        functools.partial(_out_ln_kernel, widths=widths),
        grid=(n // tm,),
        in_specs=in_specs,
        out_specs=pl.BlockSpec((tm, d), lambda i: (i, 0)),
        out_shape=jax.ShapeDtypeStruct((n, d), F32),
        compiler_params=_cparams("parallel"),
        name=name,
    )(*a_list, w, h, g.reshape(1, d), b.reshape(1, d))


def _mlp_kernel(h_ref, w1_ref, w2_ref, g_ref, b_ref, o_ref, hbf_ref, acc_ref):
    f = pl.program_id(1)

    @pl.when(f == 0)
    def _():
        hbf_ref[...] = h_ref[...].astype(BF16)
        acc_ref[...] = jnp.zeros_like(acc_ref)

    u = jnp.maximum(_dot(hbf_ref[...], w1_ref[...]), 0.0)
    acc_ref[...] += _dot((u * u).astype(BF16), w2_ref[...])

    @pl.when(f == pl.num_programs(1) - 1)
    def _():
        y = DEEPNORM_ALPHA * h_ref[...] + acc_ref[...]
        o_ref[...] = _ln_rows(y, g_ref[...], b_ref[...])


def _mlp_ln(h, w1, w2, layer, g, b, tm=512, tf=1024):
    n, d = h.shape
    dff = w1.shape[2]
    return pl.pallas_call(
        _mlp_kernel,
        grid=(n // tm, dff // tf),
        in_specs=[pl.BlockSpec((tm, d), lambda i, f: (i, 0)),
                  pl.BlockSpec((None, d, tf), lambda i, f: (layer, 0, f)),
                  pl.BlockSpec((None, tf, d), lambda i, f: (layer, f, 0)),
                  pl.BlockSpec((1, d), lambda i, f: (0, 0)),
                  pl.BlockSpec((1, d), lambda i, f: (0, 0))],
        out_specs=pl.BlockSpec((tm, d), lambda i, f: (i, 0)),
        out_shape=jax.ShapeDtypeStruct((n, d), F32),
        scratch_shapes=[pltpu.VMEM((tm, d), BF16), pltpu.VMEM((tm, d), F32)],
        compiler_params=_cparams("parallel", "arbitrary"),
        name="mlp_ln",
    )(h, w1, w2, g.reshape(1, d), b.reshape(1, d))


def _mla_up_kernel(cq_ref, ckv_ref, kr_ref, gq_ref, gkv_ref, wq_ref, wkv_ref, tc_ref, ts1_ref, ts2_ref,
                   q_ref, k_ref, v_ref, *, scale):
    half = MLA_ROPE // 2
    c, s1, s2 = tc_ref[...], ts1_ref[...], ts2_ref[...]
    nq = _rms_rows(cq_ref[...], gq_ref[...]).astype(BF16)
    nkv = _rms_rows(ckv_ref[...], gkv_ref[...]).astype(BF16)
    qf = _dot(nq, wq_ref[...])
    kvf = _dot(nkv, wkv_ref[...])
    kr = _rope128(kr_ref[...], c, s1, s2, half).astype(BF16)
    for h in range(MLA_HEADS):
        o = h * MLA_SLOT
        q_ref[0, h, :, 0:LANES] = (qf[:, o:o + LANES] * scale).astype(BF16)
        q_ref[0, h, :, LANES:] = (_rope128(qf[:, o + LANES:o + MLA_SLOT], c, s1, s2, half) * scale).astype(BF16)
        k_ref[0, h, :, 0:LANES] = kvf[:, h * LANES:(h + 1) * LANES].astype(BF16)
        k_ref[0, h, :, LANES:] = kr
        v_ref[0, h, :, 0:LANES] = kvf[:, (MLA_HEADS + h) * LANES:(MLA_HEADS + h + 1) * LANES].astype(BF16)
        v_ref[0, h, :, LANES:] = jnp.ones((kr.shape[0], LANES), BF16)


def _mla_up(proj, gq, gkv, wq, wkv, tabs, batch, seq, tm=512):
    nt = seq // tm
    row = lambda b, i: b * nt + i
    tab_spec = pl.BlockSpec((tm, LANES), lambda b, i: (i, 0))
    hm = lambda w: pl.BlockSpec((1, MLA_HEADS, tm, w), lambda b, i: (b, 0, i, 0))
    return pl.pallas_call(
        functools.partial(_mla_up_kernel, scale=(MLA_NOPE + MLA_ROPE) ** -0.5),
        grid=(batch, nt),
        in_specs=[pl.BlockSpec((tm, 512), lambda b, i: (row(b, i), COL512["c_q"])),
                  pl.BlockSpec((tm, 512), lambda b, i: (row(b, i), COL512["c_kv"])),
                  pl.BlockSpec((tm, LANES), lambda b, i: (row(b, i), COL128["k_rope"])),
                  pl.BlockSpec((1, MLA_Q_RANK), lambda b, i: (0, 0)),
                  pl.BlockSpec((1, MLA_KV_RANK), lambda b, i: (0, 0)),
                  pl.BlockSpec(wq.shape, lambda b, i: (0, 0)),
                  pl.BlockSpec(wkv.shape, lambda b, i: (0, 0)),
                  tab_spec, tab_spec, tab_spec],
        out_specs=[hm(MLA_SLOT), hm(MLA_SLOT), hm(2 * HEAD_DIM)],
        out_shape=[jax.ShapeDtypeStruct((batch, MLA_HEADS, seq, MLA_SLOT), BF16),
                   jax.ShapeDtypeStruct((batch, MLA_HEADS, seq, MLA_SLOT), BF16),
                   jax.ShapeDtypeStruct((batch, MLA_HEADS, seq, 2 * HEAD_DIM), BF16)],
        compiler_params=_cparams("parallel", "parallel"),
        name="mla_up",
    )(proj, proj, proj, gq.reshape(1, -1), gkv.reshape(1, -1), wq, wkv, *tabs)


def _mla_attn_kernel(q_ref, k_ref, v_ref, o_ref, acc_sc, *, tq, tk, rs, ahead):
    qi = pl.program_id(1)
    heads = q_ref.shape[1]
    sub = tq // tk
    row = lax.broadcasted_iota(jnp.int32, (tq, tk), 0)
    col = lax.broadcasted_iota(jnp.int32, (tq, tk), 1)

    nrs = tq // rs

    def step(kb, carry, mask):
        r = pl.ds(pl.multiple_of(kb * tk, tk), tk)
        def scores(c):
            h, rows = c // nrs, slice((c % nrs) * rs, (c % nrs + 1) * rs)
            s = _dot_nt(q_ref[0, h, rows, :], k_ref[0, h, r, :])
            return s if mask is None else jnp.where(mask[rows], s, MASKED)

        n_chain = heads * nrs
        out, queue = [], [scores(c) for c in range(ahead)]
        for c in range(n_chain):
            if c + ahead < n_chain:
                queue.append(scores(c + ahead))
            m_new, acc_sc[c] = _flash_update(queue.pop(0), carry[c], acc_sc[c], v_ref[0, c // nrs, r, :])
            out.append(m_new)
        return tuple(out)

    acc_sc[...] = jnp.zeros(acc_sc.shape, F32)
    carry = tuple(jnp.full((rs, 1), NEG_INF, F32) for _ in range(heads * nrs))
    carry = lax.fori_loop(0, qi * sub, lambda kb, c: step(kb, c, None), carry)
    for j in range(sub):
        carry = step(qi * sub + j, carry, col + j * tk <= row)
    for h in range(heads):
        for i in range(nrs):
            acc = acc_sc[h * nrs + i]
            o_ref[0, i * rs:(i + 1) * rs, h * HEAD_DIM:(h + 1) * HEAD_DIM] = (
                acc[:, :HEAD_DIM] / acc[:, HEAD_DIM:]).astype(o_ref.dtype)


def _mla_attn(q, k, v, tq=512, tk=512, rs=256, ahead=1):
    batch, heads, seq, _ = q.shape
    return pl.pallas_call(
        functools.partial(_mla_attn_kernel, tq=tq, tk=tk, rs=rs, ahead=ahead),
        grid=(batch, seq // tq),
        in_specs=[pl.BlockSpec((1, heads, tq, MLA_SLOT), lambda b, i: (b, 0, i, 0)),
                  pl.BlockSpec((1, heads, seq, MLA_SLOT), lambda b, i: (b, 0, 0, 0), pipeline_mode=pl.Buffered(1)),
                  pl.BlockSpec((1, heads, seq, 2 * HEAD_DIM), lambda b, i: (b, 0, 0, 0),
                               pipeline_mode=pl.Buffered(1))],
        out_specs=pl.BlockSpec((1, tq, heads * HEAD_DIM), lambda b, i: (b, i, 0)),
        out_shape=jax.ShapeDtypeStruct((batch, seq, heads * HEAD_DIM), BF16),
        scratch_shapes=[pltpu.VMEM((heads * tq // rs, rs, 2 * HEAD_DIM), F32)],
        compiler_params=_cparams("parallel", "arbitrary"),
        name="mla_attn",
    )(q, k, v)


def _nsa_kernel(nq_ref, kc_ref, vc_ref, ks_ref, vs_ref, kw_ref, vw_ref, ng_ref,
                w1_ref, w2_ref, pos_ref, ovt_ref, tc_ref, ts1_ref, ts2_ref,
                o_ref,
                tmp_sc, kc_sc, vc_sc, ks_sc, vs_sc, kw_sc, vw_sc, acc_sc, *, tq, seq, scale):
    qi = pl.program_id(1)
    half = PARTIAL_ROT // 2
    heads = NSA_HEADS
    n_chunk = seq // NSA_CMP_STRIDE
    n_sel = seq // NSA_SEL_LEN

    @pl.when(qi == 0)
    def _prep():
        c, s1, s2 = tc_ref[...], ts1_ref[...], ts2_ref[...]
        ones = jnp.ones((seq, LANES), BF16)
        blk = lax.broadcasted_iota(jnp.int32, (seq, LANES), 0) >> 6
        ks_sc[:, 0:LANES] = _rope128(ks_ref[...], c, s1, s2, half).astype(BF16)
        ks_sc[:, LANES:] = jnp.where(blk == lax.broadcasted_iota(jnp.int32, (seq, LANES), 1), 1.0, 0.0).astype(BF16)
        kw_sc[...] = _rope128(kw_ref[...], c, s1, s2, half).astype(BF16)
        vs_sc[:, 0:LANES] = vs_ref[...].astype(BF16)
        vs_sc[:, LANES:] = ones
        vw_sc[:, 0:LANES] = vw_ref[...].astype(BF16)
        vw_sc[:, LANES:] = ones
        for i, (src, dst) in enumerate(((kc_ref, kc_sc), (vc_ref, vc_sc))):
            tmp_sc[...] = _rope128(src[...], c, s1, s2, half) if i == 0 else src[...]
            lo, hi = [], []
            for t in range(NSA_CMP_STRIDE):
                x = tmp_sc[pl.ds(t, n_chunk, stride=NSA_CMP_STRIDE), :]
                lo.append((x + pos_ref[i, t:t + 1, :]).astype(BF16))
                hi.append((x + pos_ref[i, NSA_CMP_STRIDE + t:NSA_CMP_STRIDE + t + 1, :]).astype(BF16))
            kw1 = NSA_CMP_STRIDE * HEAD_DIM
            a = _dot(jnp.concatenate(lo, axis=1), w1_ref[i, 0:kw1, :])
            bm = _dot(jnp.concatenate(hi, axis=1), w1_ref[i, kw1:2 * kw1, :])
            hid = jax.nn.gelu(a + pltpu.roll(bm, n_chunk - 1, 0))
            dst[...] = _dot(hid.astype(BF16), w2_ref[i]).astype(BF16)

    t0 = pl.multiple_of(qi * tq, tq)
    rq = pl.ds(t0, tq)
    c, s1, s2 = tc_ref[rq, :], ts1_ref[rq, :], ts2_ref[rq, :]
    qf = nq_ref[...]
    qs = jnp.concatenate(
        [_rope128(qf[:, h * LANES:(h + 1) * LANES], c, s1, s2, half) * scale for h in range(heads)],
        axis=0)
    qs_bf = qs.astype(BF16)

    lane = lax.broadcasted_iota(jnp.int32, (tq, LANES), 1)
    tpos = t0 + lax.broadcasted_iota(jnp.int32, (tq, LANES), 0)

    s = _dot_nt(qs_bf, kc_sc[...]).reshape(heads, tq, LANES)
    cmask = (lane * NSA_CMP_STRIDE + (NSA_CMP_LEN - 1) <= tpos)[None]
    s = jnp.where(cmask, s, NEG_INF)
    m = jnp.max(s, axis=-1, keepdims=True)
    m = jnp.where(m == NEG_INF, 0.0, m)
    e = jnp.exp(s - m)
    p_cmp = e / jnp.maximum(jnp.sum(e, axis=-1, keepdims=True), 1e-30)
    o_cmp = _dot(p_cmp.reshape(heads * tq, LANES).astype(BF16), vc_sc[...]).reshape(heads, tq, HEAD_DIM)

    p_hi, p_lo = _split_bf16(p_cmp[0] + p_cmp[1] + p_cmp[2] + p_cmp[3])
    imp_t = (_dot_nt(ovt_ref[...], p_hi) + _dot_nt(ovt_ref[...], p_lo))[0:n_sel]
    blk_t = lax.broadcasted_iota(jnp.int32, (n_sel, tq), 0)
    cur_t = (t0 + lax.broadcasted_iota(jnp.int32, (n_sel, tq), 1)) >> 6
    eligible = blk_t <= cur_t
    forced = (blk_t == 0) | (blk_t == cur_t) | (blk_t == cur_t - 1)
    score = jnp.where(eligible, jnp.where(forced, NSA_FORCE_SCORE, imp_t), NEG_INF)
    rank = _rank_rows(score, n_sel)
    bias_t = jnp.where(eligible & (rank < NSA_SEL_TOPK), 0.0, MASKED)
    bias_t = jnp.concatenate([bias_t, jnp.zeros((LANES - n_sel, tq), F32)], axis=0)
    bias = _rows_to_lanes(bias_t, tq)
    q_aug = jnp.concatenate([qs, jnp.concatenate([bias] * heads, axis=0)], axis=1).astype(BF16)

    assert tq == NSA_WINDOW
    local_r = lax.broadcasted_iota(jnp.int32, (tq, tq), 0)
    local_c = lax.broadcasted_iota(jnp.int32, (tq, tq), 1)
    causal = local_c <= local_r
    rc = acc_sc.shape[1]
    n_chain = heads * tq // rc

    def run(q_rows, k_sc, v_sc, kb, carry, mask):
        r = pl.ds(pl.multiple_of(kb * tq, tq), tq)

        def scores(c):
            s = _dot_nt(q_rows[c * rc:(c + 1) * rc], k_sc[r, :])
            r0 = (c * rc) % tq
            return s if mask is None else jnp.where(mask[r0:r0 + rc], s, MASKED)

        out, s_next = [], scores(0)
        for c in range(n_chain):
            s, s_next = s_next, (scores(c + 1) if c + 1 < n_chain else None)
            m_new, acc_sc[c] = _flash_update(s, carry[c], acc_sc[c], v_sc[r, :])
            out.append(m_new)
        return tuple(out)

    def finish():
        acc = jnp.concatenate([acc_sc[c] for c in range(n_chain)], axis=0)
        return (acc[:, :HEAD_DIM] / acc[:, HEAD_DIM:]).reshape(heads, tq, HEAD_DIM)

    init = tuple(jnp.full((rc, 1), NEG_INF, F32) for _ in range(n_chain))

    acc_sc[...] = jnp.zeros(acc_sc.shape, F32)
    carry = lax.fori_loop(0, qi, lambda kb, cr: run(q_aug, ks_sc, vs_sc, kb, cr, None), init)
    run(q_aug, ks_sc, vs_sc, qi, carry, causal)
    o_sel = finish()

    acc_sc[...] = jnp.zeros(acc_sc.shape, F32)
    carry = run(qs_bf, kw_sc, vw_sc, jnp.maximum(qi - 1, 0), init, (local_c > local_r) & (qi >= 1))
    run(qs_bf, kw_sc, vw_sc, qi, carry, causal)
    o_win = finish()

    g = jax.nn.sigmoid(ng_ref[...])
    for h in range(heads):
        o = (g[:, 3 * h:3 * h + 1] * o_cmp[h] + g[:, 3 * h + 1:3 * h + 2] * o_sel[h]
             + g[:, 3 * h + 2:3 * h + 3] * o_win[h])
        o_ref[:, h * LANES:(h + 1) * LANES] = o.astype(o_ref.dtype)


def _nsa(proj, w1, w2, pos, tabs, batch, seq, tq=NSA_WINDOW, rc=256):
    nt = seq // tq
    n_cmp = (seq - NSA_CMP_LEN) // NSA_CMP_STRIDE + 1
    n_sel = seq // NSA_SEL_LEN
    starts = np.arange(LANES) * NSA_CMP_STRIDE
    sel_start = np.arange(LANES) * NSA_SEL_LEN
    overlap = ((starts[:, None] < sel_start[None, :] + NSA_SEL_LEN)
               & (starts[:, None] + NSA_CMP_LEN > sel_start[None, :])
               & (np.arange(LANES)[:, None] < n_cmp) & (np.arange(LANES)[None, :] < n_sel))
    ovt = jnp.asarray(overlap.T.astype(np.float32), BF16)
    seq_col = lambda name: pl.BlockSpec((seq, LANES), lambda b, i: (b, COL128[name]))
    full = lambda a: pl.BlockSpec(a.shape, lambda b, i: (0,) * a.ndim)
    return pl.pallas_call(
        functools.partial(_nsa_kernel, tq=tq, seq=seq, scale=HEAD_DIM ** -0.5),
        grid=(batch, nt),
        in_specs=[pl.BlockSpec((tq, 512), lambda b, i: (b * nt + i, COL512["nq"])),
                  seq_col("nkc"), seq_col("nvc"), seq_col("nks"), seq_col("nvs"), seq_col("nkw"), seq_col("nvw"),
                  pl.BlockSpec((tq, LANES), lambda b, i: (b * nt + i, COL128["ng"])),
                  full(w1), full(w2), full(pos), full(ovt), full(tabs[0]), full(tabs[1]), full(tabs[2])],
        out_specs=pl.BlockSpec((tq, NSA_HEADS * HEAD_DIM), lambda b, i: (b * nt + i, 0)),
        out_shape=jax.ShapeDtypeStruct((batch * seq, NSA_HEADS * HEAD_DIM), BF16),
        scratch_shapes=[pltpu.VMEM((seq, LANES), F32),
                        pltpu.VMEM((LANES, LANES), BF16), pltpu.VMEM((LANES, LANES), BF16),
                        pltpu.VMEM((seq, 2 * LANES), BF16), pltpu.VMEM((seq, 2 * LANES), BF16),
                        pltpu.VMEM((seq, LANES), BF16), pltpu.VMEM((seq, 2 * LANES), BF16),
                        pltpu.VMEM((NSA_HEADS * tq // rc, rc, 2 * HEAD_DIM), F32)],
        compiler_params=_cparams("parallel", "arbitrary"),
        name="nsa",
    )(proj, proj, proj, proj, proj, proj, proj, proj, w1, w2, pos, ovt, *tabs)


def _moba_kernel(mq_ref, mk_ref, mv_ref, tc_ref, ts1_ref, ts2_ref, o_ref, k_sc, v_sc, km_sc, acc_sc,
                 *, tq, seq, scale):
    qi = pl.program_id(1)
    half = PARTIAL_ROT // 2
    heads = MOBA_HEADS
    n_blk = seq // MOBA_BLOCK
    slot = 2 * LANES

    @pl.when(qi == 0)
    def _prep():
        c, s1, s2 = tc_ref[...], ts1_ref[...], ts2_ref[...]
        km_sc[...] = jnp.zeros_like(km_sc)
        ones = jnp.ones((seq, LANES), BF16)
        blk = lax.broadcasted_iota(jnp.int32, (seq, LANES), 0) >> 8
        onehot = jnp.where(blk == lax.broadcasted_iota(jnp.int32, (seq, LANES), 1), 1.0, 0.0).astype(BF16)
        for h in range(heads):
            hs = slice(h * LANES, (h + 1) * LANES)
            kh = _rope128(mk_ref[:, hs], c, s1, s2, half)
            k_sc[:, h * slot:h * slot + LANES] = kh.astype(BF16)
            k_sc[:, h * slot + LANES:(h + 1) * slot] = onehot
            v_sc[:, h * slot:h * slot + LANES] = mv_ref[:, hs].astype(BF16)
            v_sc[:, h * slot + LANES:(h + 1) * slot] = ones
            for j in range(n_blk):
                km_sc[h * n_blk + j:h * n_blk + j + 1, :] = jnp.mean(
                    kh[j * MOBA_BLOCK:(j + 1) * MOBA_BLOCK], axis=0, keepdims=True)

    t0 = pl.multiple_of(qi * tq, tq)
    rq = pl.ds(t0, tq)
    c, s1, s2 = tc_ref[rq, :], ts1_ref[rq, :], ts2_ref[rq, :]
    km_hi, km_lo = _split_bf16(km_sc[...])
    tk = tq
    rc = MOBA_BLOCK
    sub = tq // rc
    blk_t = lax.broadcasted_iota(jnp.int32, (n_blk, tq), 0)
    cur_t = (t0 + lax.broadcasted_iota(jnp.int32, (n_blk, tq), 1)) >> 8
    row = lax.broadcasted_iota(jnp.int32, (tq, tk), 0)
    col = lax.broadcasted_iota(jnp.int32, (tq, tk), 1)

    q_aug = []
    for h in range(heads):
        qf = _rope128(mq_ref[:, h * LANES:(h + 1) * LANES], c, s1, s2, half) * scale
        q_hi, q_lo = _split_bf16(qf)
        gate_t = (_dot_nt(km_hi, q_hi) + (_dot_nt(km_lo, q_hi) + _dot_nt(km_hi, q_lo)))[h * n_blk:(h + 1) * n_blk]
        eligible = blk_t < cur_t
        score = jnp.where(eligible, gate_t, NEG_INF)
        picked = eligible & (_rank_rows(score, n_blk) < MOBA_TOPK)
        bias_t = jnp.where(picked | (blk_t == cur_t), 0.0, MASKED)
        bias_t = jnp.concatenate([bias_t, jnp.zeros((LANES - n_blk, tq), F32)], axis=0)
        q_aug.append(jnp.concatenate([qf, _rows_to_lanes(bias_t, tq)], axis=1).astype(BF16))

    n_chain = heads * sub

    def step(kb, carry, mask):
        r = pl.ds(pl.multiple_of(kb * tk, tk), tk)

        def scores(c):
            h, rows = c // sub, slice((c % sub) * rc, (c % sub + 1) * rc)
            s = _dot_nt(q_aug[h][rows], k_sc[r, h * slot:(h + 1) * slot])
            return s if mask is None else jnp.where(mask[rows], s, MASKED)

        out, s_next = [], scores(0)
        for c in range(n_chain):
            s, s_next = s_next, (scores(c + 1) if c + 1 < n_chain else None)
            h = c // sub
            m_new, acc_sc[c] = _flash_update(s, carry[c], acc_sc[c], v_sc[r, h * slot:(h + 1) * slot])
            out.append(m_new)
        return tuple(out)

    acc_sc[...] = jnp.zeros(acc_sc.shape, F32)
    carry = tuple(jnp.full((rc, 1), NEG_INF, F32) for _ in range(n_chain))
    carry = lax.fori_loop(0, qi, lambda kb, cr: step(kb, cr, None), carry)
    carry = step(qi, carry, col <= row)
    for c in range(n_chain):
        h, r0 = c // sub, (c % sub) * rc
        acc = acc_sc[c]
        o_ref[r0:r0 + rc, h * LANES:(h + 1) * LANES] = (acc[:, :HEAD_DIM] / acc[:, HEAD_DIM:]).astype(o_ref.dtype)


def _moba(proj, tabs, batch, seq):
    tq = 2 * MOBA_BLOCK
    nt = seq // tq
    width = MOBA_HEADS * HEAD_DIM
    full = lambda a: pl.BlockSpec(a.shape, lambda b, i: (0,) * a.ndim)
    return pl.pallas_call(
        functools.partial(_moba_kernel, tq=tq, seq=seq, scale=HEAD_DIM ** -0.5),
        grid=(batch, nt),
        in_specs=[pl.BlockSpec((tq, width), lambda b, i: (b * nt + i, COL512["mq"])),
                  pl.BlockSpec((seq, width), lambda b, i: (b, COL512["mk"])),
                  pl.BlockSpec((seq, width), lambda b, i: (b, COL512["mv"])),
                  full(tabs[0]), full(tabs[1]), full(tabs[2])],
        out_specs=pl.BlockSpec((tq, width), lambda b, i: (b * nt + i, 0)),
        out_shape=jax.ShapeDtypeStruct((batch * seq, width), BF16),
        scratch_shapes=[pltpu.VMEM((seq, 2 * width), BF16), pltpu.VMEM((seq, 2 * width), BF16),
                        pltpu.VMEM((LANES, LANES), F32),
                        pltpu.VMEM((MOBA_HEADS * tq // MOBA_BLOCK, MOBA_BLOCK, 2 * HEAD_DIM), F32)],
        compiler_params=_cparams("parallel", "arbitrary"),
        name="moba",
    )(proj, proj, proj, *tabs)


def _xattn_kernel(q_ref, kv_ref, o_ref):
    d = q_ref.shape[1]
    hd = d // MEM_HEADS
    for h in range(MEM_HEADS):
        hs = slice(h * hd, (h + 1) * hd)
        s = _dot_nt(q_ref[:, hs], kv_ref[0, :, hs])
        e = jnp.exp(s - jnp.max(s, axis=-1, keepdims=True))
        p = e / jnp.sum(e, axis=-1, keepdims=True)
        o_ref[:, hs] = _dot(p.astype(BF16), kv_ref[0, :, d + h * hd:d + (h + 1) * hd]).astype(o_ref.dtype)


def _xattn(q, kv, batch, seq, tq=512):
    d = q.shape[1]
    nt = seq // tq
    m_len = kv.shape[0] // batch
    return pl.pallas_call(
        _xattn_kernel,
        grid=(batch, nt),
        in_specs=[pl.BlockSpec((tq, d), lambda b, i: (b * nt + i, 0)),
                  pl.BlockSpec((1, m_len, 2 * d), lambda b, i: (b, 0, 0))],
        out_specs=pl.BlockSpec((tq, d), lambda b, i: (b * nt + i, 0)),
        out_shape=jax.ShapeDtypeStruct((batch * seq, d), BF16),
        compiler_params=_cparams("parallel", "parallel"),
        name="xattn",
    )(q, kv.reshape(batch, m_len, 2 * d))


IN_SPLIT_NAMES = ("c_q", "c_kv", "k_rope", "nq", "nkc", "nvc", "nks", "nvs", "nkw", "nvw", "ng", "mq", "mk", "mv")
IN_SPLIT_SIZES = (512, 512, 64, 512, 128, 128, 128, 128, 128, 128, 12, 512, 512, 512)


def _pack_w_in_kernel(w_ref, o_ref):
    rows = w_ref.shape[0]
    off = 0
    for name, sz in zip(IN_SPLIT_NAMES, IN_SPLIT_SIZES):
        width = 512 if name in COL512 else LANES
        dst = COL512[name] * 512 if name in COL512 else COL128[name] * LANES
        x = w_ref[:, off:off + sz].astype(BF16)
        if sz < width:
            x = jnp.concatenate([x, jnp.zeros((rows, width - sz), BF16)], axis=1)
        o_ref[0, :, dst:dst + width] = x
        off += sz


def _pack_w_in(w, tr=256):
    layers, d, width = w.shape
    return pl.pallas_call(
        _pack_w_in_kernel,
        grid=(layers, d // tr),
        in_specs=[pl.BlockSpec((None, tr, width), lambda l, i: (l, i, 0))],
        out_specs=pl.BlockSpec((1, tr, PROJ_WIDTH), lambda l, i: (l, i, 0)),
        out_shape=jax.ShapeDtypeStruct((layers, d, PROJ_WIDTH), BF16),
        compiler_params=_cparams("parallel", "parallel"),
        name="pack_w_in",
    )(w)


def _pack_w_uq(w):
    r = w.shape[0]
    w = w.reshape(r, MLA_HEADS, MLA_NOPE + MLA_ROPE)
    w = jnp.pad(w, ((0, 0), (0, 0), (0, MLA_SLOT - MLA_NOPE - MLA_ROPE)))
    return w.reshape(r, MLA_HEADS * MLA_SLOT).astype(BF16)


def _pack_w_ukv(w):
    r = w.shape[0]
    w = w.reshape(r, MLA_HEADS, 2, HEAD_DIM).transpose(0, 2, 1, 3)
    return w.reshape(r, 2 * MLA_HEADS * HEAD_DIM).astype(BF16)


def kernel(x, mem, ln_in_g, ln_in_b, w_in, mla_q_norm, mla_kv_norm, mla_w_uq, mla_w_ukv, nsa_cmp_w1, nsa_cmp_w2, nsa_cmp_pos, w_out, ln1_g, ln1_b, mem_wq, mem_wkv, mem_wo, ln2_g, ln2_b, mlp_w1, mlp_w2, ln3_g, ln3_b):
    batch, seq, d = x.shape
    n = batch * seq
    mla_tabs = _rope_tables(seq, MLA_ROPE)
    rot_tabs = _rope_tables(seq, PARTIAL_ROT)
    mem2 = mem.reshape(batch * mem.shape[1], d)
    w_in_packed = _pack_w_in(w_in)
    w_out_bf, mem_wkv_bf, mem_wo_bf = w_out.astype(BF16), mem_wkv.astype(BF16), mem_wo.astype(BF16)
    mem_wq_bf = (mem_wq * (d // MEM_HEADS) ** -0.5).astype(BF16)
    mlp_w1_bf, mlp_w2_bf = mlp_w1.astype(BF16), mlp_w2.astype(BF16)

    h = _layer_norm(x.reshape(n, d), ln_in_g, ln_in_b)
    for l in range(DEPTH):
        proj = _matmul(h, w_in_packed, l, F32, name="in_proj")
        q, k, v = _mla_up(proj, mla_q_norm[l], mla_kv_norm[l], _pack_w_uq(mla_w_uq[l]), _pack_w_ukv(mla_w_ukv[l]),
                          mla_tabs, batch, seq)
        o_a = _mla_attn(q, k, v).reshape(n, MLA_HEADS * HEAD_DIM)
        o_b = _nsa(proj, nsa_cmp_w1[l].astype(BF16), nsa_cmp_w2[l].astype(BF16), nsa_cmp_pos[l], rot_tabs, batch, seq)
        o_c = _moba(proj, rot_tabs, batch, seq)
        h = _out_ln([o_a, o_b, o_c], w_out_bf, l, h, ln1_g[l], ln1_b[l], name="mix_out_ln")

        xq = _matmul(h, mem_wq_bf, l, BF16, name="mem_q")
        xkv = _matmul(mem2, mem_wkv_bf, l, BF16, name="mem_kv")
        ctx = _xattn(xq, xkv, batch, seq)
        h = _out_ln([ctx], mem_wo_bf, l, h, ln2_g[l], ln2_b[l], name="mem_out_ln")

        h = _mlp_ln(h, mlp_w1_bf, mlp_w2_bf, l, ln3_g[l], ln3_b[l])
    return h.reshape(batch, seq, d)
```

```python
import functools

import numpy as np
import jax
import jax.numpy as jnp
from jax import lax
from jax.experimental import pallas as pl
from jax.experimental.pallas import tpu as pltpu

F32 = jnp.float32
BF16 = jnp.bfloat16
NEG_INF = float("-inf")

D_MODEL = 2048
DEPTH = 2
HEAD_DIM = 128
MLA_HEADS = 8
NSA_HEADS = 4
MOBA_HEADS = 4
ROPE_THETA = 500000.0
PARTIAL_ROT = HEAD_DIM // 4
MLA_Q_RANK = 512
MLA_KV_RANK = 512
MLA_NOPE = 128
MLA_ROPE = 64
MLA_SLOT = 256
NSA_CMP_LEN = 32
NSA_CMP_STRIDE = 16
NSA_SEL_LEN = 64
NSA_SEL_TOPK = 16
NSA_WINDOW = 512
NSA_FORCE_SCORE = 1.0e4
MOBA_BLOCK = 256
MOBA_TOPK = 3
MEM_HEADS = 4
D_FF = 4 * D_MODEL
DEEPNORM_ALPHA = (2 * DEPTH) ** 0.25
LANES = 128

PROJ_WIDTH = 4096
COL512 = dict(c_q=0, c_kv=1, nq=2, mq=3, mk=4, mv=5)
COL128 = dict(k_rope=24, nkc=25, nvc=26, nks=27, nvs=28, nkw=29, nvw=30, ng=31)

VMEM_LIMIT = 56 * 1024 * 1024


def _cparams(*sem):
    return pltpu.CompilerParams(dimension_semantics=sem, vmem_limit_bytes=VMEM_LIMIT)


def _dot(a, b):
    return jnp.dot(a, b, preferred_element_type=F32)


def _dot_nt(a, b):
    return lax.dot_general(a, b, (((1,), (1,)), ((), ())), preferred_element_type=F32)


def _split_bf16(x):
    hi = x.astype(BF16)
    return hi, (x - hi.astype(F32)).astype(BF16)


def _ln_rows(x, g, b, eps=1e-5):
    mu = jnp.mean(x, axis=-1, keepdims=True)
    xc = x - mu
    var = jnp.mean(xc * xc, axis=-1, keepdims=True)
    return xc * lax.rsqrt(var + eps) * g + b


def _rms_rows(x, g, eps=1e-6):
    return x * lax.rsqrt(jnp.mean(x * x, axis=-1, keepdims=True) + eps) * g


def _rope128(x, c, s1, s2, half):
    return x * c + pltpu.roll(x, LANES - half, 1) * s1 + pltpu.roll(x, half, 1) * s2


def _rope_tables(n_pos, dim):
    half = dim // 2
    inv = ROPE_THETA ** (-jnp.arange(0, dim, 2, dtype=F32) / dim)
    ang = jnp.arange(n_pos, dtype=F32)[:, None] * inv[None, :]
    cos, sin = jnp.cos(ang), jnp.sin(ang)
    ones = jnp.ones((n_pos, LANES - dim), F32)
    z = lambda w: jnp.zeros((n_pos, w), F32)
    c = jnp.concatenate([cos, cos, ones], axis=1)
    s1 = jnp.concatenate([-sin, z(LANES - half)], axis=1)
    s2 = jnp.concatenate([z(half), sin, z(LANES - dim)], axis=1)
    return c, s1, s2


MASKED = -1.0e30


def _rank_rows(score, n_cand):
    row = lax.broadcasted_iota(jnp.int32, score.shape, 0)
    rank = jnp.zeros(score.shape, F32)
    for jp in range(n_cand):
        cand = score[jp:jp + 1, :]
        ahead = (cand > score) | ((cand == score) & (row > jp))
        rank = rank + jnp.where(ahead, 1.0, 0.0)
    return rank


def _rows_to_lanes(x_t, tq):
    eye = jnp.where(lax.broadcasted_iota(jnp.int32, (tq, tq), 0) == lax.broadcasted_iota(jnp.int32, (tq, tq), 1),
                    1.0, 0.0).astype(BF16)
    return _dot_nt(eye, x_t.astype(BF16))


def _flash_update(s, m, acc, v_aug):
    m_new = jnp.maximum(m, jnp.max(s, axis=-1, keepdims=True))
    p = jnp.exp(s - m_new).astype(BF16)
    return m_new, jnp.exp(m - m_new) * acc + _dot(p, v_aug)


def _ln_kernel(x_ref, g_ref, b_ref, o_ref):
    o_ref[...] = _ln_rows(x_ref[...], g_ref[...], b_ref[...])


def _layer_norm(x, g, b, tm=512):
    n, d = x.shape
    return pl.pallas_call(
        _ln_kernel,
        grid=(n // tm,),
        in_specs=[pl.BlockSpec((tm, d), lambda i: (i, 0)),
                  pl.BlockSpec((1, d), lambda i: (0, 0)),
                  pl.BlockSpec((1, d), lambda i: (0, 0))],
        out_specs=pl.BlockSpec((tm, d), lambda i: (i, 0)),
        out_shape=jax.ShapeDtypeStruct((n, d), F32),
        compiler_params=_cparams("parallel"),
        name="ln_in",
    )(x, g.reshape(1, d), b.reshape(1, d))


def _mm_kernel(a_ref, w_ref, o_ref, abf_ref):
    @pl.when(pl.program_id(1) == 0)
    def _():
        abf_ref[...] = a_ref[...].astype(BF16)

    o_ref[...] = _dot(abf_ref[...], w_ref[...]).astype(o_ref.dtype)


def _matmul(a, w, layer, out_dtype, tm=1024, tn=2048, name="mm"):
    m, k = a.shape
    n = w.shape[2]
    tm = min(tm, m)
    return pl.pallas_call(
        _mm_kernel,
        grid=(m // tm, n // tn),
        in_specs=[pl.BlockSpec((tm, k), lambda i, j: (i, 0)),
                  pl.BlockSpec((None, k, tn), lambda i, j: (layer, 0, j))],
        out_specs=pl.BlockSpec((tm, tn), lambda i, j: (i, j)),
        out_shape=jax.ShapeDtypeStruct((m, n), out_dtype),
        scratch_shapes=[pltpu.VMEM((tm, k), BF16)],
        compiler_params=_cparams("parallel", "arbitrary"),
        name=name,
    )(a, w)


def _out_ln_kernel(*refs, widths):
    n_a = len(widths)
    a_refs = refs[:n_a]
    w_ref, h_ref, g_ref, b_ref, o_ref = refs[n_a:]
    tm = h_ref.shape[0]
    halves = (slice(0, tm // 2), slice(tm // 2, tm))
    accs = []
    for rows in halves:
        acc = DEEPNORM_ALPHA * h_ref[rows, :]
        off = 0
        for a_ref, wd in zip(a_refs, widths):
            acc = acc + _dot(a_ref[rows, :], w_ref[off:off + wd, :])
            off += wd
        accs.append(acc)
    for rows, acc in zip(halves, accs):
        o_ref[rows, :] = _ln_rows(acc, g_ref[...], b_ref[...])


def _out_ln(a_list, w, layer, h, g, b, tm=512, name="out_ln"):
    n, d = h.shape
    widths = tuple(a.shape[1] for a in a_list)
    k = sum(widths)
    in_specs = [pl.BlockSpec((tm, wd), lambda i: (i, 0)) for wd in widths]
    in_specs += [pl.BlockSpec((None, k, d), lambda i: (layer, 0, 0)),
                 pl.BlockSpec((tm, d), lambda i: (i, 0)),
                 pl.BlockSpec((1, d), lambda i: (0, 0)),
                 pl.BlockSpec((1, d), lambda i: (0, 0))]
    return pl.pallas_call(
        functools.partial(_out_ln_kernel, widths=widths),
        grid=(n // tm,),
        in_specs=in_specs,
        out_specs=pl.BlockSpec((tm, d), lambda i: (i, 0)),
        out_shape=jax.ShapeDtypeStruct((n, d), F32),
        compiler_params=_cparams("parallel"),
        name=name,
    )(*a_list, w, h, g.reshape(1, d), b.reshape(1, d))


def _mlp_kernel(h_ref, w1_ref, w2_ref, g_ref, b_ref, o_ref, hbf_ref, acc_ref):
    f = pl.program_id(1)

    @pl.when(f == 0)
    def _():
        hbf_ref[...] = h_ref[...].astype(BF16)
        acc_ref[...] = jnp.zeros_like(acc_ref)

    u = jnp.maximum(_dot(hbf_ref[...], w1_ref[...]), 0.0)
    acc_ref[...] += _dot((u * u).astype(BF16), w2_ref[...])

    @pl.when(f == pl.num_programs(1) - 1)
    def _():
        y = DEEPNORM_ALPHA * h_ref[...] + acc_ref[...]
        o_ref[...] = _ln_rows(y, g_ref[...], b_ref[...])


def _mlp_ln(h, w1, w2, layer, g, b, tm=512, tf=1024):
    n, d = h.shape
    dff = w1.shape[2]
    return pl.pallas_call(
        _mlp_kernel,
        grid=(n // tm, dff // tf),
        in_specs=[pl.BlockSpec((tm, d), lambda i, f: (i, 0)),
                  pl.BlockSpec((None, d, tf), lambda i, f: (layer, 0, f)),
                  pl.BlockSpec((None, tf, d), lambda i, f: (layer, f, 0)),
                  pl.BlockSpec((1, d), lambda i, f: (0, 0)),
                  pl.BlockSpec((1, d), lambda i, f: (0, 0))],
        out_specs=pl.BlockSpec((tm, d), lambda i, f: (i, 0)),
        out_shape=jax.ShapeDtypeStruct((n, d), F32),
        scratch_shapes=[pltpu.VMEM((tm, d), BF16), pltpu.VMEM((tm, d), F32)],
        compiler_params=_cparams("parallel", "arbitrary"),
        name="mlp_ln",
    )(h, w1, w2, g.reshape(1, d), b.reshape(1, d))


def _mla_up_kernel(cq_ref, ckv_ref, kr_ref, gq_ref, gkv_ref, wq_ref, wkv_ref, tc_ref, ts1_ref, ts2_ref,
                   q_ref, k_ref, v_ref, *, scale):
    half = MLA_ROPE // 2
    c, s1, s2 = tc_ref[...], ts1_ref[...], ts2_ref[...]
    nq = _rms_rows(cq_ref[...].astype(F32), gq_ref[...]).astype(BF16)
    nkv = _rms_rows(ckv_ref[...].astype(F32), gkv_ref[...]).astype(BF16)
    qf = _dot(nq, wq_ref[...])
    kvf = _dot(nkv, wkv_ref[...])
    kr = _rope128(kr_ref[...].astype(F32), c, s1, s2, half).astype(BF16)
    for h in range(MLA_HEADS):
        o = h * MLA_SLOT
        q_ref[0, h, :, 0:LANES] = (qf[:, o:o + LANES] * scale).astype(BF16)
        q_ref[0, h, :, LANES:] = (_rope128(qf[:, o + LANES:o + MLA_SLOT], c, s1, s2, half) * scale).astype(BF16)
        k_ref[0, h, :, 0:LANES] = kvf[:, h * LANES:(h + 1) * LANES].astype(BF16)
        k_ref[0, h, :, LANES:] = kr
        v_ref[0, h, :, 0:LANES] = kvf[:, (MLA_HEADS + h) * LANES:(MLA_HEADS + h + 1) * LANES].astype(BF16)
        v_ref[0, h, :, LANES:] = jnp.ones((kr.shape[0], LANES), BF16)


def _mla_up(proj, gq, gkv, wq, wkv, tabs, batch, seq, tm=512):
    nt = seq // tm
    row = lambda b, i: b * nt + i
    tab_spec = pl.BlockSpec((tm, LANES), lambda b, i: (i, 0))
    hm = lambda w: pl.BlockSpec((1, MLA_HEADS, tm, w), lambda b, i: (b, 0, i, 0))
    return pl.pallas_call(
        functools.partial(_mla_up_kernel, scale=(MLA_NOPE + MLA_ROPE) ** -0.5),
        grid=(batch, nt),
        in_specs=[pl.BlockSpec((tm, 512), lambda b, i: (row(b, i), COL512["c_q"])),
                  pl.BlockSpec((tm, 512), lambda b, i: (row(b, i), COL512["c_kv"])),
                  pl.BlockSpec((tm, LANES), lambda b, i: (row(b, i), COL128["k_rope"])),
                  pl.BlockSpec((1, MLA_Q_RANK), lambda b, i: (0, 0)),
                  pl.BlockSpec((1, MLA_KV_RANK), lambda b, i: (0, 0)),
                  pl.BlockSpec(wq.shape, lambda b, i: (0, 0)),
                  pl.BlockSpec(wkv.shape, lambda b, i: (0, 0)),
                  tab_spec, tab_spec, tab_spec],
        out_specs=[hm(MLA_SLOT), hm(MLA_SLOT), hm(2 * HEAD_DIM)],
        out_shape=[jax.ShapeDtypeStruct((batch, MLA_HEADS, seq, MLA_SLOT), BF16),
                   jax.ShapeDtypeStruct((batch, MLA_HEADS, seq, MLA_SLOT), BF16),
                   jax.ShapeDtypeStruct((batch, MLA_HEADS, seq, 2 * HEAD_DIM), BF16)],
        compiler_params=_cparams("parallel", "parallel"),
        name="mla_up",
    )(proj, proj, proj, gq.reshape(1, -1), gkv.reshape(1, -1), wq, wkv, *tabs)


def _mla_attn_kernel(q_ref, k_ref, v_ref, o_ref, acc_sc, *, tq, rs):
    qi = pl.program_id(1)
    heads = q_ref.shape[1]
    causal = lax.broadcasted_iota(jnp.int32, (tq, tq), 1) <= lax.broadcasted_iota(jnp.int32, (tq, tq), 0)
    nrs = tq // rs
    n_chain = heads * nrs

    def step(kb, carry, diag):
        def keys(c):
            nk = (c % nrs + 1) * rs if diag else tq
            return pl.ds(pl.multiple_of(kb * tq, tq), nk), nk

        def scores(c):
            h, r0 = c // nrs, (c % nrs) * rs
            r, nk = keys(c)
            s = _dot_nt(q_ref[0, h, r0:r0 + rs, :], k_ref[0, h, r, :])
            return jnp.where(causal[r0:r0 + rs, :nk], s, MASKED) if diag else s

        out, s_next = [], scores(0)
        for c in range(n_chain):
            s, s_next = s_next, (scores(c + 1) if c + 1 < n_chain else None)
            m_new, acc_sc[c] = _flash_update(s, carry[c], acc_sc[c], v_ref[0, c // nrs, keys(c)[0], :])
            out.append(m_new)
        return tuple(out)

    acc_sc[...] = jnp.zeros(acc_sc.shape, F32)
    carry = tuple(jnp.full((rs, 1), NEG_INF, F32) for _ in range(n_chain))
    carry = lax.fori_loop(0, qi, lambda kb, c: step(kb, c, False), carry)
    step(qi, carry, True)
    for h in range(heads):
        for i in range(nrs):
            acc = acc_sc[h * nrs + i]
            o_ref[0, i * rs:(i + 1) * rs, h * HEAD_DIM:(h + 1) * HEAD_DIM] = (
                acc[:, :HEAD_DIM] / acc[:, HEAD_DIM:]).astype(o_ref.dtype)


def _mla_attn(q, k, v, tq=512, rs=256):
    batch, heads, seq, _ = q.shape
    return pl.pallas_call(
        functools.partial(_mla_attn_kernel, tq=tq, rs=rs),
        grid=(batch, seq // tq),
        in_specs=[pl.BlockSpec((1, heads, tq, MLA_SLOT), lambda b, i: (b, 0, i, 0)),
                  pl.BlockSpec((1, heads, seq, MLA_SLOT), lambda b, i: (b, 0, 0, 0), pipeline_mode=pl.Buffered(1)),
                  pl.BlockSpec((1, heads, seq, 2 * HEAD_DIM), lambda b, i: (b, 0, 0, 0),
                               pipeline_mode=pl.Buffered(1))],
        out_specs=pl.BlockSpec((1, tq, heads * HEAD_DIM), lambda b, i: (b, i, 0)),
        out_shape=jax.ShapeDtypeStruct((batch, seq, heads * HEAD_DIM), BF16),
        scratch_shapes=[pltpu.VMEM((heads * tq // rs, rs, 2 * HEAD_DIM), F32)],
        compiler_params=_cparams("parallel", "arbitrary"),
        name="mla_attn",
    )(q, k, v)


def _nsa_kernel(nq_ref, kc_ref, vc_ref, ks_ref, vs_ref, kw_ref, vw_ref, ng_ref,
                w1_ref, w2_ref, pos_ref, ovt_ref, tc_ref, ts1_ref, ts2_ref,
                o_ref,
                tmp_sc, kc_sc, vc_sc, ks_sc, vs_sc, kw_sc, vw_sc, acc_sc, *, tq, seq, scale):
    qi = pl.program_id(1)
    half = PARTIAL_ROT // 2
    heads = NSA_HEADS
    n_chunk = seq // NSA_CMP_STRIDE
    n_sel = seq // NSA_SEL_LEN

    @pl.when(qi == 0)
    def _prep():
        c, s1, s2 = tc_ref[...], ts1_ref[...], ts2_ref[...]
        ones = jnp.ones((seq, LANES), BF16)
        blk = lax.broadcasted_iota(jnp.int32, (seq, LANES), 0) >> 6
        ks_sc[:, 0:LANES] = _rope128(ks_ref[...].astype(F32), c, s1, s2, half).astype(BF16)
        ks_sc[:, LANES:] = jnp.where(blk == lax.broadcasted_iota(jnp.int32, (seq, LANES), 1), 1.0, 0.0).astype(BF16)
        kw_sc[...] = _rope128(kw_ref[...].astype(F32), c, s1, s2, half).astype(BF16)
        vs_sc[:, 0:LANES] = vs_ref[...].astype(BF16)
        vs_sc[:, LANES:] = ones
        vw_sc[:, 0:LANES] = vw_ref[...].astype(BF16)
        vw_sc[:, LANES:] = ones
        for i, (src, dst) in enumerate(((kc_ref, kc_sc), (vc_ref, vc_sc))):
            x_all = src[...].astype(F32)
            tmp_sc[...] = _rope128(x_all, c, s1, s2, half) if i == 0 else x_all
            lo, hi = [], []
            for t in range(NSA_CMP_STRIDE):
                x = tmp_sc[pl.ds(t, n_chunk, stride=NSA_CMP_STRIDE), :]
                lo.append((x + pos_ref[i, t:t + 1, :]).astype(BF16))
                hi.append((x + pos_ref[i, NSA_CMP_STRIDE + t:NSA_CMP_STRIDE + t + 1, :]).astype(BF16))
            kw1 = NSA_CMP_STRIDE * HEAD_DIM
            a = _dot(jnp.concatenate(lo, axis=1), w1_ref[i, 0:kw1, :])
            bm = _dot(jnp.concatenate(hi, axis=1), w1_ref[i, kw1:2 * kw1, :])
            hid = jax.nn.gelu(a + pltpu.roll(bm, n_chunk - 1, 0))
            dst[...] = _dot(hid.astype(BF16), w2_ref[i]).astype(BF16)

    t0 = pl.multiple_of(qi * tq, tq)
    rq = pl.ds(t0, tq)
    c, s1, s2 = tc_ref[rq, :], ts1_ref[rq, :], ts2_ref[rq, :]
    qf = nq_ref[...].astype(F32)
    qs = jnp.concatenate(
        [_rope128(qf[:, h * LANES:(h + 1) * LANES], c, s1, s2, half) * scale for h in range(heads)],
        axis=0)
    qs_bf = qs.astype(BF16)

    lane = lax.broadcasted_iota(jnp.int32, (tq, LANES), 1)
    tpos = t0 + lax.broadcasted_iota(jnp.int32, (tq, LANES), 0)

    s = _dot_nt(qs_bf, kc_sc[...]).reshape(heads, tq, LANES)
    cmask = (lane * NSA_CMP_STRIDE + (NSA_CMP_LEN - 1) <= tpos)[None]
    s = jnp.where(cmask, s, NEG_INF)
    m = jnp.max(s, axis=-1, keepdims=True)
    m = jnp.where(m == NEG_INF, 0.0, m)
    e = jnp.exp(s - m)
    p_cmp = e / jnp.maximum(jnp.sum(e, axis=-1, keepdims=True), 1e-30)
    o_cmp = _dot(p_cmp.reshape(heads * tq, LANES).astype(BF16), vc_sc[...]).reshape(heads, tq, HEAD_DIM)

    p_hi, p_lo = _split_bf16(p_cmp[0] + p_cmp[1] + p_cmp[2] + p_cmp[3])
    imp_t = (_dot_nt(ovt_ref[...], p_hi) + _dot_nt(ovt_ref[...], p_lo))[0:n_sel]
    blk_t = lax.broadcasted_iota(jnp.int32, (n_sel, tq), 0)
    cur_t = (t0 + lax.broadcasted_iota(jnp.int32, (n_sel, tq), 1)) >> 6
    eligible = blk_t <= cur_t
    forced = (blk_t == 0) | (blk_t == cur_t) | (blk_t == cur_t - 1)
    score = jnp.where(eligible, jnp.where(forced, NSA_FORCE_SCORE, imp_t), NEG_INF)
    rank = _rank_rows(score, n_sel)
    bias_t = jnp.where(eligible & (rank < NSA_SEL_TOPK), 0.0, MASKED)
    bias_t = jnp.concatenate([bias_t, jnp.zeros((LANES - n_sel, tq), F32)], axis=0)
    bias = _rows_to_lanes(bias_t, tq)
    q_aug = jnp.concatenate([qs, jnp.concatenate([bias] * heads, axis=0)], axis=1).astype(BF16)

    assert tq == NSA_WINDOW
    local_r = lax.broadcasted_iota(jnp.int32, (tq, tq), 0)
    local_c = lax.broadcasted_iota(jnp.int32, (tq, tq), 1)
    causal = local_c <= local_r
    rc = acc_sc.shape[1]
    n_chain = heads * tq // rc

    def run(q_rows, k_sc, v_sc, kb, carry, mask, lower):
        def keys(c):
            r0 = (c * rc) % tq
            k0, nk = (0, tq) if mask is None else ((0, r0 + rc) if lower else (r0, tq - r0))
            return pl.ds(pl.multiple_of(kb * tq + k0, rc), nk), r0, k0, nk

        def scores(c):
            r, r0, k0, nk = keys(c)
            s = _dot_nt(q_rows[c * rc:(c + 1) * rc], k_sc[r, :])
            return s if mask is None else jnp.where(mask[r0:r0 + rc, k0:k0 + nk], s, MASKED)

        out, s_next = [], scores(0)
        for c in range(n_chain):
            s, s_next = s_next, (scores(c + 1) if c + 1 < n_chain else None)
            m_new, acc_sc[c] = _flash_update(s, carry[c], acc_sc[c], v_sc[keys(c)[0], :])
            out.append(m_new)
        return tuple(out)

    def finish():
        acc = jnp.concatenate([acc_sc[c] for c in range(n_chain)], axis=0)
        return (acc[:, :HEAD_DIM] / acc[:, HEAD_DIM:]).reshape(heads, tq, HEAD_DIM)

    init = tuple(jnp.full((rc, 1), NEG_INF, F32) for _ in range(n_chain))

    acc_sc[...] = jnp.zeros(acc_sc.shape, F32)
    carry = lax.fori_loop(0, qi, lambda kb, cr: run(q_aug, ks_sc, vs_sc, kb, cr, None, True), init)
    run(q_aug, ks_sc, vs_sc, qi, carry, causal, True)
    o_sel = finish()

    acc_sc[...] = jnp.zeros(acc_sc.shape, F32)
    carry = run(qs_bf, kw_sc, vw_sc, jnp.maximum(qi - 1, 0), init, (local_c > local_r) & (qi >= 1), False)
    run(qs_bf, kw_sc, vw_sc, qi, carry, causal, True)
    o_win = finish()

    g = jax.nn.sigmoid(ng_ref[...].astype(F32))
    for h in range(heads):
        o = (g[:, 3 * h:3 * h + 1] * o_cmp[h] + g[:, 3 * h + 1:3 * h + 2] * o_sel[h]
             + g[:, 3 * h + 2:3 * h + 3] * o_win[h])
        o_ref[:, h * LANES:(h + 1) * LANES] = o.astype(o_ref.dtype)


def _nsa(proj, w1, w2, pos, tabs, batch, seq, tq=NSA_WINDOW, rc=256):
    nt = seq // tq
    n_cmp = (seq - NSA_CMP_LEN) // NSA_CMP_STRIDE + 1
    n_sel = seq // NSA_SEL_LEN
    starts = np.arange(LANES) * NSA_CMP_STRIDE
    sel_start = np.arange(LANES) * NSA_SEL_LEN
    overlap = ((starts[:, None] < sel_start[None, :] + NSA_SEL_LEN)
               & (starts[:, None] + NSA_CMP_LEN > sel_start[None, :])
               & (np.arange(LANES)[:, None] < n_cmp) & (np.arange(LANES)[None, :] < n_sel))
    ovt = jnp.asarray(overlap.T.astype(np.float32), BF16)
    seq_col = lambda name: pl.BlockSpec((seq, LANES), lambda b, i: (b, COL128[name]))
    full = lambda a: pl.BlockSpec(a.shape, lambda b, i: (0,) * a.ndim)
    return pl.pallas_call(
        functools.partial(_nsa_kernel, tq=tq, seq=seq, scale=HEAD_DIM ** -0.5),
        grid=(batch, nt),
        in_specs=[pl.BlockSpec((tq, 512), lambda b, i: (b * nt + i, COL512["nq"])),
                  seq_col("nkc"), seq_col("nvc"), seq_col("nks"), seq_col("nvs"), seq_col("nkw"), seq_col("nvw"),
                  pl.BlockSpec((tq, LANES), lambda b, i: (b * nt + i, COL128["ng"])),
                  full(w1), full(w2), full(pos), full(ovt), full(tabs[0]), full(tabs[1]), full(tabs[2])],
        out_specs=pl.BlockSpec((tq, NSA_HEADS * HEAD_DIM), lambda b, i: (b * nt + i, 0)),
        out_shape=jax.ShapeDtypeStruct((batch * seq, NSA_HEADS * HEAD_DIM), BF16),
        scratch_shapes=[pltpu.VMEM((seq, LANES), F32),
                        pltpu.VMEM((LANES, LANES), BF16), pltpu.VMEM((LANES, LANES), BF16),
                        pltpu.VMEM((seq, 2 * LANES), BF16), pltpu.VMEM((seq, 2 * LANES), BF16),
                        pltpu.VMEM((seq, LANES), BF16), pltpu.VMEM((seq, 2 * LANES), BF16),
                        pltpu.VMEM((NSA_HEADS * tq // rc, rc, 2 * HEAD_DIM), F32)],
        compiler_params=_cparams("parallel", "arbitrary"),
        name="nsa",
    )(proj, proj, proj, proj, proj, proj, proj, proj, w1, w2, pos, ovt, *tabs)


def _moba_kernel(mq_ref, mk_ref, mv_ref, tc_ref, ts1_ref, ts2_ref, o_ref, k_sc, v_sc, km_sc, acc_sc,
                 *, tq, seq, scale):
    qi = pl.program_id(1)
    half = PARTIAL_ROT // 2
    heads = MOBA_HEADS
    n_blk = seq // MOBA_BLOCK
    slot = 2 * LANES

    @pl.when(qi == 0)
    def _prep():
        c, s1, s2 = tc_ref[...], ts1_ref[...], ts2_ref[...]
        km_sc[...] = jnp.zeros_like(km_sc)
        ones = jnp.ones((seq, LANES), BF16)
        blk = lax.broadcasted_iota(jnp.int32, (seq, LANES), 0) >> 8
        onehot = jnp.where(blk == lax.broadcasted_iota(jnp.int32, (seq, LANES), 1), 1.0, 0.0).astype(BF16)
        for h in range(heads):
            hs = slice(h * LANES, (h + 1) * LANES)
            kh = _rope128(mk_ref[:, hs].astype(F32), c, s1, s2, half)
            k_sc[:, h * slot:h * slot + LANES] = kh.astype(BF16)
            k_sc[:, h * slot + LANES:(h + 1) * slot] = onehot
            v_sc[:, h * slot:h * slot + LANES] = mv_ref[:, hs].astype(BF16)
            v_sc[:, h * slot + LANES:(h + 1) * slot] = ones
            for j in range(n_blk):
                km_sc[h * n_blk + j:h * n_blk + j + 1, :] = jnp.mean(
                    kh[j * MOBA_BLOCK:(j + 1) * MOBA_BLOCK], axis=0, keepdims=True)

    t0 = pl.multiple_of(qi * tq, tq)
    rq = pl.ds(t0, tq)
    c, s1, s2 = tc_ref[rq, :], ts1_ref[rq, :], ts2_ref[rq, :]
    km_hi, km_lo = _split_bf16(km_sc[...])
    tk = tq
    rc = MOBA_BLOCK
    sub = tq // rc
    blk_t = lax.broadcasted_iota(jnp.int32, (n_blk, tq), 0)
    cur_t = (t0 + lax.broadcasted_iota(jnp.int32, (n_blk, tq), 1)) >> 8
    row = lax.broadcasted_iota(jnp.int32, (tq, tk), 0)
    col = lax.broadcasted_iota(jnp.int32, (tq, tk), 1)

    q_aug = []
    for h in range(heads):
        qf = _rope128(mq_ref[:, h * LANES:(h + 1) * LANES].astype(F32), c, s1, s2, half) * scale
        q_hi, q_lo = _split_bf16(qf)
        gate_t = (_dot_nt(km_hi, q_hi) + (_dot_nt(km_lo, q_hi) + _dot_nt(km_hi, q_lo)))[h * n_blk:(h + 1) * n_blk]
        eligible = blk_t < cur_t
        score = jnp.where(eligible, gate_t, NEG_INF)
        picked = eligible & (_rank_rows(score, n_blk) < MOBA_TOPK)
        bias_t = jnp.where(picked | (blk_t == cur_t), 0.0, MASKED)
        bias_t = jnp.concatenate([bias_t, jnp.zeros((LANES - n_blk, tq), F32)], axis=0)
        q_aug.append(jnp.concatenate([qf, _rows_to_lanes(bias_t, tq)], axis=1).astype(BF16))

    n_chain = heads * sub

    causal = col <= row

    def step(kb, carry, diag):
        def keys(c):
            nk = (c % sub + 1) * rc if diag else tk
            return pl.ds(pl.multiple_of(kb * tk, tk), nk), nk

        def scores(c):
            h, r0 = c // sub, (c % sub) * rc
            r, nk = keys(c)
            s = _dot_nt(q_aug[h][r0:r0 + rc], k_sc[r, h * slot:(h + 1) * slot])
            return jnp.where(causal[r0:r0 + rc, :nk], s, MASKED) if diag else s

        out, s_next = [], scores(0)
        for c in range(n_chain):
            s, s_next = s_next, (scores(c + 1) if c + 1 < n_chain else None)
            h = c // sub
            m_new, acc_sc[c] = _flash_update(s, carry[c], acc_sc[c], v_sc[keys(c)[0], h * slot:(h + 1) * slot])
            out.append(m_new)
        return tuple(out)

    acc_sc[...] = jnp.zeros(acc_sc.shape, F32)
    carry = tuple(jnp.full((rc, 1), NEG_INF, F32) for _ in range(n_chain))
    carry = lax.fori_loop(0, qi, lambda kb, cr: step(kb, cr, False), carry)
    step(qi, carry, True)
    for c in range(n_chain):
        h, r0 = c // sub, (c % sub) * rc
        acc = acc_sc[c]
        o_ref[r0:r0 + rc, h * LANES:(h + 1) * LANES] = (acc[:, :HEAD_DIM] / acc[:, HEAD_DIM:]).astype(o_ref.dtype)


def _moba(proj, tabs, batch, seq):
    tq = 2 * MOBA_BLOCK
    nt = seq // tq
    width = MOBA_HEADS * HEAD_DIM
    full = lambda a: pl.BlockSpec(a.shape, lambda b, i: (0,) * a.ndim)
    return pl.pallas_call(
        functools.partial(_moba_kernel, tq=tq, seq=seq, scale=HEAD_DIM ** -0.5),
        grid=(batch, nt),
        in_specs=[pl.BlockSpec((tq, width), lambda b, i: (b * nt + i, COL512["mq"])),
                  pl.BlockSpec((seq, width), lambda b, i: (b, COL512["mk"])),
                  pl.BlockSpec((seq, width), lambda b, i: (b, COL512["mv"])),
                  full(tabs[0]), full(tabs[1]), full(tabs[2])],
        out_specs=pl.BlockSpec((tq, width), lambda b, i: (b * nt + i, 0)),
        out_shape=jax.ShapeDtypeStruct((batch * seq, width), BF16),
        scratch_shapes=[pltpu.VMEM((seq, 2 * width), BF16), pltpu.VMEM((seq, 2 * width), BF16),
                        pltpu.VMEM((LANES, LANES), F32),
                        pltpu.VMEM((MOBA_HEADS * tq // MOBA_BLOCK, MOBA_BLOCK, 2 * HEAD_DIM), F32)],
        compiler_params=_cparams("parallel", "arbitrary"),
        name="moba",
    )(proj, proj, proj, *tabs)


def _xattn_kernel(q_ref, kv_ref, o_ref):
    d = q_ref.shape[1]
    hd = d // MEM_HEADS
    for h in range(MEM_HEADS):
        hs = slice(h * hd, (h + 1) * hd)
        s = _dot_nt(q_ref[:, hs], kv_ref[0, :, hs])
        e = jnp.exp(s - jnp.max(s, axis=-1, keepdims=True))
        p = e / jnp.sum(e, axis=-1, keepdims=True)
        o_ref[:, hs] = _dot(p.astype(BF16), kv_ref[0, :, d + h * hd:d + (h + 1) * hd]).astype(o_ref.dtype)


def _xattn(q, kv, batch, seq, tq=512):
    d = q.shape[1]
    nt = seq // tq
    m_len = kv.shape[0] // batch
    return pl.pallas_call(
        _xattn_kernel,
        grid=(batch, nt),
        in_specs=[pl.BlockSpec((tq, d), lambda b, i: (b * nt + i, 0)),
                  pl.BlockSpec((1, m_len, 2 * d), lambda b, i: (b, 0, 0))],
        out_specs=pl.BlockSpec((tq, d), lambda b, i: (b * nt + i, 0)),
        out_shape=jax.ShapeDtypeStruct((batch * seq, d), BF16),
        compiler_params=_cparams("parallel", "parallel"),
        name="xattn",
    )(q, kv.reshape(batch, m_len, 2 * d))


IN_SPLIT_NAMES = ("c_q", "c_kv", "k_rope", "nq", "nkc", "nvc", "nks", "nvs", "nkw", "nvw", "ng", "mq", "mk", "mv")
IN_SPLIT_SIZES = (512, 512, 64, 512, 128, 128, 128, 128, 128, 128, 12, 512, 512, 512)


def _pack_w_in_kernel(w_ref, o_ref):
    rows = w_ref.shape[0]
    off = 0
    for name, sz in zip(IN_SPLIT_NAMES, IN_SPLIT_SIZES):
        width = 512 if name in COL512 else LANES
        dst = COL512[name] * 512 if name in COL512 else COL128[name] * LANES
        x = w_ref[:, off:off + sz].astype(BF16)
        if sz < width:
            x = jnp.concatenate([x, jnp.zeros((rows, width - sz), BF16)], axis=1)
        o_ref[0, :, dst:dst + width] = x
        off += sz


def _pack_w_in(w, tr=256):
    layers, d, width = w.shape
    return pl.pallas_call(
        _pack_w_in_kernel,
        grid=(layers, d // tr),
        in_specs=[pl.BlockSpec((None, tr, width), lambda l, i: (l, i, 0))],
        out_specs=pl.BlockSpec((1, tr, PROJ_WIDTH), lambda l, i: (l, i, 0)),
        out_shape=jax.ShapeDtypeStruct((layers, d, PROJ_WIDTH), BF16),
        compiler_params=_cparams("parallel", "parallel"),
        name="pack_w_in",
    )(w)


def _pack_w_uq(w):
    r = w.shape[0]
    w = w.reshape(r, MLA_HEADS, MLA_NOPE + MLA_ROPE)
    w = jnp.pad(w, ((0, 0), (0, 0), (0, MLA_SLOT - MLA_NOPE - MLA_ROPE)))
    return w.reshape(r, MLA_HEADS * MLA_SLOT).astype(BF16)


def _pack_w_ukv(w):
    r = w.shape[0]
    w = w.reshape(r, MLA_HEADS, 2, HEAD_DIM).transpose(0, 2, 1, 3)
    return w.reshape(r, 2 * MLA_HEADS * HEAD_DIM).astype(BF16)


def kernel(x, mem, ln_in_g, ln_in_b, w_in, mla_q_norm, mla_kv_norm, mla_w_uq, mla_w_ukv, nsa_cmp_w1, nsa_cmp_w2, nsa_cmp_pos, w_out, ln1_g, ln1_b, mem_wq, mem_wkv, mem_wo, ln2_g, ln2_b, mlp_w1, mlp_w2, ln3_g, ln3_b):
    batch, seq, d = x.shape
    n = batch * seq
    mla_tabs = _rope_tables(seq, MLA_ROPE)
    rot_tabs = _rope_tables(seq, PARTIAL_ROT)
    mem2 = mem.reshape(batch * mem.shape[1], d)
    w_in_packed = _pack_w_in(w_in)
    w_out_bf, mem_wkv_bf, mem_wo_bf = w_out.astype(BF16), mem_wkv.astype(BF16), mem_wo.astype(BF16)
    mem_wq_bf = (mem_wq * (d // MEM_HEADS) ** -0.5).astype(BF16)
    mlp_w1_bf, mlp_w2_bf = mlp_w1.astype(BF16), mlp_w2.astype(BF16)

    h = _layer_norm(x.reshape(n, d), ln_in_g, ln_in_b)
    for l in range(DEPTH):
        proj = _matmul(h, w_in_packed, l, BF16, name="in_proj")
        q, k, v = _mla_up(proj, mla_q_norm[l], mla_kv_norm[l], _pack_w_uq(mla_w_uq[l]), _pack_w_ukv(mla_w_ukv[l]),
                          mla_tabs, batch, seq)
        o_a = _mla_attn(q, k, v).reshape(n, MLA_HEADS * HEAD_DIM)
        o_b = _nsa(proj, nsa_cmp_w1[l].astype(BF16), nsa_cmp_w2[l].astype(BF16), nsa_cmp_pos[l], rot_tabs, batch, seq)
        o_c = _moba(proj, rot_tabs, batch, seq)
        h = _out_ln([o_a, o_b, o_c], w_out_bf, l, h, ln1_g[l], ln1_b[l], name="mix_out_ln")

        xq = _matmul(h, mem_wq_bf, l, BF16, name="mem_q")
        xkv = _matmul(mem2, mem_wkv_bf, l, BF16, name="mem_kv")
        ctx = _xattn(xq, xkv, batch, seq)
        h = _out_ln([ctx], mem_wo_bf, l, h, ln2_g[l], ln2_b[l], name="mem_out_ln")

        h = _mlp_ln(h, mlp_w1_bf, mlp_w2_bf, l, ln3_g[l], ln3_b[l])
    return h.reshape(batch, seq, d)
```

```python
import functools

import numpy as np
import jax
import jax.numpy as jnp
from jax import lax
from jax.experimental import pallas as pl
from jax.experimental.pallas import tpu as pltpu

F32 = jnp.float32
BF16 = jnp.bfloat16
NEG_INF = float("-inf")

D_MODEL = 2048
DEPTH = 2
HEAD_DIM = 128
MLA_HEADS = 8
NSA_HEADS = 4
MOBA_HEADS = 4
ROPE_THETA = 500000.0
PARTIAL_ROT = HEAD_DIM // 4
MLA_Q_RANK = 512
MLA_KV_RANK = 512
MLA_NOPE = 128
MLA_ROPE = 64
MLA_SLOT = 256
NSA_CMP_LEN = 32
NSA_CMP_STRIDE = 16
NSA_SEL_LEN = 64
NSA_SEL_TOPK = 16
NSA_WINDOW = 512
NSA_FORCE_SCORE = 1.0e4
MOBA_BLOCK = 256
MOBA_TOPK = 3
MEM_HEADS = 4
D_FF = 4 * D_MODEL
DEEPNORM_ALPHA = (2 * DEPTH) ** 0.25
LANES = 128

PROJ_WIDTH = 4096
COL512 = dict(c_q=0, c_kv=1, nq=2, mq=3, mk=4, mv=5)
COL128 = dict(k_rope=24, nkc=25, nvc=26, nks=27, nvs=28, nkw=29, nvw=30, ng=31)

VMEM_LIMIT = 56 * 1024 * 1024


def _cparams(*sem):
    return pltpu.CompilerParams(dimension_semantics=sem, vmem_limit_bytes=VMEM_LIMIT)


def _dot(a, b):
    return jnp.dot(a, b, preferred_element_type=F32)


def _dot_nt(a, b):
    return lax.dot_general(a, b, (((1,), (1,)), ((), ())), preferred_element_type=F32)


def _split_bf16(x):
    hi = x.astype(BF16)
    return hi, (x - hi.astype(F32)).astype(BF16)


def _ln_rows(x, g, b, eps=1e-5):
    mu = jnp.mean(x, axis=-1, keepdims=True)
    xc = x - mu
    var = jnp.mean(xc * xc, axis=-1, keepdims=True)
    return xc * lax.rsqrt(var + eps) * g + b


def _rms_rows(x, g, eps=1e-6):
    return x * lax.rsqrt(jnp.mean(x * x, axis=-1, keepdims=True) + eps) * g


def _rope128(x, c, s1, s2, half):
    return x * c + pltpu.roll(x, LANES - half, 1) * s1 + pltpu.roll(x, half, 1) * s2


def _rope_tables(n_pos, dim):
    half = dim // 2
    inv = ROPE_THETA ** (-jnp.arange(0, dim, 2, dtype=F32) / dim)
    ang = jnp.arange(n_pos, dtype=F32)[:, None] * inv[None, :]
    cos, sin = jnp.cos(ang), jnp.sin(ang)
    ones = jnp.ones((n_pos, LANES - dim), F32)
    z = lambda w: jnp.zeros((n_pos, w), F32)
    c = jnp.concatenate([cos, cos, ones], axis=1)
    s1 = jnp.concatenate([-sin, z(LANES - half)], axis=1)
    s2 = jnp.concatenate([z(half), sin, z(LANES - dim)], axis=1)
    return c, s1, s2


MASKED = -1.0e30


def _rank_rows(score, n_cand):
    row = lax.broadcasted_iota(jnp.int32, score.shape, 0)
    rank = jnp.zeros(score.shape, F32)
    for jp in range(n_cand):
        cand = score[jp:jp + 1, :]
        ahead = (cand > score) | ((cand == score) & (row > jp))
        rank = rank + jnp.where(ahead, 1.0, 0.0)
    return rank


def _rows_to_lanes(x_t, tq):
    eye = jnp.where(lax.broadcasted_iota(jnp.int32, (tq, tq), 0) == lax.broadcasted_iota(jnp.int32, (tq, tq), 1),
                    1.0, 0.0).astype(BF16)
    return _dot_nt(eye, x_t.astype(BF16))


def _flash_update(s, m, acc, v_aug):
    m_new = jnp.maximum(m, jnp.max(s, axis=-1, keepdims=True))
    p = jnp.exp(s - m_new).astype(BF16)
    return m_new, jnp.exp(m - m_new) * acc + _dot(p, v_aug)


def _ln_kernel(x_ref, g_ref, b_ref, o_ref):
    o_ref[...] = _ln_rows(x_ref[...], g_ref[...], b_ref[...])


def _layer_norm(x, g, b, tm=512):
    n, d = x.shape
    return pl.pallas_call(
        _ln_kernel,
        grid=(n // tm,),
        in_specs=[pl.BlockSpec((tm, d), lambda i: (i, 0)),
                  pl.BlockSpec((1, d), lambda i: (0, 0)),
                  pl.BlockSpec((1, d), lambda i: (0, 0))],
        out_specs=pl.BlockSpec((tm, d), lambda i: (i, 0)),
        out_shape=jax.ShapeDtypeStruct((n, d), F32),
        compiler_params=_cparams("parallel"),
        name="ln_in",
    )(x, g.reshape(1, d), b.reshape(1, d))


def _mm_kernel(a_ref, w_ref, o_ref, abf_ref):
    @pl.when(pl.program_id(1) == 0)
    def _():
        abf_ref[...] = a_ref[...].astype(BF16)

    o_ref[...] = _dot(abf_ref[...], w_ref[...]).astype(o_ref.dtype)


def _matmul(a, w, layer, out_dtype, tm=1024, tn=2048, name="mm"):
    m, k = a.shape
    n = w.shape[2]
    tm = min(tm, m)
    return pl.pallas_call(
        _mm_kernel,
        grid=(m // tm, n // tn),
        in_specs=[pl.BlockSpec((tm, k), lambda i, j: (i, 0)),
                  pl.BlockSpec((None, k, tn), lambda i, j: (layer, 0, j))],
        out_specs=pl.BlockSpec((tm, tn), lambda i, j: (i, j)),
        out_shape=jax.ShapeDtypeStruct((m, n), out_dtype),
        scratch_shapes=[pltpu.VMEM((tm, k), BF16)],
        compiler_params=_cparams("parallel", "arbitrary"),
        name=name,
    )(a, w)


def _out_ln_kernel(*refs, widths):
    n_a = len(widths)
    a_refs = refs[:n_a]
    w_ref, h_ref, g_ref, b_ref, o_ref = refs[n_a:]
    tm = h_ref.shape[0]
    halves = (slice(0, tm // 2), slice(tm // 2, tm))
    accs = []
    for rows in halves:
        acc = DEEPNORM_ALPHA * h_ref[rows, :]
        off = 0
        for a_ref, wd in zip(a_refs, widths):
            acc = acc + _dot(a_ref[rows, :], w_ref[off:off + wd, :])
            off += wd
        accs.append(acc)
    for rows, acc in zip(halves, accs):
        o_ref[rows, :] = _ln_rows(acc, g_ref[...], b_ref[...])


def _out_ln(a_list, w, layer, h, g, b, tm=512, name="out_ln"):
    n, d = h.shape
    widths = tuple(a.shape[1] for a in a_list)
    k = sum(widths)
    in_specs = [pl.BlockSpec((tm, wd), lambda i: (i, 0)) for wd in widths]
    in_specs += [pl.BlockSpec((None, k, d), lambda i: (layer, 0, 0)),
                 pl.BlockSpec((tm, d), lambda i: (i, 0)),
                 pl.BlockSpec((1, d), lambda i: (0, 0)),
                 pl.BlockSpec((1, d), lambda i: (0, 0))]
    return pl.pallas_call(
        functools.partial(_out_ln_kernel, widths=widths),
        grid=(n // tm,),
        in_specs=in_specs,
        out_specs=pl.BlockSpec((tm, d), lambda i: (i, 0)),
        out_shape=jax.ShapeDtypeStruct((n, d), F32),
        compiler_params=_cparams("parallel"),
        name=name,
    )(*a_list, w, h, g.reshape(1, d), b.reshape(1, d))


def _mlp_kernel(h_ref, w1_ref, w2_ref, g_ref, b_ref, o_ref, hbf_ref, acc_ref):
    f = pl.program_id(1)

    @pl.when(f == 0)
    def _():
        hbf_ref[...] = h_ref[...].astype(BF16)
        acc_ref[...] = jnp.zeros_like(acc_ref)

    u = jnp.maximum(_dot(hbf_ref[...], w1_ref[...]), 0.0)
    acc_ref[...] += _dot((u * u).astype(BF16), w2_ref[...])

    @pl.when(f == pl.num_programs(1) - 1)
    def _():
        y = DEEPNORM_ALPHA * h_ref[...] + acc_ref[...]
        o_ref[...] = _ln_rows(y, g_ref[...], b_ref[...])


def _mlp_ln(h, w1, w2, layer, g, b, tm=512, tf=1024):
    n, d = h.shape
    dff = w1.shape[2]
    return pl.pallas_call(
        _mlp_kernel,
        grid=(n // tm, dff // tf),
        in_specs=[pl.BlockSpec((tm, d), lambda i, f: (i, 0)),
                  pl.BlockSpec((None, d, tf), lambda i, f: (layer, 0, f)),
                  pl.BlockSpec((None, tf, d), lambda i, f: (layer, f, 0)),
                  pl.BlockSpec((1, d), lambda i, f: (0, 0)),
                  pl.BlockSpec((1, d), lambda i, f: (0, 0))],
        out_specs=pl.BlockSpec((tm, d), lambda i, f: (i, 0)),
        out_shape=jax.ShapeDtypeStruct((n, d), F32),
        scratch_shapes=[pltpu.VMEM((tm, d), BF16), pltpu.VMEM((tm, d), F32)],
        compiler_params=_cparams("parallel", "arbitrary"),
        name="mlp_ln",
    )(h, w1, w2, g.reshape(1, d), b.reshape(1, d))


def _mla_up_kernel(cq_ref, ckv_ref, kr_ref, gq_ref, gkv_ref, wq_ref, wkv_ref, tc_ref, ts1_ref, ts2_ref,
                   q_ref, k_ref, v_ref, *, scale):
    half = MLA_ROPE // 2
    c, s1, s2 = tc_ref[...], ts1_ref[...], ts2_ref[...]
    nq = _rms_rows(cq_ref[...].astype(F32), gq_ref[...]).astype(BF16)
    nkv = _rms_rows(ckv_ref[...].astype(F32), gkv_ref[...]).astype(BF16)
    qf = _dot(nq, wq_ref[...])
    kvf = _dot(nkv, wkv_ref[...])
    kr = _rope128(kr_ref[...].astype(F32), c, s1, s2, half).astype(BF16)
    for h in range(MLA_HEADS):
        o = h * MLA_SLOT
        q_ref[0, h, :, 0:LANES] = (qf[:, o:o + LANES] * scale).astype(BF16)
        q_ref[0, h, :, LANES:] = (_rope128(qf[:, o + LANES:o + MLA_SLOT], c, s1, s2, half) * scale).astype(BF16)
        k_ref[0, h, :, 0:LANES] = kvf[:, h * LANES:(h + 1) * LANES].astype(BF16)
        k_ref[0, h, :, LANES:] = kr
        v_ref[0, h, :, 0:LANES] = kvf[:, (MLA_HEADS + h) * LANES:(MLA_HEADS + h + 1) * LANES].astype(BF16)
        v_ref[0, h, :, LANES:] = jnp.ones((kr.shape[0], LANES), BF16)


def _mla_up(proj, gq, gkv, wq, wkv, tabs, batch, seq, tm=512):
    nt = seq // tm
    row = lambda b, i: b * nt + i
    tab_spec = pl.BlockSpec((tm, LANES), lambda b, i: (i, 0))
    hm = lambda w: pl.BlockSpec((1, MLA_HEADS, tm, w), lambda b, i: (b, 0, i, 0))
    return pl.pallas_call(
        functools.partial(_mla_up_kernel, scale=(MLA_NOPE + MLA_ROPE) ** -0.5),
        grid=(batch, nt),
        in_specs=[pl.BlockSpec((tm, 512), lambda b, i: (row(b, i), COL512["c_q"])),
                  pl.BlockSpec((tm, 512), lambda b, i: (row(b, i), COL512["c_kv"])),
                  pl.BlockSpec((tm, LANES), lambda b, i: (row(b, i), COL128["k_rope"])),
                  pl.BlockSpec((1, MLA_Q_RANK), lambda b, i: (0, 0)),
                  pl.BlockSpec((1, MLA_KV_RANK), lambda b, i: (0, 0)),
                  pl.BlockSpec(wq.shape, lambda b, i: (0, 0)),
                  pl.BlockSpec(wkv.shape, lambda b, i: (0, 0)),
                  tab_spec, tab_spec, tab_spec],
        out_specs=[hm(MLA_SLOT), hm(MLA_SLOT), hm(2 * HEAD_DIM)],
        out_shape=[jax.ShapeDtypeStruct((batch, MLA_HEADS, seq, MLA_SLOT), BF16),
                   jax.ShapeDtypeStruct((batch, MLA_HEADS, seq, MLA_SLOT), BF16),
                   jax.ShapeDtypeStruct((batch, MLA_HEADS, seq, 2 * HEAD_DIM), BF16)],
        compiler_params=_cparams("parallel", "parallel"),
        name="mla_up",
    )(proj, proj, proj, gq.reshape(1, -1), gkv.reshape(1, -1), wq, wkv, *tabs)


def _mla_attn_kernel(q_ref, k_ref, v_ref, o_ref, acc_sc, *, tq, rs):
    qi = pl.program_id(2)
    heads = q_ref.shape[1]
    causal = lax.broadcasted_iota(jnp.int32, (tq, tq), 1) <= lax.broadcasted_iota(jnp.int32, (tq, tq), 0)
    nrs = tq // rs
    n_chain = heads * nrs

    def step(kb, carry, diag):
        def keys(c):
            nk = (c % nrs + 1) * rs if diag else tq
            return pl.ds(pl.multiple_of(kb * tq, tq), nk), nk

        def scores(c):
            h, r0 = c // nrs, (c % nrs) * rs
            r, nk = keys(c)
            s = _dot_nt(q_ref[0, h, r0:r0 + rs, :], k_ref[0, h, r, :])
            return jnp.where(causal[r0:r0 + rs, :nk], s, MASKED) if diag else s

        out, s_next = [], scores(0)
        for c in range(n_chain):
            s, s_next = s_next, (scores(c + 1) if c + 1 < n_chain else None)
            m_new, acc_sc[c] = _flash_update(s, carry[c], acc_sc[c], v_ref[0, c // nrs, keys(c)[0], :])
            out.append(m_new)
        return tuple(out)

    acc_sc[...] = jnp.zeros(acc_sc.shape, F32)
    carry = tuple(jnp.full((rs, 1), NEG_INF, F32) for _ in range(n_chain))
    carry = lax.fori_loop(0, qi, lambda kb, c: step(kb, c, False), carry)
    step(qi, carry, True)
    for h in range(heads):
        for i in range(nrs):
            acc = acc_sc[h * nrs + i]
            o_ref[0, i * rs:(i + 1) * rs, h * HEAD_DIM:(h + 1) * HEAD_DIM] = (
                acc[:, :HEAD_DIM] / acc[:, HEAD_DIM:]).astype(o_ref.dtype)


def _mla_attn(q, k, v, tq=512, rs=256, hg=4):
    batch, heads, seq, _ = q.shape
    return pl.pallas_call(
        functools.partial(_mla_attn_kernel, tq=tq, rs=rs),
        grid=(batch, heads // hg, seq // tq),
        in_specs=[pl.BlockSpec((1, hg, tq, MLA_SLOT), lambda b, g, i: (b, g, i, 0)),
                  pl.BlockSpec((1, hg, seq, MLA_SLOT), lambda b, g, i: (b, g, 0, 0)),
                  pl.BlockSpec((1, hg, seq, 2 * HEAD_DIM), lambda b, g, i: (b, g, 0, 0))],
        out_specs=pl.BlockSpec((1, tq, hg * HEAD_DIM), lambda b, g, i: (b, i, g)),
        out_shape=jax.ShapeDtypeStruct((batch, seq, heads * HEAD_DIM), BF16),
        scratch_shapes=[pltpu.VMEM((hg * tq // rs, rs, 2 * HEAD_DIM), F32)],
        compiler_params=_cparams("parallel", "parallel", "arbitrary"),
        name="mla_attn",
    )(q, k, v)


def _nsa_kernel(nq_ref, kc_ref, vc_ref, ks_ref, vs_ref, kw_ref, vw_ref, ng_ref,
                w1_ref, w2_ref, pos_ref, ovt_ref, tc_ref, ts1_ref, ts2_ref,
                o_ref,
                tmp_sc, kc_sc, vc_sc, ks_sc, vs_sc, kw_sc, vw_sc, acc_sc, *, tq, seq, scale):
    qi = pl.program_id(1)
    half = PARTIAL_ROT // 2
    heads = NSA_HEADS
    n_chunk = seq // NSA_CMP_STRIDE
    n_sel = seq // NSA_SEL_LEN

    @pl.when(qi == 0)
    def _prep():
        c, s1, s2 = tc_ref[...], ts1_ref[...], ts2_ref[...]
        ones = jnp.ones((seq, LANES), BF16)
        blk = lax.broadcasted_iota(jnp.int32, (seq, LANES), 0) >> 6
        ks_sc[:, 0:LANES] = _rope128(ks_ref[...].astype(F32), c, s1, s2, half).astype(BF16)
        ks_sc[:, LANES:] = jnp.where(blk == lax.broadcasted_iota(jnp.int32, (seq, LANES), 1), 1.0, 0.0).astype(BF16)
        kw_sc[...] = _rope128(kw_ref[...].astype(F32), c, s1, s2, half).astype(BF16)
        vs_sc[:, 0:LANES] = vs_ref[...].astype(BF16)
        vs_sc[:, LANES:] = ones
        vw_sc[:, 0:LANES] = vw_ref[...].astype(BF16)
        vw_sc[:, LANES:] = ones
        for i, (src, dst) in enumerate(((kc_ref, kc_sc), (vc_ref, vc_sc))):
            x_all = src[...].astype(F32)
            tmp_sc[...] = _rope128(x_all, c, s1, s2, half) if i == 0 else x_all
            lo, hi = [], []
            for t in range(NSA_CMP_STRIDE):
                x = tmp_sc[pl.ds(t, n_chunk, stride=NSA_CMP_STRIDE), :]
                lo.append((x + pos_ref[i, t:t + 1, :]).astype(BF16))
                hi.append((x + pos_ref[i, NSA_CMP_STRIDE + t:NSA_CMP_STRIDE + t + 1, :]).astype(BF16))
            kw1 = NSA_CMP_STRIDE * HEAD_DIM
            a = _dot(jnp.concatenate(lo, axis=1), w1_ref[i, 0:kw1, :])
            bm = _dot(jnp.concatenate(hi, axis=1), w1_ref[i, kw1:2 * kw1, :])
            hid = jax.nn.gelu(a + pltpu.roll(bm, n_chunk - 1, 0))
            dst[...] = _dot(hid.astype(BF16), w2_ref[i]).astype(BF16)

    t0 = pl.multiple_of(qi * tq, tq)
    rq = pl.ds(t0, tq)
    c, s1, s2 = tc_ref[rq, :], ts1_ref[rq, :], ts2_ref[rq, :]
    qf = nq_ref[...].astype(F32)
    qs = jnp.concatenate(
        [_rope128(qf[:, h * LANES:(h + 1) * LANES], c, s1, s2, half) * scale for h in range(heads)],
        axis=0)
    qs_bf = qs.astype(BF16)

    lane = lax.broadcasted_iota(jnp.int32, (tq, LANES), 1)
    tpos = t0 + lax.broadcasted_iota(jnp.int32, (tq, LANES), 0)

    s = _dot_nt(qs_bf, kc_sc[...]).reshape(heads, tq, LANES)
    cmask = (lane * NSA_CMP_STRIDE + (NSA_CMP_LEN - 1) <= tpos)[None]
    s = jnp.where(cmask, s, NEG_INF)
    m = jnp.max(s, axis=-1, keepdims=True)
    m = jnp.where(m == NEG_INF, 0.0, m)
    e = jnp.exp(s - m)
    p_cmp = e / jnp.maximum(jnp.sum(e, axis=-1, keepdims=True), 1e-30)
    o_cmp = _dot(p_cmp.reshape(heads * tq, LANES).astype(BF16), vc_sc[...]).reshape(heads, tq, HEAD_DIM)

    p_hi, p_lo = _split_bf16(p_cmp[0] + p_cmp[1] + p_cmp[2] + p_cmp[3])
    imp_t = (_dot_nt(ovt_ref[...], p_hi) + _dot_nt(ovt_ref[...], p_lo))[0:n_sel]
    blk_t = lax.broadcasted_iota(jnp.int32, (n_sel, tq), 0)
    cur_t = (t0 + lax.broadcasted_iota(jnp.int32, (n_sel, tq), 1)) >> 6
    eligible = blk_t <= cur_t
    forced = (blk_t == 0) | (blk_t == cur_t) | (blk_t == cur_t - 1)
    score = jnp.where(eligible, jnp.where(forced, NSA_FORCE_SCORE, imp_t), NEG_INF)
    rank = _rank_rows(score, n_sel)
    bias_t = jnp.where(eligible & (rank < NSA_SEL_TOPK), 0.0, MASKED)
    bias_t = jnp.concatenate([bias_t, jnp.zeros((LANES - n_sel, tq), F32)], axis=0)
    bias = _rows_to_lanes(bias_t, tq)
    q_aug = jnp.concatenate([qs, jnp.concatenate([bias] * heads, axis=0)], axis=1).astype(BF16)

    assert tq == NSA_WINDOW
    local_r = lax.broadcasted_iota(jnp.int32, (tq, tq), 0)
    local_c = lax.broadcasted_iota(jnp.int32, (tq, tq), 1)
    causal = local_c <= local_r
    rc = acc_sc.shape[1]
    n_chain = heads * tq // rc

    def run(q_rows, k_sc, v_sc, kb, carry, mask, lower):
        def keys(c):
            r0 = (c * rc) % tq
            k0, nk = (0, tq) if mask is None else ((0, r0 + rc) if lower else (r0, tq - r0))
            return pl.ds(pl.multiple_of(kb * tq + k0, rc), nk), r0, k0, nk

        def scores(c):
            r, r0, k0, nk = keys(c)
            s = _dot_nt(q_rows[c * rc:(c + 1) * rc], k_sc[r, :])
            return s if mask is None else jnp.where(mask[r0:r0 + rc, k0:k0 + nk], s, MASKED)

        out, s_next = [], scores(0)
        for c in range(n_chain):
            s, s_next = s_next, (scores(c + 1) if c + 1 < n_chain else None)
            m_new, acc_sc[c] = _flash_update(s, carry[c], acc_sc[c], v_sc[keys(c)[0], :])
            out.append(m_new)
        return tuple(out)

    def finish():
        acc = jnp.concatenate([acc_sc[c] for c in range(n_chain)], axis=0)
        return (acc[:, :HEAD_DIM] / acc[:, HEAD_DIM:]).reshape(heads, tq, HEAD_DIM)

    init = tuple(jnp.full((rc, 1), NEG_INF, F32) for _ in range(n_chain))

    acc_sc[...] = jnp.zeros(acc_sc.shape, F32)
    carry = lax.fori_loop(0, qi, lambda kb, cr: run(q_aug, ks_sc, vs_sc, kb, cr, None, True), init)
    run(q_aug, ks_sc, vs_sc, qi, carry, causal, True)
    o_sel = finish()

    acc_sc[...] = jnp.zeros(acc_sc.shape, F32)
    carry = run(qs_bf, kw_sc, vw_sc, jnp.maximum(qi - 1, 0), init, (local_c > local_r) & (qi >= 1), False)
    run(qs_bf, kw_sc, vw_sc, qi, carry, causal, True)
    o_win = finish()

    g = jax.nn.sigmoid(ng_ref[...].astype(F32))
    for h in range(heads):
        o = (g[:, 3 * h:3 * h + 1] * o_cmp[h] + g[:, 3 * h + 1:3 * h + 2] * o_sel[h]
             + g[:, 3 * h + 2:3 * h + 3] * o_win[h])
        o_ref[:, h * LANES:(h + 1) * LANES] = o.astype(o_ref.dtype)


def _nsa(proj, w1, w2, pos, tabs, batch, seq, tq=NSA_WINDOW, rc=256):
    nt = seq // tq
    n_cmp = (seq - NSA_CMP_LEN) // NSA_CMP_STRIDE + 1
    n_sel = seq // NSA_SEL_LEN
    starts = np.arange(LANES) * NSA_CMP_STRIDE
    sel_start = np.arange(LANES) * NSA_SEL_LEN
    overlap = ((starts[:, None] < sel_start[None, :] + NSA_SEL_LEN)
               & (starts[:, None] + NSA_CMP_LEN > sel_start[None, :])
               & (np.arange(LANES)[:, None] < n_cmp) & (np.arange(LANES)[None, :] < n_sel))
    ovt = jnp.asarray(overlap.T.astype(np.float32), BF16)
    seq_col = lambda name: pl.BlockSpec((seq, LANES), lambda b, i: (b, COL128[name]))
    full = lambda a: pl.BlockSpec(a.shape, lambda b, i: (0,) * a.ndim)
    return pl.pallas_call(
        functools.partial(_nsa_kernel, tq=tq, seq=seq, scale=HEAD_DIM ** -0.5),
        grid=(batch, nt),
        in_specs=[pl.BlockSpec((tq, 512), lambda b, i: (b * nt + i, COL512["nq"])),
                  seq_col("nkc"), seq_col("nvc"), seq_col("nks"), seq_col("nvs"), seq_col("nkw"), seq_col("nvw"),
                  pl.BlockSpec((tq, LANES), lambda b, i: (b * nt + i, COL128["ng"])),
                  full(w1), full(w2), full(pos), full(ovt), full(tabs[0]), full(tabs[1]), full(tabs[2])],
        out_specs=pl.BlockSpec((tq, NSA_HEADS * HEAD_DIM), lambda b, i: (b * nt + i, 0)),
        out_shape=jax.ShapeDtypeStruct((batch * seq, NSA_HEADS * HEAD_DIM), BF16),
        scratch_shapes=[pltpu.VMEM((seq, LANES), F32),
                        pltpu.VMEM((LANES, LANES), BF16), pltpu.VMEM((LANES, LANES), BF16),
                        pltpu.VMEM((seq, 2 * LANES), BF16), pltpu.VMEM((seq, 2 * LANES), BF16),
                        pltpu.VMEM((seq, LANES), BF16), pltpu.VMEM((seq, 2 * LANES), BF16),
                        pltpu.VMEM((NSA_HEADS * tq // rc, rc, 2 * HEAD_DIM), F32)],
        compiler_params=_cparams("parallel", "arbitrary"),
        name="nsa",
    )(proj, proj, proj, proj, proj, proj, proj, proj, w1, w2, pos, ovt, *tabs)


def _moba_kernel(mq_ref, mk_ref, mv_ref, tc_ref, ts1_ref, ts2_ref, o_ref, k_sc, v_sc, km_sc, acc_sc,
                 *, tq, seq, scale):
    qi = pl.program_id(1)
    half = PARTIAL_ROT // 2
    heads = MOBA_HEADS
    n_blk = seq // MOBA_BLOCK
    slot = 2 * LANES

    @pl.when(qi == 0)
    def _prep():
        c, s1, s2 = tc_ref[...], ts1_ref[...], ts2_ref[...]
        km_sc[...] = jnp.zeros_like(km_sc)
        ones = jnp.ones((seq, LANES), BF16)
        blk = lax.broadcasted_iota(jnp.int32, (seq, LANES), 0) >> 8
        onehot = jnp.where(blk == lax.broadcasted_iota(jnp.int32, (seq, LANES), 1), 1.0, 0.0).astype(BF16)
        for h in range(heads):
            hs = slice(h * LANES, (h + 1) * LANES)
            kh = _rope128(mk_ref[:, hs].astype(F32), c, s1, s2, half)
            k_sc[:, h * slot:h * slot + LANES] = kh.astype(BF16)
            k_sc[:, h * slot + LANES:(h + 1) * slot] = onehot
            v_sc[:, h * slot:h * slot + LANES] = mv_ref[:, hs].astype(BF16)
            v_sc[:, h * slot + LANES:(h + 1) * slot] = ones
            for j in range(n_blk):
                km_sc[h * n_blk + j:h * n_blk + j + 1, :] = jnp.mean(
                    kh[j * MOBA_BLOCK:(j + 1) * MOBA_BLOCK], axis=0, keepdims=True)

    t0 = pl.multiple_of(qi * tq, tq)
    rq = pl.ds(t0, tq)
    c, s1, s2 = tc_ref[rq, :], ts1_ref[rq, :], ts2_ref[rq, :]
    km_hi, km_lo = _split_bf16(km_sc[...])
    tk = tq
    rc = MOBA_BLOCK
    sub = tq // rc
    blk_t = lax.broadcasted_iota(jnp.int32, (n_blk, tq), 0)
    cur_t = (t0 + lax.broadcasted_iota(jnp.int32, (n_blk, tq), 1)) >> 8
    row = lax.broadcasted_iota(jnp.int32, (tq, tk), 0)
    col = lax.broadcasted_iota(jnp.int32, (tq, tk), 1)

    q_aug = []
    for h in range(heads):
        qf = _rope128(mq_ref[:, h * LANES:(h + 1) * LANES].astype(F32), c, s1, s2, half) * scale
        q_hi, q_lo = _split_bf16(qf)
        gate_t = (_dot_nt(km_hi, q_hi) + (_dot_nt(km_lo, q_hi) + _dot_nt(km_hi, q_lo)))[h * n_blk:(h + 1) * n_blk]
        eligible = blk_t < cur_t
        score = jnp.where(eligible, gate_t, NEG_INF)
        picked = eligible & (_rank_rows(score, n_blk) < MOBA_TOPK)
        bias_t = jnp.where(picked | (blk_t == cur_t), 0.0, MASKED)
        bias_t = jnp.concatenate([bias_t, jnp.zeros((LANES - n_blk, tq), F32)], axis=0)
        q_aug.append(jnp.concatenate([qf, _rows_to_lanes(bias_t, tq)], axis=1).astype(BF16))

    n_chain = heads * sub

    causal = col <= row

    def step(kb, carry, diag):
        def keys(c):
            nk = (c % sub + 1) * rc if diag else tk
            return pl.ds(pl.multiple_of(kb * tk, tk), nk), nk

        def scores(c):
            h, r0 = c // sub, (c % sub) * rc
            r, nk = keys(c)
            s = _dot_nt(q_aug[h][r0:r0 + rc], k_sc[r, h * slot:(h + 1) * slot])
            return jnp.where(causal[r0:r0 + rc, :nk], s, MASKED) if diag else s

        out, s_next = [], scores(0)
        for c in range(n_chain):
            s, s_next = s_next, (scores(c + 1) if c + 1 < n_chain else None)
            h = c // sub
            m_new, acc_sc[c] = _flash_update(s, carry[c], acc_sc[c], v_sc[keys(c)[0], h * slot:(h + 1) * slot])
            out.append(m_new)
        return tuple(out)

    acc_sc[...] = jnp.zeros(acc_sc.shape, F32)
    carry = tuple(jnp.full((rc, 1), NEG_INF, F32) for _ in range(n_chain))
    carry = lax.fori_loop(0, qi, lambda kb, cr: step(kb, cr, False), carry)
    step(qi, carry, True)
    for c in range(n_chain):
        h, r0 = c // sub, (c % sub) * rc
        acc = acc_sc[c]
        o_ref[r0:r0 + rc, h * LANES:(h + 1) * LANES] = (acc[:, :HEAD_DIM] / acc[:, HEAD_DIM:]).astype(o_ref.dtype)


def _moba(proj, tabs, batch, seq):
    tq = 2 * MOBA_BLOCK
    nt = seq // tq
    width = MOBA_HEADS * HEAD_DIM
    full = lambda a: pl.BlockSpec(a.shape, lambda b, i: (0,) * a.ndim)
    return pl.pallas_call(
        functools.partial(_moba_kernel, tq=tq, seq=seq, scale=HEAD_DIM ** -0.5),
        grid=(batch, nt),
        in_specs=[pl.BlockSpec((tq, width), lambda b, i: (b * nt + i, COL512["mq"])),
                  pl.BlockSpec((seq, width), lambda b, i: (b, COL512["mk"])),
                  pl.BlockSpec((seq, width), lambda b, i: (b, COL512["mv"])),
                  full(tabs[0]), full(tabs[1]), full(tabs[2])],
        out_specs=pl.BlockSpec((tq, width), lambda b, i: (b * nt + i, 0)),
        out_shape=jax.ShapeDtypeStruct((batch * seq, width), BF16),
        scratch_shapes=[pltpu.VMEM((seq, 2 * width), BF16), pltpu.VMEM((seq, 2 * width), BF16),
                        pltpu.VMEM((LANES, LANES), F32),
                        pltpu.VMEM((MOBA_HEADS * tq // MOBA_BLOCK, MOBA_BLOCK, 2 * HEAD_DIM), F32)],
        compiler_params=_cparams("parallel", "arbitrary"),
        name="moba",
    )(proj, proj, proj, *tabs)


def _xattn_kernel(q_ref, kv_ref, o_ref):
    d = q_ref.shape[1]
    hd = d // MEM_HEADS
    for h in range(MEM_HEADS):
        hs = slice(h * hd, (h + 1) * hd)
        s = _dot_nt(q_ref[:, hs], kv_ref[0, :, hs])
        e = jnp.exp(s - jnp.max(s, axis=-1, keepdims=True))
        p = e / jnp.sum(e, axis=-1, keepdims=True)
        o_ref[:, hs] = _dot(p.astype(BF16), kv_ref[0, :, d + h * hd:d + (h + 1) * hd]).astype(o_ref.dtype)


def _xattn(q, kv, batch, seq, tq=512):
    d = q.shape[1]
    nt = seq // tq
    m_len = kv.shape[0] // batch
    return pl.pallas_call(
        _xattn_kernel,
        grid=(batch, nt),
        in_specs=[pl.BlockSpec((tq, d), lambda b, i: (b * nt + i, 0)),
                  pl.BlockSpec((1, m_len, 2 * d), lambda b, i: (b, 0, 0))],
        out_specs=pl.BlockSpec((tq, d), lambda b, i: (b * nt + i, 0)),
        out_shape=jax.ShapeDtypeStruct((batch * seq, d), BF16),
        compiler_params=_cparams("parallel", "parallel"),
        name="xattn",
    )(q, kv.reshape(batch, m_len, 2 * d))


IN_SPLIT_NAMES = ("c_q", "c_kv", "k_rope", "nq", "nkc", "nvc", "nks", "nvs", "nkw", "nvw", "ng", "mq", "mk", "mv")
IN_SPLIT_SIZES = (512, 512, 64, 512, 128, 128, 128, 128, 128, 128, 12, 512, 512, 512)


def _pack_w_in_kernel(w_ref, o_ref):
    rows = w_ref.shape[0]
    off = 0
    for name, sz in zip(IN_SPLIT_NAMES, IN_SPLIT_SIZES):
        width = 512 if name in COL512 else LANES
        dst = COL512[name] * 512 if name in COL512 else COL128[name] * LANES
        x = w_ref[:, off:off + sz].astype(BF16)
        if sz < width:
            x = jnp.concatenate([x, jnp.zeros((rows, width - sz), BF16)], axis=1)
        o_ref[0, :, dst:dst + width] = x
        off += sz


def _pack_w_in(w, tr=256):
    layers, d, width = w.shape
    return pl.pallas_call(
        _pack_w_in_kernel,
        grid=(layers, d // tr),
        in_specs=[pl.BlockSpec((None, tr, width), lambda l, i: (l, i, 0))],
        out_specs=pl.BlockSpec((1, tr, PROJ_WIDTH), lambda l, i: (l, i, 0)),
        out_shape=jax.ShapeDtypeStruct((layers, d, PROJ_WIDTH), BF16),
        compiler_params=_cparams("parallel", "parallel"),
        name="pack_w_in",
    )(w)


def _pack_w_uq(w):
    r = w.shape[0]
    w = w.reshape(r, MLA_HEADS, MLA_NOPE + MLA_ROPE)
    w = jnp.pad(w, ((0, 0), (0, 0), (0, MLA_SLOT - MLA_NOPE - MLA_ROPE)))
    return w.reshape(r, MLA_HEADS * MLA_SLOT).astype(BF16)


def _pack_w_ukv(w):
    r = w.shape[0]
    w = w.reshape(r, MLA_HEADS, 2, HEAD_DIM).transpose(0, 2, 1, 3)
    return w.reshape(r, 2 * MLA_HEADS * HEAD_DIM).astype(BF16)


def kernel(x, mem, ln_in_g, ln_in_b, w_in, mla_q_norm, mla_kv_norm, mla_w_uq, mla_w_ukv, nsa_cmp_w1, nsa_cmp_w2, nsa_cmp_pos, w_out, ln1_g, ln1_b, mem_wq, mem_wkv, mem_wo, ln2_g, ln2_b, mlp_w1, mlp_w2, ln3_g, ln3_b):
    batch, seq, d = x.shape
    n = batch * seq
    mla_tabs = _rope_tables(seq, MLA_ROPE)
    rot_tabs = _rope_tables(seq, PARTIAL_ROT)
    mem2 = mem.reshape(batch * mem.shape[1], d)
    w_in_packed = _pack_w_in(w_in)
    w_out_bf, mem_wkv_bf, mem_wo_bf = w_out.astype(BF16), mem_wkv.astype(BF16), mem_wo.astype(BF16)
    mem_wq_bf = (mem_wq * (d // MEM_HEADS) ** -0.5).astype(BF16)
    mlp_w1_bf, mlp_w2_bf = mlp_w1.astype(BF16), mlp_w2.astype(BF16)

    h = _layer_norm(x.reshape(n, d), ln_in_g, ln_in_b)
    for l in range(DEPTH):
        proj = _matmul(h, w_in_packed, l, BF16, name="in_proj")
        q, k, v = _mla_up(proj, mla_q_norm[l], mla_kv_norm[l], _pack_w_uq(mla_w_uq[l]), _pack_w_ukv(mla_w_ukv[l]),
                          mla_tabs, batch, seq)
        o_a = _mla_attn(q, k, v).reshape(n, MLA_HEADS * HEAD_DIM)
        o_b = _nsa(proj, nsa_cmp_w1[l].astype(BF16), nsa_cmp_w2[l].astype(BF16), nsa_cmp_pos[l], rot_tabs, batch, seq)
        o_c = _moba(proj, rot_tabs, batch, seq)
        h = _out_ln([o_a, o_b, o_c], w_out_bf, l, h, ln1_g[l], ln1_b[l], name="mix_out_ln")

        xq = _matmul(h, mem_wq_bf, l, BF16, name="mem_q")
        xkv = _matmul(mem2, mem_wkv_bf, l, BF16, name="mem_kv")
        ctx = _xattn(xq, xkv, batch, seq)
        h = _out_ln([ctx], mem_wo_bf, l, h, ln2_g[l], ln2_b[l], name="mem_out_ln")

        h = _mlp_ln(h, mlp_w1_bf, mlp_w2_bf, l, ln3_g[l], ln3_b[l])
    return h.reshape(batch, seq, d)
```

```python
import functools

import numpy as np
import jax
import jax.numpy as jnp
from jax import lax
from jax.experimental import pallas as pl
from jax.experimental.pallas import tpu as pltpu

F32 = jnp.float32
BF16 = jnp.bfloat16
NEG_INF = float("-inf")

D_MODEL = 2048
DEPTH = 2
HEAD_DIM = 128
MLA_HEADS = 8
NSA_HEADS = 4
MOBA_HEADS = 4
ROPE_THETA = 500000.0
PARTIAL_ROT = HEAD_DIM // 4
MLA_Q_RANK = 512
MLA_KV_RANK = 512
MLA_NOPE = 128
MLA_ROPE = 64
MLA_SLOT = 256
NSA_CMP_LEN = 32
NSA_CMP_STRIDE = 16
NSA_SEL_LEN = 64
NSA_SEL_TOPK = 16
NSA_WINDOW = 512
NSA_FORCE_SCORE = 1.0e4
MOBA_BLOCK = 256
MOBA_TOPK = 3
MEM_HEADS = 4
DEEPNORM_ALPHA = (2 * DEPTH) ** 0.25
LANES = 128
NSA_SEL_SHIFT = NSA_SEL_LEN.bit_length() - 1
MOBA_SHIFT = MOBA_BLOCK.bit_length() - 1

PROJ_WIDTH = 4096
WIDE = 4 * HEAD_DIM
COL512 = dict(c_q=0, c_kv=1, nq=2, mq=3, mk=4, mv=5)
COL128 = dict(k_rope=24, nkc=25, nvc=26, nks=27, nvs=28, nkw=29, nvw=30, ng=31)

VMEM_LIMIT = 56 * 1024 * 1024
ROW_TILE = 512
CHAIN_ROWS = 256


def _cparams(*sem):
    return pltpu.CompilerParams(dimension_semantics=sem, vmem_limit_bytes=VMEM_LIMIT)


def _dot(a, b):
    return jnp.dot(a, b, preferred_element_type=F32)


def _dot_nt(a, b):
    return lax.dot_general(a, b, (((1,), (1,)), ((), ())), preferred_element_type=F32)


def _split_bf16(x):
    hi = x.astype(BF16)
    return hi, (x - hi.astype(F32)).astype(BF16)


def _ln_rows(x, g, b, eps=1e-5):
    mu = jnp.mean(x, axis=-1, keepdims=True)
    xc = x - mu
    var = jnp.mean(xc * xc, axis=-1, keepdims=True)
    return xc * lax.rsqrt(var + eps) * g + b


def _rms_rows(x, g, eps=1e-6):
    return x * lax.rsqrt(jnp.mean(x * x, axis=-1, keepdims=True) + eps) * g


def _rope128(x, c, s1, s2, half):
    return x * c + pltpu.roll(x, LANES - half, 1) * s1 + pltpu.roll(x, half, 1) * s2


def _rope_tables(n_pos, dim):
    half = dim // 2
    inv = ROPE_THETA ** (-jnp.arange(0, dim, 2, dtype=F32) / dim)
    ang = jnp.arange(n_pos, dtype=F32)[:, None] * inv[None, :]
    cos, sin = jnp.cos(ang), jnp.sin(ang)
    ones = jnp.ones((n_pos, LANES - dim), F32)
    z = lambda w: jnp.zeros((n_pos, w), F32)
    c = jnp.concatenate([cos, cos, ones], axis=1)
    s1 = jnp.concatenate([-sin, z(LANES - half)], axis=1)
    s2 = jnp.concatenate([z(half), sin, z(LANES - dim)], axis=1)
    return c, s1, s2


MASKED = -1.0e30


def _rank_rows(score, n_cand):
    row = lax.broadcasted_iota(jnp.int32, score.shape, 0)
    rank = jnp.zeros(score.shape, F32)
    for jp in range(n_cand):
        cand = score[jp:jp + 1, :]
        ahead = (cand > score) | ((cand == score) & (row > jp))
        rank = rank + jnp.where(ahead, 1.0, 0.0)
    return rank


def _rows_to_lanes(x_t, tq):
    eye = jnp.where(lax.broadcasted_iota(jnp.int32, (tq, tq), 0) == lax.broadcasted_iota(jnp.int32, (tq, tq), 1),
                    1.0, 0.0).astype(BF16)
    return _dot_nt(eye, x_t.astype(BF16))


def _flash_update(s, m, acc, v_aug):
    m_new = jnp.maximum(m, jnp.max(s, axis=-1, keepdims=True))
    p = jnp.exp(s - m_new).astype(BF16)
    return m_new, jnp.exp(m - m_new) * acc + _dot(p, v_aug)


def _ln_kernel(x_ref, g_ref, b_ref, o_ref):
    o_ref[...] = _ln_rows(x_ref[...], g_ref[...], b_ref[...])


def _layer_norm(x, g, b, tm=ROW_TILE):
    n, d = x.shape
    return pl.pallas_call(
        _ln_kernel,
        grid=(n // tm,),
        in_specs=[pl.BlockSpec((tm, d), lambda i: (i, 0)),
                  pl.BlockSpec((1, d), lambda i: (0, 0)),
                  pl.BlockSpec((1, d), lambda i: (0, 0))],
        out_specs=pl.BlockSpec((tm, d), lambda i: (i, 0)),
        out_shape=jax.ShapeDtypeStruct((n, d), F32),
        compiler_params=_cparams("parallel"),
        name="ln_in",
    )(x, g.reshape(1, d), b.reshape(1, d))


def _mm_kernel(a_ref, w_ref, o_ref, abf_ref):
    @pl.when(pl.program_id(1) == 0)
    def _():
        abf_ref[...] = a_ref[...].astype(BF16)

    o_ref[...] = _dot(abf_ref[...], w_ref[...]).astype(o_ref.dtype)


def _matmul(a, w, layer, out_dtype, tm=1024, tn=2048, name="mm"):
    m, k = a.shape
    n = w.shape[2]
    tm = min(tm, m)
    return pl.pallas_call(
        _mm_kernel,
        grid=(m // tm, n // tn),
        in_specs=[pl.BlockSpec((tm, k), lambda i, j: (i, 0)),
                  pl.BlockSpec((None, k, tn), lambda i, j: (layer, 0, j))],
        out_specs=pl.BlockSpec((tm, tn), lambda i, j: (i, j)),
        out_shape=jax.ShapeDtypeStruct((m, n), out_dtype),
        scratch_shapes=[pltpu.VMEM((tm, k), BF16)],
        compiler_params=_cparams("parallel", "arbitrary"),
        name=name,
    )(a, w)


def _out_ln_kernel(*refs, widths):
    n_a = len(widths)
    a_refs = refs[:n_a]
    w_ref, h_ref, g_ref, b_ref, o_ref = refs[n_a:]
    tm = h_ref.shape[0]
    halves = (slice(0, tm // 2), slice(tm // 2, tm))
    accs = []
    for rows in halves:
        a = jnp.concatenate([a_ref[rows, :] for a_ref in a_refs], axis=1)
        accs.append(DEEPNORM_ALPHA * h_ref[rows, :] + _dot(a, w_ref[...]))
    for rows, acc in zip(halves, accs):
        o_ref[rows, :] = _ln_rows(acc, g_ref[...], b_ref[...])


def _out_ln(a_list, w, layer, h, g, b, tm=ROW_TILE, name="out_ln"):
    n, d = h.shape
    widths = tuple(a.shape[1] for a in a_list)
    k = sum(widths)
    in_specs = [pl.BlockSpec((tm, wd), lambda i: (i, 0)) for wd in widths]
    in_specs += [pl.BlockSpec((None, k, d), lambda i: (layer, 0, 0)),
                 pl.BlockSpec((tm, d), lambda i: (i, 0)),
                 pl.BlockSpec((1, d), lambda i: (0, 0)),
                 pl.BlockSpec((1, d), lambda i: (0, 0))]
    return pl.pallas_call(
        functools.partial(_out_ln_kernel, widths=widths),
        grid=(n // tm,),
        in_specs=in_specs,
        out_specs=pl.BlockSpec((tm, d), lambda i: (i, 0)),
        out_shape=jax.ShapeDtypeStruct((n, d), F32),
        compiler_params=_cparams("parallel"),
        name=name,
    )(*a_list, w, h, g.reshape(1, d), b.reshape(1, d))


def _mlp_kernel(h_ref, w1_ref, w2_ref, g_ref, b_ref, o_ref, hbf_ref, acc_ref):
    f = pl.program_id(1)

    @pl.when(f == 0)
    def _():
        hbf_ref[...] = h_ref[...].astype(BF16)
        acc_ref[...] = jnp.zeros_like(acc_ref)

    u = jnp.maximum(_dot(hbf_ref[...], w1_ref[...]), 0.0)
    acc_ref[...] += _dot((u * u).astype(BF16), w2_ref[...])

    @pl.when(f == pl.num_programs(1) - 1)
    def _():
        y = DEEPNORM_ALPHA * h_ref[...] + acc_ref[...]
        o_ref[...] = _ln_rows(y, g_ref[...], b_ref[...])


def _mlp_ln(h, w1, w2, layer, g, b, tm=ROW_TILE, tf=1024):
    n, d = h.shape
    dff = w1.shape[2]
    return pl.pallas_call(
        _mlp_kernel,
        grid=(n // tm, dff // tf),
        in_specs=[pl.BlockSpec((tm, d), lambda i, f: (i, 0)),
                  pl.BlockSpec((None, d, tf), lambda i, f: (layer, 0, f)),
                  pl.BlockSpec((None, tf, d), lambda i, f: (layer, f, 0)),
                  pl.BlockSpec((1, d), lambda i, f: (0, 0)),
                  pl.BlockSpec((1, d), lambda i, f: (0, 0))],
        out_specs=pl.BlockSpec((tm, d), lambda i, f: (i, 0)),
        out_shape=jax.ShapeDtypeStruct((n, d), F32),
        scratch_shapes=[pltpu.VMEM((tm, d), BF16), pltpu.VMEM((tm, d), F32)],
        compiler_params=_cparams("parallel", "arbitrary"),
        name="mlp_ln",
    )(h, w1, w2, g.reshape(1, d), b.reshape(1, d))


def _mla_up_kernel(cq_ref, ckv_ref, kr_ref, gq_ref, gkv_ref, wq_ref, wkv_ref, tc_ref, ts1_ref, ts2_ref,
                   q_ref, k_ref, v_ref, *, scale):
    half = MLA_ROPE // 2
    c, s1, s2 = tc_ref[...], ts1_ref[...], ts2_ref[...]
    nq = _rms_rows(cq_ref[...].astype(F32), gq_ref[...]).astype(BF16)
    nkv = _rms_rows(ckv_ref[...].astype(F32), gkv_ref[...]).astype(BF16)
    qf = _dot(nq, wq_ref[...])
    kvf = _dot(nkv, wkv_ref[...])
    kr = _rope128(kr_ref[...].astype(F32), c, s1, s2, half).astype(BF16)
    for h in range(MLA_HEADS):
        o = h * MLA_SLOT
        q_ref[0, h, :, 0:LANES] = (qf[:, o:o + LANES] * scale).astype(BF16)
        q_ref[0, h, :, LANES:] = (_rope128(qf[:, o + LANES:o + MLA_SLOT], c, s1, s2, half) * scale).astype(BF16)
        k_ref[0, h, :, 0:LANES] = kvf[:, h * LANES:(h + 1) * LANES].astype(BF16)
        k_ref[0, h, :, LANES:] = kr
        v_ref[0, h, :, 0:LANES] = kvf[:, (MLA_HEADS + h) * LANES:(MLA_HEADS + h + 1) * LANES].astype(BF16)
        v_ref[0, h, :, LANES:] = jnp.ones((kr.shape[0], LANES), BF16)


def _mla_up(proj, gq, gkv, wq, wkv, tabs, batch, seq, tm=ROW_TILE):
    nt = seq // tm
    row = lambda b, i: b * nt + i
    tab_spec = pl.BlockSpec((tm, LANES), lambda b, i: (i, 0))
    hm = lambda w: pl.BlockSpec((1, MLA_HEADS, tm, w), lambda b, i: (b, 0, i, 0))
    return pl.pallas_call(
        functools.partial(_mla_up_kernel, scale=(MLA_NOPE + MLA_ROPE) ** -0.5),
        grid=(batch, nt),
        in_specs=[pl.BlockSpec((tm, WIDE), lambda b, i: (row(b, i), COL512["c_q"])),
                  pl.BlockSpec((tm, WIDE), lambda b, i: (row(b, i), COL512["c_kv"])),
                  pl.BlockSpec((tm, LANES), lambda b, i: (row(b, i), COL128["k_rope"])),
                  pl.BlockSpec((1, MLA_Q_RANK), lambda b, i: (0, 0)),
                  pl.BlockSpec((1, MLA_KV_RANK), lambda b, i: (0, 0)),
                  pl.BlockSpec(wq.shape, lambda b, i: (0, 0)),
                  pl.BlockSpec(wkv.shape, lambda b, i: (0, 0)),
                  tab_spec, tab_spec, tab_spec],
        out_specs=[hm(MLA_SLOT), hm(MLA_SLOT), hm(2 * HEAD_DIM)],
        out_shape=[jax.ShapeDtypeStruct((batch, MLA_HEADS, seq, MLA_SLOT), BF16),
                   jax.ShapeDtypeStruct((batch, MLA_HEADS, seq, MLA_SLOT), BF16),
                   jax.ShapeDtypeStruct((batch, MLA_HEADS, seq, 2 * HEAD_DIM), BF16)],
        compiler_params=_cparams("parallel", "parallel"),
        name="mla_up",
    )(proj, proj, proj, gq.reshape(1, -1), gkv.reshape(1, -1), wq, wkv, *tabs)


def _mla_attn_kernel(q_ref, k_ref, v_ref, o_ref, acc_sc, *, tq, rs):
    qi = pl.program_id(2)
    heads = q_ref.shape[1]
    causal = lax.broadcasted_iota(jnp.int32, (tq, tq), 1) <= lax.broadcasted_iota(jnp.int32, (tq, tq), 0)
    nrs = tq // rs
    n_chain = heads * nrs

    def step(kb, carry, diag):
        def keys(c):
            nk = (c % nrs + 1) * rs if diag else tq
            return pl.ds(pl.multiple_of(kb * tq, tq), nk), nk

        def scores(c):
            h, r0 = c // nrs, (c % nrs) * rs
            r, nk = keys(c)
            s = _dot_nt(q_ref[0, h, r0:r0 + rs, :], k_ref[0, h, r, :])
            return jnp.where(causal[r0:r0 + rs, :nk], s, MASKED) if diag else s

        out, s_next = [], scores(0)
        for c in range(n_chain):
            s, s_next = s_next, (scores(c + 1) if c + 1 < n_chain else None)
            m_new, acc_sc[c] = _flash_update(s, carry[c], acc_sc[c], v_ref[0, c // nrs, keys(c)[0], :])
            out.append(m_new)
        return tuple(out)

    acc_sc[...] = jnp.zeros(acc_sc.shape, F32)
    carry = tuple(jnp.full((rs, 1), NEG_INF, F32) for _ in range(n_chain))
    carry = lax.fori_loop(0, qi, lambda kb, c: step(kb, c, False), carry)
    step(qi, carry, True)
    for h in range(heads):
        for i in range(nrs):
            acc = acc_sc[h * nrs + i]
            o_ref[0, i * rs:(i + 1) * rs, h * HEAD_DIM:(h + 1) * HEAD_DIM] = (
                acc[:, :HEAD_DIM] / acc[:, HEAD_DIM:]).astype(o_ref.dtype)


def _mla_attn(q, k, v, tq=ROW_TILE, rs=CHAIN_ROWS, hg=MLA_HEADS // 2):
    batch, heads, seq, _ = q.shape
    return pl.pallas_call(
        functools.partial(_mla_attn_kernel, tq=tq, rs=rs),
        grid=(batch, heads // hg, seq // tq),
        in_specs=[pl.BlockSpec((1, hg, tq, MLA_SLOT), lambda b, g, i: (b, g, i, 0)),
                  pl.BlockSpec((1, hg, seq, MLA_SLOT), lambda b, g, i: (b, g, 0, 0)),
                  pl.BlockSpec((1, hg, seq, 2 * HEAD_DIM), lambda b, g, i: (b, g, 0, 0))],
        out_specs=pl.BlockSpec((1, tq, hg * HEAD_DIM), lambda b, g, i: (b, i, g)),
        out_shape=jax.ShapeDtypeStruct((batch, seq, heads * HEAD_DIM), BF16),
        scratch_shapes=[pltpu.VMEM((hg * tq // rs, rs, 2 * HEAD_DIM), F32)],
        compiler_params=_cparams("parallel", "parallel", "arbitrary"),
        name="mla_attn",
    )(q, k, v)


def _nsa_kernel(nq_ref, kc_ref, vc_ref, ks_ref, vs_ref, kw_ref, vw_ref, ng_ref,
                w1_ref, w2_ref, pos_ref, ovt_ref, tc_ref, ts1_ref, ts2_ref,
                o_ref,
                tmp_sc, kc_sc, vc_sc, ks_sc, vs_sc, kw_sc, vw_sc, acc_sc, *, tq, seq, scale):
    qi = pl.program_id(1)
    half = PARTIAL_ROT // 2
    heads = NSA_HEADS
    n_chunk = seq // NSA_CMP_STRIDE
    n_sel = seq // NSA_SEL_LEN

    @pl.when(qi == 0)
    def _prep():
        c, s1, s2 = tc_ref[...], ts1_ref[...], ts2_ref[...]
        ones = jnp.ones((seq, LANES), BF16)
        blk = lax.broadcasted_iota(jnp.int32, (seq, LANES), 0) >> NSA_SEL_SHIFT
        ks_sc[:, 0:LANES] = _rope128(ks_ref[...].astype(F32), c, s1, s2, half).astype(BF16)
        ks_sc[:, LANES:] = jnp.where(blk == lax.broadcasted_iota(jnp.int32, (seq, LANES), 1), 1.0, 0.0).astype(BF16)
        kw_sc[...] = _rope128(kw_ref[...].astype(F32), c, s1, s2, half).astype(BF16)
        vs_sc[:, 0:LANES] = vs_ref[...].astype(BF16)
        vs_sc[:, LANES:] = ones
        vw_sc[:, 0:LANES] = vw_ref[...].astype(BF16)
        vw_sc[:, LANES:] = ones
        for i, (src, dst) in enumerate(((kc_ref, kc_sc), (vc_ref, vc_sc))):
            x_all = src[...].astype(F32)
            tmp_sc[...] = _rope128(x_all, c, s1, s2, half) if i == 0 else x_all
            lo, hi = [], []
            for t in range(NSA_CMP_STRIDE):
                x = tmp_sc[pl.ds(t, n_chunk, stride=NSA_CMP_STRIDE), :]
                lo.append((x + pos_ref[i, t:t + 1, :]).astype(BF16))
                hi.append((x + pos_ref[i, NSA_CMP_STRIDE + t:NSA_CMP_STRIDE + t + 1, :]).astype(BF16))
            kw1 = NSA_CMP_STRIDE * HEAD_DIM
            a = _dot(jnp.concatenate(lo, axis=1), w1_ref[i, 0:kw1, :])
            bm = _dot(jnp.concatenate(hi, axis=1), w1_ref[i, kw1:2 * kw1, :])
            hid = jax.nn.gelu(a + pltpu.roll(bm, n_chunk - 1, 0))
            dst[...] = _dot(hid.astype(BF16), w2_ref[i]).astype(BF16)

    t0 = pl.multiple_of(qi * tq, tq)
    rq = pl.ds(t0, tq)
    c, s1, s2 = tc_ref[rq, :], ts1_ref[rq, :], ts2_ref[rq, :]
    qf = nq_ref[...].astype(F32)
    qs = jnp.concatenate(
        [_rope128(qf[:, h * LANES:(h + 1) * LANES], c, s1, s2, half) * scale for h in range(heads)],
        axis=0)
    qs_bf = qs.astype(BF16)

    lane = lax.broadcasted_iota(jnp.int32, (tq, LANES), 1)
    tpos = t0 + lax.broadcasted_iota(jnp.int32, (tq, LANES), 0)

    s = _dot_nt(qs_bf, kc_sc[...]).reshape(heads, tq, LANES)
    cmask = (lane * NSA_CMP_STRIDE + (NSA_CMP_LEN - 1) <= tpos)[None]
    s = jnp.where(cmask, s, NEG_INF)
    m = jnp.max(s, axis=-1, keepdims=True)
    m = jnp.where(m == NEG_INF, 0.0, m)
    e = jnp.exp(s - m)
    p_cmp = e / jnp.maximum(jnp.sum(e, axis=-1, keepdims=True), 1e-30)
    o_cmp = _dot(p_cmp.reshape(heads * tq, LANES).astype(BF16), vc_sc[...]).reshape(heads, tq, HEAD_DIM)

    p_hi, p_lo = _split_bf16(p_cmp[0] + p_cmp[1] + p_cmp[2] + p_cmp[3])
    imp_t = (_dot_nt(ovt_ref[...], p_hi) + _dot_nt(ovt_ref[...], p_lo))[0:n_sel]
    blk_t = lax.broadcasted_iota(jnp.int32, (n_sel, tq), 0)
    cur_t = (t0 + lax.broadcasted_iota(jnp.int32, (n_sel, tq), 1)) >> NSA_SEL_SHIFT
    eligible = blk_t <= cur_t
    forced = (blk_t == 0) | (blk_t == cur_t) | (blk_t == cur_t - 1)
    score = jnp.where(eligible, jnp.where(forced, NSA_FORCE_SCORE, imp_t), NEG_INF)
    rank = _rank_rows(score, n_sel)
    bias_t = jnp.where(eligible & (rank < NSA_SEL_TOPK), 0.0, MASKED)
    bias_t = jnp.concatenate([bias_t, jnp.zeros((LANES - n_sel, tq), F32)], axis=0)
    bias = _rows_to_lanes(bias_t, tq)
    q_aug = jnp.concatenate([qs, jnp.concatenate([bias] * heads, axis=0)], axis=1).astype(BF16)

    assert tq == NSA_WINDOW
    local_r = lax.broadcasted_iota(jnp.int32, (tq, tq), 0)
    local_c = lax.broadcasted_iota(jnp.int32, (tq, tq), 1)
    causal = local_c <= local_r
    rc = acc_sc.shape[1]
    n_chain = heads * tq // rc

    def run(q_rows, k_sc, v_sc, kb, carry, mask, lower):
        def keys(c):
            r0 = (c * rc) % tq
            k0, nk = (0, tq) if mask is None else ((0, r0 + rc) if lower else (r0, tq - r0))
            return pl.ds(pl.multiple_of(kb * tq + k0, rc), nk), r0, k0, nk

        def scores(c):
            r, r0, k0, nk = keys(c)
            s = _dot_nt(q_rows[c * rc:(c + 1) * rc], k_sc[r, :])
            return s if mask is None else jnp.where(mask[r0:r0 + rc, k0:k0 + nk], s, MASKED)

        out, s_next = [], scores(0)
        for c in range(n_chain):
            s, s_next = s_next, (scores(c + 1) if c + 1 < n_chain else None)
            m_new, acc_sc[c] = _flash_update(s, carry[c], acc_sc[c], v_sc[keys(c)[0], :])
            out.append(m_new)
        return tuple(out)

    def finish():
        acc = jnp.concatenate([acc_sc[c] for c in range(n_chain)], axis=0)
        return (acc[:, :HEAD_DIM] / acc[:, HEAD_DIM:]).reshape(heads, tq, HEAD_DIM)

    init = tuple(jnp.full((rc, 1), NEG_INF, F32) for _ in range(n_chain))

    acc_sc[...] = jnp.zeros(acc_sc.shape, F32)
    carry = lax.fori_loop(0, qi, lambda kb, cr: run(q_aug, ks_sc, vs_sc, kb, cr, None, True), init)
    run(q_aug, ks_sc, vs_sc, qi, carry, causal, True)
    o_sel = finish()

    acc_sc[...] = jnp.zeros(acc_sc.shape, F32)
    carry = run(qs_bf, kw_sc, vw_sc, jnp.maximum(qi - 1, 0), init, (local_c > local_r) & (qi >= 1), False)
    run(qs_bf, kw_sc, vw_sc, qi, carry, causal, True)
    o_win = finish()

    g = jax.nn.sigmoid(ng_ref[...].astype(F32))
    for h in range(heads):
        o = (g[:, 3 * h:3 * h + 1] * o_cmp[h] + g[:, 3 * h + 1:3 * h + 2] * o_sel[h]
             + g[:, 3 * h + 2:3 * h + 3] * o_win[h])
        o_ref[:, h * LANES:(h + 1) * LANES] = o.astype(o_ref.dtype)


def _nsa(proj, w1, w2, pos, tabs, batch, seq, tq=NSA_WINDOW, rc=CHAIN_ROWS):
    nt = seq // tq
    n_cmp = (seq - NSA_CMP_LEN) // NSA_CMP_STRIDE + 1
    n_sel = seq // NSA_SEL_LEN
    starts = np.arange(LANES) * NSA_CMP_STRIDE
    sel_start = np.arange(LANES) * NSA_SEL_LEN
    overlap = ((starts[:, None] < sel_start[None, :] + NSA_SEL_LEN)
               & (starts[:, None] + NSA_CMP_LEN > sel_start[None, :])
               & (np.arange(LANES)[:, None] < n_cmp) & (np.arange(LANES)[None, :] < n_sel))
    ovt = jnp.asarray(overlap.T.astype(np.float32), BF16)
    seq_col = lambda name: pl.BlockSpec((seq, LANES), lambda b, i: (b, COL128[name]))
    full = lambda a: pl.BlockSpec(a.shape, lambda b, i: (0,) * a.ndim)
    return pl.pallas_call(
        functools.partial(_nsa_kernel, tq=tq, seq=seq, scale=HEAD_DIM ** -0.5),
        grid=(batch, nt),
        in_specs=[pl.BlockSpec((tq, WIDE), lambda b, i: (b * nt + i, COL512["nq"])),
                  seq_col("nkc"), seq_col("nvc"), seq_col("nks"), seq_col("nvs"), seq_col("nkw"), seq_col("nvw"),
                  pl.BlockSpec((tq, LANES), lambda b, i: (b * nt + i, COL128["ng"])),
                  full(w1), full(w2), full(pos), full(ovt), full(tabs[0]), full(tabs[1]), full(tabs[2])],
        out_specs=pl.BlockSpec((tq, NSA_HEADS * HEAD_DIM), lambda b, i: (b * nt + i, 0)),
        out_shape=jax.ShapeDtypeStruct((batch * seq, NSA_HEADS * HEAD_DIM), BF16),
        scratch_shapes=[pltpu.VMEM((seq, LANES), F32),
                        pltpu.VMEM((LANES, LANES), BF16), pltpu.VMEM((LANES, LANES), BF16),
                        pltpu.VMEM((seq, 2 * LANES), BF16), pltpu.VMEM((seq, 2 * LANES), BF16),
                        pltpu.VMEM((seq, LANES), BF16), pltpu.VMEM((seq, 2 * LANES), BF16),
                        pltpu.VMEM((NSA_HEADS * tq // rc, rc, 2 * HEAD_DIM), F32)],
        compiler_params=_cparams("parallel", "arbitrary"),
        name="nsa",
    )(proj, proj, proj, proj, proj, proj, proj, proj, w1, w2, pos, ovt, *tabs)


def _moba_kernel(mq_ref, mk_ref, mv_ref, tc_ref, ts1_ref, ts2_ref, o_ref, k_sc, v_sc, km_sc, acc_sc,
                 *, tq, seq, scale):
    qi = pl.program_id(1)
    half = PARTIAL_ROT // 2
    heads = MOBA_HEADS
    n_blk = seq // MOBA_BLOCK
    slot = 2 * LANES

    @pl.when(qi == 0)
    def _prep():
        c, s1, s2 = tc_ref[...], ts1_ref[...], ts2_ref[...]
        km_sc[...] = jnp.zeros_like(km_sc)
        ones = jnp.ones((seq, LANES), BF16)
        blk = lax.broadcasted_iota(jnp.int32, (seq, LANES), 0) >> MOBA_SHIFT
        onehot = jnp.where(blk == lax.broadcasted_iota(jnp.int32, (seq, LANES), 1), 1.0, 0.0).astype(BF16)
        for h in range(heads):
            hs = slice(h * LANES, (h + 1) * LANES)
            kh = _rope128(mk_ref[:, hs].astype(F32), c, s1, s2, half)
            k_sc[:, h * slot:h * slot + LANES] = kh.astype(BF16)
            k_sc[:, h * slot + LANES:(h + 1) * slot] = onehot
            v_sc[:, h * slot:h * slot + LANES] = mv_ref[:, hs].astype(BF16)
            v_sc[:, h * slot + LANES:(h + 1) * slot] = ones
            for j in range(n_blk):
                km_sc[h * n_blk + j:h * n_blk + j + 1, :] = jnp.mean(
                    kh[j * MOBA_BLOCK:(j + 1) * MOBA_BLOCK], axis=0, keepdims=True)

    t0 = pl.multiple_of(qi * tq, tq)
    rq = pl.ds(t0, tq)
    c, s1, s2 = tc_ref[rq, :], ts1_ref[rq, :], ts2_ref[rq, :]
    km_hi, km_lo = _split_bf16(km_sc[...])
    tk = tq
    rc = MOBA_BLOCK
    sub = tq // rc
    blk_t = lax.broadcasted_iota(jnp.int32, (n_blk, tq), 0)
    cur_t = (t0 + lax.broadcasted_iota(jnp.int32, (n_blk, tq), 1)) >> MOBA_SHIFT
    row = lax.broadcasted_iota(jnp.int32, (tq, tk), 0)
    col = lax.broadcasted_iota(jnp.int32, (tq, tk), 1)

    qfs, bias_ts = [], []
    for h in range(heads):
        qf = _rope128(mq_ref[:, h * LANES:(h + 1) * LANES].astype(F32), c, s1, s2, half) * scale
        q_hi, q_lo = _split_bf16(qf)
        gate_t = (_dot_nt(km_hi, q_hi) + (_dot_nt(km_lo, q_hi) + _dot_nt(km_hi, q_lo)))[h * n_blk:(h + 1) * n_blk]
        eligible = blk_t < cur_t
        score = jnp.where(eligible, gate_t, NEG_INF)
        picked = eligible & (_rank_rows(score, n_blk) < MOBA_TOPK)
        bias_t = jnp.where(picked | (blk_t == cur_t), 0.0, MASKED)
        qfs.append(qf)
        bias_ts.append(jnp.concatenate([bias_t, jnp.zeros((LANES - n_blk, tq), F32)], axis=0))
    bias = _rows_to_lanes(jnp.concatenate(bias_ts, axis=0), tq)
    q_aug = [jnp.concatenate([qfs[h], bias[:, h * LANES:(h + 1) * LANES]], axis=1).astype(BF16) for h in range(heads)]

    n_chain = heads * sub

    causal = col <= row

    def step(kb, carry, diag):
        def keys(c):
            nk = (c % sub + 1) * rc if diag else tk
            return pl.ds(pl.multiple_of(kb * tk, tk), nk), nk

        def scores(c):
            h, r0 = c // sub, (c % sub) * rc
            r, nk = keys(c)
            s = _dot_nt(q_aug[h][r0:r0 + rc], k_sc[r, h * slot:(h + 1) * slot])
            return jnp.where(causal[r0:r0 + rc, :nk], s, MASKED) if diag else s

        out, s_next = [], scores(0)
        for c in range(n_chain):
            s, s_next = s_next, (scores(c + 1) if c + 1 < n_chain else None)
            h = c // sub
            m_new, acc_sc[c] = _flash_update(s, carry[c], acc_sc[c], v_sc[keys(c)[0], h * slot:(h + 1) * slot])
            out.append(m_new)
        return tuple(out)

    acc_sc[...] = jnp.zeros(acc_sc.shape, F32)
    carry = tuple(jnp.full((rc, 1), NEG_INF, F32) for _ in range(n_chain))
    carry = lax.fori_loop(0, qi, lambda kb, cr: step(kb, cr, False), carry)
    step(qi, carry, True)
    for c in range(n_chain):
        h, r0 = c // sub, (c % sub) * rc
        acc = acc_sc[c]
        o_ref[r0:r0 + rc, h * LANES:(h + 1) * LANES] = (acc[:, :HEAD_DIM] / acc[:, HEAD_DIM:]).astype(o_ref.dtype)


def _moba(proj, tabs, batch, seq):
    tq = 2 * MOBA_BLOCK
    nt = seq // tq
    width = MOBA_HEADS * HEAD_DIM
    full = lambda a: pl.BlockSpec(a.shape, lambda b, i: (0,) * a.ndim)
    return pl.pallas_call(
        functools.partial(_moba_kernel, tq=tq, seq=seq, scale=HEAD_DIM ** -0.5),
        grid=(batch, nt),
        in_specs=[pl.BlockSpec((tq, width), lambda b, i: (b * nt + i, COL512["mq"])),
                  pl.BlockSpec((seq, width), lambda b, i: (b, COL512["mk"])),
                  pl.BlockSpec((seq, width), lambda b, i: (b, COL512["mv"])),
                  full(tabs[0]), full(tabs[1]), full(tabs[2])],
        out_specs=pl.BlockSpec((tq, width), lambda b, i: (b * nt + i, 0)),
        out_shape=jax.ShapeDtypeStruct((batch * seq, width), BF16),
        scratch_shapes=[pltpu.VMEM((seq, 2 * width), BF16), pltpu.VMEM((seq, 2 * width), BF16),
                        pltpu.VMEM((LANES, LANES), F32),
                        pltpu.VMEM((MOBA_HEADS * tq // MOBA_BLOCK, MOBA_BLOCK, 2 * HEAD_DIM), F32)],
        compiler_params=_cparams("parallel", "arbitrary"),
        name="moba",
    )(proj, proj, proj, *tabs)


def _xattn_kernel(q_ref, kv_ref, o_ref):
    d = q_ref.shape[1]
    hd = d // MEM_HEADS
    head = lambda h: slice(h * hd, (h + 1) * hd)
    scores = [_dot_nt(q_ref[:, head(h)], kv_ref[0, :, head(h)]) for h in range(MEM_HEADS)]
    for h, s in enumerate(scores):
        hs = head(h)
        e = jnp.exp(s - jnp.max(s, axis=-1, keepdims=True))
        p = e / jnp.sum(e, axis=-1, keepdims=True)
        o_ref[:, hs] = _dot(p.astype(BF16), kv_ref[0, :, d + h * hd:d + (h + 1) * hd]).astype(o_ref.dtype)


def _xattn(q, kv, batch, seq, tq=ROW_TILE):
    d = q.shape[1]
    nt = seq // tq
    m_len = kv.shape[0] // batch
    return pl.pallas_call(
        _xattn_kernel,
        grid=(batch, nt),
        in_specs=[pl.BlockSpec((tq, d), lambda b, i: (b * nt + i, 0)),
                  pl.BlockSpec((1, m_len, 2 * d), lambda b, i: (b, 0, 0))],
        out_specs=pl.BlockSpec((tq, d), lambda b, i: (b * nt + i, 0)),
        out_shape=jax.ShapeDtypeStruct((batch * seq, d), BF16),
        compiler_params=_cparams("parallel", "parallel"),
        name="xattn",
    )(q, kv.reshape(batch, m_len, 2 * d))


IN_SPLIT_NAMES = ("c_q", "c_kv", "k_rope", "nq", "nkc", "nvc", "nks", "nvs", "nkw", "nvw", "ng", "mq", "mk", "mv")
IN_SPLIT_SIZES = (MLA_Q_RANK, MLA_KV_RANK, MLA_ROPE, WIDE) + (HEAD_DIM,) * 6 + (3 * NSA_HEADS, WIDE, WIDE, WIDE)


def _pack_w_in_kernel(w_ref, o_ref):
    rows = w_ref.shape[0]
    off = 0
    for name, sz in zip(IN_SPLIT_NAMES, IN_SPLIT_SIZES):
        width = WIDE if name in COL512 else LANES
        dst = COL512[name] * WIDE if name in COL512 else COL128[name] * LANES
        x = w_ref[:, off:off + sz].astype(BF16)
        if sz < width:
            x = jnp.concatenate([x, jnp.zeros((rows, width - sz), BF16)], axis=1)
        o_ref[0, :, dst:dst + width] = x
        off += sz


def _pack_w_in(w, tr=256):
    layers, d, width = w.shape
    return pl.pallas_call(
        _pack_w_in_kernel,
        grid=(layers, d // tr),
        in_specs=[pl.BlockSpec((None, tr, width), lambda l, i: (l, i, 0))],
        out_specs=pl.BlockSpec((1, tr, PROJ_WIDTH), lambda l, i: (l, i, 0)),
        out_shape=jax.ShapeDtypeStruct((layers, d, PROJ_WIDTH), BF16),
        compiler_params=_cparams("parallel", "parallel"),
        name="pack_w_in",
    )(w)


def _pack_w_uq(w):
    r = w.shape[0]
    w = w.reshape(r, MLA_HEADS, MLA_NOPE + MLA_ROPE)
    w = jnp.pad(w, ((0, 0), (0, 0), (0, MLA_SLOT - MLA_NOPE - MLA_ROPE)))
    return w.reshape(r, MLA_HEADS * MLA_SLOT).astype(BF16)


def _pack_w_ukv(w):
    r = w.shape[0]
    w = w.reshape(r, MLA_HEADS, 2, HEAD_DIM).transpose(0, 2, 1, 3)
    return w.reshape(r, 2 * MLA_HEADS * HEAD_DIM).astype(BF16)


def kernel(x, mem, ln_in_g, ln_in_b, w_in, mla_q_norm, mla_kv_norm, mla_w_uq, mla_w_ukv, nsa_cmp_w1, nsa_cmp_w2, nsa_cmp_pos, w_out, ln1_g, ln1_b, mem_wq, mem_wkv, mem_wo, ln2_g, ln2_b, mlp_w1, mlp_w2, ln3_g, ln3_b):
    batch, seq, d = x.shape
    n = batch * seq
    mla_tabs = _rope_tables(seq, MLA_ROPE)
    rot_tabs = _rope_tables(seq, PARTIAL_ROT)
    mem2 = mem.reshape(batch * mem.shape[1], d)
    w_in_packed = _pack_w_in(w_in)
    w_out_bf, mem_wkv_bf, mem_wo_bf = w_out.astype(BF16), mem_wkv.astype(BF16), mem_wo.astype(BF16)
    mem_wq_bf = (mem_wq * (d // MEM_HEADS) ** -0.5).astype(BF16)
    mlp_w1_bf, mlp_w2_bf = mlp_w1.astype(BF16), mlp_w2.astype(BF16)

    h = _layer_norm(x.reshape(n, d), ln_in_g, ln_in_b)
    for l in range(DEPTH):
        proj = _matmul(h, w_in_packed, l, BF16, name="in_proj")
        q, k, v = _mla_up(proj, mla_q_norm[l], mla_kv_norm[l], _pack_w_uq(mla_w_uq[l]), _pack_w_ukv(mla_w_ukv[l]),
                          mla_tabs, batch, seq)
        o_a = _mla_attn(q, k, v).reshape(n, MLA_HEADS * HEAD_DIM)
        o_b = _nsa(proj, nsa_cmp_w1[l].astype(BF16), nsa_cmp_w2[l].astype(BF16), nsa_cmp_pos[l], rot_tabs, batch, seq)
        o_c = _moba(proj, rot_tabs, batch, seq)
        h = _out_ln([o_a, o_b, o_c], w_out_bf, l, h, ln1_g[l], ln1_b[l], name="mix_out_ln")

        xq = _matmul(h, mem_wq_bf, l, BF16, name="mem_q")
        xkv = _matmul(mem2, mem_wkv_bf, l, BF16, name="mem_kv")
        ctx = _xattn(xq, xkv, batch, seq)
        h = _out_ln([ctx], mem_wo_bf, l, h, ln2_g[l], ln2_b[l], name="mem_out_ln")

        h = _mlp_ln(h, mlp_w1_bf, mlp_w2_bf, l, ln3_g[l], ln3_b[l])
    return h.reshape(batch, seq, d)
```

```python
import functools

import numpy as np
import jax
import jax.numpy as jnp
from jax import lax
from jax.experimental import pallas as pl
from jax.experimental.pallas import tpu as pltpu

F32 = jnp.float32
BF16 = jnp.bfloat16
NEG_INF = float("-inf")

D_MODEL = 2048
DEPTH = 2
HEAD_DIM = 128
MLA_HEADS = 8
NSA_HEADS = 4
MOBA_HEADS = 4
ROPE_THETA = 500000.0
PARTIAL_ROT = HEAD_DIM // 4
MLA_Q_RANK = 512
MLA_KV_RANK = 512
MLA_NOPE = 128
MLA_ROPE = 64
MLA_SLOT = 256
NSA_CMP_LEN = 32
NSA_CMP_STRIDE = 16
NSA_SEL_LEN = 64
NSA_SEL_TOPK = 16
NSA_WINDOW = 512
NSA_FORCE_SCORE = 1.0e4
MOBA_BLOCK = 256
MOBA_TOPK = 3
MEM_HEADS = 4
DEEPNORM_ALPHA = (2 * DEPTH) ** 0.25
LANES = 128
NSA_SEL_SHIFT = NSA_SEL_LEN.bit_length() - 1
MOBA_SHIFT = MOBA_BLOCK.bit_length() - 1

PROJ_WIDTH = 4096
WIDE = 4 * HEAD_DIM
COL512 = dict(c_q=0, c_kv=1, nq=2, mq=3, mk=4, mv=5)
COL128 = dict(k_rope=24, nkc=25, nvc=26, nks=27, nvs=28, nkw=29, nvw=30, ng=31)

VMEM_LIMIT = 56 * 1024 * 1024
ROW_TILE = 512
CHAIN_ROWS = 256


def _cparams(*sem):
    return pltpu.CompilerParams(dimension_semantics=sem, vmem_limit_bytes=VMEM_LIMIT)


def _dot(a, b):
    return jnp.dot(a, b, preferred_element_type=F32)


def _dot_nt(a, b):
    return lax.dot_general(a, b, (((1,), (1,)), ((), ())), preferred_element_type=F32)


def _split_bf16(x):
    hi = x.astype(BF16)
    return hi, (x - hi.astype(F32)).astype(BF16)


def _ln_rows(x, g, b, eps=1e-5):
    mu = jnp.mean(x, axis=-1, keepdims=True)
    xc = x - mu
    var = jnp.mean(xc * xc, axis=-1, keepdims=True)
    return xc * lax.rsqrt(var + eps) * g + b


def _rms_rows(x, g, eps=1e-6):
    return x * lax.rsqrt(jnp.mean(x * x, axis=-1, keepdims=True) + eps) * g


def _rope128(x, c, s1, s2, half):
    return x * c + pltpu.roll(x, LANES - half, 1) * s1 + pltpu.roll(x, half, 1) * s2


def _rope_tables(n_pos, dim):
    half = dim // 2
    inv = ROPE_THETA ** (-jnp.arange(0, dim, 2, dtype=F32) / dim)
    ang = jnp.arange(n_pos, dtype=F32)[:, None] * inv[None, :]
    cos, sin = jnp.cos(ang), jnp.sin(ang)
    ones = jnp.ones((n_pos, LANES - dim), F32)
    z = lambda w: jnp.zeros((n_pos, w), F32)
    c = jnp.concatenate([cos, cos, ones], axis=1)
    s1 = jnp.concatenate([-sin, z(LANES - half)], axis=1)
    s2 = jnp.concatenate([z(half), sin, z(LANES - dim)], axis=1)
    return c, s1, s2


MASKED = -1.0e30


def _rank_rows(score, n_cand):
    row = lax.broadcasted_iota(jnp.int32, score.shape, 0)
    rank = jnp.zeros(score.shape, F32)
    for jp in range(n_cand):
        cand = score[jp:jp + 1, :]
        ahead = (cand > score) | ((cand == score) & (row > jp))
        rank = rank + jnp.where(ahead, 1.0, 0.0)
    return rank


def _rows_to_lanes(x_t, tq):
    eye = jnp.where(lax.broadcasted_iota(jnp.int32, (tq, tq), 0) == lax.broadcasted_iota(jnp.int32, (tq, tq), 1),
                    1.0, 0.0).astype(BF16)
    return _dot_nt(eye, x_t.astype(BF16))


def _flash_update(s, m, acc, v_aug):
    m_new = jnp.maximum(m, jnp.max(s, axis=-1, keepdims=True))
    p = jnp.exp(s - m_new).astype(BF16)
    return m_new, jnp.exp(m - m_new) * acc + _dot(p, v_aug)


def _ln_kernel(x_ref, g_ref, b_ref, o_ref):
    o_ref[...] = _ln_rows(x_ref[...], g_ref[...], b_ref[...])


def _layer_norm(x, g, b, tm=ROW_TILE):
    n, d = x.shape
    return pl.pallas_call(
        _ln_kernel,
        grid=(n // tm,),
        in_specs=[pl.BlockSpec((tm, d), lambda i: (i, 0)),
                  pl.BlockSpec((1, d), lambda i: (0, 0)),
                  pl.BlockSpec((1, d), lambda i: (0, 0))],
        out_specs=pl.BlockSpec((tm, d), lambda i: (i, 0)),
        out_shape=jax.ShapeDtypeStruct((n, d), F32),
        compiler_params=_cparams("parallel"),
        name="ln_in",
    )(x, g.reshape(1, d), b.reshape(1, d))


def _mm_kernel(a_ref, w_ref, o_ref, abf_ref):
    @pl.when(pl.program_id(1) == 0)
    def _():
        abf_ref[...] = a_ref[...].astype(BF16)

    o_ref[...] = _dot(abf_ref[...], w_ref[...]).astype(o_ref.dtype)


def _matmul(a, w, layer, out_dtype, tm=1024, tn=2048, name="mm"):
    m, k = a.shape
    n = w.shape[2]
    tm = min(tm, m)
    return pl.pallas_call(
        _mm_kernel,
        grid=(m // tm, n // tn),
        in_specs=[pl.BlockSpec((tm, k), lambda i, j: (i, 0)),
                  pl.BlockSpec((None, k, tn), lambda i, j: (layer, 0, j))],
        out_specs=pl.BlockSpec((tm, tn), lambda i, j: (i, j)),
        out_shape=jax.ShapeDtypeStruct((m, n), out_dtype),
        scratch_shapes=[pltpu.VMEM((tm, k), BF16)],
        compiler_params=_cparams("parallel", "arbitrary"),
        name=name,
    )(a, w)


def _out_ln_kernel(*refs, widths):
    n_a = len(widths)
    a_refs = refs[:n_a]
    w_ref, h_ref, g_ref, b_ref, o_ref = refs[n_a:]
    tm = h_ref.shape[0]
    parts = tuple(slice(i * tm // 4, (i + 1) * tm // 4) for i in range(4))
    accs = []
    for rows in parts:
        a = jnp.concatenate([a_ref[rows, :] for a_ref in a_refs], axis=1)
        accs.append(DEEPNORM_ALPHA * h_ref[rows, :] + _dot(a, w_ref[...]))
    for rows, acc in zip(parts, accs):
        o_ref[rows, :] = _ln_rows(acc, g_ref[...], b_ref[...])


def _out_ln(a_list, w, layer, h, g, b, tm=ROW_TILE, name="out_ln"):
    n, d = h.shape
    widths = tuple(a.shape[1] for a in a_list)
    k = sum(widths)
    in_specs = [pl.BlockSpec((tm, wd), lambda i: (i, 0)) for wd in widths]
    in_specs += [pl.BlockSpec((None, k, d), lambda i: (layer, 0, 0)),
                 pl.BlockSpec((tm, d), lambda i: (i, 0)),
                 pl.BlockSpec((1, d), lambda i: (0, 0)),
                 pl.BlockSpec((1, d), lambda i: (0, 0))]
    return pl.pallas_call(
        functools.partial(_out_ln_kernel, widths=widths),
        grid=(n // tm,),
        in_specs=in_specs,
        out_specs=pl.BlockSpec((tm, d), lambda i: (i, 0)),
        out_shape=jax.ShapeDtypeStruct((n, d), F32),
        compiler_params=_cparams("parallel"),
        name=name,
    )(*a_list, w, h, g.reshape(1, d), b.reshape(1, d))


def _mlp_kernel(h_ref, w1_ref, w2_ref, g_ref, b_ref, o_ref, hbf_ref, acc_ref):
    f = pl.program_id(1)

    @pl.when(f == 0)
    def _():
        hbf_ref[...] = h_ref[...].astype(BF16)
        acc_ref[...] = jnp.zeros_like(acc_ref)

    u = jnp.maximum(_dot(hbf_ref[...], w1_ref[...]), 0.0)
    acc_ref[...] += _dot((u * u).astype(BF16), w2_ref[...])

    @pl.when(f == pl.num_programs(1) - 1)
    def _():
        y = DEEPNORM_ALPHA * h_ref[...] + acc_ref[...]
        o_ref[...] = _ln_rows(y, g_ref[...], b_ref[...])


def _mlp_ln(h, w1, w2, layer, g, b, tm=ROW_TILE, tf=1024):
    n, d = h.shape
    dff = w1.shape[2]
    return pl.pallas_call(
        _mlp_kernel,
        grid=(n // tm, dff // tf),
        in_specs=[pl.BlockSpec((tm, d), lambda i, f: (i, 0)),
                  pl.BlockSpec((None, d, tf), lambda i, f: (layer, 0, f)),
                  pl.BlockSpec((None, tf, d), lambda i, f: (layer, f, 0)),
                  pl.BlockSpec((1, d), lambda i, f: (0, 0)),
                  pl.BlockSpec((1, d), lambda i, f: (0, 0))],
        out_specs=pl.BlockSpec((tm, d), lambda i, f: (i, 0)),
        out_shape=jax.ShapeDtypeStruct((n, d), F32),
        scratch_shapes=[pltpu.VMEM((tm, d), BF16), pltpu.VMEM((tm, d), F32)],
        compiler_params=_cparams("parallel", "arbitrary"),
        name="mlp_ln",
    )(h, w1, w2, g.reshape(1, d), b.reshape(1, d))


def _mla_up_kernel(cq_ref, ckv_ref, kr_ref, gq_ref, gkv_ref, wq_ref, wkv_ref, tc_ref, ts1_ref, ts2_ref,
                   q_ref, k_ref, v_ref, *, scale):
    half = MLA_ROPE // 2
    c, s1, s2 = tc_ref[...], ts1_ref[...], ts2_ref[...]
    nq = _rms_rows(cq_ref[...].astype(F32), gq_ref[...]).astype(BF16)
    nkv = _rms_rows(ckv_ref[...].astype(F32), gkv_ref[...]).astype(BF16)
    qf = _dot(nq, wq_ref[...])
    kvf = _dot(nkv, wkv_ref[...])
    kr = _rope128(kr_ref[...].astype(F32), c, s1, s2, half).astype(BF16)
    for h in range(MLA_HEADS):
        o = h * MLA_SLOT
        q_ref[0, h, :, 0:LANES] = (qf[:, o:o + LANES] * scale).astype(BF16)
        q_ref[0, h, :, LANES:] = (_rope128(qf[:, o + LANES:o + MLA_SLOT], c, s1, s2, half) * scale).astype(BF16)
        k_ref[0, h, :, 0:LANES] = kvf[:, h * LANES:(h + 1) * LANES].astype(BF16)
        k_ref[0, h, :, LANES:] = kr
        v_ref[0, h, :, 0:LANES] = kvf[:, (MLA_HEADS + h) * LANES:(MLA_HEADS + h + 1) * LANES].astype(BF16)
        v_ref[0, h, :, LANES:] = jnp.ones((kr.shape[0], LANES), BF16)


def _mla_up(proj, gq, gkv, wq, wkv, tabs, batch, seq, tm=ROW_TILE):
    nt = seq // tm
    row = lambda b, i: b * nt + i
    tab_spec = pl.BlockSpec((tm, LANES), lambda b, i: (i, 0))
    hm = lambda w: pl.BlockSpec((1, MLA_HEADS, tm, w), lambda b, i: (b, 0, i, 0))
    return pl.pallas_call(
        functools.partial(_mla_up_kernel, scale=(MLA_NOPE + MLA_ROPE) ** -0.5),
        grid=(batch, nt),
        in_specs=[pl.BlockSpec((tm, WIDE), lambda b, i: (row(b, i), COL512["c_q"])),
                  pl.BlockSpec((tm, WIDE), lambda b, i: (row(b, i), COL512["c_kv"])),
                  pl.BlockSpec((tm, LANES), lambda b, i: (row(b, i), COL128["k_rope"])),
                  pl.BlockSpec((1, MLA_Q_RANK), lambda b, i: (0, 0)),
                  pl.BlockSpec((1, MLA_KV_RANK), lambda b, i: (0, 0)),
                  pl.BlockSpec(wq.shape, lambda b, i: (0, 0)),
                  pl.BlockSpec(wkv.shape, lambda b, i: (0, 0)),
                  tab_spec, tab_spec, tab_spec],
        out_specs=[hm(MLA_SLOT), hm(MLA_SLOT), hm(2 * HEAD_DIM)],
        out_shape=[jax.ShapeDtypeStruct((batch, MLA_HEADS, seq, MLA_SLOT), BF16),
                   jax.ShapeDtypeStruct((batch, MLA_HEADS, seq, MLA_SLOT), BF16),
                   jax.ShapeDtypeStruct((batch, MLA_HEADS, seq, 2 * HEAD_DIM), BF16)],
        compiler_params=_cparams("parallel", "parallel"),
        name="mla_up",
    )(proj, proj, proj, gq.reshape(1, -1), gkv.reshape(1, -1), wq, wkv, *tabs)


def _mla_attn_kernel(q_ref, k_ref, v_ref, o_ref, acc_sc, *, tq, rs):
    qi = pl.program_id(2)
    heads = q_ref.shape[1]
    causal = lax.broadcasted_iota(jnp.int32, (tq, tq), 1) <= lax.broadcasted_iota(jnp.int32, (tq, tq), 0)
    nrs = tq // rs
    n_chain = heads * nrs

    def step(kb, carry, diag):
        def keys(c):
            nk = (c % nrs + 1) * rs if diag else tq
            return pl.ds(pl.multiple_of(kb * tq, tq), nk), nk

        def scores(c):
            h, r0 = c // nrs, (c % nrs) * rs
            r, nk = keys(c)
            s = _dot_nt(q_ref[0, h, r0:r0 + rs, :], k_ref[0, h, r, :])
            return jnp.where(causal[r0:r0 + rs, :nk], s, MASKED) if diag else s

        out, s_next = [], scores(0)
        for c in range(n_chain):
            s, s_next = s_next, (scores(c + 1) if c + 1 < n_chain else None)
            m_new, acc_sc[c] = _flash_update(s, carry[c], acc_sc[c], v_ref[0, c // nrs, keys(c)[0], :])
            out.append(m_new)
        return tuple(out)

    acc_sc[...] = jnp.zeros(acc_sc.shape, F32)
    carry = tuple(jnp.full((rs, 1), NEG_INF, F32) for _ in range(n_chain))
    carry = lax.fori_loop(0, qi, lambda kb, c: step(kb, c, False), carry)
    step(qi, carry, True)
    for h in range(heads):
        for i in range(nrs):
            acc = acc_sc[h * nrs + i]
            o_ref[0, i * rs:(i + 1) * rs, h * HEAD_DIM:(h + 1) * HEAD_DIM] = (
                acc[:, :HEAD_DIM] / acc[:, HEAD_DIM:]).astype(o_ref.dtype)


def _mla_attn(q, k, v, tq=ROW_TILE, rs=CHAIN_ROWS, hg=MLA_HEADS // 2):
    batch, heads, seq, _ = q.shape
    return pl.pallas_call(
        functools.partial(_mla_attn_kernel, tq=tq, rs=rs),
        grid=(batch, heads // hg, seq // tq),
        in_specs=[pl.BlockSpec((1, hg, tq, MLA_SLOT), lambda b, g, i: (b, g, i, 0)),
                  pl.BlockSpec((1, hg, seq, MLA_SLOT), lambda b, g, i: (b, g, 0, 0)),
                  pl.BlockSpec((1, hg, seq, 2 * HEAD_DIM), lambda b, g, i: (b, g, 0, 0))],
        out_specs=pl.BlockSpec((1, tq, hg * HEAD_DIM), lambda b, g, i: (b, i, g)),
        out_shape=jax.ShapeDtypeStruct((batch, seq, heads * HEAD_DIM), BF16),
        scratch_shapes=[pltpu.VMEM((hg * tq // rs, rs, 2 * HEAD_DIM), F32)],
        compiler_params=_cparams("parallel", "parallel", "arbitrary"),
        name="mla_attn",
    )(q, k, v)


def _nsa_kernel(nq_ref, kc_ref, vc_ref, ks_ref, vs_ref, kw_ref, vw_ref, ng_ref,
                w1_ref, w2_ref, pos_ref, ovt_ref, tc_ref, ts1_ref, ts2_ref,
                o_ref,
                tmp_sc, kc_sc, vc_sc, ks_sc, vs_sc, kw_sc, vw_sc, acc_sc, *, tq, seq, scale):
    qi = pl.program_id(1)
    half = PARTIAL_ROT // 2
    heads = NSA_HEADS
    n_chunk = seq // NSA_CMP_STRIDE
    n_sel = seq // NSA_SEL_LEN

    @pl.when(qi == 0)
    def _prep():
        c, s1, s2 = tc_ref[...], ts1_ref[...], ts2_ref[...]
        ones = jnp.ones((seq, LANES), BF16)
        blk = lax.broadcasted_iota(jnp.int32, (seq, LANES), 0) >> NSA_SEL_SHIFT
        ks_sc[:, 0:LANES] = _rope128(ks_ref[...].astype(F32), c, s1, s2, half).astype(BF16)
        ks_sc[:, LANES:] = jnp.where(blk == lax.broadcasted_iota(jnp.int32, (seq, LANES), 1), 1.0, 0.0).astype(BF16)
        kw_sc[...] = _rope128(kw_ref[...].astype(F32), c, s1, s2, half).astype(BF16)
        vs_sc[:, 0:LANES] = vs_ref[...].astype(BF16)
        vs_sc[:, LANES:] = ones
        vw_sc[:, 0:LANES] = vw_ref[...].astype(BF16)
        vw_sc[:, LANES:] = ones
        for i, (src, dst) in enumerate(((kc_ref, kc_sc), (vc_ref, vc_sc))):
            x_all = src[...].astype(F32)
            tmp_sc[...] = _rope128(x_all, c, s1, s2, half) if i == 0 else x_all
            lo, hi = [], []
            for t in range(NSA_CMP_STRIDE):
                x = tmp_sc[pl.ds(t, n_chunk, stride=NSA_CMP_STRIDE), :]
                lo.append((x + pos_ref[i, t:t + 1, :]).astype(BF16))
                hi.append((x + pos_ref[i, NSA_CMP_STRIDE + t:NSA_CMP_STRIDE + t + 1, :]).astype(BF16))
            kw1 = NSA_CMP_STRIDE * HEAD_DIM
            a = _dot(jnp.concatenate(lo, axis=1), w1_ref[i, 0:kw1, :])
            bm = _dot(jnp.concatenate(hi, axis=1), w1_ref[i, kw1:2 * kw1, :])
            hid = jax.nn.gelu(a + pltpu.roll(bm, n_chunk - 1, 0))
            dst[...] = _dot(hid.astype(BF16), w2_ref[i]).astype(BF16)

    t0 = pl.multiple_of(qi * tq, tq)
    rq = pl.ds(t0, tq)
    c, s1, s2 = tc_ref[rq, :], ts1_ref[rq, :], ts2_ref[rq, :]
    qf = nq_ref[...].astype(F32)
    qs = jnp.concatenate(
        [_rope128(qf[:, h * LANES:(h + 1) * LANES], c, s1, s2, half) * scale for h in range(heads)],
        axis=0)
    qs_bf = qs.astype(BF16)

    lane = lax.broadcasted_iota(jnp.int32, (tq, LANES), 1)
    tpos = t0 + lax.broadcasted_iota(jnp.int32, (tq, LANES), 0)

    s = _dot_nt(qs_bf, kc_sc[...]).reshape(heads, tq, LANES)
    cmask = (lane * NSA_CMP_STRIDE + (NSA_CMP_LEN - 1) <= tpos)[None]
    s = jnp.where(cmask, s, NEG_INF)
    m = jnp.max(s, axis=-1, keepdims=True)
    m = jnp.where(m == NEG_INF, 0.0, m)
    e = jnp.exp(s - m)
    p_cmp = e / jnp.maximum(jnp.sum(e, axis=-1, keepdims=True), 1e-30)
    o_cmp = _dot(p_cmp.reshape(heads * tq, LANES).astype(BF16), vc_sc[...]).reshape(heads, tq, HEAD_DIM)

    p_hi, p_lo = _split_bf16(p_cmp[0] + p_cmp[1] + p_cmp[2] + p_cmp[3])
    imp_t = (_dot_nt(ovt_ref[...], p_hi) + _dot_nt(ovt_ref[...], p_lo))[0:n_sel]
    blk_t = lax.broadcasted_iota(jnp.int32, (n_sel, tq), 0)
    cur_t = (t0 + lax.broadcasted_iota(jnp.int32, (n_sel, tq), 1)) >> NSA_SEL_SHIFT
    eligible = blk_t <= cur_t
    forced = (blk_t == 0) | (blk_t == cur_t) | (blk_t == cur_t - 1)
    score = jnp.where(eligible, jnp.where(forced, NSA_FORCE_SCORE, imp_t), NEG_INF)
    rank = _rank_rows(score, n_sel)
    bias_t = jnp.where(eligible & (rank < NSA_SEL_TOPK), 0.0, MASKED)
    bias_t = jnp.concatenate([bias_t, jnp.zeros((LANES - n_sel, tq), F32)], axis=0)
    bias = _rows_to_lanes(bias_t, tq)
    q_aug = jnp.concatenate([qs, jnp.concatenate([bias] * heads, axis=0)], axis=1).astype(BF16)

    assert tq == NSA_WINDOW
    local_r = lax.broadcasted_iota(jnp.int32, (tq, tq), 0)
    local_c = lax.broadcasted_iota(jnp.int32, (tq, tq), 1)
    causal = local_c <= local_r
    rc = acc_sc.shape[1]
    n_chain = heads * tq // rc

    def run(q_rows, k_sc, v_sc, kb, carry, mask, lower):
        def keys(c):
            r0 = (c * rc) % tq
            k0, nk = (0, tq) if mask is None else ((0, r0 + rc) if lower else (r0, tq - r0))
            return pl.ds(pl.multiple_of(kb * tq + k0, rc), nk), r0, k0, nk

        def scores(c):
            r, r0, k0, nk = keys(c)
            s = _dot_nt(q_rows[c * rc:(c + 1) * rc], k_sc[r, :])
            return s if mask is None else jnp.where(mask[r0:r0 + rc, k0:k0 + nk], s, MASKED)

        out, s_next = [], scores(0)
        for c in range(n_chain):
            s, s_next = s_next, (scores(c + 1) if c + 1 < n_chain else None)
            m_new, acc_sc[c] = _flash_update(s, carry[c], acc_sc[c], v_sc[keys(c)[0], :])
            out.append(m_new)
        return tuple(out)

    def finish():
        acc = jnp.concatenate([acc_sc[c] for c in range(n_chain)], axis=0)
        return (acc[:, :HEAD_DIM] / acc[:, HEAD_DIM:]).reshape(heads, tq, HEAD_DIM)

    init = tuple(jnp.full((rc, 1), NEG_INF, F32) for _ in range(n_chain))

    acc_sc[...] = jnp.zeros(acc_sc.shape, F32)
    carry = lax.fori_loop(0, qi, lambda kb, cr: run(q_aug, ks_sc, vs_sc, kb, cr, None, True), init)
    run(q_aug, ks_sc, vs_sc, qi, carry, causal, True)
    o_sel = finish()

    acc_sc[...] = jnp.zeros(acc_sc.shape, F32)
    carry = run(qs_bf, kw_sc, vw_sc, jnp.maximum(qi - 1, 0), init, (local_c > local_r) & (qi >= 1), False)
    run(qs_bf, kw_sc, vw_sc, qi, carry, causal, True)
    o_win = finish()

    g = jax.nn.sigmoid(ng_ref[...].astype(F32))
    for h in range(heads):
        o = (g[:, 3 * h:3 * h + 1] * o_cmp[h] + g[:, 3 * h + 1:3 * h + 2] * o_sel[h]
             + g[:, 3 * h + 2:3 * h + 3] * o_win[h])
        o_ref[:, h * LANES:(h + 1) * LANES] = o.astype(o_ref.dtype)


def _nsa(proj, w1, w2, pos, tabs, batch, seq, tq=NSA_WINDOW, rc=CHAIN_ROWS):
    nt = seq // tq
    n_cmp = (seq - NSA_CMP_LEN) // NSA_CMP_STRIDE + 1
    n_sel = seq // NSA_SEL_LEN
    starts = np.arange(LANES) * NSA_CMP_STRIDE
    sel_start = np.arange(LANES) * NSA_SEL_LEN
    overlap = ((starts[:, None] < sel_start[None, :] + NSA_SEL_LEN)
               & (starts[:, None] + NSA_CMP_LEN > sel_start[None, :])
               & (np.arange(LANES)[:, None] < n_cmp) & (np.arange(LANES)[None, :] < n_sel))
    ovt = jnp.asarray(overlap.T.astype(np.float32), BF16)
    seq_col = lambda name: pl.BlockSpec((seq, LANES), lambda b, i: (b, COL128[name]))
    full = lambda a: pl.BlockSpec(a.shape, lambda b, i: (0,) * a.ndim)
    return pl.pallas_call(
        functools.partial(_nsa_kernel, tq=tq, seq=seq, scale=HEAD_DIM ** -0.5),
        grid=(batch, nt),
        in_specs=[pl.BlockSpec((tq, WIDE), lambda b, i: (b * nt + i, COL512["nq"])),
                  seq_col("nkc"), seq_col("nvc"), seq_col("nks"), seq_col("nvs"), seq_col("nkw"), seq_col("nvw"),
                  pl.BlockSpec((tq, LANES), lambda b, i: (b * nt + i, COL128["ng"])),
                  full(w1), full(w2), full(pos), full(ovt), full(tabs[0]), full(tabs[1]), full(tabs[2])],
        out_specs=pl.BlockSpec((tq, NSA_HEADS * HEAD_DIM), lambda b, i: (b * nt + i, 0)),
        out_shape=jax.ShapeDtypeStruct((batch * seq, NSA_HEADS * HEAD_DIM), BF16),
        scratch_shapes=[pltpu.VMEM((seq, LANES), F32),
                        pltpu.VMEM((LANES, LANES), BF16), pltpu.VMEM((LANES, LANES), BF16),
                        pltpu.VMEM((seq, 2 * LANES), BF16), pltpu.VMEM((seq, 2 * LANES), BF16),
                        pltpu.VMEM((seq, LANES), BF16), pltpu.VMEM((seq, 2 * LANES), BF16),
                        pltpu.VMEM((NSA_HEADS * tq // rc, rc, 2 * HEAD_DIM), F32)],
        compiler_params=_cparams("parallel", "arbitrary"),
        name="nsa",
    )(proj, proj, proj, proj, proj, proj, proj, proj, w1, w2, pos, ovt, *tabs)


def _moba_kernel(mq_ref, mk_ref, mv_ref, tc_ref, ts1_ref, ts2_ref, o_ref, k_sc, v_sc, km_sc, acc_sc,
                 *, tq, seq, scale):
    qi = pl.program_id(1)
    half = PARTIAL_ROT // 2
    heads = MOBA_HEADS
    n_blk = seq // MOBA_BLOCK
    slot = 2 * LANES

    @pl.when(qi == 0)
    def _prep():
        c, s1, s2 = tc_ref[...], ts1_ref[...], ts2_ref[...]
        km_sc[...] = jnp.zeros_like(km_sc)
        ones = jnp.ones((seq, LANES), BF16)
        blk = lax.broadcasted_iota(jnp.int32, (seq, LANES), 0) >> MOBA_SHIFT
        onehot = jnp.where(blk == lax.broadcasted_iota(jnp.int32, (seq, LANES), 1), 1.0, 0.0).astype(BF16)
        for h in range(heads):
            hs = slice(h * LANES, (h + 1) * LANES)
            kh = _rope128(mk_ref[:, hs].astype(F32), c, s1, s2, half)
            k_sc[:, h * slot:h * slot + LANES] = kh.astype(BF16)
            k_sc[:, h * slot + LANES:(h + 1) * slot] = onehot
            v_sc[:, h * slot:h * slot + LANES] = mv_ref[:, hs].astype(BF16)
            v_sc[:, h * slot + LANES:(h + 1) * slot] = ones
            for j in range(n_blk):
                km_sc[h * n_blk + j:h * n_blk + j + 1, :] = jnp.mean(
                    kh[j * MOBA_BLOCK:(j + 1) * MOBA_BLOCK], axis=0, keepdims=True)

    t0 = pl.multiple_of(qi * tq, tq)
    rq = pl.ds(t0, tq)
    c, s1, s2 = tc_ref[rq, :], ts1_ref[rq, :], ts2_ref[rq, :]
    km_hi, km_lo = _split_bf16(km_sc[...])
    tk = tq
    rc = MOBA_BLOCK
    sub = tq // rc
    blk_t = lax.broadcasted_iota(jnp.int32, (n_blk, tq), 0)
    cur_t = (t0 + lax.broadcasted_iota(jnp.int32, (n_blk, tq), 1)) >> MOBA_SHIFT
    row = lax.broadcasted_iota(jnp.int32, (tq, tk), 0)
    col = lax.broadcasted_iota(jnp.int32, (tq, tk), 1)

    qfs, bias_ts = [], []
    for h in range(heads):
        qf = _rope128(mq_ref[:, h * LANES:(h + 1) * LANES].astype(F32), c, s1, s2, half) * scale
        q_hi, q_lo = _split_bf16(qf)
        gate_t = (_dot_nt(km_hi, q_hi) + (_dot_nt(km_lo, q_hi) + _dot_nt(km_hi, q_lo)))[h * n_blk:(h + 1) * n_blk]
        eligible = blk_t < cur_t
        score = jnp.where(eligible, gate_t, NEG_INF)
        picked = eligible & (_rank_rows(score, n_blk) < MOBA_TOPK)
        bias_t = jnp.where(picked | (blk_t == cur_t), 0.0, MASKED)
        qfs.append(qf)
        bias_ts.append(jnp.concatenate([bias_t, jnp.zeros((LANES - n_blk, tq), F32)], axis=0))
    bias = _rows_to_lanes(jnp.concatenate(bias_ts, axis=0), tq)
    q_aug = [jnp.concatenate([qfs[h], bias[:, h * LANES:(h + 1) * LANES]], axis=1).astype(BF16) for h in range(heads)]

    n_chain = heads * sub

    causal = col <= row

    def step(kb, carry, diag):
        def keys(c):
            nk = (c % sub + 1) * rc if diag else tk
            return pl.ds(pl.multiple_of(kb * tk, tk), nk), nk

        def scores(c):
            h, r0 = c // sub, (c % sub) * rc
            r, nk = keys(c)
            s = _dot_nt(q_aug[h][r0:r0 + rc], k_sc[r, h * slot:(h + 1) * slot])
            return jnp.where(causal[r0:r0 + rc, :nk], s, MASKED) if diag else s

        out, s_next = [], scores(0)
        for c in range(n_chain):
            s, s_next = s_next, (scores(c + 1) if c + 1 < n_chain else None)
            h = c // sub
            m_new, acc_sc[c] = _flash_update(s, carry[c], acc_sc[c], v_sc[keys(c)[0], h * slot:(h + 1) * slot])
            out.append(m_new)
        return tuple(out)

    acc_sc[...] = jnp.zeros(acc_sc.shape, F32)
    carry = tuple(jnp.full((rc, 1), NEG_INF, F32) for _ in range(n_chain))
    carry = lax.fori_loop(0, qi, lambda kb, cr: step(kb, cr, False), carry)
    step(qi, carry, True)
    for c in range(n_chain):
        h, r0 = c // sub, (c % sub) * rc
        acc = acc_sc[c]
        o_ref[r0:r0 + rc, h * LANES:(h + 1) * LANES] = (acc[:, :HEAD_DIM] / acc[:, HEAD_DIM:]).astype(o_ref.dtype)


def _moba(proj, tabs, batch, seq):
    tq = 2 * MOBA_BLOCK
    nt = seq // tq
    width = MOBA_HEADS * HEAD_DIM
    full = lambda a: pl.BlockSpec(a.shape, lambda b, i: (0,) * a.ndim)
    return pl.pallas_call(
        functools.partial(_moba_kernel, tq=tq, seq=seq, scale=HEAD_DIM ** -0.5),
        grid=(batch, nt),
        in_specs=[pl.BlockSpec((tq, width), lambda b, i: (b * nt + i, COL512["mq"])),
                  pl.BlockSpec((seq, width), lambda b, i: (b, COL512["mk"])),
                  pl.BlockSpec((seq, width), lambda b, i: (b, COL512["mv"])),
                  full(tabs[0]), full(tabs[1]), full(tabs[2])],
        out_specs=pl.BlockSpec((tq, width), lambda b, i: (b * nt + i, 0)),
        out_shape=jax.ShapeDtypeStruct((batch * seq, width), BF16),
        scratch_shapes=[pltpu.VMEM((seq, 2 * width), BF16), pltpu.VMEM((seq, 2 * width), BF16),
                        pltpu.VMEM((LANES, LANES), F32),
                        pltpu.VMEM((MOBA_HEADS * tq // MOBA_BLOCK, MOBA_BLOCK, 2 * HEAD_DIM), F32)],
        compiler_params=_cparams("parallel", "arbitrary"),
        name="moba",
    )(proj, proj, proj, *tabs)


def _xattn_kernel(q_ref, kv_ref, o_ref):
    d = q_ref.shape[1]
    hd = d // MEM_HEADS
    head = lambda h: slice(h * hd, (h + 1) * hd)
    scores = [_dot_nt(q_ref[:, head(h)], kv_ref[0, :, head(h)]) for h in range(MEM_HEADS)]
    for h, s in enumerate(scores):
        hs = head(h)
        e = jnp.exp(s - jnp.max(s, axis=-1, keepdims=True))
        p = e / jnp.sum(e, axis=-1, keepdims=True)
        o_ref[:, hs] = _dot(p.astype(BF16), kv_ref[0, :, d + h * hd:d + (h + 1) * hd]).astype(o_ref.dtype)


def _xattn(q, kv, batch, seq, tq=ROW_TILE):
    d = q.shape[1]
    nt = seq // tq
    m_len = kv.shape[0] // batch
    return pl.pallas_call(
        _xattn_kernel,
        grid=(batch, nt),
        in_specs=[pl.BlockSpec((tq, d), lambda b, i: (b * nt + i, 0)),
                  pl.BlockSpec((1, m_len, 2 * d), lambda b, i: (b, 0, 0))],
        out_specs=pl.BlockSpec((tq, d), lambda b, i: (b * nt + i, 0)),
        out_shape=jax.ShapeDtypeStruct((batch * seq, d), BF16),
        compiler_params=_cparams("parallel", "parallel"),
        name="xattn",
    )(q, kv.reshape(batch, m_len, 2 * d))


IN_SPLIT_NAMES = ("c_q", "c_kv", "k_rope", "nq", "nkc", "nvc", "nks", "nvs", "nkw", "nvw", "ng", "mq", "mk", "mv")
IN_SPLIT_SIZES = (MLA_Q_RANK, MLA_KV_RANK, MLA_ROPE, WIDE) + (HEAD_DIM,) * 6 + (3 * NSA_HEADS, WIDE, WIDE, WIDE)


def _pack_w_in_kernel(w_ref, o_ref):
    rows = w_ref.shape[0]
    off = 0
    for name, sz in zip(IN_SPLIT_NAMES, IN_SPLIT_SIZES):
        width = WIDE if name in COL512 else LANES
        dst = COL512[name] * WIDE if name in COL512 else COL128[name] * LANES
        x = w_ref[:, off:off + sz].astype(BF16)
        if sz < width:
            x = jnp.concatenate([x, jnp.zeros((rows, width - sz), BF16)], axis=1)
        o_ref[0, :, dst:dst + width] = x
        off += sz


def _pack_w_in(w, tr=256):
    layers, d, width = w.shape
    return pl.pallas_call(
        _pack_w_in_kernel,
        grid=(layers, d // tr),
        in_specs=[pl.BlockSpec((None, tr, width), lambda l, i: (l, i, 0))],
        out_specs=pl.BlockSpec((1, tr, PROJ_WIDTH), lambda l, i: (l, i, 0)),
        out_shape=jax.ShapeDtypeStruct((layers, d, PROJ_WIDTH), BF16),
        compiler_params=_cparams("parallel", "parallel"),
        name="pack_w_in",
    )(w)


def _pack_w_uq(w):
    r = w.shape[0]
    w = w.reshape(r, MLA_HEADS, MLA_NOPE + MLA_ROPE)
    w = jnp.pad(w, ((0, 0), (0, 0), (0, MLA_SLOT - MLA_NOPE - MLA_ROPE)))
    return w.reshape(r, MLA_HEADS * MLA_SLOT).astype(BF16)


def _pack_w_ukv(w):
    r = w.shape[0]
    w = w.reshape(r, MLA_HEADS, 2, HEAD_DIM).transpose(0, 2, 1, 3)
    return w.reshape(r, 2 * MLA_HEADS * HEAD_DIM).astype(BF16)


def kernel(x, mem, ln_in_g, ln_in_b, w_in, mla_q_norm, mla_kv_norm, mla_w_uq, mla_w_ukv, nsa_cmp_w1, nsa_cmp_w2, nsa_cmp_pos, w_out, ln1_g, ln1_b, mem_wq, mem_wkv, mem_wo, ln2_g, ln2_b, mlp_w1, mlp_w2, ln3_g, ln3_b):
    batch, seq, d = x.shape
    n = batch * seq
    mla_tabs = _rope_tables(seq, MLA_ROPE)
    rot_tabs = _rope_tables(seq, PARTIAL_ROT)
    mem2 = mem.reshape(batch * mem.shape[1], d)
    w_in_packed = _pack_w_in(w_in)
    w_out_bf, mem_wkv_bf, mem_wo_bf = w_out.astype(BF16), mem_wkv.astype(BF16), mem_wo.astype(BF16)
    mem_wq_bf = (mem_wq * (d // MEM_HEADS) ** -0.5).astype(BF16)
    mlp_w1_bf, mlp_w2_bf = mlp_w1.astype(BF16), mlp_w2.astype(BF16)

    h = _layer_norm(x.reshape(n, d), ln_in_g, ln_in_b)
    for l in range(DEPTH):
        proj = _matmul(h, w_in_packed, l, BF16, name="in_proj")
        q, k, v = _mla_up(proj, mla_q_norm[l], mla_kv_norm[l], _pack_w_uq(mla_w_uq[l]), _pack_w_ukv(mla_w_ukv[l]),
                          mla_tabs, batch, seq)
        o_a = _mla_attn(q, k, v).reshape(n, MLA_HEADS * HEAD_DIM)
        o_b = _nsa(proj, nsa_cmp_w1[l].astype(BF16), nsa_cmp_w2[l].astype(BF16), nsa_cmp_pos[l], rot_tabs, batch, seq)
        o_c = _moba(proj, rot_tabs, batch, seq)
        h = _out_ln([o_a, o_b, o_c], w_out_bf, l, h, ln1_g[l], ln1_b[l], name="mix_out_ln")

        xq = _matmul(h, mem_wq_bf, l, BF16, name="mem_q")
        xkv = _matmul(mem2, mem_wkv_bf, l, BF16, name="mem_kv")
        ctx = _xattn(xq, xkv, batch, seq)
        h = _out_ln([ctx], mem_wo_bf, l, h, ln2_g[l], ln2_b[l], name="mem_out_ln")

        h = _mlp_ln(h, mlp_w1_bf, mlp_w2_bf, l, ln3_g[l], ln3_b[l])
    return h.reshape(batch, seq, d)
```

```python
import functools

import numpy as np
import jax
import jax.numpy as jnp
from jax import lax
from jax.experimental import pallas as pl
from jax.experimental.pallas import tpu as pltpu

F32 = jnp.float32
BF16 = jnp.bfloat16
NEG_INF = float("-inf")

D_MODEL = 2048
DEPTH = 2
HEAD_DIM = 128
MLA_HEADS = 8
NSA_HEADS = 4
MOBA_HEADS = 4
ROPE_THETA = 500000.0
PARTIAL_ROT = HEAD_DIM // 4
MLA_Q_RANK = 512
MLA_KV_RANK = 512
MLA_NOPE = 128
MLA_ROPE = 64
MLA_SLOT = 256
NSA_CMP_LEN = 32
NSA_CMP_STRIDE = 16
NSA_SEL_LEN = 64
NSA_SEL_TOPK = 16
NSA_WINDOW = 512
NSA_FORCE_SCORE = 1.0e4
MOBA_BLOCK = 256
MOBA_TOPK = 3
MEM_HEADS = 4
DEEPNORM_ALPHA = (2 * DEPTH) ** 0.25
LANES = 128
NSA_SEL_SHIFT = NSA_SEL_LEN.bit_length() - 1
MOBA_SHIFT = MOBA_BLOCK.bit_length() - 1

PROJ_WIDTH = 4096
WIDE = 4 * HEAD_DIM
COL512 = dict(c_q=0, c_kv=1, nq=2, mq=3, mk=4, mv=5)
COL128 = dict(k_rope=24, nkc=25, nvc=26, nks=27, nvs=28, nkw=29, nvw=30, ng=31)

VMEM_LIMIT = 56 * 1024 * 1024
ROW_TILE = 512
CHAIN_ROWS = 256


def _cparams(*sem):
    return pltpu.CompilerParams(dimension_semantics=sem, vmem_limit_bytes=VMEM_LIMIT)


def _dot(a, b):
    return jnp.dot(a, b, preferred_element_type=F32)


def _dot_nt(a, b):
    return lax.dot_general(a, b, (((1,), (1,)), ((), ())), preferred_element_type=F32)


def _split_bf16(x):
    hi = x.astype(BF16)
    return hi, (x - hi.astype(F32)).astype(BF16)


def _ln_rows(x, g, b, eps=1e-5):
    mu = jnp.mean(x, axis=-1, keepdims=True)
    xc = x - mu
    var = jnp.mean(xc * xc, axis=-1, keepdims=True)
    return xc * lax.rsqrt(var + eps) * g + b


def _rms_rows(x, g, eps=1e-6):
    return x * lax.rsqrt(jnp.mean(x * x, axis=-1, keepdims=True) + eps) * g


def _rope128(x, c, s1, s2, half):
    return x * c + pltpu.roll(x, LANES - half, 1) * s1 + pltpu.roll(x, half, 1) * s2


def _rope_tables(n_pos, dim):
    half = dim // 2
    inv = ROPE_THETA ** (-jnp.arange(0, dim, 2, dtype=F32) / dim)
    ang = jnp.arange(n_pos, dtype=F32)[:, None] * inv[None, :]
    cos, sin = jnp.cos(ang), jnp.sin(ang)
    ones = jnp.ones((n_pos, LANES - dim), F32)
    z = lambda w: jnp.zeros((n_pos, w), F32)
    c = jnp.concatenate([cos, cos, ones], axis=1)
    s1 = jnp.concatenate([-sin, z(LANES - half)], axis=1)
    s2 = jnp.concatenate([z(half), sin, z(LANES - dim)], axis=1)
    return c, s1, s2


MASKED = -1.0e30


def _rank_rows(score, n_cand):
    row = lax.broadcasted_iota(jnp.int32, score.shape, 0)
    rank = jnp.zeros(score.shape, F32)
    for jp in range(n_cand):
        cand = score[jp:jp + 1, :]
        ahead = (cand > score) | ((cand == score) & (row > jp))
        rank = rank + jnp.where(ahead, 1.0, 0.0)
    return rank


def _rows_to_lanes(x_t, tq):
    eye = jnp.where(lax.broadcasted_iota(jnp.int32, (tq, tq), 0) == lax.broadcasted_iota(jnp.int32, (tq, tq), 1),
                    1.0, 0.0).astype(BF16)
    return _dot_nt(eye, x_t.astype(BF16))


def _flash_update(s, m, acc, v_aug):
    m_new = jnp.maximum(m, jnp.max(s, axis=-1, keepdims=True))
    p = jnp.exp(s - m_new).astype(BF16)
    return m_new, jnp.exp(m - m_new) * acc + _dot(p, v_aug)


def _ln_kernel(x_ref, g_ref, b_ref, o_ref):
    o_ref[...] = _ln_rows(x_ref[...], g_ref[...], b_ref[...])


def _layer_norm(x, g, b, tm=ROW_TILE):
    n, d = x.shape
    return pl.pallas_call(
        _ln_kernel,
        grid=(n // tm,),
        in_specs=[pl.BlockSpec((tm, d), lambda i: (i, 0)),
                  pl.BlockSpec((1, d), lambda i: (0, 0)),
                  pl.BlockSpec((1, d), lambda i: (0, 0))],
        out_specs=pl.BlockSpec((tm, d), lambda i: (i, 0)),
        out_shape=jax.ShapeDtypeStruct((n, d), F32),
        compiler_params=_cparams("parallel"),
        name="ln_in",
    )(x, g.reshape(1, d), b.reshape(1, d))


def _mm_kernel(a_ref, w_ref, o_ref):
    o_ref[...] = _dot(a_ref[...].astype(BF16), w_ref[...]).astype(o_ref.dtype)


def _matmul(a, w, layer, out_dtype, tm=1024, tn=2048, name="mm"):
    m, k = a.shape
    n = w.shape[2]
    tm = min(tm, m)
    return pl.pallas_call(
        _mm_kernel,
        grid=(m // tm, n // tn),
        in_specs=[pl.BlockSpec((tm, k), lambda i, j: (i, 0)),
                  pl.BlockSpec((None, k, tn), lambda i, j: (layer, 0, j))],
        out_specs=pl.BlockSpec((tm, tn), lambda i, j: (i, j)),
        out_shape=jax.ShapeDtypeStruct((m, n), out_dtype),
        compiler_params=_cparams("parallel", "arbitrary"),
        name=name,
    )(a, w)


def _out_ln_kernel(*refs, widths):
    n_a = len(widths)
    a_refs = refs[:n_a]
    w_ref, h_ref, g_ref, b_ref, o_ref = refs[n_a:]
    tm = h_ref.shape[0]
    parts = tuple(slice(i * tm // 4, (i + 1) * tm // 4) for i in range(4))
    accs = []
    for rows in parts:
        a = jnp.concatenate([a_ref[rows, :] for a_ref in a_refs], axis=1)
        accs.append(DEEPNORM_ALPHA * h_ref[rows, :] + _dot(a, w_ref[...]))
    for rows, acc in zip(parts, accs):
        o_ref[rows, :] = _ln_rows(acc, g_ref[...], b_ref[...])


def _out_ln(a_list, w, layer, h, g, b, tm=ROW_TILE, name="out_ln"):
    n, d = h.shape
    widths = tuple(a.shape[1] for a in a_list)
    k = sum(widths)
    in_specs = [pl.BlockSpec((tm, wd), lambda i: (i, 0)) for wd in widths]
    in_specs += [pl.BlockSpec((None, k, d), lambda i: (layer, 0, 0)),
                 pl.BlockSpec((tm, d), lambda i: (i, 0)),
                 pl.BlockSpec((1, d), lambda i: (0, 0)),
                 pl.BlockSpec((1, d), lambda i: (0, 0))]
    return pl.pallas_call(
        functools.partial(_out_ln_kernel, widths=widths),
        grid=(n // tm,),
        in_specs=in_specs,
        out_specs=pl.BlockSpec((tm, d), lambda i: (i, 0)),
        out_shape=jax.ShapeDtypeStruct((n, d), F32),
        compiler_params=_cparams("parallel"),
        name=name,
    )(*a_list, w, h, g.reshape(1, d), b.reshape(1, d))


def _mlp_kernel(h_ref, w1_ref, w2_ref, g_ref, b_ref, o_ref, acc_ref):
    f = pl.program_id(1)

    @pl.when(f == 0)
    def _():
        acc_ref[...] = jnp.zeros_like(acc_ref)

    u = jnp.maximum(_dot(h_ref[...].astype(BF16), w1_ref[...]), 0.0)
    acc_ref[...] += _dot((u * u).astype(BF16), w2_ref[...])

    @pl.when(f == pl.num_programs(1) - 1)
    def _():
        y = DEEPNORM_ALPHA * h_ref[...] + acc_ref[...]
        o_ref[...] = _ln_rows(y, g_ref[...], b_ref[...])


def _mlp_ln(h, w1, w2, layer, g, b, tm=ROW_TILE, tf=1024):
    n, d = h.shape
    dff = w1.shape[2]
    return pl.pallas_call(
        _mlp_kernel,
        grid=(n // tm, dff // tf),
        in_specs=[pl.BlockSpec((tm, d), lambda i, f: (i, 0)),
                  pl.BlockSpec((None, d, tf), lambda i, f: (layer, 0, f)),
                  pl.BlockSpec((None, tf, d), lambda i, f: (layer, f, 0)),
                  pl.BlockSpec((1, d), lambda i, f: (0, 0)),
                  pl.BlockSpec((1, d), lambda i, f: (0, 0))],
        out_specs=pl.BlockSpec((tm, d), lambda i, f: (i, 0)),
        out_shape=jax.ShapeDtypeStruct((n, d), F32),
        scratch_shapes=[pltpu.VMEM((tm, d), F32)],
        compiler_params=_cparams("parallel", "arbitrary"),
        name="mlp_ln",
    )(h, w1, w2, g.reshape(1, d), b.reshape(1, d))


def _mla_up_kernel(cq_ref, ckv_ref, kr_ref, gq_ref, gkv_ref, wq_ref, wkv_ref, tc_ref, ts1_ref, ts2_ref,
                   q_ref, k_ref, v_ref, *, scale):
    half = MLA_ROPE // 2
    c, s1, s2 = tc_ref[...], ts1_ref[...], ts2_ref[...]
    nq = _rms_rows(cq_ref[...].astype(F32), gq_ref[...]).astype(BF16)
    nkv = _rms_rows(ckv_ref[...].astype(F32), gkv_ref[...]).astype(BF16)
    qf = _dot(nq, wq_ref[...])
    kvf = _dot(nkv, wkv_ref[...])
    kr = _rope128(kr_ref[...].astype(F32), c, s1, s2, half).astype(BF16)
    for h in range(MLA_HEADS):
        o = h * MLA_SLOT
        q_ref[0, h, :, 0:LANES] = (qf[:, o:o + LANES] * scale).astype(BF16)
        q_ref[0, h, :, LANES:] = (_rope128(qf[:, o + LANES:o + MLA_SLOT], c, s1, s2, half) * scale).astype(BF16)
        k_ref[0, h, :, 0:LANES] = kvf[:, h * LANES:(h + 1) * LANES].astype(BF16)
        k_ref[0, h, :, LANES:] = kr
        v_ref[0, h, :, 0:LANES] = kvf[:, (MLA_HEADS + h) * LANES:(MLA_HEADS + h + 1) * LANES].astype(BF16)
        v_ref[0, h, :, LANES:] = jnp.ones((kr.shape[0], LANES), BF16)


def _mla_up(proj, gq, gkv, wq, wkv, tabs, batch, seq, tm=ROW_TILE):
    nt = seq // tm
    row = lambda b, i: b * nt + i
    tab_spec = pl.BlockSpec((tm, LANES), lambda b, i: (i, 0))
    hm = lambda w: pl.BlockSpec((1, MLA_HEADS, tm, w), lambda b, i: (b, 0, i, 0))
    return pl.pallas_call(
        functools.partial(_mla_up_kernel, scale=(MLA_NOPE + MLA_ROPE) ** -0.5),
        grid=(batch, nt),
        in_specs=[pl.BlockSpec((tm, WIDE), lambda b, i: (row(b, i), COL512["c_q"])),
                  pl.BlockSpec((tm, WIDE), lambda b, i: (row(b, i), COL512["c_kv"])),
                  pl.BlockSpec((tm, LANES), lambda b, i: (row(b, i), COL128["k_rope"])),
                  pl.BlockSpec((1, MLA_Q_RANK), lambda b, i: (0, 0)),
                  pl.BlockSpec((1, MLA_KV_RANK), lambda b, i: (0, 0)),
                  pl.BlockSpec(wq.shape, lambda b, i: (0, 0)),
                  pl.BlockSpec(wkv.shape, lambda b, i: (0, 0)),
                  tab_spec, tab_spec, tab_spec],
        out_specs=[hm(MLA_SLOT), hm(MLA_SLOT), hm(2 * HEAD_DIM)],
        out_shape=[jax.ShapeDtypeStruct((batch, MLA_HEADS, seq, MLA_SLOT), BF16),
                   jax.ShapeDtypeStruct((batch, MLA_HEADS, seq, MLA_SLOT), BF16),
                   jax.ShapeDtypeStruct((batch, MLA_HEADS, seq, 2 * HEAD_DIM), BF16)],
        compiler_params=_cparams("parallel", "parallel"),
        name="mla_up",
    )(proj, proj, proj, gq.reshape(1, -1), gkv.reshape(1, -1), wq, wkv, *tabs)


def _mla_attn_kernel(q_ref, k_ref, v_ref, o_ref, acc_sc, *, tq, rs):
    qi = pl.program_id(2)
    heads = q_ref.shape[1]
    causal = lax.broadcasted_iota(jnp.int32, (tq, tq), 1) <= lax.broadcasted_iota(jnp.int32, (tq, tq), 0)
    nrs = tq // rs
    n_chain = heads * nrs

    def step(kb, carry, diag):
        def keys(c):
            nk = (c % nrs + 1) * rs if diag else tq
            return pl.ds(pl.multiple_of(kb * tq, tq), nk), nk

        def scores(c):
            h, r0 = c // nrs, (c % nrs) * rs
            r, nk = keys(c)
            s = _dot_nt(q_ref[0, h, r0:r0 + rs, :], k_ref[0, h, r, :])
            return jnp.where(causal[r0:r0 + rs, :nk], s, MASKED) if diag else s

        out, s_next = [], scores(0)
        for c in range(n_chain):
            s, s_next = s_next, (scores(c + 1) if c + 1 < n_chain else None)
            m_new, acc_sc[c] = _flash_update(s, carry[c], acc_sc[c], v_ref[0, c // nrs, keys(c)[0], :])
            out.append(m_new)
        return tuple(out)

    acc_sc[...] = jnp.zeros(acc_sc.shape, F32)
    carry = tuple(jnp.full((rs, 1), NEG_INF, F32) for _ in range(n_chain))
    carry = lax.fori_loop(0, qi, lambda kb, c: step(kb, c, False), carry)
    step(qi, carry, True)
    for h in range(heads):
        for i in range(nrs):
            acc = acc_sc[h * nrs + i]
            o_ref[0, i * rs:(i + 1) * rs, h * HEAD_DIM:(h + 1) * HEAD_DIM] = (
                acc[:, :HEAD_DIM] / acc[:, HEAD_DIM:]).astype(o_ref.dtype)


def _mla_attn(q, k, v, tq=ROW_TILE, rs=CHAIN_ROWS, hg=MLA_HEADS // 2):
    batch, heads, seq, _ = q.shape
    return pl.pallas_call(
        functools.partial(_mla_attn_kernel, tq=tq, rs=rs),
        grid=(batch, heads // hg, seq // tq),
        in_specs=[pl.BlockSpec((1, hg, tq, MLA_SLOT), lambda b, g, i: (b, g, i, 0)),
                  pl.BlockSpec((1, hg, seq, MLA_SLOT), lambda b, g, i: (b, g, 0, 0)),
                  pl.BlockSpec((1, hg, seq, 2 * HEAD_DIM), lambda b, g, i: (b, g, 0, 0))],
        out_specs=pl.BlockSpec((1, tq, hg * HEAD_DIM), lambda b, g, i: (b, i, g)),
        out_shape=jax.ShapeDtypeStruct((batch, seq, heads * HEAD_DIM), BF16),
        scratch_shapes=[pltpu.VMEM((hg * tq // rs, rs, 2 * HEAD_DIM), F32)],
        compiler_params=_cparams("parallel", "parallel", "arbitrary"),
        name="mla_attn",
    )(q, k, v)


def _nsa_kernel(nq_ref, kc_ref, vc_ref, ks_ref, vs_ref, kw_ref, vw_ref, ng_ref,
                w1_ref, w2_ref, pos_ref, ovt_ref, tc_ref, ts1_ref, ts2_ref,
                o_ref,
                tmp_sc, kc_sc, vc_sc, ks_sc, vs_sc, kw_sc, vw_sc, acc_sc, *, tq, seq, scale):
    qi = pl.program_id(1)
    half = PARTIAL_ROT // 2
    heads = NSA_HEADS
    n_chunk = seq // NSA_CMP_STRIDE
    n_sel = seq // NSA_SEL_LEN

    @pl.when(qi == 0)
    def _prep():
        c, s1, s2 = tc_ref[...], ts1_ref[...], ts2_ref[...]
        ones = jnp.ones((seq, LANES), BF16)
        blk = lax.broadcasted_iota(jnp.int32, (seq, LANES), 0) >> NSA_SEL_SHIFT
        ks_sc[:, 0:LANES] = _rope128(ks_ref[...].astype(F32), c, s1, s2, half).astype(BF16)
        ks_sc[:, LANES:] = jnp.where(blk == lax.broadcasted_iota(jnp.int32, (seq, LANES), 1), 1.0, 0.0).astype(BF16)
        kw_sc[...] = _rope128(kw_ref[...].astype(F32), c, s1, s2, half).astype(BF16)
        vs_sc[:, 0:LANES] = vs_ref[...].astype(BF16)
        vs_sc[:, LANES:] = ones
        vw_sc[:, 0:LANES] = vw_ref[...].astype(BF16)
        vw_sc[:, LANES:] = ones
        for i, (src, dst) in enumerate(((kc_ref, kc_sc), (vc_ref, vc_sc))):
            x_all = src[...].astype(F32)
            tmp_sc[...] = _rope128(x_all, c, s1, s2, half) if i == 0 else x_all
            lo, hi = [], []
            for t in range(NSA_CMP_STRIDE):
                x = tmp_sc[pl.ds(t, n_chunk, stride=NSA_CMP_STRIDE), :]
                lo.append((x + pos_ref[i, t:t + 1, :]).astype(BF16))
                hi.append((x + pos_ref[i, NSA_CMP_STRIDE + t:NSA_CMP_STRIDE + t + 1, :]).astype(BF16))
            kw1 = NSA_CMP_STRIDE * HEAD_DIM
            a = _dot(jnp.concatenate(lo, axis=1), w1_ref[i, 0:kw1, :])
            bm = _dot(jnp.concatenate(hi, axis=1), w1_ref[i, kw1:2 * kw1, :])
            hid = jax.nn.gelu(a + pltpu.roll(bm, n_chunk - 1, 0))
            dst[...] = _dot(hid.astype(BF16), w2_ref[i]).astype(BF16)

    t0 = pl.multiple_of(qi * tq, tq)
    rq = pl.ds(t0, tq)
    c, s1, s2 = tc_ref[rq, :], ts1_ref[rq, :], ts2_ref[rq, :]
    qf = nq_ref[...].astype(F32)
    qs = jnp.concatenate(
        [_rope128(qf[:, h * LANES:(h + 1) * LANES], c, s1, s2, half) * scale for h in range(heads)],
        axis=0)
    qs_bf = qs.astype(BF16)

    lane = lax.broadcasted_iota(jnp.int32, (tq, LANES), 1)
    tpos = t0 + lax.broadcasted_iota(jnp.int32, (tq, LANES), 0)

    s = _dot_nt(qs_bf, kc_sc[...]).reshape(heads, tq, LANES)
    cmask = (lane * NSA_CMP_STRIDE + (NSA_CMP_LEN - 1) <= tpos)[None]
    s = jnp.where(cmask, s, NEG_INF)
    m = jnp.max(s, axis=-1, keepdims=True)
    m = jnp.where(m == NEG_INF, 0.0, m)
    e = jnp.exp(s - m)
    p_cmp = e / jnp.maximum(jnp.sum(e, axis=-1, keepdims=True), 1e-30)
    o_cmp = _dot(p_cmp.reshape(heads * tq, LANES).astype(BF16), vc_sc[...]).reshape(heads, tq, HEAD_DIM)

    p_hi, p_lo = _split_bf16(p_cmp[0] + p_cmp[1] + p_cmp[2] + p_cmp[3])
    imp_t = (_dot_nt(ovt_ref[...], p_hi) + _dot_nt(ovt_ref[...], p_lo))[0:n_sel]
    blk_t = lax.broadcasted_iota(jnp.int32, (n_sel, tq), 0)
    cur_t = (t0 + lax.broadcasted_iota(jnp.int32, (n_sel, tq), 1)) >> NSA_SEL_SHIFT
    eligible = blk_t <= cur_t
    forced = (blk_t == 0) | (blk_t == cur_t) | (blk_t == cur_t - 1)
    score = jnp.where(eligible, jnp.where(forced, NSA_FORCE_SCORE, imp_t), NEG_INF)
    rank = _rank_rows(score, n_sel)
    bias_t = jnp.where(eligible & (rank < NSA_SEL_TOPK), 0.0, MASKED)
    bias_t = jnp.concatenate([bias_t, jnp.zeros((LANES - n_sel, tq), F32)], axis=0)
    bias = _rows_to_lanes(bias_t, tq)
    q_aug = jnp.concatenate([qs, jnp.concatenate([bias] * heads, axis=0)], axis=1).astype(BF16)

    assert tq == NSA_WINDOW
    local_r = lax.broadcasted_iota(jnp.int32, (tq, tq), 0)
    local_c = lax.broadcasted_iota(jnp.int32, (tq, tq), 1)
    causal = local_c <= local_r
    rc = acc_sc.shape[1]
    n_chain = heads * tq // rc

    def run(q_rows, k_sc, v_sc, kb, carry, mask, lower):
        def keys(c):
            r0 = (c * rc) % tq
            k0, nk = (0, tq) if mask is None else ((0, r0 + rc) if lower else (r0, tq - r0))
            return pl.ds(pl.multiple_of(kb * tq + k0, rc), nk), r0, k0, nk

        def scores(c):
            r, r0, k0, nk = keys(c)
            s = _dot_nt(q_rows[c * rc:(c + 1) * rc], k_sc[r, :])
            return s if mask is None else jnp.where(mask[r0:r0 + rc, k0:k0 + nk], s, MASKED)

        out, s_next = [], scores(0)
        for c in range(n_chain):
            s, s_next = s_next, (scores(c + 1) if c + 1 < n_chain else None)
            m_new, acc_sc[c] = _flash_update(s, carry[c], acc_sc[c], v_sc[keys(c)[0], :])
            out.append(m_new)
        return tuple(out)

    def finish():
        acc = jnp.concatenate([acc_sc[c] for c in range(n_chain)], axis=0)
        return (acc[:, :HEAD_DIM] / acc[:, HEAD_DIM:]).reshape(heads, tq, HEAD_DIM)

    init = tuple(jnp.full((rc, 1), NEG_INF, F32) for _ in range(n_chain))

    acc_sc[...] = jnp.zeros(acc_sc.shape, F32)
    carry = lax.fori_loop(0, qi, lambda kb, cr: run(q_aug, ks_sc, vs_sc, kb, cr, None, True), init)
    run(q_aug, ks_sc, vs_sc, qi, carry, causal, True)
    o_sel = finish()

    acc_sc[...] = jnp.zeros(acc_sc.shape, F32)
    carry = run(qs_bf, kw_sc, vw_sc, jnp.maximum(qi - 1, 0), init, (local_c > local_r) & (qi >= 1), False)
    run(qs_bf, kw_sc, vw_sc, qi, carry, causal, True)
    o_win = finish()

    g = jax.nn.sigmoid(ng_ref[...].astype(F32))
    for h in range(heads):
        o = (g[:, 3 * h:3 * h + 1] * o_cmp[h] + g[:, 3 * h + 1:3 * h + 2] * o_sel[h]
             + g[:, 3 * h + 2:3 * h + 3] * o_win[h])
        o_ref[:, h * LANES:(h + 1) * LANES] = o.astype(o_ref.dtype)


def _nsa(proj, w1, w2, pos, tabs, batch, seq, tq=NSA_WINDOW, rc=CHAIN_ROWS):
    nt = seq // tq
    n_cmp = (seq - NSA_CMP_LEN) // NSA_CMP_STRIDE + 1
    n_sel = seq // NSA_SEL_LEN
    starts = np.arange(LANES) * NSA_CMP_STRIDE
    sel_start = np.arange(LANES) * NSA_SEL_LEN
    overlap = ((starts[:, None] < sel_start[None, :] + NSA_SEL_LEN)
               & (starts[:, None] + NSA_CMP_LEN > sel_start[None, :])
               & (np.arange(LANES)[:, None] < n_cmp) & (np.arange(LANES)[None, :] < n_sel))
    ovt = jnp.asarray(overlap.T.astype(np.float32), BF16)
    seq_col = lambda name: pl.BlockSpec((seq, LANES), lambda b, i: (b, COL128[name]))
    full = lambda a: pl.BlockSpec(a.shape, lambda b, i: (0,) * a.ndim)
    return pl.pallas_call(
        functools.partial(_nsa_kernel, tq=tq, seq=seq, scale=HEAD_DIM ** -0.5),
        grid=(batch, nt),
        in_specs=[pl.BlockSpec((tq, WIDE), lambda b, i: (b * nt + i, COL512["nq"])),
                  seq_col("nkc"), seq_col("nvc"), seq_col("nks"), seq_col("nvs"), seq_col("nkw"), seq_col("nvw"),
                  pl.BlockSpec((tq, LANES), lambda b, i: (b * nt + i, COL128["ng"])),
                  full(w1), full(w2), full(pos), full(ovt), full(tabs[0]), full(tabs[1]), full(tabs[2])],
        out_specs=pl.BlockSpec((tq, NSA_HEADS * HEAD_DIM), lambda b, i: (b * nt + i, 0)),
        out_shape=jax.ShapeDtypeStruct((batch * seq, NSA_HEADS * HEAD_DIM), BF16),
        scratch_shapes=[pltpu.VMEM((seq, LANES), F32),
                        pltpu.VMEM((LANES, LANES), BF16), pltpu.VMEM((LANES, LANES), BF16),
                        pltpu.VMEM((seq, 2 * LANES), BF16), pltpu.VMEM((seq, 2 * LANES), BF16),
                        pltpu.VMEM((seq, LANES), BF16), pltpu.VMEM((seq, 2 * LANES), BF16),
                        pltpu.VMEM((NSA_HEADS * tq // rc, rc, 2 * HEAD_DIM), F32)],
        compiler_params=_cparams("parallel", "arbitrary"),
        name="nsa",
    )(proj, proj, proj, proj, proj, proj, proj, proj, w1, w2, pos, ovt, *tabs)


def _moba_kernel(mq_ref, mk_ref, mv_ref, tc_ref, ts1_ref, ts2_ref, o_ref, k_sc, v_sc, km_sc, acc_sc,
                 *, tq, seq, scale):
    qi = pl.program_id(1)
    half = PARTIAL_ROT // 2
    heads = MOBA_HEADS
    n_blk = seq // MOBA_BLOCK
    slot = 2 * LANES

    @pl.when(qi == 0)
    def _prep():
        c, s1, s2 = tc_ref[...], ts1_ref[...], ts2_ref[...]
        km_sc[...] = jnp.zeros_like(km_sc)
        ones = jnp.ones((seq, LANES), BF16)
        blk = lax.broadcasted_iota(jnp.int32, (seq, LANES), 0) >> MOBA_SHIFT
        onehot = jnp.where(blk == lax.broadcasted_iota(jnp.int32, (seq, LANES), 1), 1.0, 0.0).astype(BF16)
        for h in range(heads):
            hs = slice(h * LANES, (h + 1) * LANES)
            kh = _rope128(mk_ref[:, hs].astype(F32), c, s1, s2, half)
            k_sc[:, h * slot:h * slot + LANES] = kh.astype(BF16)
            k_sc[:, h * slot + LANES:(h + 1) * slot] = onehot
            v_sc[:, h * slot:h * slot + LANES] = mv_ref[:, hs].astype(BF16)
            v_sc[:, h * slot + LANES:(h + 1) * slot] = ones
            for j in range(n_blk):
                km_sc[h * n_blk + j:h * n_blk + j + 1, :] = jnp.mean(
                    kh[j * MOBA_BLOCK:(j + 1) * MOBA_BLOCK], axis=0, keepdims=True)

    t0 = pl.multiple_of(qi * tq, tq)
    rq = pl.ds(t0, tq)
    c, s1, s2 = tc_ref[rq, :], ts1_ref[rq, :], ts2_ref[rq, :]
    km_hi, km_lo = _split_bf16(km_sc[...])
    tk = tq
    rc = MOBA_BLOCK
    sub = tq // rc
    blk_t = lax.broadcasted_iota(jnp.int32, (n_blk, tq), 0)
    cur_t = (t0 + lax.broadcasted_iota(jnp.int32, (n_blk, tq), 1)) >> MOBA_SHIFT
    row = lax.broadcasted_iota(jnp.int32, (tq, tk), 0)
    col = lax.broadcasted_iota(jnp.int32, (tq, tk), 1)

    qfs, bias_ts = [], []
    for h in range(heads):
        qf = _rope128(mq_ref[:, h * LANES:(h + 1) * LANES].astype(F32), c, s1, s2, half) * scale
        q_hi, q_lo = _split_bf16(qf)
        gate_t = (_dot_nt(km_hi, q_hi) + (_dot_nt(km_lo, q_hi) + _dot_nt(km_hi, q_lo)))[h * n_blk:(h + 1) * n_blk]
        eligible = blk_t < cur_t
        score = jnp.where(eligible, gate_t, NEG_INF)
        picked = eligible & (_rank_rows(score, n_blk) < MOBA_TOPK)
        bias_t = jnp.where(picked | (blk_t == cur_t), 0.0, MASKED)
        qfs.append(qf)
        bias_ts.append(jnp.concatenate([bias_t, jnp.zeros((LANES - n_blk, tq), F32)], axis=0))
    bias = _rows_to_lanes(jnp.concatenate(bias_ts, axis=0), tq)
    q_aug = [jnp.concatenate([qfs[h], bias[:, h * LANES:(h + 1) * LANES]], axis=1).astype(BF16) for h in range(heads)]

    n_chain = heads * sub

    causal = col <= row

    def step(kb, carry, diag):
        def keys(c):
            nk = (c % sub + 1) * rc if diag else tk
            return pl.ds(pl.multiple_of(kb * tk, tk), nk), nk

        def scores(c):
            h, r0 = c // sub, (c % sub) * rc
            r, nk = keys(c)
            s = _dot_nt(q_aug[h][r0:r0 + rc], k_sc[r, h * slot:(h + 1) * slot])
            return jnp.where(causal[r0:r0 + rc, :nk], s, MASKED) if diag else s

        out, s_next = [], scores(0)
        for c in range(n_chain):
            s, s_next = s_next, (scores(c + 1) if c + 1 < n_chain else None)
            h = c // sub
            m_new, acc_sc[c] = _flash_update(s, carry[c], acc_sc[c], v_sc[keys(c)[0], h * slot:(h + 1) * slot])
            out.append(m_new)
        return tuple(out)

    acc_sc[...] = jnp.zeros(acc_sc.shape, F32)
    carry = tuple(jnp.full((rc, 1), NEG_INF, F32) for _ in range(n_chain))
    carry = lax.fori_loop(0, qi, lambda kb, cr: step(kb, cr, False), carry)
    step(qi, carry, True)
    for c in range(n_chain):
        h, r0 = c // sub, (c % sub) * rc
        acc = acc_sc[c]
        o_ref[r0:r0 + rc, h * LANES:(h + 1) * LANES] = (acc[:, :HEAD_DIM] / acc[:, HEAD_DIM:]).astype(o_ref.dtype)


def _moba(proj, tabs, batch, seq):
    tq = 2 * MOBA_BLOCK
    nt = seq // tq
    width = MOBA_HEADS * HEAD_DIM
    full = lambda a: pl.BlockSpec(a.shape, lambda b, i: (0,) * a.ndim)
    return pl.pallas_call(
        functools.partial(_moba_kernel, tq=tq, seq=seq, scale=HEAD_DIM ** -0.5),
        grid=(batch, nt),
        in_specs=[pl.BlockSpec((tq, width), lambda b, i: (b * nt + i, COL512["mq"])),
                  pl.BlockSpec((seq, width), lambda b, i: (b, COL512["mk"])),
                  pl.BlockSpec((seq, width), lambda b, i: (b, COL512["mv"])),
                  full(tabs[0]), full(tabs[1]), full(tabs[2])],
        out_specs=pl.BlockSpec((tq, width), lambda b, i: (b * nt + i, 0)),
        out_shape=jax.ShapeDtypeStruct((batch * seq, width), BF16),
        scratch_shapes=[pltpu.VMEM((seq, 2 * width), BF16), pltpu.VMEM((seq, 2 * width), BF16),
                        pltpu.VMEM((LANES, LANES), F32),
                        pltpu.VMEM((MOBA_HEADS * tq // MOBA_BLOCK, MOBA_BLOCK, 2 * HEAD_DIM), F32)],
        compiler_params=_cparams("parallel", "arbitrary"),
        name="moba",
    )(proj, proj, proj, *tabs)


def _xattn_kernel(q_ref, kv_ref, o_ref):
    d = q_ref.shape[1]
    hd = d // MEM_HEADS
    head = lambda h: slice(h * hd, (h + 1) * hd)
    scores = [_dot_nt(q_ref[:, head(h)], kv_ref[0, :, head(h)]) for h in range(MEM_HEADS)]
    for h, s in enumerate(scores):
        hs = head(h)
        e = jnp.exp(s - jnp.max(s, axis=-1, keepdims=True))
        p = e / jnp.sum(e, axis=-1, keepdims=True)
        o_ref[:, hs] = _dot(p.astype(BF16), kv_ref[0, :, d + h * hd:d + (h + 1) * hd]).astype(o_ref.dtype)


def _xattn(q, kv, batch, seq, tq=ROW_TILE):
    d = q.shape[1]
    nt = seq // tq
    m_len = kv.shape[0] // batch
    return pl.pallas_call(
        _xattn_kernel,
        grid=(batch, nt),
        in_specs=[pl.BlockSpec((tq, d), lambda b, i: (b * nt + i, 0)),
                  pl.BlockSpec((1, m_len, 2 * d), lambda b, i: (b, 0, 0))],
        out_specs=pl.BlockSpec((tq, d), lambda b, i: (b * nt + i, 0)),
        out_shape=jax.ShapeDtypeStruct((batch * seq, d), BF16),
        compiler_params=_cparams("parallel", "parallel"),
        name="xattn",
    )(q, kv.reshape(batch, m_len, 2 * d))


IN_SPLIT_NAMES = ("c_q", "c_kv", "k_rope", "nq", "nkc", "nvc", "nks", "nvs", "nkw", "nvw", "ng", "mq", "mk", "mv")
IN_SPLIT_SIZES = (MLA_Q_RANK, MLA_KV_RANK, MLA_ROPE, WIDE) + (HEAD_DIM,) * 6 + (3 * NSA_HEADS, WIDE, WIDE, WIDE)


def _pack_w_in_kernel(w_ref, o_ref):
    rows = w_ref.shape[0]
    off = 0
    for name, sz in zip(IN_SPLIT_NAMES, IN_SPLIT_SIZES):
        width = WIDE if name in COL512 else LANES
        dst = COL512[name] * WIDE if name in COL512 else COL128[name] * LANES
        x = w_ref[:, off:off + sz].astype(BF16)
        if sz < width:
            x = jnp.concatenate([x, jnp.zeros((rows, width - sz), BF16)], axis=1)
        o_ref[0, :, dst:dst + width] = x
        off += sz


def _pack_w_in(w, tr=256):
    layers, d, width = w.shape
    return pl.pallas_call(
        _pack_w_in_kernel,
        grid=(layers, d // tr),
        in_specs=[pl.BlockSpec((None, tr, width), lambda l, i: (l, i, 0))],
        out_specs=pl.BlockSpec((1, tr, PROJ_WIDTH), lambda l, i: (l, i, 0)),
        out_shape=jax.ShapeDtypeStruct((layers, d, PROJ_WIDTH), BF16),
        compiler_params=_cparams("parallel", "parallel"),
        name="pack_w_in",
    )(w)


def _pack_w_uq(w):
    r = w.shape[0]
    w = w.reshape(r, MLA_HEADS, MLA_NOPE + MLA_ROPE)
    w = jnp.pad(w, ((0, 0), (0, 0), (0, MLA_SLOT - MLA_NOPE - MLA_ROPE)))
    return w.reshape(r, MLA_HEADS * MLA_SLOT).astype(BF16)


def _pack_w_ukv(w):
    r = w.shape[0]
    w = w.reshape(r, MLA_HEADS, 2, HEAD_DIM).transpose(0, 2, 1, 3)
    return w.reshape(r, 2 * MLA_HEADS * HEAD_DIM).astype(BF16)


def kernel(x, mem, ln_in_g, ln_in_b, w_in, mla_q_norm, mla_kv_norm, mla_w_uq, mla_w_ukv, nsa_cmp_w1, nsa_cmp_w2, nsa_cmp_pos, w_out, ln1_g, ln1_b, mem_wq, mem_wkv, mem_wo, ln2_g, ln2_b, mlp_w1, mlp_w2, ln3_g, ln3_b):
    batch, seq, d = x.shape
    n = batch * seq
    mla_tabs = _rope_tables(seq, MLA_ROPE)
    rot_tabs = _rope_tables(seq, PARTIAL_ROT)
    mem2 = mem.reshape(batch * mem.shape[1], d)
    w_in_packed = _pack_w_in(w_in)
    w_out_bf, mem_wkv_bf, mem_wo_bf = w_out.astype(BF16), mem_wkv.astype(BF16), mem_wo.astype(BF16)
    mem_wq_bf = (mem_wq * (d // MEM_HEADS) ** -0.5).astype(BF16)
    mlp_w1_bf, mlp_w2_bf = mlp_w1.astype(BF16), mlp_w2.astype(BF16)

    h = _layer_norm(x.reshape(n, d), ln_in_g, ln_in_b)
    for l in range(DEPTH):
        proj = _matmul(h, w_in_packed, l, BF16, name="in_proj")
        q, k, v = _mla_up(proj, mla_q_norm[l], mla_kv_norm[l], _pack_w_uq(mla_w_uq[l]), _pack_w_ukv(mla_w_ukv[l]),
                          mla_tabs, batch, seq)
        o_a = _mla_attn(q, k, v).reshape(n, MLA_HEADS * HEAD_DIM)
        o_b = _nsa(proj, nsa_cmp_w1[l].astype(BF16), nsa_cmp_w2[l].astype(BF16), nsa_cmp_pos[l], rot_tabs, batch, seq)
        o_c = _moba(proj, rot_tabs, batch, seq)
        h = _out_ln([o_a, o_b, o_c], w_out_bf, l, h, ln1_g[l], ln1_b[l], name="mix_out_ln")

        xq = _matmul(h, mem_wq_bf, l, BF16, name="mem_q")
        xkv = _matmul(mem2, mem_wkv_bf, l, BF16, name="mem_kv")
        ctx = _xattn(xq, xkv, batch, seq)
        h = _out_ln([ctx], mem_wo_bf, l, h, ln2_g[l], ln2_b[l], name="mem_out_ln")

        h = _mlp_ln(h, mlp_w1_bf, mlp_w2_bf, l, ln3_g[l], ln3_b[l])
    return h.reshape(batch, seq, d)
```

```python
import functools

import numpy as np
import jax
import jax.numpy as jnp
from jax import lax
from jax.experimental import pallas as pl
from jax.experimental.pallas import tpu as pltpu

F32 = jnp.float32
BF16 = jnp.bfloat16
NEG_INF = float("-inf")

D_MODEL = 2048
DEPTH = 2
HEAD_DIM = 128
MLA_HEADS = 8
NSA_HEADS = 4
MOBA_HEADS = 4
ROPE_THETA = 500000.0
PARTIAL_ROT = HEAD_DIM // 4
MLA_Q_RANK = 512
MLA_KV_RANK = 512
MLA_NOPE = 128
MLA_ROPE = 64
MLA_SLOT = 256
NSA_CMP_LEN = 32
NSA_CMP_STRIDE = 16
NSA_SEL_LEN = 64
NSA_SEL_TOPK = 16
NSA_WINDOW = 512
NSA_FORCE_SCORE = 1.0e4
MOBA_BLOCK = 256
MOBA_TOPK = 3
MEM_HEADS = 4
DEEPNORM_ALPHA = (2 * DEPTH) ** 0.25
LANES = 128
NSA_SEL_SHIFT = NSA_SEL_LEN.bit_length() - 1
MOBA_SHIFT = MOBA_BLOCK.bit_length() - 1

PROJ_WIDTH = 4096
WIDE = 4 * HEAD_DIM
COL512 = dict(c_q=0, c_kv=1, nq=2, mq=3, mk=4, mv=5)
COL128 = dict(k_rope=24, nkc=25, nvc=26, nks=27, nvs=28, nkw=29, nvw=30, ng=31)

VMEM_LIMIT = 56 * 1024 * 1024
ROW_TILE = 512
CHAIN_ROWS = 256


def _cparams(*sem):
    return pltpu.CompilerParams(dimension_semantics=sem, vmem_limit_bytes=VMEM_LIMIT)


def _dot(a, b):
    return jnp.dot(a, b, preferred_element_type=F32)


def _dot_nt(a, b):
    return lax.dot_general(a, b, (((1,), (1,)), ((), ())), preferred_element_type=F32)


def _split_bf16(x):
    hi = x.astype(BF16)
    return hi, (x - hi.astype(F32)).astype(BF16)


def _ln_rows(x, g, b, eps=1e-5):
    mu = jnp.mean(x, axis=-1, keepdims=True)
    xc = x - mu
    var = jnp.mean(xc * xc, axis=-1, keepdims=True)
    return xc * lax.rsqrt(var + eps) * g + b


def _rms_rows(x, g, eps=1e-6):
    return x * lax.rsqrt(jnp.mean(x * x, axis=-1, keepdims=True) + eps) * g


def _rope128(x, c, s1, s2, half):
    return x * c + pltpu.roll(x, LANES - half, 1) * s1 + pltpu.roll(x, half, 1) * s2


def _rope_tables(n_pos, dim):
    half = dim // 2
    inv = ROPE_THETA ** (-jnp.arange(0, dim, 2, dtype=F32) / dim)
    ang = jnp.arange(n_pos, dtype=F32)[:, None] * inv[None, :]
    cos, sin = jnp.cos(ang), jnp.sin(ang)
    ones = jnp.ones((n_pos, LANES - dim), F32)
    z = lambda w: jnp.zeros((n_pos, w), F32)
    c = jnp.concatenate([cos, cos, ones], axis=1)
    s1 = jnp.concatenate([-sin, z(LANES - half)], axis=1)
    s2 = jnp.concatenate([z(half), sin, z(LANES - dim)], axis=1)
    return c, s1, s2


MASKED = -1.0e30


def _rank_rows(score, n_cand):
    row = lax.broadcasted_iota(jnp.int32, score.shape, 0)
    rank = jnp.zeros(score.shape, F32)
    for jp in range(n_cand):
        cand = score[jp:jp + 1, :]
        ahead = (cand > score) | ((cand == score) & (row > jp))
        rank = rank + jnp.where(ahead, 1.0, 0.0)
    return rank


def _rows_to_lanes(x_t, tq):
    eye = jnp.where(lax.broadcasted_iota(jnp.int32, (tq, tq), 0) == lax.broadcasted_iota(jnp.int32, (tq, tq), 1),
                    1.0, 0.0).astype(BF16)
    return _dot_nt(eye, x_t.astype(BF16))


def _flash_update(s, m, acc, v_aug):
    row_max = jnp.max(s, axis=-1, keepdims=True)
    if m is None:
        return row_max, _dot(jnp.exp(s - row_max).astype(BF16), v_aug)
    m_new = jnp.maximum(m, row_max)
    p = jnp.exp(s - m_new).astype(BF16)
    return m_new, jnp.exp(m - m_new) * acc + _dot(p, v_aug)


def _ln_kernel(x_ref, g_ref, b_ref, o_ref):
    o_ref[...] = _ln_rows(x_ref[...], g_ref[...], b_ref[...])


def _layer_norm(x, g, b, tm=ROW_TILE):
    n, d = x.shape
    return pl.pallas_call(
        _ln_kernel,
        grid=(n // tm,),
        in_specs=[pl.BlockSpec((tm, d), lambda i: (i, 0)),
                  pl.BlockSpec((1, d), lambda i: (0, 0)),
                  pl.BlockSpec((1, d), lambda i: (0, 0))],
        out_specs=pl.BlockSpec((tm, d), lambda i: (i, 0)),
        out_shape=jax.ShapeDtypeStruct((n, d), F32),
        compiler_params=_cparams("parallel"),
        name="ln_in",
    )(x, g.reshape(1, d), b.reshape(1, d))


def _mm_kernel(a_ref, w_ref, o_ref):
    o_ref[...] = _dot(a_ref[...].astype(BF16), w_ref[...]).astype(o_ref.dtype)


def _matmul(a, w, layer, out_dtype, tm=1024, tn=2048, name="mm"):
    m, k = a.shape
    n = w.shape[2]
    tm = min(tm, m)
    return pl.pallas_call(
        _mm_kernel,
        grid=(m // tm, n // tn),
        in_specs=[pl.BlockSpec((tm, k), lambda i, j: (i, 0)),
                  pl.BlockSpec((None, k, tn), lambda i, j: (layer, 0, j))],
        out_specs=pl.BlockSpec((tm, tn), lambda i, j: (i, j)),
        out_shape=jax.ShapeDtypeStruct((m, n), out_dtype),
        compiler_params=_cparams("parallel", "arbitrary"),
        name=name,
    )(a, w)


def _out_ln_kernel(*refs, widths):
    n_a = len(widths)
    a_refs = refs[:n_a]
    w_ref, h_ref, g_ref, b_ref, o_ref = refs[n_a:]
    tm = h_ref.shape[0]
    parts = tuple(slice(i * tm // 4, (i + 1) * tm // 4) for i in range(4))
    accs = []
    for rows in parts:
        a = jnp.concatenate([a_ref[rows, :] for a_ref in a_refs], axis=1)
        accs.append(DEEPNORM_ALPHA * h_ref[rows, :] + _dot(a, w_ref[...]))
    for rows, acc in zip(parts, accs):
        o_ref[rows, :] = _ln_rows(acc, g_ref[...], b_ref[...])


def _out_ln(a_list, w, layer, h, g, b, tm=ROW_TILE, name="out_ln"):
    n, d = h.shape
    widths = tuple(a.shape[1] for a in a_list)
    k = sum(widths)
    in_specs = [pl.BlockSpec((tm, wd), lambda i: (i, 0)) for wd in widths]
    in_specs += [pl.BlockSpec((None, k, d), lambda i: (layer, 0, 0)),
                 pl.BlockSpec((tm, d), lambda i: (i, 0)),
                 pl.BlockSpec((1, d), lambda i: (0, 0)),
                 pl.BlockSpec((1, d), lambda i: (0, 0))]
    return pl.pallas_call(
        functools.partial(_out_ln_kernel, widths=widths),
        grid=(n // tm,),
        in_specs=in_specs,
        out_specs=pl.BlockSpec((tm, d), lambda i: (i, 0)),
        out_shape=jax.ShapeDtypeStruct((n, d), F32),
        compiler_params=_cparams("parallel"),
        name=name,
    )(*a_list, w, h, g.reshape(1, d), b.reshape(1, d))


def _mlp_kernel(h_ref, w1_ref, w2_ref, g_ref, b_ref, o_ref, acc_ref):
    f = pl.program_id(1)

    @pl.when(f == 0)
    def _():
        acc_ref[...] = jnp.zeros_like(acc_ref)

    u = jnp.maximum(_dot(h_ref[...].astype(BF16), w1_ref[...]), 0.0)
    acc_ref[...] += _dot((u * u).astype(BF16), w2_ref[...])

    @pl.when(f == pl.num_programs(1) - 1)
    def _():
        y = DEEPNORM_ALPHA * h_ref[...] + acc_ref[...]
        o_ref[...] = _ln_rows(y, g_ref[...], b_ref[...])


def _mlp_ln(h, w1, w2, layer, g, b, tm=ROW_TILE, tf=1024):
    n, d = h.shape
    dff = w1.shape[2]
    return pl.pallas_call(
        _mlp_kernel,
        grid=(n // tm, dff // tf),
        in_specs=[pl.BlockSpec((tm, d), lambda i, f: (i, 0)),
                  pl.BlockSpec((None, d, tf), lambda i, f: (layer, 0, f)),
                  pl.BlockSpec((None, tf, d), lambda i, f: (layer, f, 0)),
                  pl.BlockSpec((1, d), lambda i, f: (0, 0)),
                  pl.BlockSpec((1, d), lambda i, f: (0, 0))],
        out_specs=pl.BlockSpec((tm, d), lambda i, f: (i, 0)),
        out_shape=jax.ShapeDtypeStruct((n, d), F32),
        scratch_shapes=[pltpu.VMEM((tm, d), F32)],
        compiler_params=_cparams("parallel", "arbitrary"),
        name="mlp_ln",
    )(h, w1, w2, g.reshape(1, d), b.reshape(1, d))


def _mla_up_kernel(cq_ref, ckv_ref, kr_ref, gq_ref, gkv_ref, wq_ref, wkv_ref, tc_ref, ts1_ref, ts2_ref,
                   q_ref, k_ref, v_ref, *, scale):
    half = MLA_ROPE // 2
    c, s1, s2 = tc_ref[...], ts1_ref[...], ts2_ref[...]
    nq = _rms_rows(cq_ref[...].astype(F32), gq_ref[...]).astype(BF16)
    nkv = _rms_rows(ckv_ref[...].astype(F32), gkv_ref[...]).astype(BF16)
    qf = _dot(nq, wq_ref[...])
    kvf = _dot(nkv, wkv_ref[...])
    kr = _rope128(kr_ref[...].astype(F32), c, s1, s2, half).astype(BF16)
    for h in range(MLA_HEADS):
        o = h * MLA_SLOT
        q_ref[0, h, :, 0:LANES] = (qf[:, o:o + LANES] * scale).astype(BF16)
        q_ref[0, h, :, LANES:] = (_rope128(qf[:, o + LANES:o + MLA_SLOT], c, s1, s2, half) * scale).astype(BF16)
        k_ref[0, h, :, 0:LANES] = kvf[:, h * LANES:(h + 1) * LANES].astype(BF16)
        k_ref[0, h, :, LANES:] = kr
        v_ref[0, h, :, 0:LANES] = kvf[:, (MLA_HEADS + h) * LANES:(MLA_HEADS + h + 1) * LANES].astype(BF16)
        v_ref[0, h, :, LANES:] = jnp.ones((kr.shape[0], LANES), BF16)


def _mla_up(proj, gq, gkv, wq, wkv, tabs, batch, seq, tm=ROW_TILE):
    nt = seq // tm
    row = lambda b, i: b * nt + i
    tab_spec = pl.BlockSpec((tm, LANES), lambda b, i: (i, 0))
    hm = lambda w: pl.BlockSpec((1, MLA_HEADS, tm, w), lambda b, i: (b, 0, i, 0))
    return pl.pallas_call(
        functools.partial(_mla_up_kernel, scale=(MLA_NOPE + MLA_ROPE) ** -0.5),
        grid=(batch, nt),
        in_specs=[pl.BlockSpec((tm, WIDE), lambda b, i: (row(b, i), COL512["c_q"])),
                  pl.BlockSpec((tm, WIDE), lambda b, i: (row(b, i), COL512["c_kv"])),
                  pl.BlockSpec((tm, LANES), lambda b, i: (row(b, i), COL128["k_rope"])),
                  pl.BlockSpec((1, MLA_Q_RANK), lambda b, i: (0, 0)),
                  pl.BlockSpec((1, MLA_KV_RANK), lambda b, i: (0, 0)),
                  pl.BlockSpec(wq.shape, lambda b, i: (0, 0)),
                  pl.BlockSpec(wkv.shape, lambda b, i: (0, 0)),
                  tab_spec, tab_spec, tab_spec],
        out_specs=[hm(MLA_SLOT), hm(MLA_SLOT), hm(2 * HEAD_DIM)],
        out_shape=[jax.ShapeDtypeStruct((batch, MLA_HEADS, seq, MLA_SLOT), BF16),
                   jax.ShapeDtypeStruct((batch, MLA_HEADS, seq, MLA_SLOT), BF16),
                   jax.ShapeDtypeStruct((batch, MLA_HEADS, seq, 2 * HEAD_DIM), BF16)],
        compiler_params=_cparams("parallel", "parallel"),
        name="mla_up",
    )(proj, proj, proj, gq.reshape(1, -1), gkv.reshape(1, -1), wq, wkv, *tabs)


def _mla_attn_kernel(q_ref, k_ref, v_ref, o_ref, acc_sc, *, tq, rs):
    qi = pl.program_id(2)
    heads = q_ref.shape[1]
    causal = lax.broadcasted_iota(jnp.int32, (tq, tq), 1) <= lax.broadcasted_iota(jnp.int32, (tq, tq), 0)
    nrs = tq // rs
    n_chain = heads * nrs

    def step(kb, carry, diag):
        def keys(c):
            nk = (c % nrs + 1) * rs if diag else tq
            return pl.ds(pl.multiple_of(kb * tq, tq), nk), nk

        def scores(c):
            h, r0 = c // nrs, (c % nrs) * rs
            r, nk = keys(c)
            s = _dot_nt(q_ref[0, h, r0:r0 + rs, :], k_ref[0, h, r, :])
            return jnp.where(causal[r0:r0 + rs, :nk], s, MASKED) if diag else s

        out, s_next = [], scores(0)
        for c in range(n_chain):
            s, s_next = s_next, (scores(c + 1) if c + 1 < n_chain else None)
            m_old, acc_old = (None, None) if carry is None else (carry[c], acc_sc[c])
            m_new, acc_sc[c] = _flash_update(s, m_old, acc_old, v_ref[0, c // nrs, keys(c)[0], :])
            out.append(m_new)
        return tuple(out)

    carry = step(qi, None, True)
    lax.fori_loop(0, qi, lambda kb, c: step(kb, c, False), carry)
    for h in range(heads):
        for i in range(nrs):
            acc = acc_sc[h * nrs + i]
            o_ref[0, i * rs:(i + 1) * rs, h * HEAD_DIM:(h + 1) * HEAD_DIM] = (
                acc[:, :HEAD_DIM] / acc[:, HEAD_DIM:]).astype(o_ref.dtype)


def _mla_attn(q, k, v, tq=ROW_TILE, rs=CHAIN_ROWS, hg=MLA_HEADS // 2):
    batch, heads, seq, _ = q.shape
    return pl.pallas_call(
        functools.partial(_mla_attn_kernel, tq=tq, rs=rs),
        grid=(batch, heads // hg, seq // tq),
        in_specs=[pl.BlockSpec((1, hg, tq, MLA_SLOT), lambda b, g, i: (b, g, i, 0)),
                  pl.BlockSpec((1, hg, seq, MLA_SLOT), lambda b, g, i: (b, g, 0, 0)),
                  pl.BlockSpec((1, hg, seq, 2 * HEAD_DIM), lambda b, g, i: (b, g, 0, 0))],
        out_specs=pl.BlockSpec((1, tq, hg * HEAD_DIM), lambda b, g, i: (b, i, g)),
        out_shape=jax.ShapeDtypeStruct((batch, seq, heads * HEAD_DIM), BF16),
        scratch_shapes=[pltpu.VMEM((hg * tq // rs, rs, 2 * HEAD_DIM), F32)],
        compiler_params=_cparams("parallel", "parallel", "arbitrary"),
        name="mla_attn",
    )(q, k, v)


def _nsa_kernel(nq_ref, kc_ref, vc_ref, ks_ref, vs_ref, kw_ref, vw_ref, ng_ref,
                w1_ref, w2_ref, pos_ref, ovt_ref, tc_ref, ts1_ref, ts2_ref,
                o_ref,
                tmp_sc, kc_sc, vc_sc, ks_sc, vs_sc, kw_sc, vw_sc, acc_sc, *, tq, seq, scale):
    qi = pl.program_id(1)
    half = PARTIAL_ROT // 2
    heads = NSA_HEADS
    n_chunk = seq // NSA_CMP_STRIDE
    n_sel = seq // NSA_SEL_LEN

    @pl.when(qi == 0)
    def _prep():
        c, s1, s2 = tc_ref[...], ts1_ref[...], ts2_ref[...]
        ones = jnp.ones((seq, LANES), BF16)
        blk = lax.broadcasted_iota(jnp.int32, (seq, LANES), 0) >> NSA_SEL_SHIFT
        ks_sc[:, 0:LANES] = _rope128(ks_ref[...].astype(F32), c, s1, s2, half).astype(BF16)
        ks_sc[:, LANES:] = jnp.where(blk == lax.broadcasted_iota(jnp.int32, (seq, LANES), 1), 1.0, 0.0).astype(BF16)
        kw_sc[...] = _rope128(kw_ref[...].astype(F32), c, s1, s2, half).astype(BF16)
        vs_sc[:, 0:LANES] = vs_ref[...].astype(BF16)
        vs_sc[:, LANES:] = ones
        vw_sc[:, 0:LANES] = vw_ref[...].astype(BF16)
        vw_sc[:, LANES:] = ones
        for i, (src, dst) in enumerate(((kc_ref, kc_sc), (vc_ref, vc_sc))):
            x_all = src[...].astype(F32)
            tmp_sc[...] = _rope128(x_all, c, s1, s2, half) if i == 0 else x_all
            lo, hi = [], []
            for t in range(NSA_CMP_STRIDE):
                x = tmp_sc[pl.ds(t, n_chunk, stride=NSA_CMP_STRIDE), :]
                lo.append((x + pos_ref[i, t:t + 1, :]).astype(BF16))
                hi.append((x + pos_ref[i, NSA_CMP_STRIDE + t:NSA_CMP_STRIDE + t + 1, :]).astype(BF16))
            kw1 = NSA_CMP_STRIDE * HEAD_DIM
            a = _dot(jnp.concatenate(lo, axis=1), w1_ref[i, 0:kw1, :])
            bm = _dot(jnp.concatenate(hi, axis=1), w1_ref[i, kw1:2 * kw1, :])
            hid = jax.nn.gelu(a + pltpu.roll(bm, n_chunk - 1, 0))
            dst[...] = _dot(hid.astype(BF16), w2_ref[i]).astype(BF16)

    t0 = pl.multiple_of(qi * tq, tq)
    rq = pl.ds(t0, tq)
    c, s1, s2 = tc_ref[rq, :], ts1_ref[rq, :], ts2_ref[rq, :]
    qf = nq_ref[...].astype(F32)
    qs = jnp.concatenate(
        [_rope128(qf[:, h * LANES:(h + 1) * LANES], c, s1, s2, half) * scale for h in range(heads)],
        axis=0)
    qs_bf = qs.astype(BF16)

    lane = lax.broadcasted_iota(jnp.int32, (tq, LANES), 1)
    tpos = t0 + lax.broadcasted_iota(jnp.int32, (tq, LANES), 0)

    s = _dot_nt(qs_bf, kc_sc[...]).reshape(heads, tq, LANES)
    cmask = (lane * NSA_CMP_STRIDE + (NSA_CMP_LEN - 1) <= tpos)[None]
    s = jnp.where(cmask, s, NEG_INF)
    m = jnp.max(s, axis=-1, keepdims=True)
    m = jnp.where(m == NEG_INF, 0.0, m)
    e = jnp.exp(s - m)
    p_cmp = e / jnp.maximum(jnp.sum(e, axis=-1, keepdims=True), 1e-30)
    o_cmp = _dot(p_cmp.reshape(heads * tq, LANES).astype(BF16), vc_sc[...]).reshape(heads, tq, HEAD_DIM)

    p_hi, p_lo = _split_bf16(p_cmp[0] + p_cmp[1] + p_cmp[2] + p_cmp[3])
    imp_t = (_dot_nt(ovt_ref[...], p_hi) + _dot_nt(ovt_ref[...], p_lo))[0:n_sel]
    blk_t = lax.broadcasted_iota(jnp.int32, (n_sel, tq), 0)
    cur_t = (t0 + lax.broadcasted_iota(jnp.int32, (n_sel, tq), 1)) >> NSA_SEL_SHIFT
    eligible = blk_t <= cur_t
    forced = (blk_t == 0) | (blk_t == cur_t) | (blk_t == cur_t - 1)
    score = jnp.where(eligible, jnp.where(forced, NSA_FORCE_SCORE, imp_t), NEG_INF)
    rank = _rank_rows(score, n_sel)
    bias_t = jnp.where(eligible & (rank < NSA_SEL_TOPK), 0.0, MASKED)
    bias_t = jnp.concatenate([bias_t, jnp.zeros((LANES - n_sel, tq), F32)], axis=0)
    bias = _rows_to_lanes(bias_t, tq)
    q_aug = jnp.concatenate([qs, jnp.concatenate([bias] * heads, axis=0)], axis=1).astype(BF16)

    assert tq == NSA_WINDOW
    local_r = lax.broadcasted_iota(jnp.int32, (tq, tq), 0)
    local_c = lax.broadcasted_iota(jnp.int32, (tq, tq), 1)
    causal = local_c <= local_r
    rc = acc_sc.shape[1]
    n_chain = heads * tq // rc

    def run(q_rows, k_sc, v_sc, kb, carry, mask, lower):
        def keys(c):
            r0 = (c * rc) % tq
            k0, nk = (0, tq) if mask is None else ((0, r0 + rc) if lower else (r0, tq - r0))
            return pl.ds(pl.multiple_of(kb * tq + k0, rc), nk), r0, k0, nk

        def scores(c):
            r, r0, k0, nk = keys(c)
            s = _dot_nt(q_rows[c * rc:(c + 1) * rc], k_sc[r, :])
            return s if mask is None else jnp.where(mask[r0:r0 + rc, k0:k0 + nk], s, MASKED)

        out, s_next = [], scores(0)
        for c in range(n_chain):
            s, s_next = s_next, (scores(c + 1) if c + 1 < n_chain else None)
            m_new, acc_sc[c] = _flash_update(s, carry[c], acc_sc[c], v_sc[keys(c)[0], :])
            out.append(m_new)
        return tuple(out)

    def finish():
        acc = jnp.concatenate([acc_sc[c] for c in range(n_chain)], axis=0)
        return (acc[:, :HEAD_DIM] / acc[:, HEAD_DIM:]).reshape(heads, tq, HEAD_DIM)

    init = tuple(jnp.full((rc, 1), NEG_INF, F32) for _ in range(n_chain))

    acc_sc[...] = jnp.zeros(acc_sc.shape, F32)
    carry = lax.fori_loop(0, qi, lambda kb, cr: run(q_aug, ks_sc, vs_sc, kb, cr, None, True), init)
    run(q_aug, ks_sc, vs_sc, qi, carry, causal, True)
    o_sel = finish()

    acc_sc[...] = jnp.zeros(acc_sc.shape, F32)
    carry = run(qs_bf, kw_sc, vw_sc, jnp.maximum(qi - 1, 0), init, (local_c > local_r) & (qi >= 1), False)
    run(qs_bf, kw_sc, vw_sc, qi, carry, causal, True)
    o_win = finish()

    g = jax.nn.sigmoid(ng_ref[...].astype(F32))
    for h in range(heads):
        o = (g[:, 3 * h:3 * h + 1] * o_cmp[h] + g[:, 3 * h + 1:3 * h + 2] * o_sel[h]
             + g[:, 3 * h + 2:3 * h + 3] * o_win[h])
        o_ref[:, h * LANES:(h + 1) * LANES] = o.astype(o_ref.dtype)


def _nsa(proj, w1, w2, pos, tabs, batch, seq, tq=NSA_WINDOW, rc=CHAIN_ROWS):
    nt = seq // tq
    n_cmp = (seq - NSA_CMP_LEN) // NSA_CMP_STRIDE + 1
    n_sel = seq // NSA_SEL_LEN
    starts = np.arange(LANES) * NSA_CMP_STRIDE
    sel_start = np.arange(LANES) * NSA_SEL_LEN
    overlap = ((starts[:, None] < sel_start[None, :] + NSA_SEL_LEN)
               & (starts[:, None] + NSA_CMP_LEN > sel_start[None, :])
               & (np.arange(LANES)[:, None] < n_cmp) & (np.arange(LANES)[None, :] < n_sel))
    ovt = jnp.asarray(overlap.T.astype(np.float32), BF16)
    seq_col = lambda name: pl.BlockSpec((seq, LANES), lambda b, i: (b, COL128[name]))
    full = lambda a: pl.BlockSpec(a.shape, lambda b, i: (0,) * a.ndim)
    return pl.pallas_call(
        functools.partial(_nsa_kernel, tq=tq, seq=seq, scale=HEAD_DIM ** -0.5),
        grid=(batch, nt),
        in_specs=[pl.BlockSpec((tq, WIDE), lambda b, i: (b * nt + i, COL512["nq"])),
                  seq_col("nkc"), seq_col("nvc"), seq_col("nks"), seq_col("nvs"), seq_col("nkw"), seq_col("nvw"),
                  pl.BlockSpec((tq, LANES), lambda b, i: (b * nt + i, COL128["ng"])),
                  full(w1), full(w2), full(pos), full(ovt), full(tabs[0]), full(tabs[1]), full(tabs[2])],
        out_specs=pl.BlockSpec((tq, NSA_HEADS * HEAD_DIM), lambda b, i: (b * nt + i, 0)),
        out_shape=jax.ShapeDtypeStruct((batch * seq, NSA_HEADS * HEAD_DIM), BF16),
        scratch_shapes=[pltpu.VMEM((seq, LANES), F32),
                        pltpu.VMEM((LANES, LANES), BF16), pltpu.VMEM((LANES, LANES), BF16),
                        pltpu.VMEM((seq, 2 * LANES), BF16), pltpu.VMEM((seq, 2 * LANES), BF16),
                        pltpu.VMEM((seq, LANES), BF16), pltpu.VMEM((seq, 2 * LANES), BF16),
                        pltpu.VMEM((NSA_HEADS * tq // rc, rc, 2 * HEAD_DIM), F32)],
        compiler_params=_cparams("parallel", "arbitrary"),
        name="nsa",
    )(proj, proj, proj, proj, proj, proj, proj, proj, w1, w2, pos, ovt, *tabs)


def _moba_kernel(mq_ref, mk_ref, mv_ref, tc_ref, ts1_ref, ts2_ref, o_ref, k_sc, v_sc, km_sc, acc_sc,
                 *, tq, seq, scale):
    qi = pl.program_id(1)
    half = PARTIAL_ROT // 2
    heads = MOBA_HEADS
    n_blk = seq // MOBA_BLOCK
    slot = 2 * LANES

    @pl.when(qi == 0)
    def _prep():
        c, s1, s2 = tc_ref[...], ts1_ref[...], ts2_ref[...]
        km_sc[...] = jnp.zeros_like(km_sc)
        ones = jnp.ones((seq, LANES), BF16)
        blk = lax.broadcasted_iota(jnp.int32, (seq, LANES), 0) >> MOBA_SHIFT
        onehot = jnp.where(blk == lax.broadcasted_iota(jnp.int32, (seq, LANES), 1), 1.0, 0.0).astype(BF16)
        for h in range(heads):
            hs = slice(h * LANES, (h + 1) * LANES)
            kh = _rope128(mk_ref[:, hs].astype(F32), c, s1, s2, half)
            k_sc[:, h * slot:h * slot + LANES] = kh.astype(BF16)
            k_sc[:, h * slot + LANES:(h + 1) * slot] = onehot
            v_sc[:, h * slot:h * slot + LANES] = mv_ref[:, hs].astype(BF16)
            v_sc[:, h * slot + LANES:(h + 1) * slot] = ones
            for j in range(n_blk):
                km_sc[h * n_blk + j:h * n_blk + j + 1, :] = jnp.mean(
                    kh[j * MOBA_BLOCK:(j + 1) * MOBA_BLOCK], axis=0, keepdims=True)

    t0 = pl.multiple_of(qi * tq, tq)
    rq = pl.ds(t0, tq)
    c, s1, s2 = tc_ref[rq, :], ts1_ref[rq, :], ts2_ref[rq, :]
    km_hi, km_lo = _split_bf16(km_sc[...])
    tk = tq
    rc = MOBA_BLOCK
    sub = tq // rc
    blk_t = lax.broadcasted_iota(jnp.int32, (n_blk, tq), 0)
    cur_t = (t0 + lax.broadcasted_iota(jnp.int32, (n_blk, tq), 1)) >> MOBA_SHIFT
    row = lax.broadcasted_iota(jnp.int32, (tq, tk), 0)
    col = lax.broadcasted_iota(jnp.int32, (tq, tk), 1)

    qfs, bias_ts = [], []
    for h in range(heads):
        qf = _rope128(mq_ref[:, h * LANES:(h + 1) * LANES].astype(F32), c, s1, s2, half) * scale
        q_hi, q_lo = _split_bf16(qf)
        gate_t = (_dot_nt(km_hi, q_hi) + (_dot_nt(km_lo, q_hi) + _dot_nt(km_hi, q_lo)))[h * n_blk:(h + 1) * n_blk]
        eligible = blk_t < cur_t
        score = jnp.where(eligible, gate_t, NEG_INF)
        picked = eligible & (_rank_rows(score, n_blk) < MOBA_TOPK)
        bias_t = jnp.where(picked | (blk_t == cur_t), 0.0, MASKED)
        qfs.append(qf)
        bias_ts.append(jnp.concatenate([bias_t, jnp.zeros((LANES - n_blk, tq), F32)], axis=0))
    bias = _rows_to_lanes(jnp.concatenate(bias_ts, axis=0), tq)
    q_aug = [jnp.concatenate([qfs[h], bias[:, h * LANES:(h + 1) * LANES]], axis=1).astype(BF16) for h in range(heads)]

    n_chain = heads * sub

    causal = col <= row

    def step(kb, carry, diag):
        def keys(c):
            nk = (c % sub + 1) * rc if diag else tk
            return pl.ds(pl.multiple_of(kb * tk, tk), nk), nk

        def scores(c):
            h, r0 = c // sub, (c % sub) * rc
            r, nk = keys(c)
            s = _dot_nt(q_aug[h][r0:r0 + rc], k_sc[r, h * slot:(h + 1) * slot])
            return jnp.where(causal[r0:r0 + rc, :nk], s, MASKED) if diag else s

        out, s_next = [], scores(0)
        for c in range(n_chain):
            s, s_next = s_next, (scores(c + 1) if c + 1 < n_chain else None)
            h = c // sub
            m_old, acc_old = (None, None) if carry is None else (carry[c], acc_sc[c])
            m_new, acc_sc[c] = _flash_update(s, m_old, acc_old, v_sc[keys(c)[0], h * slot:(h + 1) * slot])
            out.append(m_new)
        return tuple(out)

    carry = step(qi, None, True)
    lax.fori_loop(0, qi, lambda kb, cr: step(kb, cr, False), carry)
    for c in range(n_chain):
        h, r0 = c // sub, (c % sub) * rc
        acc = acc_sc[c]
        o_ref[r0:r0 + rc, h * LANES:(h + 1) * LANES] = (acc[:, :HEAD_DIM] / acc[:, HEAD_DIM:]).astype(o_ref.dtype)


def _moba(proj, tabs, batch, seq):
    tq = 2 * MOBA_BLOCK
    nt = seq // tq
    width = MOBA_HEADS * HEAD_DIM
    full = lambda a: pl.BlockSpec(a.shape, lambda b, i: (0,) * a.ndim)
    return pl.pallas_call(
        functools.partial(_moba_kernel, tq=tq, seq=seq, scale=HEAD_DIM ** -0.5),
        grid=(batch, nt),
        in_specs=[pl.BlockSpec((tq, width), lambda b, i: (b * nt + i, COL512["mq"])),
                  pl.BlockSpec((seq, width), lambda b, i: (b, COL512["mk"])),
                  pl.BlockSpec((seq, width), lambda b, i: (b, COL512["mv"])),
                  full(tabs[0]), full(tabs[1]), full(tabs[2])],
        out_specs=pl.BlockSpec((tq, width), lambda b, i: (b * nt + i, 0)),
        out_shape=jax.ShapeDtypeStruct((batch * seq, width), BF16),
        scratch_shapes=[pltpu.VMEM((seq, 2 * width), BF16), pltpu.VMEM((seq, 2 * width), BF16),
                        pltpu.VMEM((LANES, LANES), F32),
                        pltpu.VMEM((MOBA_HEADS * tq // MOBA_BLOCK, MOBA_BLOCK, 2 * HEAD_DIM), F32)],
        compiler_params=_cparams("parallel", "arbitrary"),
        name="moba",
    )(proj, proj, proj, *tabs)


def _xattn_kernel(q_ref, kv_ref, o_ref):
    d = q_ref.shape[1]
    hd = d // MEM_HEADS
    head = lambda h: slice(h * hd, (h + 1) * hd)
    scores = [_dot_nt(q_ref[:, head(h)], kv_ref[0, :, head(h)]) for h in range(MEM_HEADS)]
    for h, s in enumerate(scores):
        hs = head(h)
        e = jnp.exp(s - jnp.max(s, axis=-1, keepdims=True))
        p = e / jnp.sum(e, axis=-1, keepdims=True)
        o_ref[:, hs] = _dot(p.astype(BF16), kv_ref[0, :, d + h * hd:d + (h + 1) * hd]).astype(o_ref.dtype)


def _xattn(q, kv, batch, seq, tq=ROW_TILE):
    d = q.shape[1]
    nt = seq // tq
    m_len = kv.shape[0] // batch
    return pl.pallas_call(
        _xattn_kernel,
        grid=(batch, nt),
        in_specs=[pl.BlockSpec((tq, d), lambda b, i: (b * nt + i, 0)),
                  pl.BlockSpec((1, m_len, 2 * d), lambda b, i: (b, 0, 0))],
        out_specs=pl.BlockSpec((tq, d), lambda b, i: (b * nt + i, 0)),
        out_shape=jax.ShapeDtypeStruct((batch * seq, d), BF16),
        compiler_params=_cparams("parallel", "parallel"),
        name="xattn",
    )(q, kv.reshape(batch, m_len, 2 * d))


IN_SPLIT_NAMES = ("c_q", "c_kv", "k_rope", "nq", "nkc", "nvc", "nks", "nvs", "nkw", "nvw", "ng", "mq", "mk", "mv")
IN_SPLIT_SIZES = (MLA_Q_RANK, MLA_KV_RANK, MLA_ROPE, WIDE) + (HEAD_DIM,) * 6 + (3 * NSA_HEADS, WIDE, WIDE, WIDE)


def _pack_w_in_kernel(w_ref, o_ref):
    rows = w_ref.shape[0]
    off = 0
    for name, sz in zip(IN_SPLIT_NAMES, IN_SPLIT_SIZES):
        width = WIDE if name in COL512 else LANES
        dst = COL512[name] * WIDE if name in COL512 else COL128[name] * LANES
        x = w_ref[:, off:off + sz].astype(BF16)
        if sz < width:
            x = jnp.concatenate([x, jnp.zeros((rows, width - sz), BF16)], axis=1)
        o_ref[0, :, dst:dst + width] = x
        off += sz


def _pack_w_in(w, tr=256):
    layers, d, width = w.shape
    return pl.pallas_call(
        _pack_w_in_kernel,
        grid=(layers, d // tr),
        in_specs=[pl.BlockSpec((None, tr, width), lambda l, i: (l, i, 0))],
        out_specs=pl.BlockSpec((1, tr, PROJ_WIDTH), lambda l, i: (l, i, 0)),
        out_shape=jax.ShapeDtypeStruct((layers, d, PROJ_WIDTH), BF16),
        compiler_params=_cparams("parallel", "parallel"),
        name="pack_w_in",
    )(w)


def _pack_w_uq(w):
    r = w.shape[0]
    w = w.reshape(r, MLA_HEADS, MLA_NOPE + MLA_ROPE)
    w = jnp.pad(w, ((0, 0), (0, 0), (0, MLA_SLOT - MLA_NOPE - MLA_ROPE)))
    return w.reshape(r, MLA_HEADS * MLA_SLOT).astype(BF16)


def _pack_w_ukv(w):
    r = w.shape[0]
    w = w.reshape(r, MLA_HEADS, 2, HEAD_DIM).transpose(0, 2, 1, 3)
    return w.reshape(r, 2 * MLA_HEADS * HEAD_DIM).astype(BF16)


def kernel(x, mem, ln_in_g, ln_in_b, w_in, mla_q_norm, mla_kv_norm, mla_w_uq, mla_w_ukv, nsa_cmp_w1, nsa_cmp_w2, nsa_cmp_pos, w_out, ln1_g, ln1_b, mem_wq, mem_wkv, mem_wo, ln2_g, ln2_b, mlp_w1, mlp_w2, ln3_g, ln3_b):
    batch, seq, d = x.shape
    n = batch * seq
    mla_tabs = _rope_tables(seq, MLA_ROPE)
    rot_tabs = _rope_tables(seq, PARTIAL_ROT)
    mem2 = mem.reshape(batch * mem.shape[1], d)
    w_in_packed = _pack_w_in(w_in)
    w_out_bf, mem_wkv_bf, mem_wo_bf = w_out.astype(BF16), mem_wkv.astype(BF16), mem_wo.astype(BF16)
    mem_wq_bf = (mem_wq * (d // MEM_HEADS) ** -0.5).astype(BF16)
    mlp_w1_bf, mlp_w2_bf = mlp_w1.astype(BF16), mlp_w2.astype(BF16)

    h = _layer_norm(x.reshape(n, d), ln_in_g, ln_in_b)
    for l in range(DEPTH):
        proj = _matmul(h, w_in_packed, l, BF16, name="in_proj")
        q, k, v = _mla_up(proj, mla_q_norm[l], mla_kv_norm[l], _pack_w_uq(mla_w_uq[l]), _pack_w_ukv(mla_w_ukv[l]),
                          mla_tabs, batch, seq)
        o_a = _mla_attn(q, k, v).reshape(n, MLA_HEADS * HEAD_DIM)
        o_b = _nsa(proj, nsa_cmp_w1[l].astype(BF16), nsa_cmp_w2[l].astype(BF16), nsa_cmp_pos[l], rot_tabs, batch, seq)
        o_c = _moba(proj, rot_tabs, batch, seq)
        h = _out_ln([o_a, o_b, o_c], w_out_bf, l, h, ln1_g[l], ln1_b[l], name="mix_out_ln")

        xq = _matmul(h, mem_wq_bf, l, BF16, name="mem_q")
        xkv = _matmul(mem2, mem_wkv_bf, l, BF16, name="mem_kv")
        ctx = _xattn(xq, xkv, batch, seq)
        h = _out_ln([ctx], mem_wo_bf, l, h, ln2_g[l], ln2_b[l], name="mem_out_ln")

        h = _mlp_ln(h, mlp_w1_bf, mlp_w2_bf, l, ln3_g[l], ln3_b[l])
    return h.reshape(batch, seq, d)
```

```python
import functools

import numpy as np
import jax
import jax.numpy as jnp
from jax import lax
from jax.experimental import pallas as pl
from jax.experimental.pallas import tpu as pltpu

F32 = jnp.float32
BF16 = jnp.bfloat16
NEG_INF = float("-inf")

D_MODEL = 2048
DEPTH = 2
HEAD_DIM = 128
MLA_HEADS = 8
NSA_HEADS = 4
MOBA_HEADS = 4
ROPE_THETA = 500000.0
PARTIAL_ROT = HEAD_DIM // 4
MLA_Q_RANK = 512
MLA_KV_RANK = 512
MLA_NOPE = 128
MLA_ROPE = 64
MLA_SLOT = 256
NSA_CMP_LEN = 32
NSA_CMP_STRIDE = 16
NSA_SEL_LEN = 64
NSA_SEL_TOPK = 16
NSA_WINDOW = 512
NSA_FORCE_SCORE = 1.0e4
MOBA_BLOCK = 256
MOBA_TOPK = 3
MEM_HEADS = 4
DEEPNORM_ALPHA = (2 * DEPTH) ** 0.25
LANES = 128
NSA_SEL_SHIFT = NSA_SEL_LEN.bit_length() - 1
MOBA_SHIFT = MOBA_BLOCK.bit_length() - 1

PROJ_WIDTH = 4096
WIDE = 4 * HEAD_DIM
COL512 = dict(c_q=0, c_kv=1, nq=2, mq=3, mk=4, mv=5)
COL128 = dict(k_rope=24, nkc=25, nvc=26, nks=27, nvs=28, nkw=29, nvw=30, ng=31)

VMEM_LIMIT = 56 * 1024 * 1024
ROW_TILE = 512
CHAIN_ROWS = 256


def _cparams(*sem):
    return pltpu.CompilerParams(dimension_semantics=sem, vmem_limit_bytes=VMEM_LIMIT)


def _dot(a, b):
    return jnp.dot(a, b, preferred_element_type=F32)


def _dot_nt(a, b):
    return lax.dot_general(a, b, (((1,), (1,)), ((), ())), preferred_element_type=F32)


def _split_bf16(x):
    hi = x.astype(BF16)
    return hi, (x - hi.astype(F32)).astype(BF16)


def _ln_rows(x, g, b, eps=1e-5):
    mu = jnp.mean(x, axis=-1, keepdims=True)
    xc = x - mu
    var = jnp.mean(xc * xc, axis=-1, keepdims=True)
    return xc * lax.rsqrt(var + eps) * g + b


def _rms_rows(x, g, eps=1e-6):
    return x * lax.rsqrt(jnp.mean(x * x, axis=-1, keepdims=True) + eps) * g


def _rope128(x, c, s1, s2, half):
    return x * c + pltpu.roll(x, LANES - half, 1) * s1 + pltpu.roll(x, half, 1) * s2


def _rope_tables(n_pos, dim):
    half = dim // 2
    inv = ROPE_THETA ** (-jnp.arange(0, dim, 2, dtype=F32) / dim)
    ang = jnp.arange(n_pos, dtype=F32)[:, None] * inv[None, :]
    cos, sin = jnp.cos(ang), jnp.sin(ang)
    ones = jnp.ones((n_pos, LANES - dim), F32)
    z = lambda w: jnp.zeros((n_pos, w), F32)
    c = jnp.concatenate([cos, cos, ones], axis=1)
    s1 = jnp.concatenate([-sin, z(LANES - half)], axis=1)
    s2 = jnp.concatenate([z(half), sin, z(LANES - dim)], axis=1)
    return c, s1, s2


MASKED = -1.0e30


def _rank_rows(score, n_cand):
    row = lax.broadcasted_iota(jnp.int32, score.shape, 0)
    rank = jnp.zeros(score.shape, F32)
    for jp in range(n_cand):
        cand = score[jp:jp + 1, :]
        ahead = (cand > score) | ((cand == score) & (row > jp))
        rank = rank + jnp.where(ahead, 1.0, 0.0)
    return rank


def _rows_to_lanes(x_t, tq):
    eye = jnp.where(lax.broadcasted_iota(jnp.int32, (tq, tq), 0) == lax.broadcasted_iota(jnp.int32, (tq, tq), 1),
                    1.0, 0.0).astype(BF16)
    return _dot_nt(eye, x_t.astype(BF16))


def _flash_update(s, m, acc, v_aug):
    row_max = jnp.max(s, axis=-1, keepdims=True)
    if m is None:
        return row_max, _dot(jnp.exp(s - row_max).astype(BF16), v_aug)
    m_new = jnp.maximum(m, row_max)
    p = jnp.exp(s - m_new).astype(BF16)
    return m_new, jnp.exp(m - m_new) * acc + _dot(p, v_aug)


def _ln_kernel(x_ref, g_ref, b_ref, o_ref):
    o_ref[...] = _ln_rows(x_ref[...], g_ref[...], b_ref[...])


def _layer_norm(x, g, b, tm=ROW_TILE):
    n, d = x.shape
    return pl.pallas_call(
        _ln_kernel,
        grid=(n // tm,),
        in_specs=[pl.BlockSpec((tm, d), lambda i: (i, 0)),
                  pl.BlockSpec((1, d), lambda i: (0, 0)),
                  pl.BlockSpec((1, d), lambda i: (0, 0))],
        out_specs=pl.BlockSpec((tm, d), lambda i: (i, 0)),
        out_shape=jax.ShapeDtypeStruct((n, d), F32),
        compiler_params=_cparams("parallel"),
        name="ln_in",
    )(x, g.reshape(1, d), b.reshape(1, d))


def _mm_kernel(a_ref, w_ref, o_ref):
    o_ref[...] = _dot(a_ref[...].astype(BF16), w_ref[...]).astype(o_ref.dtype)


def _matmul(a, w, layer, out_dtype, tm=1024, tn=2048, name="mm"):
    m, k = a.shape
    n = w.shape[2]
    tm = min(tm, m)
    return pl.pallas_call(
        _mm_kernel,
        grid=(m // tm, n // tn),
        in_specs=[pl.BlockSpec((tm, k), lambda i, j: (i, 0)),
                  pl.BlockSpec((None, k, tn), lambda i, j: (layer, 0, j))],
        out_specs=pl.BlockSpec((tm, tn), lambda i, j: (i, j)),
        out_shape=jax.ShapeDtypeStruct((m, n), out_dtype),
        compiler_params=_cparams("parallel", "arbitrary"),
        name=name,
    )(a, w)


def _out_ln_kernel(*refs, widths):
    n_a = len(widths)
    a_refs = refs[:n_a]
    w_ref, h_ref, g_ref, b_ref, o_ref = refs[n_a:]
    tm = h_ref.shape[0]
    parts = tuple(slice(i * tm // 4, (i + 1) * tm // 4) for i in range(4))
    accs = []
    for rows in parts:
        a = jnp.concatenate([a_ref[rows, :] for a_ref in a_refs], axis=1)
        accs.append(DEEPNORM_ALPHA * h_ref[rows, :] + _dot(a, w_ref[...]))
    for rows, acc in zip(parts, accs):
        o_ref[rows, :] = _ln_rows(acc, g_ref[...], b_ref[...])


def _out_ln(a_list, w, layer, h, g, b, tm=ROW_TILE, name="out_ln"):
    n, d = h.shape
    widths = tuple(a.shape[1] for a in a_list)
    k = sum(widths)
    in_specs = [pl.BlockSpec((tm, wd), lambda i: (i, 0)) for wd in widths]
    in_specs += [pl.BlockSpec((None, k, d), lambda i: (layer, 0, 0)),
                 pl.BlockSpec((tm, d), lambda i: (i, 0)),
                 pl.BlockSpec((1, d), lambda i: (0, 0)),
                 pl.BlockSpec((1, d), lambda i: (0, 0))]
    return pl.pallas_call(
        functools.partial(_out_ln_kernel, widths=widths),
        grid=(n // tm,),
        in_specs=in_specs,
        out_specs=pl.BlockSpec((tm, d), lambda i: (i, 0)),
        out_shape=jax.ShapeDtypeStruct((n, d), F32),
        compiler_params=_cparams("parallel"),
        name=name,
    )(*a_list, w, h, g.reshape(1, d), b.reshape(1, d))


def _mlp_kernel(h_ref, w1_ref, w2_ref, g_ref, b_ref, o_ref, acc_ref):
    f = pl.program_id(1)

    @pl.when(f == 0)
    def _():
        acc_ref[...] = jnp.zeros_like(acc_ref)

    u = jnp.maximum(_dot(h_ref[...].astype(BF16), w1_ref[...]), 0.0)
    acc_ref[...] += _dot((u * u).astype(BF16), w2_ref[...])

    @pl.when(f == pl.num_programs(1) - 1)
    def _():
        y = DEEPNORM_ALPHA * h_ref[...] + acc_ref[...]
        o_ref[...] = _ln_rows(y, g_ref[...], b_ref[...])


def _mlp_ln(h, w1, w2, layer, g, b, tm=ROW_TILE, tf=1024):
    n, d = h.shape
    dff = w1.shape[2]
    return pl.pallas_call(
        _mlp_kernel,
        grid=(n // tm, dff // tf),
        in_specs=[pl.BlockSpec((tm, d), lambda i, f: (i, 0)),
                  pl.BlockSpec((None, d, tf), lambda i, f: (layer, 0, f)),
                  pl.BlockSpec((None, tf, d), lambda i, f: (layer, f, 0)),
                  pl.BlockSpec((1, d), lambda i, f: (0, 0)),
                  pl.BlockSpec((1, d), lambda i, f: (0, 0))],
        out_specs=pl.BlockSpec((tm, d), lambda i, f: (i, 0)),
        out_shape=jax.ShapeDtypeStruct((n, d), F32),
        scratch_shapes=[pltpu.VMEM((tm, d), F32)],
        compiler_params=_cparams("parallel", "arbitrary"),
        name="mlp_ln",
    )(h, w1, w2, g.reshape(1, d), b.reshape(1, d))


def _mla_up_kernel(cq_ref, ckv_ref, kr_ref, gq_ref, gkv_ref, wq_ref, wkv_ref, tc_ref, ts1_ref, ts2_ref,
                   q_ref, k_ref, v_ref, *, scale):
    half = MLA_ROPE // 2
    c, s1, s2 = tc_ref[...], ts1_ref[...], ts2_ref[...]
    nq = _rms_rows(cq_ref[...].astype(F32), gq_ref[...]).astype(BF16)
    nkv = _rms_rows(ckv_ref[...].astype(F32), gkv_ref[...]).astype(BF16)
    qf = _dot(nq, wq_ref[...])
    kvf = _dot(nkv, wkv_ref[...])
    kr = _rope128(kr_ref[...].astype(F32), c, s1, s2, half).astype(BF16)
    for h in range(MLA_HEADS):
        o = h * MLA_SLOT
        q_ref[0, h, :, 0:LANES] = (qf[:, o:o + LANES] * scale).astype(BF16)
        q_ref[0, h, :, LANES:] = (_rope128(qf[:, o + LANES:o + MLA_SLOT], c, s1, s2, half) * scale).astype(BF16)
        k_ref[0, h, :, 0:LANES] = kvf[:, h * LANES:(h + 1) * LANES].astype(BF16)
        k_ref[0, h, :, LANES:] = kr
        v_ref[0, h, :, 0:LANES] = kvf[:, (MLA_HEADS + h) * LANES:(MLA_HEADS + h + 1) * LANES].astype(BF16)
        v_ref[0, h, :, LANES:] = jnp.ones((kr.shape[0], LANES), BF16)


def _mla_up(proj, gq, gkv, wq, wkv, tabs, batch, seq, tm=ROW_TILE):
    nt = seq // tm
    row = lambda b, i: b * nt + i
    tab_spec = pl.BlockSpec((tm, LANES), lambda b, i: (i, 0))
    hm = lambda w: pl.BlockSpec((1, MLA_HEADS, tm, w), lambda b, i: (b, 0, i, 0))
    return pl.pallas_call(
        functools.partial(_mla_up_kernel, scale=(MLA_NOPE + MLA_ROPE) ** -0.5),
        grid=(batch, nt),
        in_specs=[pl.BlockSpec((tm, WIDE), lambda b, i: (row(b, i), COL512["c_q"])),
                  pl.BlockSpec((tm, WIDE), lambda b, i: (row(b, i), COL512["c_kv"])),
                  pl.BlockSpec((tm, LANES), lambda b, i: (row(b, i), COL128["k_rope"])),
                  pl.BlockSpec((1, MLA_Q_RANK), lambda b, i: (0, 0)),
                  pl.BlockSpec((1, MLA_KV_RANK), lambda b, i: (0, 0)),
                  pl.BlockSpec(wq.shape, lambda b, i: (0, 0)),
                  pl.BlockSpec(wkv.shape, lambda b, i: (0, 0)),
                  tab_spec, tab_spec, tab_spec],
        out_specs=[hm(MLA_SLOT), hm(MLA_SLOT), hm(2 * HEAD_DIM)],
        out_shape=[jax.ShapeDtypeStruct((batch, MLA_HEADS, seq, MLA_SLOT), BF16),
                   jax.ShapeDtypeStruct((batch, MLA_HEADS, seq, MLA_SLOT), BF16),
                   jax.ShapeDtypeStruct((batch, MLA_HEADS, seq, 2 * HEAD_DIM), BF16)],
        compiler_params=_cparams("parallel", "parallel"),
        name="mla_up",
    )(proj, proj, proj, gq.reshape(1, -1), gkv.reshape(1, -1), wq, wkv, *tabs)


def _mla_attn_kernel(q_ref, k_ref, v_ref, o_ref, acc_sc, *, tq, rs):
    qi = pl.program_id(2)
    heads = q_ref.shape[1]
    causal = lax.broadcasted_iota(jnp.int32, (tq, tq), 1) <= lax.broadcasted_iota(jnp.int32, (tq, tq), 0)
    nrs = tq // rs
    n_chain = heads * nrs

    def step(kb, carry, diag):
        def keys(c):
            nk = (c % nrs + 1) * rs if diag else tq
            return pl.ds(pl.multiple_of(kb * tq, tq), nk), nk

        def scores(c):
            h, r0 = c // nrs, (c % nrs) * rs
            r, nk = keys(c)
            s = _dot_nt(q_ref[0, h, r0:r0 + rs, :], k_ref[0, h, r, :])
            return jnp.where(causal[r0:r0 + rs, :nk], s, MASKED) if diag else s

        out, s_next = [], scores(0)
        for c in range(n_chain):
            s, s_next = s_next, (scores(c + 1) if c + 1 < n_chain else None)
            m_old, acc_old = (None, None) if carry is None else (carry[c], acc_sc[c])
            m_new, acc_sc[c] = _flash_update(s, m_old, acc_old, v_ref[0, c // nrs, keys(c)[0], :])
            out.append(m_new)
        return tuple(out)

    carry = step(qi, None, True)
    lax.fori_loop(0, qi, lambda kb, c: step(kb, c, False), carry)
    for h in range(heads):
        for i in range(nrs):
            acc = acc_sc[h * nrs + i]
            o_ref[0, i * rs:(i + 1) * rs, h * HEAD_DIM:(h + 1) * HEAD_DIM] = (
                acc[:, :HEAD_DIM] / acc[:, HEAD_DIM:]).astype(o_ref.dtype)


def _mla_attn(q, k, v, tq=ROW_TILE, rs=CHAIN_ROWS, hg=MLA_HEADS // 2):
    batch, heads, seq, _ = q.shape
    return pl.pallas_call(
        functools.partial(_mla_attn_kernel, tq=tq, rs=rs),
        grid=(batch, heads // hg, seq // tq),
        in_specs=[pl.BlockSpec((1, hg, tq, MLA_SLOT), lambda b, g, i: (b, g, i, 0)),
                  pl.BlockSpec((1, hg, seq, MLA_SLOT), lambda b, g, i: (b, g, 0, 0)),
                  pl.BlockSpec((1, hg, seq, 2 * HEAD_DIM), lambda b, g, i: (b, g, 0, 0))],
        out_specs=pl.BlockSpec((1, tq, hg * HEAD_DIM), lambda b, g, i: (b, i, g)),
        out_shape=jax.ShapeDtypeStruct((batch, seq, heads * HEAD_DIM), BF16),
        scratch_shapes=[pltpu.VMEM((hg * tq // rs, rs, 2 * HEAD_DIM), F32)],
        compiler_params=_cparams("parallel", "parallel", "arbitrary"),
        name="mla_attn",
    )(q, k, v)


def _nsa_kernel(nq_ref, kc_ref, vc_ref, ks_ref, vs_ref, kw_ref, vw_ref, ng_ref,
                w1_ref, w2_ref, pos_ref, ovt_ref, tc_ref, ts1_ref, ts2_ref,
                o_ref,
                tmp_sc, kc_sc, vc_sc, ks_sc, vs_sc, kw_sc, vw_sc, acc_sc, *, tq, seq, scale):
    qi = pl.program_id(1)
    half = PARTIAL_ROT // 2
    heads = NSA_HEADS
    n_chunk = seq // NSA_CMP_STRIDE
    n_sel = seq // NSA_SEL_LEN

    @pl.when(qi == 0)
    def _prep():
        c, s1, s2 = tc_ref[...], ts1_ref[...], ts2_ref[...]
        ones = jnp.ones((seq, LANES), BF16)
        blk = lax.broadcasted_iota(jnp.int32, (seq, LANES), 0) >> NSA_SEL_SHIFT
        ks_sc[:, 0:LANES] = _rope128(ks_ref[...].astype(F32), c, s1, s2, half).astype(BF16)
        ks_sc[:, LANES:] = jnp.where(blk == lax.broadcasted_iota(jnp.int32, (seq, LANES), 1), 1.0, 0.0).astype(BF16)
        kw_sc[...] = _rope128(kw_ref[...].astype(F32), c, s1, s2, half).astype(BF16)
        vs_sc[:, 0:LANES] = vs_ref[...].astype(BF16)
        vs_sc[:, LANES:] = ones
        vw_sc[:, 0:LANES] = vw_ref[...].astype(BF16)
        vw_sc[:, LANES:] = ones
        for i, (src, dst) in enumerate(((kc_ref, kc_sc), (vc_ref, vc_sc))):
            x_all = src[...].astype(F32)
            tmp_sc[...] = _rope128(x_all, c, s1, s2, half) if i == 0 else x_all
            lo, hi = [], []
            for t in range(NSA_CMP_STRIDE):
                x = tmp_sc[pl.ds(t, n_chunk, stride=NSA_CMP_STRIDE), :]
                lo.append((x + pos_ref[i, t:t + 1, :]).astype(BF16))
                hi.append((x + pos_ref[i, NSA_CMP_STRIDE + t:NSA_CMP_STRIDE + t + 1, :]).astype(BF16))
            kw1 = NSA_CMP_STRIDE * HEAD_DIM
            a = _dot(jnp.concatenate(lo, axis=1), w1_ref[i, 0:kw1, :])
            bm = _dot(jnp.concatenate(hi, axis=1), w1_ref[i, kw1:2 * kw1, :])
            hid = jax.nn.gelu(a + pltpu.roll(bm, n_chunk - 1, 0))
            dst[...] = _dot(hid.astype(BF16), w2_ref[i]).astype(BF16)

    t0 = pl.multiple_of(qi * tq, tq)
    rq = pl.ds(t0, tq)
    c, s1, s2 = tc_ref[rq, :], ts1_ref[rq, :], ts2_ref[rq, :]
    qf = nq_ref[...].astype(F32)
    qs = jnp.concatenate(
        [_rope128(qf[:, h * LANES:(h + 1) * LANES], c, s1, s2, half) * scale for h in range(heads)],
        axis=0)
    qs_bf = qs.astype(BF16)

    lane = lax.broadcasted_iota(jnp.int32, (tq, LANES), 1)
    tpos = t0 + lax.broadcasted_iota(jnp.int32, (tq, LANES), 0)

    s = _dot_nt(qs_bf, kc_sc[...]).reshape(heads, tq, LANES)
    cmask = (lane * NSA_CMP_STRIDE + (NSA_CMP_LEN - 1) <= tpos)[None]
    s = jnp.where(cmask, s, NEG_INF)
    m = jnp.max(s, axis=-1, keepdims=True)
    m = jnp.where(m == NEG_INF, 0.0, m)
    e = jnp.exp(s - m)
    p_cmp = e / jnp.maximum(jnp.sum(e, axis=-1, keepdims=True), 1e-30)
    o_cmp = _dot(p_cmp.reshape(heads * tq, LANES).astype(BF16), vc_sc[...]).reshape(heads, tq, HEAD_DIM)

    p_hi, p_lo = _split_bf16(p_cmp[0] + p_cmp[1] + p_cmp[2] + p_cmp[3])
    imp_t = (_dot_nt(ovt_ref[...], p_hi) + _dot_nt(ovt_ref[...], p_lo))[0:n_sel]
    blk_t = lax.broadcasted_iota(jnp.int32, (n_sel, tq), 0)
    cur_t = (t0 + lax.broadcasted_iota(jnp.int32, (n_sel, tq), 1)) >> NSA_SEL_SHIFT
    eligible = blk_t <= cur_t
    forced = (blk_t == 0) | (blk_t == cur_t) | (blk_t == cur_t - 1)
    score = jnp.where(eligible, jnp.where(forced, NSA_FORCE_SCORE, imp_t), NEG_INF)
    rank = _rank_rows(score, n_sel)
    bias_t = jnp.where(eligible & (rank < NSA_SEL_TOPK), 0.0, MASKED)
    bias_t = jnp.concatenate([bias_t, jnp.zeros((LANES - n_sel, tq), F32)], axis=0)
    bias = _rows_to_lanes(bias_t, tq)
    q_aug = jnp.concatenate([qs, jnp.concatenate([bias] * heads, axis=0)], axis=1).astype(BF16)

    assert tq == NSA_WINDOW
    local_r = lax.broadcasted_iota(jnp.int32, (tq, tq), 0)
    local_c = lax.broadcasted_iota(jnp.int32, (tq, tq), 1)
    causal = local_c <= local_r
    rc = acc_sc.shape[1]
    n_chain = heads * tq // rc

    def run(q_rows, k_sc, v_sc, kb, carry, mask, lower):
        def keys(c):
            r0 = (c * rc) % tq
            k0, nk = (0, tq) if mask is None else ((0, r0 + rc) if lower else (r0, tq - r0))
            return pl.ds(pl.multiple_of(kb * tq + k0, rc), nk), r0, k0, nk

        def scores(c):
            r, r0, k0, nk = keys(c)
            s = _dot_nt(q_rows[c * rc:(c + 1) * rc], k_sc[r, :])
            return s if mask is None else jnp.where(mask[r0:r0 + rc, k0:k0 + nk], s, MASKED)

        out, s_next = [], scores(0)
        for c in range(n_chain):
            s, s_next = s_next, (scores(c + 1) if c + 1 < n_chain else None)
            m_new, acc_sc[c] = _flash_update(s, carry[c], acc_sc[c], v_sc[keys(c)[0], :])
            out.append(m_new)
        return tuple(out)

    def finish():
        acc = jnp.concatenate([acc_sc[c] for c in range(n_chain)], axis=0)
        return (acc[:, :HEAD_DIM] / acc[:, HEAD_DIM:]).reshape(heads, tq, HEAD_DIM)

    init = tuple(jnp.full((rc, 1), NEG_INF, F32) for _ in range(n_chain))

    acc_sc[...] = jnp.zeros(acc_sc.shape, F32)
    carry = lax.fori_loop(0, qi, lambda kb, cr: run(q_aug, ks_sc, vs_sc, kb, cr, None, True), init)
    run(q_aug, ks_sc, vs_sc, qi, carry, causal, True)
    o_sel = finish()

    acc_sc[...] = jnp.zeros(acc_sc.shape, F32)
    carry = run(qs_bf, kw_sc, vw_sc, jnp.maximum(qi - 1, 0), init, (local_c > local_r) & (qi >= 1), False)
    run(qs_bf, kw_sc, vw_sc, qi, carry, causal, True)
    o_win = finish()

    g = jax.nn.sigmoid(ng_ref[...].astype(F32))
    for h in range(heads):
        o = (g[:, 3 * h:3 * h + 1] * o_cmp[h] + g[:, 3 * h + 1:3 * h + 2] * o_sel[h]
             + g[:, 3 * h + 2:3 * h + 3] * o_win[h])
        o_ref[:, h * LANES:(h + 1) * LANES] = o.astype(o_ref.dtype)


def _nsa(proj, w1, w2, pos, tabs, batch, seq, tq=NSA_WINDOW, rc=CHAIN_ROWS):
    nt = seq // tq
    n_cmp = (seq - NSA_CMP_LEN) // NSA_CMP_STRIDE + 1
    n_sel = seq // NSA_SEL_LEN
    starts = np.arange(LANES) * NSA_CMP_STRIDE
    sel_start = np.arange(LANES) * NSA_SEL_LEN
    overlap = ((starts[:, None] < sel_start[None, :] + NSA_SEL_LEN)
               & (starts[:, None] + NSA_CMP_LEN > sel_start[None, :])
               & (np.arange(LANES)[:, None] < n_cmp) & (np.arange(LANES)[None, :] < n_sel))
    ovt = jnp.asarray(overlap.T.astype(np.float32), BF16)
    seq_col = lambda name: pl.BlockSpec((seq, LANES), lambda b, i: (b, COL128[name]))
    full = lambda a: pl.BlockSpec(a.shape, lambda b, i: (0,) * a.ndim)
    return pl.pallas_call(
        functools.partial(_nsa_kernel, tq=tq, seq=seq, scale=HEAD_DIM ** -0.5),
        grid=(batch, nt),
        in_specs=[pl.BlockSpec((tq, WIDE), lambda b, i: (b * nt + i, COL512["nq"])),
                  seq_col("nkc"), seq_col("nvc"), seq_col("nks"), seq_col("nvs"), seq_col("nkw"), seq_col("nvw"),
                  pl.BlockSpec((tq, LANES), lambda b, i: (b * nt + i, COL128["ng"])),
                  full(w1), full(w2), full(pos), full(ovt), full(tabs[0]), full(tabs[1]), full(tabs[2])],
        out_specs=pl.BlockSpec((tq, NSA_HEADS * HEAD_DIM), lambda b, i: (b * nt + i, 0)),
        out_shape=jax.ShapeDtypeStruct((batch * seq, NSA_HEADS * HEAD_DIM), BF16),
        scratch_shapes=[pltpu.VMEM((seq, LANES), F32),
                        pltpu.VMEM((LANES, LANES), BF16), pltpu.VMEM((LANES, LANES), BF16),
                        pltpu.VMEM((seq, 2 * LANES), BF16), pltpu.VMEM((seq, 2 * LANES), BF16),
                        pltpu.VMEM((seq, LANES), BF16), pltpu.VMEM((seq, 2 * LANES), BF16),
                        pltpu.VMEM((NSA_HEADS * tq // rc, rc, 2 * HEAD_DIM), F32)],
        compiler_params=_cparams("parallel", "arbitrary"),
        name="nsa",
    )(proj, proj, proj, proj, proj, proj, proj, proj, w1, w2, pos, ovt, *tabs)


def _moba_kernel(mq_ref, mk_ref, mv_ref, tc_ref, ts1_ref, ts2_ref, o_ref, k_sc, v_sc, km_sc, acc_sc,
                 *, tq, seq, scale):
    qi = pl.program_id(1)
    half = PARTIAL_ROT // 2
    heads = MOBA_HEADS
    n_blk = seq // MOBA_BLOCK
    slot = 2 * LANES

    @pl.when(qi == 0)
    def _prep():
        c, s1, s2 = tc_ref[...], ts1_ref[...], ts2_ref[...]
        km_sc[...] = jnp.zeros_like(km_sc)
        ones = jnp.ones((seq, LANES), BF16)
        blk = lax.broadcasted_iota(jnp.int32, (seq, LANES), 0) >> MOBA_SHIFT
        onehot = jnp.where(blk == lax.broadcasted_iota(jnp.int32, (seq, LANES), 1), 1.0, 0.0).astype(BF16)
        for h in range(heads):
            hs = slice(h * LANES, (h + 1) * LANES)
            kh = _rope128(mk_ref[:, hs].astype(F32), c, s1, s2, half)
            k_sc[:, h * slot:h * slot + LANES] = kh.astype(BF16)
            k_sc[:, h * slot + LANES:(h + 1) * slot] = onehot
            v_sc[:, h * slot:h * slot + LANES] = mv_ref[:, hs].astype(BF16)
            v_sc[:, h * slot + LANES:(h + 1) * slot] = ones
            for j in range(n_blk):
                km_sc[h * n_blk + j:h * n_blk + j + 1, :] = jnp.mean(
                    kh[j * MOBA_BLOCK:(j + 1) * MOBA_BLOCK], axis=0, keepdims=True)

    t0 = pl.multiple_of(qi * tq, tq)
    rq = pl.ds(t0, tq)
    c, s1, s2 = tc_ref[rq, :], ts1_ref[rq, :], ts2_ref[rq, :]
    km_hi, km_lo = _split_bf16(km_sc[...])
    tk = tq
    rc = MOBA_BLOCK
    sub = tq // rc
    blk_t = lax.broadcasted_iota(jnp.int32, (n_blk, tq), 0)
    cur_t = (t0 + lax.broadcasted_iota(jnp.int32, (n_blk, tq), 1)) >> MOBA_SHIFT
    row = lax.broadcasted_iota(jnp.int32, (tq, tk), 0)
    col = lax.broadcasted_iota(jnp.int32, (tq, tk), 1)

    qfs, bias_ts = [], []
    for h in range(heads):
        qf = _rope128(mq_ref[:, h * LANES:(h + 1) * LANES].astype(F32), c, s1, s2, half) * scale
        q_hi, q_lo = _split_bf16(qf)
        gate_t = (_dot_nt(km_hi, q_hi) + (_dot_nt(km_lo, q_hi) + _dot_nt(km_hi, q_lo)))[h * n_blk:(h + 1) * n_blk]
        eligible = blk_t < cur_t
        score = jnp.where(eligible, gate_t, NEG_INF)
        picked = eligible & (_rank_rows(score, n_blk) < MOBA_TOPK)
        bias_t = jnp.where(picked | (blk_t == cur_t), 0.0, MASKED)
        qfs.append(qf)
        bias_ts.append(jnp.concatenate([bias_t, jnp.zeros((LANES - n_blk, tq), F32)], axis=0))
    bias = _rows_to_lanes(jnp.concatenate(bias_ts, axis=0), tq)
    q_aug = [jnp.concatenate([qfs[h], bias[:, h * LANES:(h + 1) * LANES]], axis=1).astype(BF16) for h in range(heads)]

    n_chain = heads * sub

    causal = col <= row

    def step(kb, carry, diag):
        def keys(c):
            nk = (c % sub + 1) * rc if diag else tk
            return pl.ds(pl.multiple_of(kb * tk, tk), nk), nk

        def scores(c):
            h, r0 = c // sub, (c % sub) * rc
            r, nk = keys(c)
            s = _dot_nt(q_aug[h][r0:r0 + rc], k_sc[r, h * slot:(h + 1) * slot])
            return jnp.where(causal[r0:r0 + rc, :nk], s, MASKED) if diag else s

        out, s_next = [], scores(0)
        for c in range(n_chain):
            s, s_next = s_next, (scores(c + 1) if c + 1 < n_chain else None)
            h = c // sub
            m_old, acc_old = (None, None) if carry is None else (carry[c], acc_sc[c])
            m_new, acc_sc[c] = _flash_update(s, m_old, acc_old, v_sc[keys(c)[0], h * slot:(h + 1) * slot])
            out.append(m_new)
        return tuple(out)

    carry = step(qi, None, True)
    lax.fori_loop(0, qi, lambda kb, cr: step(kb, cr, False), carry)
    for c in range(n_chain):
        h, r0 = c // sub, (c % sub) * rc
        acc = acc_sc[c]
        o_ref[r0:r0 + rc, h * LANES:(h + 1) * LANES] = (acc[:, :HEAD_DIM] / acc[:, HEAD_DIM:]).astype(o_ref.dtype)


def _moba(proj, tabs, batch, seq):
    tq = 2 * MOBA_BLOCK
    nt = seq // tq
    width = MOBA_HEADS * HEAD_DIM
    full = lambda a: pl.BlockSpec(a.shape, lambda b, i: (0,) * a.ndim)
    return pl.pallas_call(
        functools.partial(_moba_kernel, tq=tq, seq=seq, scale=HEAD_DIM ** -0.5),
        grid=(batch, nt),
        in_specs=[pl.BlockSpec((tq, width), lambda b, i: (b * nt + i, COL512["mq"])),
                  pl.BlockSpec((seq, width), lambda b, i: (b, COL512["mk"])),
                  pl.BlockSpec((seq, width), lambda b, i: (b, COL512["mv"])),
                  full(tabs[0]), full(tabs[1]), full(tabs[2])],
        out_specs=pl.BlockSpec((tq, width), lambda b, i: (b * nt + i, 0)),
        out_shape=jax.ShapeDtypeStruct((batch * seq, width), BF16),
        scratch_shapes=[pltpu.VMEM((seq, 2 * width), BF16), pltpu.VMEM((seq, 2 * width), BF16),
                        pltpu.VMEM((LANES, LANES), F32),
                        pltpu.VMEM((MOBA_HEADS * tq // MOBA_BLOCK, MOBA_BLOCK, 2 * HEAD_DIM), F32)],
        compiler_params=_cparams("parallel", "arbitrary"),
        name="moba",
    )(proj, proj, proj, *tabs)


def _xattn_kernel(q_ref, kv_ref, o_ref):
    d = q_ref.shape[1]
    hd = d // MEM_HEADS
    head = lambda h: slice(h * hd, (h + 1) * hd)
    scores = [_dot_nt(q_ref[:, head(h)], kv_ref[0, :, head(h)]) for h in range(MEM_HEADS)]
    for h, s in enumerate(scores):
        hs = head(h)
        e = jnp.exp(s - jnp.max(s, axis=-1, keepdims=True))
        p = e / jnp.sum(e, axis=-1, keepdims=True)
        o_ref[:, hs] = _dot(p.astype(BF16), kv_ref[0, :, d + h * hd:d + (h + 1) * hd]).astype(o_ref.dtype)


def _xattn(q, kv, batch, seq, tq=4 * ROW_TILE):
    d = q.shape[1]
    nt = seq // tq
    m_len = kv.shape[0] // batch
    return pl.pallas_call(
        _xattn_kernel,
        grid=(batch, nt),
        in_specs=[pl.BlockSpec((tq, d), lambda b, i: (b * nt + i, 0)),
                  pl.BlockSpec((1, m_len, 2 * d), lambda b, i: (b, 0, 0))],
        out_specs=pl.BlockSpec((tq, d), lambda b, i: (b * nt + i, 0)),
        out_shape=jax.ShapeDtypeStruct((batch * seq, d), BF16),
        compiler_params=_cparams("parallel", "parallel"),
        name="xattn",
    )(q, kv.reshape(batch, m_len, 2 * d))


IN_SPLIT_NAMES = ("c_q", "c_kv", "k_rope", "nq", "nkc", "nvc", "nks", "nvs", "nkw", "nvw", "ng", "mq", "mk", "mv")
IN_SPLIT_SIZES = (MLA_Q_RANK, MLA_KV_RANK, MLA_ROPE, WIDE) + (HEAD_DIM,) * 6 + (3 * NSA_HEADS, WIDE, WIDE, WIDE)


def _pack_w_in_kernel(w_ref, o_ref):
    rows = w_ref.shape[0]
    off = 0
    for name, sz in zip(IN_SPLIT_NAMES, IN_SPLIT_SIZES):
        width = WIDE if name in COL512 else LANES
        dst = COL512[name] * WIDE if name in COL512 else COL128[name] * LANES
        x = w_ref[:, off:off + sz].astype(BF16)
        if sz < width:
            x = jnp.concatenate([x, jnp.zeros((rows, width - sz), BF16)], axis=1)
        o_ref[0, :, dst:dst + width] = x
        off += sz


def _pack_w_in(w, tr=256):
    layers, d, width = w.shape
    return pl.pallas_call(
        _pack_w_in_kernel,
        grid=(layers, d // tr),
        in_specs=[pl.BlockSpec((None, tr, width), lambda l, i: (l, i, 0))],
        out_specs=pl.BlockSpec((1, tr, PROJ_WIDTH), lambda l, i: (l, i, 0)),
        out_shape=jax.ShapeDtypeStruct((layers, d, PROJ_WIDTH), BF16),
        compiler_params=_cparams("parallel", "parallel"),
        name="pack_w_in",
    )(w)


def _pack_w_uq(w):
    r = w.shape[0]
    w = w.reshape(r, MLA_HEADS, MLA_NOPE + MLA_ROPE)
    w = jnp.pad(w, ((0, 0), (0, 0), (0, MLA_SLOT - MLA_NOPE - MLA_ROPE)))
    return w.reshape(r, MLA_HEADS * MLA_SLOT).astype(BF16)


def _pack_w_ukv(w):
    r = w.shape[0]
    w = w.reshape(r, MLA_HEADS, 2, HEAD_DIM).transpose(0, 2, 1, 3)
    return w.reshape(r, 2 * MLA_HEADS * HEAD_DIM).astype(BF16)


def kernel(x, mem, ln_in_g, ln_in_b, w_in, mla_q_norm, mla_kv_norm, mla_w_uq, mla_w_ukv, nsa_cmp_w1, nsa_cmp_w2, nsa_cmp_pos, w_out, ln1_g, ln1_b, mem_wq, mem_wkv, mem_wo, ln2_g, ln2_b, mlp_w1, mlp_w2, ln3_g, ln3_b):
    batch, seq, d = x.shape
    n = batch * seq
    mla_tabs = _rope_tables(seq, MLA_ROPE)
    rot_tabs = _rope_tables(seq, PARTIAL_ROT)
    mem2 = mem.reshape(batch * mem.shape[1], d)
    w_in_packed = _pack_w_in(w_in)
    w_out_bf, mem_wkv_bf, mem_wo_bf = w_out.astype(BF16), mem_wkv.astype(BF16), mem_wo.astype(BF16)
    mem_wq_bf = (mem_wq * (d // MEM_HEADS) ** -0.5).astype(BF16)
    mlp_w1_bf, mlp_w2_bf = mlp_w1.astype(BF16), mlp_w2.astype(BF16)

    h = _layer_norm(x.reshape(n, d), ln_in_g, ln_in_b)
    for l in range(DEPTH):
        proj = _matmul(h, w_in_packed, l, BF16, name="in_proj")
        q, k, v = _mla_up(proj, mla_q_norm[l], mla_kv_norm[l], _pack_w_uq(mla_w_uq[l]), _pack_w_ukv(mla_w_ukv[l]),
                          mla_tabs, batch, seq)
        o_a = _mla_attn(q, k, v).reshape(n, MLA_HEADS * HEAD_DIM)
        o_b = _nsa(proj, nsa_cmp_w1[l].astype(BF16), nsa_cmp_w2[l].astype(BF16), nsa_cmp_pos[l], rot_tabs, batch, seq)
        o_c = _moba(proj, rot_tabs, batch, seq)
        h = _out_ln([o_a, o_b, o_c], w_out_bf, l, h, ln1_g[l], ln1_b[l], name="mix_out_ln")

        xq = _matmul(h, mem_wq_bf, l, BF16, name="mem_q")
        xkv = _matmul(mem2, mem_wkv_bf, l, BF16, name="mem_kv")
        ctx = _xattn(xq, xkv, batch, seq)
        h = _out_ln([ctx], mem_wo_bf, l, h, ln2_g[l], ln2_b[l], name="mem_out_ln")

        h = _mlp_ln(h, mlp_w1_bf, mlp_w2_bf, l, ln3_g[l], ln3_b[l])
    return h.reshape(batch, seq, d)
```

```python
import functools

import numpy as np
import jax
import jax.numpy as jnp
from jax import lax
from jax.experimental import pallas as pl
from jax.experimental.pallas import tpu as pltpu

F32 = jnp.float32
BF16 = jnp.bfloat16
NEG_INF = float("-inf")

D_MODEL = 2048
DEPTH = 2
HEAD_DIM = 128
MLA_HEADS = 8
NSA_HEADS = 4
MOBA_HEADS = 4
ROPE_THETA = 500000.0
PARTIAL_ROT = HEAD_DIM // 4
MLA_Q_RANK = 512
MLA_KV_RANK = 512
MLA_NOPE = 128
MLA_ROPE = 64
MLA_SLOT = 256
NSA_CMP_LEN = 32
NSA_CMP_STRIDE = 16
NSA_SEL_LEN = 64
NSA_SEL_TOPK = 16
NSA_WINDOW = 512
NSA_FORCE_SCORE = 1.0e4
MOBA_BLOCK = 256
MOBA_TOPK = 3
MEM_HEADS = 4
DEEPNORM_ALPHA = (2 * DEPTH) ** 0.25
LANES = 128
NSA_SEL_SHIFT = NSA_SEL_LEN.bit_length() - 1
MOBA_SHIFT = MOBA_BLOCK.bit_length() - 1

PROJ_WIDTH = 4096
WIDE = 4 * HEAD_DIM
COL512 = dict(c_q=0, c_kv=1, nq=2, mq=3, mk=4, mv=5)
COL128 = dict(k_rope=24, nkc=25, nvc=26, nks=27, nvs=28, nkw=29, nvw=30, ng=31)

VMEM_LIMIT = 56 * 1024 * 1024
ROW_TILE = 512
CHAIN_ROWS = 256


def _cparams(*sem):
    return pltpu.CompilerParams(dimension_semantics=sem, vmem_limit_bytes=VMEM_LIMIT)


def _dot(a, b):
    return jnp.dot(a, b, preferred_element_type=F32)


def _dot_nt(a, b):
    return lax.dot_general(a, b, (((1,), (1,)), ((), ())), preferred_element_type=F32)


def _split_bf16(x):
    hi = x.astype(BF16)
    return hi, (x - hi.astype(F32)).astype(BF16)


def _ln_rows(x, g, b, eps=1e-5):
    mu = jnp.mean(x, axis=-1, keepdims=True)
    xc = x - mu
    var = jnp.mean(xc * xc, axis=-1, keepdims=True)
    return xc * lax.rsqrt(var + eps) * g + b


def _rms_rows(x, g, eps=1e-6):
    return x * lax.rsqrt(jnp.mean(x * x, axis=-1, keepdims=True) + eps) * g


def _rope128(x, c, s1, s2, half):
    return x * c + pltpu.roll(x, LANES - half, 1) * s1 + pltpu.roll(x, half, 1) * s2


def _rope_tables(n_pos, dim):
    half = dim // 2
    inv = ROPE_THETA ** (-jnp.arange(0, dim, 2, dtype=F32) / dim)
    ang = jnp.arange(n_pos, dtype=F32)[:, None] * inv[None, :]
    cos, sin = jnp.cos(ang), jnp.sin(ang)
    ones = jnp.ones((n_pos, LANES - dim), F32)
    z = lambda w: jnp.zeros((n_pos, w), F32)
    c = jnp.concatenate([cos, cos, ones], axis=1)
    s1 = jnp.concatenate([-sin, z(LANES - half)], axis=1)
    s2 = jnp.concatenate([z(half), sin, z(LANES - dim)], axis=1)
    return c, s1, s2


MASKED = -1.0e30


def _rank_rows(score, n_cand):
    row = lax.broadcasted_iota(jnp.int32, score.shape, 0)
    rank = jnp.zeros(score.shape, F32)
    for jp in range(n_cand):
        cand = score[jp:jp + 1, :]
        ahead = (cand > score) | ((cand == score) & (row > jp))
        rank = rank + jnp.where(ahead, 1.0, 0.0)
    return rank


def _rows_to_lanes(x_t, tq):
    eye = jnp.where(lax.broadcasted_iota(jnp.int32, (tq, tq), 0) == lax.broadcasted_iota(jnp.int32, (tq, tq), 1),
                    1.0, 0.0).astype(BF16)
    return _dot_nt(eye, x_t.astype(BF16))


def _flash_update(s, m, acc, v_aug):
    row_max = jnp.max(s, axis=-1, keepdims=True)
    if m is None:
        return row_max, _dot(jnp.exp(s - row_max).astype(BF16), v_aug)
    m_new = jnp.maximum(m, row_max)
    p = jnp.exp(s - m_new).astype(BF16)
    return m_new, jnp.exp(m - m_new) * acc + _dot(p, v_aug)


def _ln_kernel(x_ref, g_ref, b_ref, o_ref):
    o_ref[...] = _ln_rows(x_ref[...], g_ref[...], b_ref[...])


def _layer_norm(x, g, b, tm=ROW_TILE):
    n, d = x.shape
    return pl.pallas_call(
        _ln_kernel,
        grid=(n // tm,),
        in_specs=[pl.BlockSpec((tm, d), lambda i: (i, 0)),
                  pl.BlockSpec((1, d), lambda i: (0, 0)),
                  pl.BlockSpec((1, d), lambda i: (0, 0))],
        out_specs=pl.BlockSpec((tm, d), lambda i: (i, 0)),
        out_shape=jax.ShapeDtypeStruct((n, d), F32),
        compiler_params=_cparams("parallel"),
        name="ln_in",
    )(x, g.reshape(1, d), b.reshape(1, d))


def _mm_kernel(a_ref, w_ref, o_ref):
    o_ref[...] = _dot(a_ref[...].astype(BF16), w_ref[...]).astype(o_ref.dtype)


def _matmul(a, w, layer, out_dtype, tm=1024, tn=2048, name="mm"):
    m, k = a.shape
    n = w.shape[2]
    tm = min(tm, m)
    return pl.pallas_call(
        _mm_kernel,
        grid=(m // tm, n // tn),
        in_specs=[pl.BlockSpec((tm, k), lambda i, j: (i, 0)),
                  pl.BlockSpec((None, k, tn), lambda i, j: (layer, 0, j))],
        out_specs=pl.BlockSpec((tm, tn), lambda i, j: (i, j)),
        out_shape=jax.ShapeDtypeStruct((m, n), out_dtype),
        compiler_params=_cparams("parallel", "arbitrary"),
        name=name,
    )(a, w)


def _out_ln_kernel(*refs, widths):
    n_a = len(widths)
    a_refs = refs[:n_a]
    w_ref, h_ref, g_ref, b_ref, o_ref = refs[n_a:]
    tm = h_ref.shape[0]
    parts = tuple(slice(i * tm // 4, (i + 1) * tm // 4) for i in range(4))
    accs = []
    for rows in parts:
        a = jnp.concatenate([a_ref[rows, :] for a_ref in a_refs], axis=1)
        accs.append(DEEPNORM_ALPHA * h_ref[rows, :] + _dot(a, w_ref[...]))
    for rows, acc in zip(parts, accs):
        o_ref[rows, :] = _ln_rows(acc, g_ref[...], b_ref[...])


def _out_ln(a_list, w, layer, h, g, b, tm=ROW_TILE, name="out_ln"):
    n, d = h.shape
    widths = tuple(a.shape[1] for a in a_list)
    k = sum(widths)
    in_specs = [pl.BlockSpec((tm, wd), lambda i: (i, 0)) for wd in widths]
    in_specs += [pl.BlockSpec((None, k, d), lambda i: (layer, 0, 0)),
                 pl.BlockSpec((tm, d), lambda i: (i, 0)),
                 pl.BlockSpec((1, d), lambda i: (0, 0)),
                 pl.BlockSpec((1, d), lambda i: (0, 0))]
    return pl.pallas_call(
        functools.partial(_out_ln_kernel, widths=widths),
        grid=(n // tm,),
        in_specs=in_specs,
        out_specs=pl.BlockSpec((tm, d), lambda i: (i, 0)),
        out_shape=jax.ShapeDtypeStruct((n, d), F32),
        compiler_params=_cparams("parallel"),
        name=name,
    )(*a_list, w, h, g.reshape(1, d), b.reshape(1, d))


def _mlp_kernel(h_ref, w1_ref, w2_ref, g_ref, b_ref, o_ref, acc_ref):
    f = pl.program_id(1)

    @pl.when(f == 0)
    def _():
        acc_ref[...] = jnp.zeros_like(acc_ref)

    u = jnp.maximum(_dot(h_ref[...].astype(BF16), w1_ref[...]), 0.0)
    acc_ref[...] += _dot((u * u).astype(BF16), w2_ref[...])

    @pl.when(f == pl.num_programs(1) - 1)
    def _():
        y = DEEPNORM_ALPHA * h_ref[...] + acc_ref[...]
        o_ref[...] = _ln_rows(y, g_ref[...], b_ref[...])


def _mlp_ln(h, w1, w2, layer, g, b, tm=ROW_TILE, tf=1024):
    n, d = h.shape
    dff = w1.shape[2]
    return pl.pallas_call(
        _mlp_kernel,
        grid=(n // tm, dff // tf),
        in_specs=[pl.BlockSpec((tm, d), lambda i, f: (i, 0)),
                  pl.BlockSpec((None, d, tf), lambda i, f: (layer, 0, f)),
                  pl.BlockSpec((None, tf, d), lambda i, f: (layer, f, 0)),
                  pl.BlockSpec((1, d), lambda i, f: (0, 0)),
                  pl.BlockSpec((1, d), lambda i, f: (0, 0))],
        out_specs=pl.BlockSpec((tm, d), lambda i, f: (i, 0)),
        out_shape=jax.ShapeDtypeStruct((n, d), F32),
        scratch_shapes=[pltpu.VMEM((tm, d), F32)],
        compiler_params=_cparams("parallel", "arbitrary"),
        name="mlp_ln",
    )(h, w1, w2, g.reshape(1, d), b.reshape(1, d))


def _mla_up_kernel(cq_ref, ckv_ref, kr_ref, gq_ref, gkv_ref, wq_ref, wkv_ref, tc_ref, ts1_ref, ts2_ref,
                   q_ref, k_ref, v_ref, *, scale):
    half = MLA_ROPE // 2
    c, s1, s2 = tc_ref[...], ts1_ref[...], ts2_ref[...]
    nq = _rms_rows(cq_ref[...].astype(F32), gq_ref[...]).astype(BF16)
    nkv = _rms_rows(ckv_ref[...].astype(F32), gkv_ref[...]).astype(BF16)
    qf = _dot(nq, wq_ref[...])
    kvf = _dot(nkv, wkv_ref[...])
    kr = _rope128(kr_ref[...].astype(F32), c, s1, s2, half).astype(BF16)
    for h in range(MLA_HEADS):
        o = h * MLA_SLOT
        q_ref[0, h, :, 0:LANES] = (qf[:, o:o + LANES] * scale).astype(BF16)
        q_ref[0, h, :, LANES:] = (_rope128(qf[:, o + LANES:o + MLA_SLOT], c, s1, s2, half) * scale).astype(BF16)
        k_ref[0, h, :, 0:LANES] = kvf[:, h * LANES:(h + 1) * LANES].astype(BF16)
        k_ref[0, h, :, LANES:] = kr
        v_ref[0, h, :, 0:LANES] = kvf[:, (MLA_HEADS + h) * LANES:(MLA_HEADS + h + 1) * LANES].astype(BF16)
        v_ref[0, h, :, LANES:] = jnp.ones((kr.shape[0], LANES), BF16)


def _mla_up(proj, gq, gkv, wq, wkv, tabs, batch, seq, tm=ROW_TILE):
    nt = seq // tm
    row = lambda b, i: b * nt + i
    tab_spec = pl.BlockSpec((tm, LANES), lambda b, i: (i, 0))
    hm = lambda w: pl.BlockSpec((1, MLA_HEADS, tm, w), lambda b, i: (b, 0, i, 0))
    return pl.pallas_call(
        functools.partial(_mla_up_kernel, scale=(MLA_NOPE + MLA_ROPE) ** -0.5),
        grid=(batch, nt),
        in_specs=[pl.BlockSpec((tm, WIDE), lambda b, i: (row(b, i), COL512["c_q"])),
                  pl.BlockSpec((tm, WIDE), lambda b, i: (row(b, i), COL512["c_kv"])),
                  pl.BlockSpec((tm, LANES), lambda b, i: (row(b, i), COL128["k_rope"])),
                  pl.BlockSpec((1, MLA_Q_RANK), lambda b, i: (0, 0)),
                  pl.BlockSpec((1, MLA_KV_RANK), lambda b, i: (0, 0)),
                  pl.BlockSpec(wq.shape, lambda b, i: (0, 0)),
                  pl.BlockSpec(wkv.shape, lambda b, i: (0, 0)),
                  tab_spec, tab_spec, tab_spec],
        out_specs=[hm(MLA_SLOT), hm(MLA_SLOT), hm(2 * HEAD_DIM)],
        out_shape=[jax.ShapeDtypeStruct((batch, MLA_HEADS, seq, MLA_SLOT), BF16),
                   jax.ShapeDtypeStruct((batch, MLA_HEADS, seq, MLA_SLOT), BF16),
                   jax.ShapeDtypeStruct((batch, MLA_HEADS, seq, 2 * HEAD_DIM), BF16)],
        compiler_params=_cparams("parallel", "parallel"),
        name="mla_up",
    )(proj, proj, proj, gq.reshape(1, -1), gkv.reshape(1, -1), wq, wkv, *tabs)


def _mla_attn_kernel(q_ref, k_ref, v_ref, o_ref, acc_sc, *, tq, rs):
    qi = pl.program_id(2)
    heads = q_ref.shape[1]
    causal = lax.broadcasted_iota(jnp.int32, (tq, tq), 1) <= lax.broadcasted_iota(jnp.int32, (tq, tq), 0)
    nrs = tq // rs
    n_chain = heads * nrs

    def step(kb, carry, diag):
        def keys(c):
            nk = (c % nrs + 1) * rs if diag else tq
            return pl.ds(pl.multiple_of(kb * tq, tq), nk), nk

        def scores(c):
            h, r0 = c // nrs, (c % nrs) * rs
            r, nk = keys(c)
            s = _dot_nt(q_ref[0, h, r0:r0 + rs, :], k_ref[0, h, r, :])
            return jnp.where(causal[r0:r0 + rs, :nk], s, MASKED) if diag else s

        out, s_next = [], scores(0)
        for c in range(n_chain):
            s, s_next = s_next, (scores(c + 1) if c + 1 < n_chain else None)
            m_old, acc_old = (None, None) if carry is None else (carry[c], acc_sc[c])
            m_new, acc_sc[c] = _flash_update(s, m_old, acc_old, v_ref[0, c // nrs, keys(c)[0], :])
            out.append(m_new)
        return tuple(out)

    carry = step(qi, None, True)
    lax.fori_loop(0, qi, lambda kb, c: step(kb, c, False), carry)
    for h in range(heads):
        for i in range(nrs):
            acc = acc_sc[h * nrs + i]
            o_ref[0, i * rs:(i + 1) * rs, h * HEAD_DIM:(h + 1) * HEAD_DIM] = (
                acc[:, :HEAD_DIM] / acc[:, HEAD_DIM:]).astype(o_ref.dtype)


def _mla_attn(q, k, v, tq=ROW_TILE, rs=CHAIN_ROWS, hg=MLA_HEADS // 2):
    batch, heads, seq, _ = q.shape
    return pl.pallas_call(
        functools.partial(_mla_attn_kernel, tq=tq, rs=rs),
        grid=(batch, heads // hg, seq // tq),
        in_specs=[pl.BlockSpec((1, hg, tq, MLA_SLOT), lambda b, g, i: (b, g, i, 0)),
                  pl.BlockSpec((1, hg, seq, MLA_SLOT), lambda b, g, i: (b, g, 0, 0)),
                  pl.BlockSpec((1, hg, seq, 2 * HEAD_DIM), lambda b, g, i: (b, g, 0, 0))],
        out_specs=pl.BlockSpec((1, tq, hg * HEAD_DIM), lambda b, g, i: (b, i, g)),
        out_shape=jax.ShapeDtypeStruct((batch, seq, heads * HEAD_DIM), BF16),
        scratch_shapes=[pltpu.VMEM((hg * tq // rs, rs, 2 * HEAD_DIM), F32)],
        compiler_params=_cparams("parallel", "parallel", "arbitrary"),
        name="mla_attn",
    )(q, k, v)


def _nsa_kernel(nq_ref, kc_ref, vc_ref, ks_ref, vs_ref, kw_ref, vw_ref, ng_ref,
                w1_ref, w2_ref, pos_ref, ovt_ref, tc_ref, ts1_ref, ts2_ref,
                o_ref,
                tmp_sc, kc_sc, vc_sc, ks_sc, vs_sc, kw_sc, vw_sc, acc_sc, *, tq, seq, scale):
    qi = pl.program_id(1)
    half = PARTIAL_ROT // 2
    heads = NSA_HEADS
    n_chunk = seq // NSA_CMP_STRIDE
    n_sel = seq // NSA_SEL_LEN

    @pl.when(qi == 0)
    def _prep():
        c, s1, s2 = tc_ref[...], ts1_ref[...], ts2_ref[...]
        ones = jnp.ones((seq, LANES), BF16)
        blk = lax.broadcasted_iota(jnp.int32, (seq, LANES), 0) >> NSA_SEL_SHIFT
        ks_sc[:, 0:LANES] = _rope128(ks_ref[...].astype(F32), c, s1, s2, half).astype(BF16)
        ks_sc[:, LANES:] = jnp.where(blk == lax.broadcasted_iota(jnp.int32, (seq, LANES), 1), 1.0, 0.0).astype(BF16)
        kw_sc[...] = _rope128(kw_ref[...].astype(F32), c, s1, s2, half).astype(BF16)
        vs_sc[:, 0:LANES] = vs_ref[...].astype(BF16)
        vs_sc[:, LANES:] = ones
        vw_sc[:, 0:LANES] = vw_ref[...].astype(BF16)
        vw_sc[:, LANES:] = ones
        for i, (src, dst) in enumerate(((kc_ref, kc_sc), (vc_ref, vc_sc))):
            x_all = src[...].astype(F32)
            tmp_sc[...] = _rope128(x_all, c, s1, s2, half) if i == 0 else x_all
            lo, hi = [], []
            for t in range(NSA_CMP_STRIDE):
                x = tmp_sc[pl.ds(t, n_chunk, stride=NSA_CMP_STRIDE), :]
                lo.append((x + pos_ref[i, t:t + 1, :]).astype(BF16))
                hi.append((x + pos_ref[i, NSA_CMP_STRIDE + t:NSA_CMP_STRIDE + t + 1, :]).astype(BF16))
            kw1 = NSA_CMP_STRIDE * HEAD_DIM
            a = _dot(jnp.concatenate(lo, axis=1), w1_ref[i, 0:kw1, :])
            bm = _dot(jnp.concatenate(hi, axis=1), w1_ref[i, kw1:2 * kw1, :])
            hid = jax.nn.gelu(a + pltpu.roll(bm, n_chunk - 1, 0))
            dst[...] = _dot(hid.astype(BF16), w2_ref[i]).astype(BF16)

    t0 = pl.multiple_of(qi * tq, tq)
    rq = pl.ds(t0, tq)
    c, s1, s2 = tc_ref[rq, :], ts1_ref[rq, :], ts2_ref[rq, :]
    qf = nq_ref[...].astype(F32)
    qs = jnp.concatenate(
        [_rope128(qf[:, h * LANES:(h + 1) * LANES], c, s1, s2, half) * scale for h in range(heads)],
        axis=0)
    qs_bf = qs.astype(BF16)

    lane = lax.broadcasted_iota(jnp.int32, (tq, LANES), 1)
    tpos = t0 + lax.broadcasted_iota(jnp.int32, (tq, LANES), 0)

    s = _dot_nt(qs_bf, kc_sc[...]).reshape(heads, tq, LANES)
    cmask = (lane * NSA_CMP_STRIDE + (NSA_CMP_LEN - 1) <= tpos)[None]
    s = jnp.where(cmask, s, NEG_INF)
    m = jnp.max(s, axis=-1, keepdims=True)
    m = jnp.where(m == NEG_INF, 0.0, m)
    e = jnp.exp(s - m)
    p_cmp = e / jnp.maximum(jnp.sum(e, axis=-1, keepdims=True), 1e-30)
    o_cmp = _dot(p_cmp.reshape(heads * tq, LANES).astype(BF16), vc_sc[...]).reshape(heads, tq, HEAD_DIM)

    p_hi, p_lo = _split_bf16(p_cmp[0] + p_cmp[1] + p_cmp[2] + p_cmp[3])
    imp_t = (_dot_nt(ovt_ref[...], p_hi) + _dot_nt(ovt_ref[...], p_lo))[0:n_sel]
    blk_t = lax.broadcasted_iota(jnp.int32, (n_sel, tq), 0)
    cur_t = (t0 + lax.broadcasted_iota(jnp.int32, (n_sel, tq), 1)) >> NSA_SEL_SHIFT
    eligible = blk_t <= cur_t
    forced = (blk_t == 0) | (blk_t == cur_t) | (blk_t == cur_t - 1)
    score = jnp.where(eligible, jnp.where(forced, NSA_FORCE_SCORE, imp_t), NEG_INF)
    rank = _rank_rows(score, n_sel)
    bias_t = jnp.where(eligible & (rank < NSA_SEL_TOPK), 0.0, MASKED)
    bias_t = jnp.concatenate([bias_t, jnp.zeros((LANES - n_sel, tq), F32)], axis=0)
    bias = _rows_to_lanes(bias_t, tq)
    q_aug = jnp.concatenate([qs, jnp.concatenate([bias] * heads, axis=0)], axis=1).astype(BF16)

    assert tq == NSA_WINDOW
    local_r = lax.broadcasted_iota(jnp.int32, (tq, tq), 0)
    local_c = lax.broadcasted_iota(jnp.int32, (tq, tq), 1)
    causal = local_c <= local_r
    rc = acc_sc.shape[1]
    n_chain = heads * tq // rc

    def run(q_rows, k_sc, v_sc, kb, carry, mask, lower):
        def keys(c):
            r0 = (c * rc) % tq
            k0, nk = (0, tq) if mask is None else ((0, r0 + rc) if lower else (r0, tq - r0))
            return pl.ds(pl.multiple_of(kb * tq + k0, rc), nk), r0, k0, nk

        def scores(c):
            r, r0, k0, nk = keys(c)
            s = _dot_nt(q_rows[c * rc:(c + 1) * rc], k_sc[r, :])
            return s if mask is None else jnp.where(mask[r0:r0 + rc, k0:k0 + nk], s, MASKED)

        out, s_next = [], scores(0)
        for c in range(n_chain):
            s, s_next = s_next, (scores(c + 1) if c + 1 < n_chain else None)
            m_new, acc_sc[c] = _flash_update(s, carry[c], acc_sc[c], v_sc[keys(c)[0], :])
            out.append(m_new)
        return tuple(out)

    def finish():
        acc = jnp.concatenate([acc_sc[c] for c in range(n_chain)], axis=0)
        return (acc[:, :HEAD_DIM] / acc[:, HEAD_DIM:]).reshape(heads, tq, HEAD_DIM)

    init = tuple(jnp.full((rc, 1), NEG_INF, F32) for _ in range(n_chain))

    acc_sc[...] = jnp.zeros(acc_sc.shape, F32)
    carry = lax.fori_loop(0, qi, lambda kb, cr: run(q_aug, ks_sc, vs_sc, kb, cr, None, True), init)
    run(q_aug, ks_sc, vs_sc, qi, carry, causal, True)
    o_sel = finish()

    acc_sc[...] = jnp.zeros(acc_sc.shape, F32)
    carry = run(qs_bf, kw_sc, vw_sc, jnp.maximum(qi - 1, 0), init, (local_c > local_r) & (qi >= 1), False)
    run(qs_bf, kw_sc, vw_sc, qi, carry, causal, True)
    o_win = finish()

    g = jax.nn.sigmoid(ng_ref[...].astype(F32))
    for h in range(heads):
        o = (g[:, 3 * h:3 * h + 1] * o_cmp[h] + g[:, 3 * h + 1:3 * h + 2] * o_sel[h]
             + g[:, 3 * h + 2:3 * h + 3] * o_win[h])
        o_ref[:, h * LANES:(h + 1) * LANES] = o.astype(o_ref.dtype)


def _nsa(proj, w1, w2, pos, tabs, batch, seq, tq=NSA_WINDOW, rc=CHAIN_ROWS):
    nt = seq // tq
    n_cmp = (seq - NSA_CMP_LEN) // NSA_CMP_STRIDE + 1
    n_sel = seq // NSA_SEL_LEN
    starts = np.arange(LANES) * NSA_CMP_STRIDE
    sel_start = np.arange(LANES) * NSA_SEL_LEN
    overlap = ((starts[:, None] < sel_start[None, :] + NSA_SEL_LEN)
               & (starts[:, None] + NSA_CMP_LEN > sel_start[None, :])
               & (np.arange(LANES)[:, None] < n_cmp) & (np.arange(LANES)[None, :] < n_sel))
    ovt = jnp.asarray(overlap.T.astype(np.float32), BF16)
    seq_col = lambda name: pl.BlockSpec((seq, LANES), lambda b, i: (b, COL128[name]))
    full = lambda a: pl.BlockSpec(a.shape, lambda b, i: (0,) * a.ndim)
    return pl.pallas_call(
        functools.partial(_nsa_kernel, tq=tq, seq=seq, scale=HEAD_DIM ** -0.5),
        grid=(batch, nt),
        in_specs=[pl.BlockSpec((tq, WIDE), lambda b, i: (b * nt + i, COL512["nq"])),
                  seq_col("nkc"), seq_col("nvc"), seq_col("nks"), seq_col("nvs"), seq_col("nkw"), seq_col("nvw"),
                  pl.BlockSpec((tq, LANES), lambda b, i: (b * nt + i, COL128["ng"])),
                  full(w1), full(w2), full(pos), full(ovt), full(tabs[0]), full(tabs[1]), full(tabs[2])],
        out_specs=pl.BlockSpec((tq, NSA_HEADS * HEAD_DIM), lambda b, i: (b * nt + i, 0)),
        out_shape=jax.ShapeDtypeStruct((batch * seq, NSA_HEADS * HEAD_DIM), BF16),
        scratch_shapes=[pltpu.VMEM((seq, LANES), F32),
                        pltpu.VMEM((LANES, LANES), BF16), pltpu.VMEM((LANES, LANES), BF16),
                        pltpu.VMEM((seq, 2 * LANES), BF16), pltpu.VMEM((seq, 2 * LANES), BF16),
                        pltpu.VMEM((seq, LANES), BF16), pltpu.VMEM((seq, 2 * LANES), BF16),
                        pltpu.VMEM((NSA_HEADS * tq // rc, rc, 2 * HEAD_DIM), F32)],
        compiler_params=_cparams("parallel", "arbitrary"),
        name="nsa",
    )(proj, proj, proj, proj, proj, proj, proj, proj, w1, w2, pos, ovt, *tabs)


def _moba_kernel(mq_ref, mk_ref, mv_ref, tc_ref, ts1_ref, ts2_ref, o_ref, k_sc, v_sc, km_sc, acc_sc,
                 *, tq, seq, scale):
    qi = pl.program_id(1)
    half = PARTIAL_ROT // 2
    heads = MOBA_HEADS
    n_blk = seq // MOBA_BLOCK
    slot = 2 * LANES

    @pl.when(qi == 0)
    def _prep():
        c, s1, s2 = tc_ref[...], ts1_ref[...], ts2_ref[...]
        km_sc[...] = jnp.zeros_like(km_sc)
        ones = jnp.ones((seq, LANES), BF16)
        blk = lax.broadcasted_iota(jnp.int32, (seq, LANES), 0) >> MOBA_SHIFT
        onehot = jnp.where(blk == lax.broadcasted_iota(jnp.int32, (seq, LANES), 1), 1.0, 0.0).astype(BF16)
        for h in range(heads):
            hs = slice(h * LANES, (h + 1) * LANES)
            kh = _rope128(mk_ref[:, hs].astype(F32), c, s1, s2, half)
            k_sc[:, h * slot:h * slot + LANES] = kh.astype(BF16)
            k_sc[:, h * slot + LANES:(h + 1) * slot] = onehot
            v_sc[:, h * slot:h * slot + LANES] = mv_ref[:, hs].astype(BF16)
            v_sc[:, h * slot + LANES:(h + 1) * slot] = ones
            for j in range(n_blk):
                km_sc[h * n_blk + j:h * n_blk + j + 1, :] = jnp.mean(
                    kh[j * MOBA_BLOCK:(j + 1) * MOBA_BLOCK], axis=0, keepdims=True)

    t0 = pl.multiple_of(qi * tq, tq)
    rq = pl.ds(t0, tq)
    c, s1, s2 = tc_ref[rq, :], ts1_ref[rq, :], ts2_ref[rq, :]
    km_hi, km_lo = _split_bf16(km_sc[...])
    tk = tq
    rc = MOBA_BLOCK
    sub = tq // rc
    blk_t = lax.broadcasted_iota(jnp.int32, (n_blk, tq), 0)
    cur_t = (t0 + lax.broadcasted_iota(jnp.int32, (n_blk, tq), 1)) >> MOBA_SHIFT
    row = lax.broadcasted_iota(jnp.int32, (tq, tk), 0)
    col = lax.broadcasted_iota(jnp.int32, (tq, tk), 1)

    qfs, bias_ts = [], []
    for h in range(heads):
        qf = _rope128(mq_ref[:, h * LANES:(h + 1) * LANES].astype(F32), c, s1, s2, half) * scale
        q_hi, q_lo = _split_bf16(qf)
        gate_t = (_dot_nt(km_hi, q_hi) + (_dot_nt(km_lo, q_hi) + _dot_nt(km_hi, q_lo)))[h * n_blk:(h + 1) * n_blk]
        eligible = blk_t < cur_t
        score = jnp.where(eligible, gate_t, NEG_INF)
        picked = eligible & (_rank_rows(score, n_blk) < MOBA_TOPK)
        bias_t = jnp.where(picked | (blk_t == cur_t), 0.0, MASKED)
        qfs.append(qf)
        bias_ts.append(jnp.concatenate([bias_t, jnp.zeros((LANES - n_blk, tq), F32)], axis=0))
    bias = _rows_to_lanes(jnp.concatenate(bias_ts, axis=0), tq)
    q_aug = [jnp.concatenate([qfs[h], bias[:, h * LANES:(h + 1) * LANES]], axis=1).astype(BF16) for h in range(heads)]

    n_chain = heads * sub

    causal = col <= row

    def step(kb, carry, diag):
        def keys(c):
            nk = (c % sub + 1) * rc if diag else tk
            return pl.ds(pl.multiple_of(kb * tk, tk), nk), nk

        def scores(c):
            h, r0 = c // sub, (c % sub) * rc
            r, nk = keys(c)
            s = _dot_nt(q_aug[h][r0:r0 + rc], k_sc[r, h * slot:(h + 1) * slot])
            return jnp.where(causal[r0:r0 + rc, :nk], s, MASKED) if diag else s

        out, s_next = [], scores(0)
        for c in range(n_chain):
            s, s_next = s_next, (scores(c + 1) if c + 1 < n_chain else None)
            h = c // sub
            m_old, acc_old = (None, None) if carry is None else (carry[c], acc_sc[c])
            m_new, acc_sc[c] = _flash_update(s, m_old, acc_old, v_sc[keys(c)[0], h * slot:(h + 1) * slot])
            out.append(m_new)
        return tuple(out)

    carry = step(qi, None, True)
    lax.fori_loop(0, qi, lambda kb, cr: step(kb, cr, False), carry)
    for c in range(n_chain):
        h, r0 = c // sub, (c % sub) * rc
        acc = acc_sc[c]
        o_ref[r0:r0 + rc, h * LANES:(h + 1) * LANES] = (acc[:, :HEAD_DIM] / acc[:, HEAD_DIM:]).astype(o_ref.dtype)


def _moba(proj, tabs, batch, seq):
    tq = 2 * MOBA_BLOCK
    nt = seq // tq
    width = MOBA_HEADS * HEAD_DIM
    full = lambda a: pl.BlockSpec(a.shape, lambda b, i: (0,) * a.ndim)
    return pl.pallas_call(
        functools.partial(_moba_kernel, tq=tq, seq=seq, scale=HEAD_DIM ** -0.5),
        grid=(batch, nt),
        in_specs=[pl.BlockSpec((tq, width), lambda b, i: (b * nt + i, COL512["mq"])),
                  pl.BlockSpec((seq, width), lambda b, i: (b, COL512["mk"])),
                  pl.BlockSpec((seq, width), lambda b, i: (b, COL512["mv"])),
                  full(tabs[0]), full(tabs[1]), full(tabs[2])],
        out_specs=pl.BlockSpec((tq, width), lambda b, i: (b * nt + i, 0)),
        out_shape=jax.ShapeDtypeStruct((batch * seq, width), BF16),
        scratch_shapes=[pltpu.VMEM((seq, 2 * width), BF16), pltpu.VMEM((seq, 2 * width), BF16),
                        pltpu.VMEM((LANES, LANES), F32),
                        pltpu.VMEM((MOBA_HEADS * tq // MOBA_BLOCK, MOBA_BLOCK, 2 * HEAD_DIM), F32)],
        compiler_params=_cparams("parallel", "arbitrary"),
        name="moba",
    )(proj, proj, proj, *tabs)


def _xattn_out_ln_kernel(q_ref, kv_ref, w_ref, h_ref, g_ref, b_ref, o_ref):
    d = q_ref.shape[1]
    hd = d // MEM_HEADS
    head = lambda h: slice(h * hd, (h + 1) * hd)
    scores = [_dot_nt(q_ref[:, head(h)], kv_ref[0, :, head(h)]) for h in range(MEM_HEADS)]
    ctx = []
    for h, s in enumerate(scores):
        e = jnp.exp(s - jnp.max(s, axis=-1, keepdims=True))
        p = e / jnp.sum(e, axis=-1, keepdims=True)
        ctx.append(_dot(p.astype(BF16), kv_ref[0, :, d + h * hd:d + (h + 1) * hd]).astype(BF16))
    ctx = jnp.concatenate(ctx, axis=1)
    tm = h_ref.shape[0]
    parts = tuple(slice(i * tm // 4, (i + 1) * tm // 4) for i in range(4))
    accs = [DEEPNORM_ALPHA * h_ref[rows, :] + _dot(ctx[rows], w_ref[...]) for rows in parts]
    for rows, acc in zip(parts, accs):
        o_ref[rows, :] = _ln_rows(acc, g_ref[...], b_ref[...])


def _xattn_out_ln(q, kv, w, layer, h, g, b, batch, seq, tm=ROW_TILE):
    n, d = h.shape
    nt = seq // tm
    m_len = kv.shape[0] // batch
    row = lambda bi, i: (bi * nt + i, 0)
    return pl.pallas_call(
        _xattn_out_ln_kernel,
        grid=(batch, nt),
        in_specs=[pl.BlockSpec((tm, d), row),
                  pl.BlockSpec((1, m_len, 2 * d), lambda bi, i: (bi, 0, 0)),
                  pl.BlockSpec((None, d, d), lambda bi, i: (layer, 0, 0)),
                  pl.BlockSpec((tm, d), row),
                  pl.BlockSpec((1, d), lambda bi, i: (0, 0)),
                  pl.BlockSpec((1, d), lambda bi, i: (0, 0))],
        out_specs=pl.BlockSpec((tm, d), row),
        out_shape=jax.ShapeDtypeStruct((n, d), F32),
        compiler_params=_cparams("parallel", "parallel"),
        name="xattn_out_ln",
    )(q, kv.reshape(batch, m_len, 2 * d), w, h, g.reshape(1, d), b.reshape(1, d))


IN_SPLIT_NAMES = ("c_q", "c_kv", "k_rope", "nq", "nkc", "nvc", "nks", "nvs", "nkw", "nvw", "ng", "mq", "mk", "mv")
IN_SPLIT_SIZES = (MLA_Q_RANK, MLA_KV_RANK, MLA_ROPE, WIDE) + (HEAD_DIM,) * 6 + (3 * NSA_HEADS, WIDE, WIDE, WIDE)


def _pack_w_in_kernel(w_ref, o_ref):
    rows = w_ref.shape[0]
    off = 0
    for name, sz in zip(IN_SPLIT_NAMES, IN_SPLIT_SIZES):
        width = WIDE if name in COL512 else LANES
        dst = COL512[name] * WIDE if name in COL512 else COL128[name] * LANES
        x = w_ref[:, off:off + sz].astype(BF16)
        if sz < width:
            x = jnp.concatenate([x, jnp.zeros((rows, width - sz), BF16)], axis=1)
        o_ref[0, :, dst:dst + width] = x
        off += sz


def _pack_w_in(w, tr=256):
    layers, d, width = w.shape
    return pl.pallas_call(
        _pack_w_in_kernel,
        grid=(layers, d // tr),
        in_specs=[pl.BlockSpec((None, tr, width), lambda l, i: (l, i, 0))],
        out_specs=pl.BlockSpec((1, tr, PROJ_WIDTH), lambda l, i: (l, i, 0)),
        out_shape=jax.ShapeDtypeStruct((layers, d, PROJ_WIDTH), BF16),
        compiler_params=_cparams("parallel", "parallel"),
        name="pack_w_in",
    )(w)


def _pack_w_uq(w):
    r = w.shape[0]
    w = w.reshape(r, MLA_HEADS, MLA_NOPE + MLA_ROPE)
    w = jnp.pad(w, ((0, 0), (0, 0), (0, MLA_SLOT - MLA_NOPE - MLA_ROPE)))
    return w.reshape(r, MLA_HEADS * MLA_SLOT).astype(BF16)


def _pack_w_ukv(w):
    r = w.shape[0]
    w = w.reshape(r, MLA_HEADS, 2, HEAD_DIM).transpose(0, 2, 1, 3)
    return w.reshape(r, 2 * MLA_HEADS * HEAD_DIM).astype(BF16)


def kernel(x, mem, ln_in_g, ln_in_b, w_in, mla_q_norm, mla_kv_norm, mla_w_uq, mla_w_ukv, nsa_cmp_w1, nsa_cmp_w2, nsa_cmp_pos, w_out, ln1_g, ln1_b, mem_wq, mem_wkv, mem_wo, ln2_g, ln2_b, mlp_w1, mlp_w2, ln3_g, ln3_b):
    batch, seq, d = x.shape
    n = batch * seq
    mla_tabs = _rope_tables(seq, MLA_ROPE)
    rot_tabs = _rope_tables(seq, PARTIAL_ROT)
    mem2 = mem.reshape(batch * mem.shape[1], d)
    w_in_packed = _pack_w_in(w_in)
    w_out_bf, mem_wkv_bf, mem_wo_bf = w_out.astype(BF16), mem_wkv.astype(BF16), mem_wo.astype(BF16)
    mem_wq_bf = (mem_wq * (d // MEM_HEADS) ** -0.5).astype(BF16)
    mlp_w1_bf, mlp_w2_bf = mlp_w1.astype(BF16), mlp_w2.astype(BF16)

    h = _layer_norm(x.reshape(n, d), ln_in_g, ln_in_b)
    for l in range(DEPTH):
        proj = _matmul(h, w_in_packed, l, BF16, name="in_proj")
        q, k, v = _mla_up(proj, mla_q_norm[l], mla_kv_norm[l], _pack_w_uq(mla_w_uq[l]), _pack_w_ukv(mla_w_ukv[l]),
                          mla_tabs, batch, seq)
        o_a = _mla_attn(q, k, v).reshape(n, MLA_HEADS * HEAD_DIM)
        o_b = _nsa(proj, nsa_cmp_w1[l].astype(BF16), nsa_cmp_w2[l].astype(BF16), nsa_cmp_pos[l], rot_tabs, batch, seq)
        o_c = _moba(proj, rot_tabs, batch, seq)
        h = _out_ln([o_a, o_b, o_c], w_out_bf, l, h, ln1_g[l], ln1_b[l], name="mix_out_ln")

        xq = _matmul(h, mem_wq_bf, l, BF16, name="mem_q")
        xkv = _matmul(mem2, mem_wkv_bf, l, BF16, name="mem_kv")
        h = _xattn_out_ln(xq, xkv, mem_wo_bf, l, h, ln2_g[l], ln2_b[l], batch, seq)

        h = _mlp_ln(h, mlp_w1_bf, mlp_w2_bf, l, ln3_g[l], ln3_b[l])
    return h.reshape(batch, seq, d)
```
